```python
import jax, jax.numpy as jnp
from jax import lax
import numpy as np

D_MODEL = 1024
BATCH = 4
SEQ = 4096
DEPTH = 1

HEAD_DIM = 64
N_HEADS_NSA = 8
N_KV_NSA = 2
GROUP_NSA = N_HEADS_NSA // N_KV_NSA
N_HEADS_DIL = 8
ROPE_THETA = 500000.0
ROPE_DIM = HEAD_DIM // 4
CMP_BLOCK = 32
CMP_STRIDE = 16
CMP_HIDDEN = 256
SLC_BLOCK = 64
SLC_TOPK = 16
WIN_NSA = 512
DIL_CONFIGS = ((128, 1), (512, 4), (2048, 16))
BAND_BLOCK = 128
SLC_CHUNK = 64
N_EXPERTS = 32
TOP_K = 4
D_FF = D_MODEL
SWIGLU_LIMIT = 7.0
SWIGLU_ALPHA = 1.702
RMS_EPS = 1e-6
NEG_INF = -1e30
FORCE_SCORE = 1e9

Q_NSA = N_HEADS_NSA * HEAD_DIM
KV_NSA = N_KV_NSA * HEAD_DIM
Q_DIL = N_HEADS_DIL * HEAD_DIM
MIX_WIDTH = Q_NSA + Q_DIL
IN_WIDTHS = (Q_NSA, KV_NSA, KV_NSA, KV_NSA, KV_NSA, KV_NSA, KV_NSA, 3 * N_HEADS_NSA, Q_DIL, Q_DIL, Q_DIL)
IN_WIDTH = sum(IN_WIDTHS)
IN_SPLITS = tuple(int(s) for s in np.cumsum(IN_WIDTHS)[:-1])
SCALE = HEAD_DIM ** -0.5

kernel_name = "hybrid_nsa_dilated_moe_layer"


def rms_norm(x, g):
    xf = x.astype(jnp.float32)
    y = xf * lax.rsqrt(jnp.mean(xf * xf, axis=-1, keepdims=True) + RMS_EPS)
    return (y * g.astype(jnp.float32)).astype(x.dtype)


def rope_partial(x, positions):
    half = ROPE_DIM // 2
    inv = ROPE_THETA ** (-jnp.arange(half, dtype=jnp.float32) / half)
    ang = positions.astype(jnp.float32)[..., None] * inv
    cos = jnp.cos(ang)[:, :, None, :]
    sin = jnp.sin(ang)[:, :, None, :]
    xr = x[..., :ROPE_DIM].astype(jnp.float32)
    x1, x2 = xr[..., :half], xr[..., half:]
    rot = jnp.concatenate([x1 * cos - x2 * sin, x2 * cos + x1 * sin], axis=-1).astype(x.dtype)
    return jnp.concatenate([rot, x[..., ROPE_DIM:]], axis=-1)


def band_attention(q, k, v, max_dist):
    B, Hk, R, L, hd = q.shape
    blk = BAND_BLOCK
    n_prev = -(-max_dist // blk)
    nb = -(-L // blk)
    pad = nb * blk - L
    qb = jnp.pad(q, ((0, 0), (0, 0), (0, 0), (0, pad), (0, 0))).reshape(B, Hk, R, nb, blk, hd)
    kv_pad = ((0, 0), (0, 0), (n_prev * blk, pad), (0, 0))
    kb = jnp.pad(k, kv_pad).reshape(B, Hk, nb + n_prev, blk, hd)
    vb = jnp.pad(v, kv_pad).reshape(B, Hk, nb + n_prev, blk, hd)
    kseg = jnp.concatenate([kb[:, :, i:i + nb] for i in range(n_prev + 1)], axis=3)
    vseg = jnp.concatenate([vb[:, :, i:i + nb] for i in range(n_prev + 1)], axis=3)
    s = jnp.einsum("bhrnqd,bhnkd->bhrnqk", qb, kseg, preferred_element_type=jnp.float32) * SCALE
    start = jnp.arange(nb)[:, None, None] * blk
    qpos = start + jnp.arange(blk)[None, :, None]
    kpos = start - n_prev * blk + jnp.arange((n_prev + 1) * blk)[None, None, :]
    dist = qpos - kpos
    mask = (dist >= 0) & (dist <= max_dist) & (kpos >= 0)
    s = jnp.where(mask, s, NEG_INF)
    lse = jax.nn.logsumexp(s, axis=-1, keepdims=True)
    p = jnp.exp(s - lse).astype(v.dtype)
    o = jnp.einsum("bhrnqk,bhnkd->bhrnqd", p, vseg).reshape(B, Hk, R, nb * blk, hd)[:, :, :, :L]
    lse = lse.reshape(B, Hk, R, nb * blk)[:, :, :, :L]
    return o, lse


def dilated_attention(q, k, v):
    B, S, H, hd = q.shape
    outs, lses = [], []
    for window, dil in DIL_CONFIGS:
        m = S // dil

        def split(t):
            return t.reshape(B, m, dil, H, hd).transpose(0, 3, 2, 1, 4).reshape(B, H * dil, m, hd)

        o, lse = band_attention(split(q)[:, :, None], split(k), split(v), window // dil)
        outs.append(o[:, :, 0].reshape(B, H, dil, m, hd).transpose(0, 3, 2, 1, 4).reshape(B, S, H, hd))
        lses.append(lse[:, :, 0].reshape(B, H, dil, m).transpose(0, 3, 2, 1).reshape(B, S, H))
    w = jax.nn.softmax(jnp.stack(lses, axis=0), axis=0)
    return jnp.einsum("cbsh,cbshd->bshd", w.astype(q.dtype), jnp.stack(outs, axis=0))


def nsa_attention(q, k_cmp, v_cmp, k_slc, v_slc, k_win, v_win, gates, positions,
                  cmp_pos, w_ck1, w_ck2, w_cv1, w_cv2):
    B, S, Hq, hd = q.shape
    Hk, R = N_KV_NSA, GROUP_NSA
    t = jnp.arange(S)

    def heads_q(a):
        return a.reshape(B, S, Hk, R, hd).transpose(0, 2, 3, 1, 4)

    def heads_kv(a):
        return a.transpose(0, 2, 1, 3)

    n16 = S // CMP_STRIDE
    n_cmp = n16 - 1

    def compress(a, w1, w2):
        seg = heads_kv(a).reshape(B, Hk, n16, CMP_STRIDE, hd)
        blocks = jnp.concatenate([seg[:, :, :-1], seg[:, :, 1:]], axis=3) + cmp_pos
        hid = jax.nn.gelu(blocks.reshape(B, Hk, n_cmp, CMP_BLOCK * hd) @ w1)
        return hid @ w2

    kc = compress(k_cmp, w_ck1, w_ck2)
    vc = compress(v_cmp, w_cv1, w_cv2)
    q_raw = heads_q(q)
    s_c = jnp.einsum("bhrqd,bhcd->bhrqc", q_raw, kc, preferred_element_type=jnp.float32) * SCALE
    c_idx = jnp.arange(n_cmp)
    valid_c = (c_idx[None, :] * CMP_STRIDE + CMP_BLOCK - 1) <= t[:, None]
    p_c = jax.nn.softmax(jnp.where(valid_c, s_c, NEG_INF), axis=-1) * valid_c
    o_cmp = jnp.einsum("bhrqc,bhcd->bhrqd", p_c.astype(vc.dtype), vc)

    n_slc = S // SLC_BLOCK
    j_idx = jnp.arange(n_slc)
    cs = c_idx * CMP_STRIDE
    js = j_idx * SLC_BLOCK
    ov = jnp.clip(jnp.minimum(cs[:, None] + CMP_BLOCK, js[None, :] + SLC_BLOCK)
                  - jnp.maximum(cs[:, None], js[None, :]), 0, None).astype(jnp.float32) / CMP_BLOCK
    p_slc = jnp.einsum("bhqc,cj->bhqj", p_c.sum(axis=2), ov)
    cur = t // SLC_BLOCK
    valid_s = j_idx[None, :] <= cur[:, None]
    forced = (j_idx[None, :] == 0) | (j_idx[None, :] == cur[:, None]) | (j_idx[None, :] == cur[:, None] - 1)
    score = jnp.where(forced, FORCE_SCORE, jnp.where(valid_s, p_slc, -1.0))
    k_sel = min(SLC_TOPK, n_slc)
    _, sel = lax.top_k(score, k_sel)

    q_rot = heads_q(rope_partial(q, positions))
    kb = heads_kv(rope_partial(k_slc, positions)).reshape(B, Hk, n_slc, SLC_BLOCK, hd)
    vb = heads_kv(v_slc).reshape(B, Hk, n_slc, SLC_BLOCK, hd)
    nc = S // SLC_CHUNK
    q_chunks = q_rot.reshape(B, Hk, R, nc, SLC_CHUNK, hd).transpose(3, 0, 1, 2, 4, 5)
    sel_chunks = sel.reshape(B, Hk, nc, SLC_CHUNK, k_sel).transpose(2, 0, 1, 3, 4)
    t_chunks = t.reshape(nc, SLC_CHUNK)

    def one_chunk(args):
        qc, ic, tc = args
        flat = ic.reshape(B, Hk, SLC_CHUNK * k_sel)[..., None, None]
        kg = jnp.take_along_axis(kb, flat, axis=2).reshape(B, Hk, SLC_CHUNK, k_sel * SLC_BLOCK, hd)
        vg = jnp.take_along_axis(vb, flat, axis=2).reshape(B, Hk, SLC_CHUNK, k_sel * SLC_BLOCK, hd)
        kpos = (ic[..., None] * SLC_BLOCK + jnp.arange(SLC_BLOCK)).reshape(B, Hk, SLC_CHUNK, k_sel * SLC_BLOCK)
        s = jnp.einsum("bhrcd,bhckd->bhrck", qc, kg, preferred_element_type=jnp.float32) * SCALE
        mask = (kpos <= tc[:, None])[:, :, None]
        p = jax.nn.softmax(jnp.where(mask, s, NEG_INF), axis=-1)
        return jnp.einsum("bhrck,bhckd->bhrcd", p.astype(vg.dtype), vg)

    o_slc = lax.map(one_chunk, (q_chunks, sel_chunks, t_chunks))
    o_slc = o_slc.transpose(1, 2, 3, 0, 4, 5).reshape(B, Hk, R, S, hd)

    o_win, _ = band_attention(q_rot, heads_kv(rope_partial(k_win, positions)), heads_kv(v_win), WIN_NSA - 1)

    g = jax.nn.sigmoid(gates.reshape(B, S, Hk, R, 3).transpose(0, 2, 3, 1, 4))
    o = g[..., 0:1] * o_cmp + g[..., 1:2] * o_slc + g[..., 2:3] * o_win
    return o.transpose(0, 3, 1, 2, 4).reshape(B, S, Q_NSA)


def mixing(h, positions, w_in, cmp_pos, w_ck1, w_ck2, w_cv1, w_cv2, g_out_nsa, g_out_dil, w_o):
    B, S, _ = h.shape
    proj = h @ w_in
    q_a, kc, vc, ks, vs, kw, vw, gts, q_b, k_b, v_b = jnp.split(proj, IN_SPLITS, axis=-1)
    kvh = lambda a: a.reshape(B, S, N_KV_NSA, HEAD_DIM)
    o_a = nsa_attention(q_a.reshape(B, S, N_HEADS_NSA, HEAD_DIM), kvh(kc), kvh(vc), kvh(ks), kvh(vs),
                        kvh(kw), kvh(vw), gts.reshape(B, S, N_HEADS_NSA, 3), positions,
                        cmp_pos, w_ck1, w_ck2, w_cv1, w_cv2)
    dh = lambda a: a.reshape(B, S, N_HEADS_DIL, HEAD_DIM)
    o_b = dilated_attention(rope_partial(dh(q_b), positions), rope_partial(dh(k_b), positions), dh(v_b))
    o_b = o_b.reshape(B, S, Q_DIL)
    y = jnp.concatenate([rms_norm(o_a, g_out_nsa), rms_norm(o_b, g_out_dil)], axis=-1)
    return y @ w_o


def clamped_swiglu(u):
    x_glu, x_lin = u[..., ::2], u[..., 1::2]
    x_glu = jnp.minimum(x_glu, SWIGLU_LIMIT)
    x_lin = jnp.clip(x_lin, -SWIGLU_LIMIT, SWIGLU_LIMIT)
    return x_glu * jax.nn.sigmoid(SWIGLU_ALPHA * x_glu) * (x_lin + 1)


def moe(h, w_router, b_router, w_up, b_up, w_down, b_down):
    B, S, D = h.shape
    hf = h.reshape(B * S, D)
    logits = (hf @ w_router + b_router).astype(jnp.float32)
    top_val, top_idx = lax.top_k(logits, TOP_K)
    gate = jax.nn.softmax(top_val, axis=-1)
    flat_e = top_idx.reshape(-1)
    order = jnp.argsort(flat_e)
    e_sorted = flat_e[order]
    tok = order // TOP_K
    g_sorted = gate.reshape(-1)[order]
    group_sizes = jnp.zeros((N_EXPERTS,), jnp.int32).at[flat_e].add(1)
    xs = hf[tok]
    up = lax.ragged_dot(xs, w_up, group_sizes) + b_up[e_sorted]
    down = lax.ragged_dot(clamped_swiglu(up), w_down, group_sizes) + b_down[e_sorted]
    out = jnp.zeros_like(hf).at[tok].add(down * g_sorted[:, None].astype(down.dtype))
    return out.reshape(B, S, D)


def setup_inputs(seed: int = 0) -> dict:
    key = jax.random.key(seed)
    ks = jax.random.split(key, 26)
    f32 = jnp.float32
    L = DEPTH

    def nrm(k, shape, scale):
        return jax.random.normal(k, shape, f32) * scale

    def gain(k, shape):
        return 1.0 + 0.05 * jax.random.normal(k, shape, f32)

    start = jax.random.randint(ks[2], (BATCH, 1), 0, 2048, jnp.int32)
    return {
        "x": nrm(ks[0], (BATCH, SEQ, D_MODEL), 1.0),
        "c": nrm(ks[1], (BATCH, D_MODEL), 1.0),
        "positions": start + jnp.arange(SEQ, dtype=jnp.int32)[None, :],
        "w_ada": nrm(ks[3], (L, D_MODEL, 6 * D_MODEL), D_MODEL ** -0.5),
        "b_ada": nrm(ks[4], (L, 6 * D_MODEL), 0.02),
        "g_pre_mix": gain(ks[5], (L, D_MODEL)),
        "g_post_mix": gain(ks[6], (L, D_MODEL)),
        "g_pre_ffn": gain(ks[7], (L, D_MODEL)),
        "g_post_ffn": gain(ks[8], (L, D_MODEL)),
        "w_in": nrm(ks[9], (L, D_MODEL, IN_WIDTH), D_MODEL ** -0.5),
        "cmp_pos": nrm(ks[10], (L, CMP_BLOCK, HEAD_DIM), 0.1),
        "w_cmp_k1": nrm(ks[11], (L, CMP_BLOCK * HEAD_DIM, CMP_HIDDEN), (CMP_BLOCK * HEAD_DIM) ** -0.5),
        "w_cmp_k2": nrm(ks[12], (L, CMP_HIDDEN, HEAD_DIM), CMP_HIDDEN ** -0.5),
        "w_cmp_v1": nrm(ks[13], (L, CMP_BLOCK * HEAD_DIM, CMP_HIDDEN), (CMP_BLOCK * HEAD_DIM) ** -0.5),
        "w_cmp_v2": nrm(ks[14], (L, CMP_HIDDEN, HEAD_DIM), CMP_HIDDEN ** -0.5),
        "g_out_nsa": gain(ks[15], (L, Q_NSA)),
        "g_out_dil": gain(ks[16], (L, Q_DIL)),
        "w_o": nrm(ks[17], (L, MIX_WIDTH, D_MODEL), MIX_WIDTH ** -0.5),
        "w_router": nrm(ks[18], (L, D_MODEL, N_EXPERTS), D_MODEL ** -0.5),
        "b_router": nrm(ks[19], (L, N_EXPERTS), 0.01),
        "w_up": nrm(ks[20], (L, N_EXPERTS, D_MODEL, 2 * D_FF), D_MODEL ** -0.5),
        "b_up": nrm(ks[21], (L, N_EXPERTS, 2 * D_FF), 0.02),
        "w_down": nrm(ks[22], (L, N_EXPERTS, D_FF, D_MODEL), D_FF ** -0.5),
        "b_down": nrm(ks[23], (L, N_EXPERTS, D_MODEL), 0.02),
    }


def reference(x, c, positions, w_ada, b_ada, g_pre_mix, g_post_mix, g_pre_ffn, g_post_ffn,
              w_in, cmp_pos, w_cmp_k1, w_cmp_k2, w_cmp_v1, w_cmp_v2, g_out_nsa, g_out_dil, w_o,
              w_router, b_router, w_up, b_up, w_down, b_down):
    for l in range(DEPTH):
        mod = jax.nn.silu(c) @ w_ada[l] + b_ada[l]
        sh_m, sc_m, gt_m, sh_f, sc_f, gt_f = [m[:, None, :] for m in jnp.split(mod, 6, axis=-1)]
        h = rms_norm(x, g_pre_mix[l]) * (1 + sc_m) + sh_m
        y = mixing(h, positions, w_in[l], cmp_pos[l], w_cmp_k1[l], w_cmp_k2[l], w_cmp_v1[l], w_cmp_v2[l],
                   g_out_nsa[l], g_out_dil[l], w_o[l])
        x = x + gt_m * rms_norm(y, g_post_mix[l])
        h = rms_norm(x, g_pre_ffn[l]) * (1 + sc_f) + sh_f
        y = moe(h, w_router[l], b_router[l], w_up[l], b_up[l], w_down[l], b_down[l])
        x = x + gt_f * rms_norm(y, g_post_ffn[l])
    return x
```

```python
import functools

import numpy as np
import jax
import jax.numpy as jnp
from jax import lax
from jax.experimental import pallas as pl
from jax.experimental.pallas import tpu as pltpu

F32 = jnp.float32
BF16 = jnp.bfloat16
I32 = jnp.int32

D_MODEL = 1024
HEAD_DIM = 64
N_HEADS_NSA = 8
N_KV_NSA = 2
GROUP_NSA = 4
N_HEADS_DIL = 8
ROPE_THETA = 500000.0
ROPE_DIM = 16
ROPE_HALF = 8
CMP_BLOCK = 32
CMP_STRIDE = 16
CMP_HIDDEN = 256
SLC_BLOCK = 64
SLC_SHIFT = 6
SLC_TOPK = 16
WIN_NSA = 512
DIL_CONFIGS = ((128, 1), (512, 4), (2048, 16))
N_EXPERTS = 32
TOP_K = 4
D_FF = 1024
SWIGLU_LIMIT = 7.0
SWIGLU_ALPHA = 1.702
RMS_EPS = 1e-6
NEG_INF = -1e30
FORCE_SCORE = 1e9
SCALE = HEAD_DIM ** -0.5

Q_NSA = 512
KV_NSA = 128
Q_DIL = 512
LANES = 128
VMEM_LIMIT = 56 * 1024 * 1024

_NT = (((1,), (1,)), ((), ()))


def _cparams(sem):
    return pltpu.CompilerParams(dimension_semantics=sem, vmem_limit_bytes=VMEM_LIMIT)


def _rms(x, g):
    return x * lax.rsqrt(jnp.mean(x * x, axis=-1, keepdims=True) + RMS_EPS) * g


def _split3_dot(a, e):
    hi = a.astype(BF16)
    r1 = a - hi.astype(F32)
    mid = r1.astype(BF16)
    lo = (r1 - mid.astype(F32)).astype(BF16)
    return (jnp.dot(hi, e, preferred_element_type=F32) + jnp.dot(mid, e, preferred_element_type=F32)
            + jnp.dot(lo, e, preferred_element_type=F32))


def _ada_kernel(c_ref, w_ref, b_ref, o_ref):
    c = c_ref[...]
    a = c * jax.nn.sigmoid(c)
    o_ref[...] = jnp.dot(a, w_ref[...], preferred_element_type=F32,
                         precision=lax.Precision.HIGHEST) + b_ref[...]


def _ada(c, w_ada, b_ada):
    B, D = c.shape
    n = w_ada.shape[1] // D
    return pl.pallas_call(
        _ada_kernel,
        grid=(n,),
        in_specs=[pl.BlockSpec((B, D), lambda j: (0, 0)),
                  pl.BlockSpec((D, D), lambda j: (0, j)),
                  pl.BlockSpec((1, D), lambda j: (0, j))],
        out_specs=pl.BlockSpec((B, D), lambda j: (0, j)),
        out_shape=jax.ShapeDtypeStruct((B, n * D), F32),
        compiler_params=_cparams(("arbitrary",)),
        name="ada",
    )(c, w_ada, b_ada.reshape(1, -1))


def _rope_tables(pos_col, inv_row, s1_row, s2_row):
    ang = pos_col * inv_row
    cs = jnp.cos(ang)
    sn = jnp.sin(ang)
    return cs, sn * s1_row, sn * s2_row


def _rope_blk(x, cs, s1, s2):
    return x * cs + pltpu.roll(x, LANES - ROPE_HALF, 1) * s1 + pltpu.roll(x, ROPE_HALF, 1) * s2


def _proj_kernel(x_ref, mod_ref, g_ref, pos_ref, tab_ref, wa_ref, wg_ref, wb_ref,
                 qraw_ref, qrot_ref, kc_ref, vc_ref, ks_ref, vs_ref, kw_ref, vw_ref, gt_ref,
                 qb_ref, kb_ref, vb_ref):
    x = x_ref[0]
    sh = mod_ref[0, 0:1, :]
    sc = mod_ref[0, 1:2, :]
    h = (_rms(x, g_ref[...]) * (1.0 + sc) + sh).astype(BF16)
    cs, s1, s2 = _rope_tables(pos_ref[0], tab_ref[0:1, :], tab_ref[1:2, :], tab_ref[2:3, :])

    pa = jnp.dot(h, wa_ref[...], preferred_element_type=F32)
    for r in range(Q_NSA // LANES):
        blk = pa[:, r * LANES:(r + 1) * LANES]
        qraw_ref[0, :, r * LANES:(r + 1) * LANES] = (blk * SCALE).astype(BF16)
        qrot_ref[0, :, r * LANES:(r + 1) * LANES] = (_rope_blk(blk, cs, s1, s2) * SCALE).astype(BF16)
    o = Q_NSA
    kc_ref[0] = pa[:, o:o + LANES]
    vc_ref[0] = pa[:, o + LANES:o + 2 * LANES]
    ks_ref[0] = _rope_blk(pa[:, o + 2 * LANES:o + 3 * LANES], cs, s1, s2).astype(BF16)
    vs_ref[0] = pa[:, o + 3 * LANES:o + 4 * LANES].astype(BF16)
    kw_ref[0] = _rope_blk(pa[:, o + 4 * LANES:o + 5 * LANES], cs, s1, s2).astype(BF16)
    vw_ref[0] = pa[:, o + 5 * LANES:o + 6 * LANES].astype(BF16)

    gt_ref[0] = jnp.dot(h, wg_ref[...], preferred_element_type=F32)

    pb = jnp.dot(h, wb_ref[...], preferred_element_type=F32)
    for r in range(Q_DIL // LANES):
        qb_ref[0, :, r * LANES:(r + 1) * LANES] = (
            _rope_blk(pb[:, r * LANES:(r + 1) * LANES], cs, s1, s2) * SCALE).astype(BF16)
        kb_ref[0, :, r * LANES:(r + 1) * LANES] = _rope_blk(
            pb[:, Q_DIL + r * LANES:Q_DIL + (r + 1) * LANES], cs, s1, s2).astype(BF16)
    vb_ref[0] = pb[:, 2 * Q_DIL:].astype(BF16)


def _nsa_perm():
    cols = []
    for r in range(GROUP_NSA):
        for hk in range(N_KV_NSA):
            hq = hk * GROUP_NSA + r
            cols.extend(range(hq * HEAD_DIM, (hq + 1) * HEAD_DIM))
    return np.asarray(cols, np.int32)


def _rope_const_table():
    half = ROPE_HALF
    inv = ROPE_THETA ** (-jnp.arange(half, dtype=F32) / half)
    lane = np.arange(LANES) % HEAD_DIM
    inv_row = jnp.where(lane < ROPE_DIM, inv[lane % half], 0.0).astype(F32)
    s1 = np.where(lane < half, -1.0, 0.0).astype(np.float32)
    s2 = np.where((lane >= half) & (lane < ROPE_DIM), 1.0, 0.0).astype(np.float32)
    tab = jnp.zeros((8, LANES), F32).at[0].set(inv_row).at[1].set(s1).at[2].set(s2)
    return tab


def _proj(x, mod6, g_pre, positions, w_in, tm=512):
    B, S, D = x.shape
    perm = _nsa_perm()
    gate_lo = Q_NSA + 6 * KV_NSA
    n_gate = 3 * N_HEADS_NSA
    w_a = jnp.concatenate([w_in[:, :Q_NSA][:, perm], w_in[:, Q_NSA:gate_lo]], axis=1).astype(BF16)
    w_g = jnp.pad(w_in[:, gate_lo:gate_lo + n_gate], ((0, 0), (0, LANES - n_gate))).astype(BF16)
    w_b = w_in[:, gate_lo + n_gate:].astype(BF16)
    pos = positions.astype(F32).reshape(B, S, 1)
    tab = _rope_const_table()
    wa_n, wb_n = w_a.shape[1], w_b.shape[1]

    def tok(width, dtype):
        return (pl.BlockSpec((1, tm, width), lambda b, i: (b, i, 0)),
                jax.ShapeDtypeStruct((B, S, width), dtype))

    outs = [tok(Q_NSA, BF16), tok(Q_NSA, BF16), tok(LANES, F32), tok(LANES, F32),
            tok(LANES, BF16), tok(LANES, BF16), tok(LANES, BF16), tok(LANES, BF16), tok(LANES, F32),
            tok(Q_DIL, BF16), tok(Q_DIL, BF16), tok(Q_DIL, BF16)]
    return pl.pallas_call(
        _proj_kernel,
        grid=(B, S // tm),
        in_specs=[pl.BlockSpec((1, tm, D), lambda b, i: (b, i, 0)),
                  pl.BlockSpec((1, 6, D), lambda b, i: (b, 0, 0)),
                  pl.BlockSpec((1, D), lambda b, i: (0, 0)),
                  pl.BlockSpec((1, tm, 1), lambda b, i: (b, i, 0)),
                  pl.BlockSpec((8, LANES), lambda b, i: (0, 0)),
                  pl.BlockSpec((D, wa_n), lambda b, i: (0, 0)),
                  pl.BlockSpec((D, LANES), lambda b, i: (0, 0)),
                  pl.BlockSpec((D, wb_n), lambda b, i: (0, 0))],
        out_specs=[o[0] for o in outs],
        out_shape=[o[1] for o in outs],
        compiler_params=_cparams(("parallel", "parallel")),
        name="proj",
    )(x, mod6, g_pre.reshape(1, D), pos, tab, w_a, w_g, w_b)


def _cmpmlp_kernel(a_ref, p_ref, w1_ref, w2_ref, o_ref):
    a = a_ref[0, 0].astype(BF16)
    w1 = w1_ref[0]
    half = CMP_STRIDE * HEAD_DIM
    u = jnp.dot(a, w1[:half], preferred_element_type=F32)
    v = jnp.dot(a, w1[half:], preferred_element_type=F32)
    bias = jnp.dot(p_ref[...], w1, preferred_element_type=F32)[0:1]
    n16 = u.shape[0]
    hid = jax.nn.gelu(u + pltpu.roll(v, n16 - 1, 0) + bias)
    o_ref[0, 0] = jnp.dot(hid.astype(BF16), w2_ref[0], preferred_element_type=F32)


def _cmpmlp(kc, vc, cmp_pos, w_k1, w_k2, w_v1, w_v2):
    B, S, _ = kc.shape
    n16 = S // CMP_STRIDE
    seg = CMP_STRIDE * HEAD_DIM

    def segs(a):
        return a.reshape(B, n16, CMP_STRIDE, N_KV_NSA, HEAD_DIM).transpose(0, 3, 1, 2, 4).reshape(
            B * N_KV_NSA, n16, seg)

    a = jnp.stack([segs(kc), segs(vc)], axis=0)
    w1 = jnp.stack([w_k1, w_v1], axis=0).astype(BF16)
    w2 = jnp.stack([w_k2, w_v2], axis=0).astype(BF16)
    p8 = jnp.broadcast_to(cmp_pos.reshape(1, CMP_BLOCK * HEAD_DIM), (8, CMP_BLOCK * HEAD_DIM)).astype(BF16)
    out = pl.pallas_call(
        _cmpmlp_kernel,
        grid=(2, B * N_KV_NSA),
        in_specs=[pl.BlockSpec((1, 1, n16, seg), lambda t, g: (t, g, 0, 0)),
                  pl.BlockSpec((8, 2 * seg), lambda t, g: (0, 0)),
                  pl.BlockSpec((1, 2 * seg, CMP_HIDDEN), lambda t, g: (t, 0, 0)),
                  pl.BlockSpec((1, CMP_HIDDEN, HEAD_DIM), lambda t, g: (t, 0, 0))],
        out_specs=pl.BlockSpec((1, 1, n16, HEAD_DIM), lambda t, g: (t, g, 0, 0)),
        out_shape=jax.ShapeDtypeStruct((2, B * N_KV_NSA, n16, HEAD_DIM), F32),
        compiler_params=_cparams(("parallel", "parallel")),
        name="cmpmlp",
    )(a, p8, w1, w2)
    out = out.reshape(2, B, N_KV_NSA, n16, HEAD_DIM).transpose(0, 1, 3, 2, 4).reshape(2, B, n16, KV_NSA)
    return out[0].astype(BF16), out[1].astype(BF16)


def _stack_heads(q, tq):
    lane = lax.broadcasted_iota(I32, (tq, LANES), 1)
    half0 = lane < HEAD_DIM
    zero = jnp.zeros((tq, LANES), q.dtype)
    rows = []
    for hk in range(N_KV_NSA):
        keep = half0 if hk == 0 else jnp.logical_not(half0)
        for r in range(GROUP_NSA):
            rows.append(jnp.where(keep, q[:, r * LANES:(r + 1) * LANES], zero))
    return jnp.concatenate(rows, axis=0), half0


def _unstack_heads(o, half0, tq, o_ref):
    for r in range(GROUP_NSA):
        o0 = o[r * tq:(r + 1) * tq]
        o1 = o[(GROUP_NSA + r) * tq:(GROUP_NSA + r + 1) * tq]
        o_ref[0, :, r * LANES:(r + 1) * LANES] = jnp.where(half0, o0, o1)


def _cmpsel_kernel(q_ref, kc_ref, vc_ref, ovt_ref, o_ref, bias_ref, *, tq):
    i = pl.program_id(1)
    qs, half0 = _stack_heads(q_ref[0], tq)
    n16 = kc_ref.shape[1]
    rows = 8 * tq
    s = lax.dot_general(qs, kc_ref[0], _NT, preferred_element_type=F32)
    t = i * tq + (lax.broadcasted_iota(I32, (rows, n16), 0) & (tq - 1))
    c = lax.broadcasted_iota(I32, (rows, n16), 1)
    valid = (c * CMP_STRIDE + (CMP_BLOCK - 1)) <= t
    s = jnp.where(valid, s, NEG_INF)
    m = jnp.max(s, axis=-1, keepdims=True)
    e = jnp.exp(s - m)
    l = jnp.sum(e, axis=-1, keepdims=True)
    p = jnp.where(valid, e / l, 0.0)
    o = jnp.dot(p.astype(BF16), vc_ref[0], preferred_element_type=F32)
    _unstack_heads(o, half0, tq, o_ref)

    n_slc = ovt_ref.shape[0]
    j = lax.broadcasted_iota(I32, (n_slc, tq), 0)
    cur = (i * tq + lax.broadcasted_iota(I32, (n_slc, tq), 1)) >> SLC_SHIFT
    forced = (j == 0) | (j == cur) | (j == cur - 1)
    ovt = ovt_ref[...]
    biases = []
    for hk in range(N_KV_NSA):
        ps = p[hk * GROUP_NSA * tq:(hk * GROUP_NSA + 1) * tq]
        for r in range(1, GROUP_NSA):
            ps = ps + p[(hk * GROUP_NSA + r) * tq:(hk * GROUP_NSA + r + 1) * tq]
        hi = ps.astype(BF16)
        lo = (ps - hi.astype(F32)).astype(BF16)
        pslc = (lax.dot_general(ovt, hi, _NT, preferred_element_type=F32)
                + lax.dot_general(ovt, lo, _NT, preferred_element_type=F32))
        score = jnp.where(forced, FORCE_SCORE, jnp.where(j <= cur, pslc, -1.0))
        rank = jnp.zeros((n_slc, tq), I32)
        for ii in range(n_slc):
            ri = score[ii:ii + 1, :]
            beats = (ri > score) | ((ri == score) & (j > ii))
            rank = rank + beats.astype(I32)
        bias_t = jnp.where(rank < SLC_TOPK, 0.0, NEG_INF)
        biases.append(bias_t.T)
    bias_ref[0] = jnp.concatenate(biases, axis=1).astype(BF16)


def _overlap_t(S):
    n16 = S // CMP_STRIDE
    n_slc = S // SLC_BLOCK
    cs = np.arange(n16) * CMP_STRIDE
    js = np.arange(n_slc) * SLC_BLOCK
    ov = np.clip(np.minimum(cs[:, None] + CMP_BLOCK, js[None, :] + SLC_BLOCK)
                 - np.maximum(cs[:, None], js[None, :]), 0, None).astype(np.float32) / CMP_BLOCK
    ov[n16 - 1] = 0.0
    return jnp.asarray(ov.T, BF16)


def _cmpsel(q_raw, kcf, vcf, tq=128):
    B, S, _ = q_raw.shape
    n16 = S // CMP_STRIDE
    n_slc = S // SLC_BLOCK
    assert n_slc == HEAD_DIM, "selection bias is laid out as one 64-lane half per kv head"
    return pl.pallas_call(
        functools.partial(_cmpsel_kernel, tq=tq),
        grid=(B, S // tq),
        in_specs=[pl.BlockSpec((1, tq, Q_NSA), lambda b, i: (b, i, 0)),
                  pl.BlockSpec((1, n16, KV_NSA), lambda b, i: (b, 0, 0)),
                  pl.BlockSpec((1, n16, KV_NSA), lambda b, i: (b, 0, 0)),
                  pl.BlockSpec((n_slc, n16), lambda b, i: (0, 0))],
        out_specs=[pl.BlockSpec((1, tq, Q_NSA), lambda b, i: (b, i, 0)),
                   pl.BlockSpec((1, tq, LANES), lambda b, i: (b, i, 0))],
        out_shape=[jax.ShapeDtypeStruct((B, S, Q_NSA), F32),
                   jax.ShapeDtypeStruct((B, S, LANES), BF16)],
        compiler_params=_cparams(("parallel", "parallel")),
        name="cmpsel",
    )(q_raw, kcf, vcf, _overlap_t(S))


def _slc_kernel(q_ref, b_ref, k_ref, v_ref, o_ref, qa_sc, m_sc, l_sc, acc_sc, *, tq, tk):
    i = pl.program_id(1)
    kj = pl.program_id(2)
    nk = (i * tq + tq - 1) // tk + 1
    rows = 8 * tq

    @pl.when(kj == 0)
    def _():
        qs, half0 = _stack_heads(q_ref[0], tq)
        bias = b_ref[0]
        zero = jnp.zeros((tq, LANES), BF16)
        bs = []
        for hk in range(N_KV_NSA):
            keep = half0 if hk == 0 else jnp.logical_not(half0)
            bh = jnp.where(keep, bias, zero)
            bs.extend([bh] * GROUP_NSA)
        qa_sc[...] = jnp.concatenate([qs, jnp.concatenate(bs, axis=0)], axis=1)
        m_sc[...] = jnp.full((rows, 1), NEG_INF, F32)
        l_sc[...] = jnp.zeros((rows, 1), F32)
        acc_sc[...] = jnp.zeros((rows, LANES), F32)

    @pl.when(kj < nk)
    def _():
        kblk = (kj * tk + lax.broadcasted_iota(I32, (tk, LANES), 0)) >> SLC_SHIFT
        lane = lax.broadcasted_iota(I32, (tk, LANES), 1) & (HEAD_DIM - 1)
        onehot = jnp.where(kblk == lane, 1.0, 0.0).astype(BF16)
        kaug = jnp.concatenate([k_ref[0], onehot], axis=1)
        s = lax.dot_general(qa_sc[...], kaug, _NT, preferred_element_type=F32)
        t = i * tq + (lax.broadcasted_iota(I32, (rows, tk), 0) & (tq - 1))
        kpos = kj * tk + lax.broadcasted_iota(I32, (rows, tk), 1)
        s = jnp.where(kpos <= t, s, NEG_INF)
        m_old = m_sc[...]
        m_new = jnp.maximum(m_old, jnp.max(s, axis=-1, keepdims=True))
        alpha = jnp.exp(m_old - m_new)
        p = jnp.exp(s - m_new)
        l_sc[...] = alpha * l_sc[...] + jnp.sum(p, axis=-1, keepdims=True)
        acc_sc[...] = alpha * acc_sc[...] + jnp.dot(p.astype(BF16), v_ref[0], preferred_element_type=F32)
        m_sc[...] = m_new

    @pl.when(kj == nk - 1)
    def _():
        half0 = lax.broadcasted_iota(I32, (tq, LANES), 1) < HEAD_DIM
        _unstack_heads(acc_sc[...] / l_sc[...], half0, tq, o_ref)


def _slc(q_rot, bias, ks, vs, tq=128, tk=512):
    B, S, _ = q_rot.shape
    nkv = S // tk

    def kv_map(b, i, kj):
        return (b, jnp.minimum(kj, (i * tq + tq - 1) // tk), 0)

    return pl.pallas_call(
        functools.partial(_slc_kernel, tq=tq, tk=tk),
        grid=(B, S // tq, nkv),
        in_specs=[pl.BlockSpec((1, tq, Q_NSA), lambda b, i, kj: (b, i, 0)),
                  pl.BlockSpec((1, tq, LANES), lambda b, i, kj: (b, i, 0)),
                  pl.BlockSpec((1, tk, KV_NSA), kv_map),
                  pl.BlockSpec((1, tk, KV_NSA), kv_map)],
        out_specs=pl.BlockSpec((1, tq, Q_NSA), lambda b, i, kj: (b, i, 0)),
        out_shape=jax.ShapeDtypeStruct((B, S, Q_NSA), F32),
        scratch_shapes=[pltpu.VMEM((8 * tq, 2 * LANES), BF16),
                        pltpu.VMEM((8 * tq, 1), F32),
                        pltpu.VMEM((8 * tq, 1), F32),
                        pltpu.VMEM((8 * tq, LANES), F32)],
        compiler_params=_cparams(("parallel", "parallel", "arbitrary")),
        name="slc",
    )(q_rot, bias, ks, vs)


def _band_kernel(*refs, tq, tk, nt, max_dist, shared_kv, with_lse):
    q_ref = refs[0]
    k_refs = refs[1:1 + nt]
    v_refs = refs[1 + nt:1 + 2 * nt]
    o_ref = refs[1 + 2 * nt]
    lse_ref = refs[2 + 2 * nt] if with_lse else None
    i = pl.program_id(2)
    a = (i * tq) // tk
    lane = lax.broadcasted_iota(I32, (tq, LANES), 1)
    half0 = lane < HEAD_DIM
    row_t = i * tq + (lax.broadcasted_iota(I32, (2 * tq, tk), 0) & (tq - 1))
    col = lax.broadcasted_iota(I32, (2 * tq, tk), 1)
    oks = []
    for jt in range(nt):
        kpos = (a - (nt - 1) + jt) * tk + col
        d = row_t - kpos
        oks.append((d >= 0) & (d <= max_dist) & (kpos >= 0))
    lse_acc = jnp.zeros((tq, LANES), F32)
    for blk in range(Q_NSA // LANES):
        qb = q_ref[0, :, blk * LANES:(blk + 1) * LANES]
        zero = jnp.zeros_like(qb)
        qs = jnp.concatenate([jnp.where(half0, qb, zero), jnp.where(half0, zero, qb)], axis=0)
        kv0 = 0 if shared_kv else blk * LANES
        ss = []
        for jt in range(nt):
            kt = k_refs[jt][0, :, kv0:kv0 + LANES]
            s = lax.dot_general(qs, kt, _NT, preferred_element_type=F32)
            ss.append(jnp.where(oks[jt], s, NEG_INF))
        m = jnp.max(ss[0], axis=-1, keepdims=True)
        for s in ss[1:]:
            m = jnp.maximum(m, jnp.max(s, axis=-1, keepdims=True))
        l = jnp.zeros((2 * tq, 1), F32)
        o = jnp.zeros((2 * tq, LANES), F32)
        for jt in range(nt):
            p = jnp.exp(ss[jt] - m)
            l = l + jnp.sum(p, axis=-1, keepdims=True)
            o = o + jnp.dot(p.astype(BF16), v_refs[jt][0, :, kv0:kv0 + LANES], preferred_element_type=F32)
        o = o / l
        o_ref[0, :, blk * LANES:(blk + 1) * LANES] = jnp.where(half0, o[:tq], o[tq:])
        if with_lse:
            lse = m + jnp.log(l)
            lse_acc = jnp.where(lane == 2 * blk, lse[:tq], lse_acc)
            lse_acc = jnp.where(lane == 2 * blk + 1, lse[tq:], lse_acc)
    if with_lse:
        lse_ref[0] = lse_acc


def _band(q, k, v, *, dil, max_dist, tq, tk, nt, shared_kv, with_lse):
    B, S, _ = q.shape
    m = S // dil
    kvw = k.shape[-1]
    qv = q.reshape(B, m, dil * Q_NSA)
    kv_ = k.reshape(B, m, dil * kvw)
    vv = v.reshape(B, m, dil * kvw)

    def kv_map(jt):
        return lambda b, r, i: (b, jnp.maximum((i * tq) // tk - (nt - 1) + jt, 0), r)

    tok_spec = pl.BlockSpec((1, tq, Q_NSA), lambda b, r, i: (b, i, r))
    in_specs = [tok_spec]
    in_specs += [pl.BlockSpec((1, tk, kvw), kv_map(jt)) for jt in range(nt)]
    in_specs += [pl.BlockSpec((1, tk, kvw), kv_map(jt)) for jt in range(nt)]
    out_specs = [tok_spec]
    out_shape = [jax.ShapeDtypeStruct((B, m, dil * Q_NSA), F32)]
    if with_lse:
        out_specs.append(pl.BlockSpec((1, tq, LANES), lambda b, r, i: (b, i, r)))
        out_shape.append(jax.ShapeDtypeStruct((B, m, dil * LANES), F32))
    res = pl.pallas_call(
        functools.partial(_band_kernel, tq=tq, tk=tk, nt=nt, max_dist=max_dist, shared_kv=shared_kv,
                          with_lse=with_lse),
        grid=(B, dil, m // tq),
        in_specs=in_specs,
        out_specs=out_specs,
        out_shape=out_shape,
        compiler_params=_cparams(("parallel", "parallel", "parallel")),
        name=f"band_d{dil}_w{max_dist}",
    )(qv, *([kv_] * nt), *([vv] * nt))
    o = res[0].reshape(B, S, Q_NSA)
    if with_lse:
        return o, res[1].reshape(B, S, LANES)
    return o


def _out_kernel(ocmp_ref, oslc_ref, owin_ref, gt_ref, od1_ref, od4_ref, od16_ref, l1_ref, l4_ref, l16_ref,
                x_ref, mod_ref, eg_ref, ed_ref, gnsa_ref, gdil_ref, wo_ref, gpost_ref, o_ref):
    sg = jax.nn.sigmoid(gt_ref[0])
    oa = (_split3_dot(sg, eg_ref[0]) * ocmp_ref[0] + _split3_dot(sg, eg_ref[1]) * oslc_ref[0]
          + _split3_dot(sg, eg_ref[2]) * owin_ref[0])
    ya = _rms(oa, gnsa_ref[...])

    l1, l4, l16 = l1_ref[0], l4_ref[0], l16_ref[0]
    mx = jnp.maximum(jnp.maximum(l1, l4), l16)
    e1, e4, e16 = jnp.exp(l1 - mx), jnp.exp(l4 - mx), jnp.exp(l16 - mx)
    den = e1 + e4 + e16
    ed = ed_ref[...]
    ob = (_split3_dot(e1 / den, ed) * od1_ref[0] + _split3_dot(e4 / den, ed) * od4_ref[0]
          + _split3_dot(e16 / den, ed) * od16_ref[0])
    yb = _rms(ob, gdil_ref[...])

    y = jnp.concatenate([ya, yb], axis=1).astype(BF16)
    z = jnp.dot(y, wo_ref[...], preferred_element_type=F32)
    gt_m = mod_ref[0, 2:3, :]
    o_ref[0] = x_ref[0] + gt_m * _rms(z, gpost_ref[...])


def _gate_expanders():
    perm = _nsa_perm()
    eg = np.zeros((3, LANES, Q_NSA), np.float32)
    for lane_out, col in enumerate(perm):
        hq = col // HEAD_DIM
        for c in range(3):
            eg[c, hq * 3 + c, lane_out] = 1.0
    ed = np.zeros((LANES, Q_DIL), np.float32)
    for h in range(N_HEADS_DIL):
        ed[h, h * HEAD_DIM:(h + 1) * HEAD_DIM] = 1.0
    return jnp.asarray(eg, BF16), jnp.asarray(ed, BF16)


def _out(o_cmp, o_slc, o_win, gates, ods, lses, x, mod6, g_out_nsa, g_out_dil, w_o, g_post, tm=256):
    B, S, D = x.shape
    perm = _nsa_perm()
    eg, ed = _gate_expanders()
    w_o_p = jnp.concatenate([w_o[:Q_NSA][perm], w_o[Q_NSA:]], axis=0).astype(BF16)
    g_nsa_p = g_out_nsa[perm].reshape(1, Q_NSA)

    def tok(width):
        return pl.BlockSpec((1, tm, width), lambda b, i: (b, i, 0))

    def const(shape):
        return pl.BlockSpec(shape, lambda b, i: (0,) * len(shape))

    return pl.pallas_call(
        _out_kernel,
        grid=(B, S // tm),
        in_specs=[tok(Q_NSA), tok(Q_NSA), tok(Q_NSA), tok(LANES), tok(Q_DIL), tok(Q_DIL), tok(Q_DIL),
                  tok(LANES), tok(LANES), tok(LANES), tok(D),
                  pl.BlockSpec((1, 6, D), lambda b, i: (b, 0, 0)),
                  const((3, LANES, Q_NSA)), const((LANES, Q_DIL)), const((1, Q_NSA)), const((1, Q_DIL)),
                  const((D, D)), const((1, D))],
        out_specs=tok(D),
        out_shape=jax.ShapeDtypeStruct((B, S, D), F32),
        compiler_params=_cparams(("parallel", "parallel")),
        name="out",
    )(o_cmp, o_slc, o_win, gates, *ods, *lses, x, mod6, eg, ed, g_nsa_p, g_out_dil.reshape(1, Q_DIL),
      w_o_p, g_post.reshape(1, D))


def _route_kernel(x_ref, mod_ref, g_ref, wr_ref, br_ref, h_ref, idx_ref, gate_ref):
    sh = mod_ref[0, 3:4, :]
    sc = mod_ref[0, 4:5, :]
    h = _rms(x_ref[0], g_ref[...]) * (1.0 + sc) + sh
    h_ref[0] = h
    logits = jnp.dot(h, wr_ref[...], preferred_element_type=F32,
                     precision=lax.Precision.HIGHEST) + br_ref[...]
    tm = logits.shape[0]
    lane = lax.broadcasted_iota(I32, (tm, LANES), 1)
    work = jnp.where(lane < N_EXPERTS, logits, -jnp.inf)
    idx_out = jnp.zeros((tm, LANES), I32)
    val_out = jnp.full((tm, LANES), -jnp.inf, F32)
    for k in range(TOP_K):
        mx = jnp.max(work, axis=-1, keepdims=True)
        ix = jnp.min(jnp.where(work == mx, lane, LANES), axis=-1, keepdims=True)
        idx_out = jnp.where(lane == k, ix, idx_out)
        val_out = jnp.where(lane == k, mx, val_out)
        work = jnp.where(lane == ix, -jnp.inf, work)
    e = jnp.exp(val_out - val_out[:, 0:1])
    idx_ref[0] = idx_out
    gate_ref[0] = e / jnp.sum(e, axis=-1, keepdims=True)


def _route(x1, mod6, g_pre_ffn, w_router, b_router, tm=512):
    B, S, D = x1.shape
    wr = jnp.pad(w_router, ((0, 0), (0, LANES - N_EXPERTS)))
    br = jnp.pad(b_router, (0, LANES - N_EXPERTS)).reshape(1, LANES)

    def tok(width):
        return pl.BlockSpec((1, tm, width), lambda b, i: (b, i, 0))

    return pl.pallas_call(
        _route_kernel,
        grid=(B, S // tm),
        in_specs=[tok(D), pl.BlockSpec((1, 6, D), lambda b, i: (b, 0, 0)),
                  pl.BlockSpec((1, D), lambda b, i: (0, 0)),
                  pl.BlockSpec((D, LANES), lambda b, i: (0, 0)),
                  pl.BlockSpec((1, LANES), lambda b, i: (0, 0))],
        out_specs=[tok(D), tok(LANES), tok(LANES)],
        out_shape=[jax.ShapeDtypeStruct((B, S, D), F32),
                   jax.ShapeDtypeStruct((B, S, LANES), I32),
                   jax.ShapeDtypeStruct((B, S, LANES), F32)],
        compiler_params=_cparams(("parallel", "parallel")),
        name="route",
    )(x1, mod6, g_pre_ffn.reshape(1, D), wr, br)


def _gather_rows(idx_ref, base, src_hbm, dst, sem, n):
    def body(r, carry):
        pltpu.make_async_copy(src_hbm.at[pl.ds(idx_ref[base + r], 1), :], dst.at[pl.ds(r, 1), :], sem).start()
        return carry
    lax.fori_loop(0, n, body, 0)


def _wait_rows(src_hbm, dst, sem, n):
    pltpu.make_async_copy(src_hbm.at[pl.ds(0, n), :], dst, sem).wait()


def _moe_kernel(te_ref, nt_ref, src_ref, h_hbm, gs_ref, wg_ref, wl_ref, bg_ref, bl_ref, wd_ref, bd_ref,
                y_ref, xs_buf, sems, *, tm):
    i = pl.program_id(0)
    n_valid = nt_ref[0]
    slot = i & 1

    @pl.when(i == 0)
    def _():
        _gather_rows(src_ref, 0, h_hbm, xs_buf.at[0], sems.at[0], tm)

    @pl.when(i + 1 < n_valid)
    def _():
        _gather_rows(src_ref, (i + 1) * tm, h_hbm, xs_buf.at[1 - slot], sems.at[1 - slot], tm)

    @pl.when(i < n_valid)
    def _():
        _wait_rows(h_hbm, xs_buf.at[slot], sems.at[slot], tm)
        xs = xs_buf[slot].astype(BF16)
        ug = jnp.dot(xs, wg_ref[0], preferred_element_type=F32) + bg_ref[0]
        ul = jnp.dot(xs, wl_ref[0], preferred_element_type=F32) + bl_ref[0]
        ug = jnp.minimum(ug, SWIGLU_LIMIT)
        ul = jnp.clip(ul, -SWIGLU_LIMIT, SWIGLU_LIMIT)
        act = ug * jax.nn.sigmoid(SWIGLU_ALPHA * ug) * (ul + 1.0)
        down = jnp.dot(act.astype(BF16), wd_ref[0], preferred_element_type=F32) + bd_ref[0]
        y_ref[...] = down * gs_ref[...]

    @pl.when(i >= n_valid)
    def _():
        y_ref[...] = jnp.zeros_like(y_ref)


def _moe(h2, top_idx, gate, w_up, b_up, w_down, b_down, tm=256):
    N, D = h2.shape
    n_asg = N * TOP_K
    n_tiles = n_asg // tm + N_EXPERTS
    P = n_tiles * tm

    flat_e = top_idx.reshape(n_asg)
    onehot = (flat_e[:, None] == jnp.arange(N_EXPERTS, dtype=I32)[None, :]).astype(I32)
    csum = jnp.cumsum(onehot, axis=0)
    counts = csum[-1]
    rank = jnp.take_along_axis(csum, flat_e[:, None], axis=1)[:, 0] - 1
    tiles_e = (counts + tm - 1) // tm
    tile_end = jnp.cumsum(tiles_e)
    start = (tile_end - tiles_e) * tm
    pos = start[flat_e] + rank
    tok = jnp.arange(n_asg, dtype=I32) // TOP_K
    src_tok = jnp.zeros((P,), I32).at[pos].set(tok)
    gate_sorted = jnp.zeros((P,), F32).at[pos].set(gate.reshape(n_asg)).reshape(P, 1)
    n_valid = tile_end[-1].astype(I32)
    tile_ids = jnp.arange(n_tiles, dtype=I32)
    tile_e = jnp.minimum(jnp.searchsorted(tile_end, tile_ids, side="right"), N_EXPERTS - 1).astype(I32)
    last_e = tile_e[jnp.maximum(n_valid - 1, 0)]
    tile_e = jnp.where(tile_ids < n_valid, tile_e, last_e)

    w_glu = w_up[:, :, 0::2].astype(BF16)
    w_lin = w_up[:, :, 1::2].astype(BF16)
    b_glu = b_up[:, 0::2].reshape(N_EXPERTS, 1, D_FF)
    b_lin = b_up[:, 1::2].reshape(N_EXPERTS, 1, D_FF)
    w_dn = w_down.astype(BF16)
    b_dn = b_down.reshape(N_EXPERTS, 1, D)

    def wmap(i, te, nt, src):
        return (te[i], 0, 0)

    grid_spec = pltpu.PrefetchScalarGridSpec(
        num_scalar_prefetch=3,
        grid=(n_tiles,),
        in_specs=[pl.BlockSpec(memory_space=pl.ANY),
                  pl.BlockSpec((tm, 1), lambda i, te, nt, src: (i, 0)),
                  pl.BlockSpec((1, D, D_FF), wmap), pl.BlockSpec((1, D, D_FF), wmap),
                  pl.BlockSpec((1, 1, D_FF), wmap), pl.BlockSpec((1, 1, D_FF), wmap),
                  pl.BlockSpec((1, D_FF, D), wmap), pl.BlockSpec((1, 1, D), wmap)],
        out_specs=pl.BlockSpec((tm, D), lambda i, te, nt, src: (i, 0)),
        scratch_shapes=[pltpu.VMEM((2, tm, D), F32), pltpu.SemaphoreType.DMA((2,))],
    )
    ys = pl.pallas_call(
        functools.partial(_moe_kernel, tm=tm),
        grid_spec=grid_spec,
        out_shape=jax.ShapeDtypeStruct((P, D), F32),
        compiler_params=_cparams(("arbitrary",)),
        name="moe",
    )(tile_e, n_valid.reshape(1), src_tok, h2, gate_sorted, w_glu, w_lin, b_glu, b_lin, w_dn, b_dn)
    return ys, pos.reshape(N, TOP_K)


def _final_kernel(pos_ref, ys_hbm, x_ref, gtf_ref, g_ref, o_ref, buf, sems, *, tm, n_steps):
    i = pl.program_id(0)
    slot = i & 1
    n = tm * TOP_K

    @pl.when(i == 0)
    def _():
        _gather_rows(pos_ref, 0, ys_hbm, buf.at[0], sems.at[0], n)

    @pl.when(i + 1 < n_steps)
    def _():
        _gather_rows(pos_ref, (i + 1) * n, ys_hbm, buf.at[1 - slot], sems.at[1 - slot], n)

    _wait_rows(ys_hbm, buf.at[slot], sems.at[slot], n)
    g = buf[slot]
    y = g[0:tm]
    for k in range(1, TOP_K):
        y = y + g[k * tm:(k + 1) * tm]
    o_ref[...] = x_ref[...] + gtf_ref[0, 5:6, :] * _rms(y, g_ref[...])


def _final(ys, pos, x1, mod6, g_post_ffn, tm=128):
    B, S, D = x1.shape
    N = B * S
    n_steps = N // tm
    steps_per_b = S // tm
    pos_sm = pos.reshape(n_steps, tm, TOP_K).transpose(0, 2, 1).reshape(N * TOP_K)
    grid_spec = pltpu.PrefetchScalarGridSpec(
        num_scalar_prefetch=1,
        grid=(n_steps,),
        in_specs=[pl.BlockSpec(memory_space=pl.ANY),
                  pl.BlockSpec((tm, D), lambda i, p: (i, 0)),
                  pl.BlockSpec((1, 6, D), lambda i, p: (i // steps_per_b, 0, 0)),
                  pl.BlockSpec((1, D), lambda i, p: (0, 0))],
        out_specs=pl.BlockSpec((tm, D), lambda i, p: (i, 0)),
        scratch_shapes=[pltpu.VMEM((2, TOP_K * tm, D), F32), pltpu.SemaphoreType.DMA((2,))],
    )
    out = pl.pallas_call(
        functools.partial(_final_kernel, tm=tm, n_steps=n_steps),
        grid_spec=grid_spec,
        out_shape=jax.ShapeDtypeStruct((N, D), F32),
        compiler_params=_cparams(("arbitrary",)),
        name="final",
    )(pos_sm, ys, x1.reshape(N, D), mod6, g_post_ffn.reshape(1, D))
    return out.reshape(B, S, D)


def _layer(x, c, positions, w_ada, b_ada, g_pre_mix, g_post_mix, g_pre_ffn, g_post_ffn,
           w_in, cmp_pos, w_cmp_k1, w_cmp_k2, w_cmp_v1, w_cmp_v2, g_out_nsa, g_out_dil, w_o,
           w_router, b_router, w_up, b_up, w_down, b_down):
    B, S, D = x.shape
    mod6 = _ada(c, w_ada, b_ada).reshape(B, 6, D)
    (q_raw, q_rot, kc, vc, ks, vs, kw, vw, gates, qb, kb, vb) = _proj(x, mod6, g_pre_mix, positions, w_in)
    kcf, vcf = _cmpmlp(kc, vc, cmp_pos, w_cmp_k1, w_cmp_k2, w_cmp_v1, w_cmp_v2)
    o_cmp, bias = _cmpsel(q_raw, kcf, vcf)
    o_slc = _slc(q_rot, bias, ks, vs)
    o_win = _band(q_rot, kw, vw, dil=1, max_dist=WIN_NSA - 1, tq=128, tk=256, nt=3, shared_kv=True,
                  with_lse=False)
    ods, lses = [], []
    for window, dil in DIL_CONFIGS:
        o, lse = _band(qb, kb, vb, dil=dil, max_dist=window // dil, tq=128, tk=128, nt=2, shared_kv=False,
                       with_lse=True)
        ods.append(o)
        lses.append(lse)
    x1 = _out(o_cmp, o_slc, o_win, gates, ods, lses, x, mod6, g_out_nsa, g_out_dil, w_o, g_post_mix)
    h2, top_idx, gate = _route(x1, mod6, g_pre_ffn, w_router, b_router)
    ys, pos = _moe(h2.reshape(B * S, D), top_idx.reshape(B * S, LANES)[:, :TOP_K],
                   gate.reshape(B * S, LANES)[:, :TOP_K], w_up, b_up, w_down, b_down)
    return _final(ys, pos, x1, mod6, g_post_ffn)


def kernel(x, c, positions, w_ada, b_ada, g_pre_mix, g_post_mix, g_pre_ffn, g_post_ffn, w_in, cmp_pos,
           w_cmp_k1, w_cmp_k2, w_cmp_v1, w_cmp_v2, g_out_nsa, g_out_dil, w_o, w_router, b_router,
           w_up, b_up, w_down, b_down):
    depth = w_ada.shape[0]
    for l in range(depth):
        x = _layer(x, c, positions, w_ada[l], b_ada[l], g_pre_mix[l], g_post_mix[l], g_pre_ffn[l],
                   g_post_ffn[l], w_in[l], cmp_pos[l], w_cmp_k1[l], w_cmp_k2[l], w_cmp_v1[l], w_cmp_v2[l],
                   g_out_nsa[l], g_out_dil[l], w_o[l], w_router[l], b_router[l], w_up[l], b_up[l],
                   w_down[l], b_down[l])
    return x
```

```python
import functools

import numpy as np
import jax
import jax.numpy as jnp
from jax import lax
from jax.experimental import pallas as pl
from jax.experimental.pallas import tpu as pltpu

F32 = jnp.float32
BF16 = jnp.bfloat16
I32 = jnp.int32

D_MODEL = 1024
HEAD_DIM = 64
N_HEADS_NSA = 8
N_KV_NSA = 2
GROUP_NSA = 4
N_HEADS_DIL = 8
ROPE_THETA = 500000.0
ROPE_DIM = 16
ROPE_HALF = 8
CMP_BLOCK = 32
CMP_STRIDE = 16
CMP_HIDDEN = 256
SLC_BLOCK = 64
SLC_SHIFT = 6
SLC_TOPK = 16
WIN_NSA = 512
DIL_CONFIGS = ((128, 1), (512, 4), (2048, 16))
N_EXPERTS = 32
TOP_K = 4
D_FF = 1024
SWIGLU_LIMIT = 7.0
SWIGLU_ALPHA = 1.702
RMS_EPS = 1e-6
NEG_INF = -1e30
FORCE_SCORE = 1e9
SCALE = HEAD_DIM ** -0.5

Q_NSA = 512
KV_NSA = 128
Q_DIL = 512
LANES = 128
VMEM_LIMIT = 56 * 1024 * 1024

_NT = (((1,), (1,)), ((), ()))


def _cparams(sem):
    return pltpu.CompilerParams(dimension_semantics=sem, vmem_limit_bytes=VMEM_LIMIT)


def _rms(x, g):
    return x * lax.rsqrt(jnp.mean(x * x, axis=-1, keepdims=True) + RMS_EPS) * g


def _split3_dot(a, e):
    hi = a.astype(BF16)
    r1 = a - hi.astype(F32)
    mid = r1.astype(BF16)
    lo = (r1 - mid.astype(F32)).astype(BF16)
    return (jnp.dot(hi, e, preferred_element_type=F32) + jnp.dot(mid, e, preferred_element_type=F32)
            + jnp.dot(lo, e, preferred_element_type=F32))


def _ada_kernel(c_ref, w_ref, b_ref, o_ref):
    c = c_ref[...]
    a = c * jax.nn.sigmoid(c)
    o_ref[...] = jnp.dot(a, w_ref[...], preferred_element_type=F32,
                         precision=lax.Precision.HIGHEST) + b_ref[...]


def _ada(c, w_ada, b_ada):
    B, D = c.shape
    n = w_ada.shape[1] // D
    return pl.pallas_call(
        _ada_kernel,
        grid=(n,),
        in_specs=[pl.BlockSpec((B, D), lambda j: (0, 0)),
                  pl.BlockSpec((D, D), lambda j: (0, j)),
                  pl.BlockSpec((1, D), lambda j: (0, j))],
        out_specs=pl.BlockSpec((B, D), lambda j: (0, j)),
        out_shape=jax.ShapeDtypeStruct((B, n * D), F32),
        compiler_params=_cparams(("arbitrary",)),
        name="ada",
    )(c, w_ada, b_ada.reshape(1, -1))


def _rope_tables(pos_col, inv_row, s1_row, s2_row):
    ang = pos_col * inv_row
    cs = jnp.cos(ang)
    sn = jnp.sin(ang)
    return cs, sn * s1_row, sn * s2_row


def _rope_blk(x, cs, s1, s2):
    return x * cs + pltpu.roll(x, LANES - ROPE_HALF, 1) * s1 + pltpu.roll(x, ROPE_HALF, 1) * s2


def _proj_kernel(x_ref, mod_ref, g_ref, pos_ref, tab_ref, wa_ref, wg_ref, wb_ref,
                 qraw_ref, qrot_ref, kc_ref, vc_ref, ks_ref, vs_ref, kw_ref, vw_ref, gt_ref,
                 qb_ref, kb_ref, vb_ref):
    x = x_ref[0]
    sh = mod_ref[0, 0:1, :]
    sc = mod_ref[0, 1:2, :]
    h = (_rms(x, g_ref[...]) * (1.0 + sc) + sh).astype(BF16)
    cs, s1, s2 = _rope_tables(pos_ref[0], tab_ref[0:1, :], tab_ref[1:2, :], tab_ref[2:3, :])

    pa = jnp.dot(h, wa_ref[...], preferred_element_type=F32)
    for r in range(Q_NSA // LANES):
        blk = pa[:, r * LANES:(r + 1) * LANES]
        qraw_ref[0, :, r * LANES:(r + 1) * LANES] = (blk * SCALE).astype(BF16)
        qrot_ref[0, :, r * LANES:(r + 1) * LANES] = (_rope_blk(blk, cs, s1, s2) * SCALE).astype(BF16)
    o = Q_NSA
    kc_ref[0] = pa[:, o:o + LANES]
    vc_ref[0] = pa[:, o + LANES:o + 2 * LANES]
    ks_ref[0] = _rope_blk(pa[:, o + 2 * LANES:o + 3 * LANES], cs, s1, s2).astype(BF16)
    vs_ref[0] = pa[:, o + 3 * LANES:o + 4 * LANES].astype(BF16)
    kw_ref[0] = _rope_blk(pa[:, o + 4 * LANES:o + 5 * LANES], cs, s1, s2).astype(BF16)
    vw_ref[0] = pa[:, o + 5 * LANES:o + 6 * LANES].astype(BF16)

    gt_ref[0] = jnp.dot(h, wg_ref[...], preferred_element_type=F32)

    pb = jnp.dot(h, wb_ref[...], preferred_element_type=F32)
    for r in range(Q_DIL // LANES):
        qb_ref[0, :, r * LANES:(r + 1) * LANES] = (
            _rope_blk(pb[:, r * LANES:(r + 1) * LANES], cs, s1, s2) * SCALE).astype(BF16)
        kb_ref[0, :, r * LANES:(r + 1) * LANES] = _rope_blk(
            pb[:, Q_DIL + r * LANES:Q_DIL + (r + 1) * LANES], cs, s1, s2).astype(BF16)
    vb_ref[0] = pb[:, 2 * Q_DIL:].astype(BF16)


def _nsa_perm():
    cols = []
    for r in range(GROUP_NSA):
        for hk in range(N_KV_NSA):
            hq = hk * GROUP_NSA + r
            cols.extend(range(hq * HEAD_DIM, (hq + 1) * HEAD_DIM))
    return np.asarray(cols, np.int32)


def _rope_const_table():
    half = ROPE_HALF
    inv = ROPE_THETA ** (-jnp.arange(half, dtype=F32) / half)
    lane = np.arange(LANES) % HEAD_DIM
    inv_row = jnp.where(lane < ROPE_DIM, inv[lane % half], 0.0).astype(F32)
    s1 = np.where(lane < half, -1.0, 0.0).astype(np.float32)
    s2 = np.where((lane >= half) & (lane < ROPE_DIM), 1.0, 0.0).astype(np.float32)
    tab = jnp.zeros((8, LANES), F32).at[0].set(inv_row).at[1].set(s1).at[2].set(s2)
    return tab


def _proj(x, mod6, g_pre, positions, w_in, tm=512):
    B, S, D = x.shape
    perm = _nsa_perm()
    gate_lo = Q_NSA + 6 * KV_NSA
    n_gate = 3 * N_HEADS_NSA
    w_a = jnp.concatenate([w_in[:, :Q_NSA][:, perm], w_in[:, Q_NSA:gate_lo]], axis=1).astype(BF16)
    w_g = jnp.pad(w_in[:, gate_lo:gate_lo + n_gate], ((0, 0), (0, LANES - n_gate))).astype(BF16)
    w_b = w_in[:, gate_lo + n_gate:].astype(BF16)
    pos = positions.astype(F32).reshape(B, S, 1)
    tab = _rope_const_table()
    wa_n, wb_n = w_a.shape[1], w_b.shape[1]

    def tok(width, dtype):
        return (pl.BlockSpec((1, tm, width), lambda b, i: (b, i, 0)),
                jax.ShapeDtypeStruct((B, S, width), dtype))

    outs = [tok(Q_NSA, BF16), tok(Q_NSA, BF16), tok(LANES, F32), tok(LANES, F32),
            tok(LANES, BF16), tok(LANES, BF16), tok(LANES, BF16), tok(LANES, BF16), tok(LANES, F32),
            tok(Q_DIL, BF16), tok(Q_DIL, BF16), tok(Q_DIL, BF16)]
    return pl.pallas_call(
        _proj_kernel,
        grid=(B, S // tm),
        in_specs=[pl.BlockSpec((1, tm, D), lambda b, i: (b, i, 0)),
                  pl.BlockSpec((1, 6, D), lambda b, i: (b, 0, 0)),
                  pl.BlockSpec((1, D), lambda b, i: (0, 0)),
                  pl.BlockSpec((1, tm, 1), lambda b, i: (b, i, 0)),
                  pl.BlockSpec((8, LANES), lambda b, i: (0, 0)),
                  pl.BlockSpec((D, wa_n), lambda b, i: (0, 0)),
                  pl.BlockSpec((D, LANES), lambda b, i: (0, 0)),
                  pl.BlockSpec((D, wb_n), lambda b, i: (0, 0))],
        out_specs=[o[0] for o in outs],
        out_shape=[o[1] for o in outs],
        compiler_params=_cparams(("parallel", "parallel")),
        name="proj",
    )(x, mod6, g_pre.reshape(1, D), pos, tab, w_a, w_g, w_b)


def _cmpmlp_kernel(a_ref, p_ref, w1_ref, w2_ref, o_ref):
    a = a_ref[0, 0].astype(BF16)
    w1 = w1_ref[0]
    half = CMP_STRIDE * HEAD_DIM
    u = jnp.dot(a, w1[:half], preferred_element_type=F32)
    v = jnp.dot(a, w1[half:], preferred_element_type=F32)
    bias = jnp.dot(p_ref[...], w1, preferred_element_type=F32)[0:1]
    n16 = u.shape[0]
    hid = jax.nn.gelu(u + pltpu.roll(v, n16 - 1, 0) + bias)
    o_ref[0, 0] = jnp.dot(hid.astype(BF16), w2_ref[0], preferred_element_type=F32)


def _cmpmlp(kc, vc, cmp_pos, w_k1, w_k2, w_v1, w_v2):
    B, S, _ = kc.shape
    n16 = S // CMP_STRIDE
    seg = CMP_STRIDE * HEAD_DIM

    def segs(a):
        return a.reshape(B, n16, CMP_STRIDE, N_KV_NSA, HEAD_DIM).transpose(0, 3, 1, 2, 4).reshape(
            B * N_KV_NSA, n16, seg)

    a = jnp.stack([segs(kc), segs(vc)], axis=0)
    w1 = jnp.stack([w_k1, w_v1], axis=0).astype(BF16)
    w2 = jnp.stack([w_k2, w_v2], axis=0).astype(BF16)
    p8 = jnp.broadcast_to(cmp_pos.reshape(1, CMP_BLOCK * HEAD_DIM), (8, CMP_BLOCK * HEAD_DIM)).astype(BF16)
    out = pl.pallas_call(
        _cmpmlp_kernel,
        grid=(2, B * N_KV_NSA),
        in_specs=[pl.BlockSpec((1, 1, n16, seg), lambda t, g: (t, g, 0, 0)),
                  pl.BlockSpec((8, 2 * seg), lambda t, g: (0, 0)),
                  pl.BlockSpec((1, 2 * seg, CMP_HIDDEN), lambda t, g: (t, 0, 0)),
                  pl.BlockSpec((1, CMP_HIDDEN, HEAD_DIM), lambda t, g: (t, 0, 0))],
        out_specs=pl.BlockSpec((1, 1, n16, HEAD_DIM), lambda t, g: (t, g, 0, 0)),
        out_shape=jax.ShapeDtypeStruct((2, B * N_KV_NSA, n16, HEAD_DIM), F32),
        compiler_params=_cparams(("parallel", "parallel")),
        name="cmpmlp",
    )(a, p8, w1, w2)
    out = out.reshape(2, B, N_KV_NSA, n16, HEAD_DIM).transpose(0, 1, 3, 2, 4).reshape(2, B, n16, KV_NSA)
    return out[0].astype(BF16), out[1].astype(BF16)


def _stack_heads(q, tq):
    lane = lax.broadcasted_iota(I32, (tq, LANES), 1)
    half0 = lane < HEAD_DIM
    zero = jnp.zeros((tq, LANES), q.dtype)
    rows = []
    for hk in range(N_KV_NSA):
        keep = half0 if hk == 0 else jnp.logical_not(half0)
        for r in range(GROUP_NSA):
            rows.append(jnp.where(keep, q[:, r * LANES:(r + 1) * LANES], zero))
    return jnp.concatenate(rows, axis=0), half0


def _unstack_heads(o, half0, tq, o_ref):
    for r in range(GROUP_NSA):
        o0 = o[r * tq:(r + 1) * tq]
        o1 = o[(GROUP_NSA + r) * tq:(GROUP_NSA + r + 1) * tq]
        o_ref[0, :, r * LANES:(r + 1) * LANES] = jnp.where(half0, o0, o1)


def _cmpsel_kernel(q_ref, kc_ref, vc_ref, ovt_ref, o_ref, bias_ref, *, tq):
    i = pl.program_id(1)
    qs, half0 = _stack_heads(q_ref[0], tq)
    n16 = kc_ref.shape[1]
    rows = 8 * tq
    s = lax.dot_general(qs, kc_ref[0], _NT, preferred_element_type=F32)
    t = i * tq + (lax.broadcasted_iota(I32, (rows, n16), 0) & (tq - 1))
    c = lax.broadcasted_iota(I32, (rows, n16), 1)
    valid = (c * CMP_STRIDE + (CMP_BLOCK - 1)) <= t
    s = jnp.where(valid, s, NEG_INF)
    m = jnp.max(s, axis=-1, keepdims=True)
    e = jnp.exp(s - m)
    l = jnp.sum(e, axis=-1, keepdims=True)
    p = jnp.where(valid, e / l, 0.0)
    o = jnp.dot(p.astype(BF16), vc_ref[0], preferred_element_type=F32)
    _unstack_heads(o, half0, tq, o_ref)

    n_slc = ovt_ref.shape[0]
    j = lax.broadcasted_iota(I32, (n_slc, tq), 0)
    cur = (i * tq + lax.broadcasted_iota(I32, (n_slc, tq), 1)) >> SLC_SHIFT
    forced = (j == 0) | (j == cur) | (j == cur - 1)
    ovt = ovt_ref[...]
    biases = []
    for hk in range(N_KV_NSA):
        ps = p[hk * GROUP_NSA * tq:(hk * GROUP_NSA + 1) * tq]
        for r in range(1, GROUP_NSA):
            ps = ps + p[(hk * GROUP_NSA + r) * tq:(hk * GROUP_NSA + r + 1) * tq]
        hi = ps.astype(BF16)
        lo = (ps - hi.astype(F32)).astype(BF16)
        pslc = (lax.dot_general(ovt, hi, _NT, preferred_element_type=F32)
                + lax.dot_general(ovt, lo, _NT, preferred_element_type=F32))
        score = jnp.where(forced, FORCE_SCORE, jnp.where(j <= cur, pslc, -1.0))
        rank = jnp.zeros((n_slc, tq), I32)
        for ii in range(n_slc):
            ri = score[ii:ii + 1, :]
            beats = (ri > score) | ((ri == score) & (j > ii))
            rank = rank + beats.astype(I32)
        bias_t = jnp.where(rank < SLC_TOPK, 0.0, NEG_INF)
        biases.append(bias_t.T)
    bias_ref[0] = jnp.concatenate(biases, axis=1).astype(BF16)


def _overlap_t(S):
    n16 = S // CMP_STRIDE
    n_slc = S // SLC_BLOCK
    cs = np.arange(n16) * CMP_STRIDE
    js = np.arange(n_slc) * SLC_BLOCK
    ov = np.clip(np.minimum(cs[:, None] + CMP_BLOCK, js[None, :] + SLC_BLOCK)
                 - np.maximum(cs[:, None], js[None, :]), 0, None).astype(np.float32) / CMP_BLOCK
    ov[n16 - 1] = 0.0
    return jnp.asarray(ov.T, BF16)


def _cmpsel(q_raw, kcf, vcf, tq=128):
    B, S, _ = q_raw.shape
    n16 = S // CMP_STRIDE
    n_slc = S // SLC_BLOCK
    assert n_slc == HEAD_DIM, "selection bias is laid out as one 64-lane half per kv head"
    return pl.pallas_call(
        functools.partial(_cmpsel_kernel, tq=tq),
        grid=(B, S // tq),
        in_specs=[pl.BlockSpec((1, tq, Q_NSA), lambda b, i: (b, i, 0)),
                  pl.BlockSpec((1, n16, KV_NSA), lambda b, i: (b, 0, 0)),
                  pl.BlockSpec((1, n16, KV_NSA), lambda b, i: (b, 0, 0)),
                  pl.BlockSpec((n_slc, n16), lambda b, i: (0, 0))],
        out_specs=[pl.BlockSpec((1, tq, Q_NSA), lambda b, i: (b, i, 0)),
                   pl.BlockSpec((1, tq, LANES), lambda b, i: (b, i, 0))],
        out_shape=[jax.ShapeDtypeStruct((B, S, Q_NSA), F32),
                   jax.ShapeDtypeStruct((B, S, LANES), BF16)],
        compiler_params=_cparams(("parallel", "parallel")),
        name="cmpsel",
    )(q_raw, kcf, vcf, _overlap_t(S))


def _slc_kernel(q_ref, b_ref, k_ref, v_ref, o_ref, qa_sc, m_sc, l_sc, acc_sc, *, tq, tk):
    i = pl.program_id(1)
    kj = pl.program_id(2)
    nk = (i * tq + tq - 1) // tk + 1
    rows = 8 * tq

    @pl.when(kj == 0)
    def _():
        qs, half0 = _stack_heads(q_ref[0], tq)
        bias = b_ref[0]
        zero = jnp.zeros((tq, LANES), BF16)
        bs = []
        for hk in range(N_KV_NSA):
            keep = half0 if hk == 0 else jnp.logical_not(half0)
            bh = jnp.where(keep, bias, zero)
            bs.extend([bh] * GROUP_NSA)
        qa_sc[...] = jnp.concatenate([qs, jnp.concatenate(bs, axis=0)], axis=1)
        m_sc[...] = jnp.full((rows, 1), NEG_INF, F32)
        l_sc[...] = jnp.zeros((rows, 1), F32)
        acc_sc[...] = jnp.zeros((rows, LANES), F32)

    @pl.when(kj < nk)
    def _():
        kblk = (kj * tk + lax.broadcasted_iota(I32, (tk, LANES), 0)) >> SLC_SHIFT
        lane = lax.broadcasted_iota(I32, (tk, LANES), 1) & (HEAD_DIM - 1)
        onehot = jnp.where(kblk == lane, 1.0, 0.0).astype(BF16)
        kaug = jnp.concatenate([k_ref[0], onehot], axis=1)
        s = lax.dot_general(qa_sc[...], kaug, _NT, preferred_element_type=F32)
        t = i * tq + (lax.broadcasted_iota(I32, (rows, tk), 0) & (tq - 1))
        kpos = kj * tk + lax.broadcasted_iota(I32, (rows, tk), 1)
        s = jnp.where(kpos <= t, s, NEG_INF)
        m_old = m_sc[...]
        m_new = jnp.maximum(m_old, jnp.max(s, axis=-1, keepdims=True))
        alpha = jnp.exp(m_old - m_new)
        p = jnp.exp(s - m_new)
        l_sc[...] = alpha * l_sc[...] + jnp.sum(p, axis=-1, keepdims=True)
        acc_sc[...] = alpha * acc_sc[...] + jnp.dot(p.astype(BF16), v_ref[0], preferred_element_type=F32)
        m_sc[...] = m_new

    @pl.when(kj == nk - 1)
    def _():
        half0 = lax.broadcasted_iota(I32, (tq, LANES), 1) < HEAD_DIM
        _unstack_heads(acc_sc[...] / l_sc[...], half0, tq, o_ref)


def _slc(q_rot, bias, ks, vs, tq=128, tk=512):
    B, S, _ = q_rot.shape
    nkv = S // tk

    def kv_map(b, i, kj):
        return (b, jnp.minimum(kj, (i * tq + tq - 1) // tk), 0)

    return pl.pallas_call(
        functools.partial(_slc_kernel, tq=tq, tk=tk),
        grid=(B, S // tq, nkv),
        in_specs=[pl.BlockSpec((1, tq, Q_NSA), lambda b, i, kj: (b, i, 0)),
                  pl.BlockSpec((1, tq, LANES), lambda b, i, kj: (b, i, 0)),
                  pl.BlockSpec((1, tk, KV_NSA), kv_map),
                  pl.BlockSpec((1, tk, KV_NSA), kv_map)],
        out_specs=pl.BlockSpec((1, tq, Q_NSA), lambda b, i, kj: (b, i, 0)),
        out_shape=jax.ShapeDtypeStruct((B, S, Q_NSA), F32),
        scratch_shapes=[pltpu.VMEM((8 * tq, 2 * LANES), BF16),
                        pltpu.VMEM((8 * tq, 1), F32),
                        pltpu.VMEM((8 * tq, 1), F32),
                        pltpu.VMEM((8 * tq, LANES), F32)],
        compiler_params=_cparams(("parallel", "parallel", "arbitrary")),
        name="slc",
    )(q_rot, bias, ks, vs)


def _band_kernel(*refs, tq, tk, nt, max_dist, shared_kv, with_lse):
    q_ref = refs[0]
    k_refs = refs[1:1 + nt]
    v_refs = refs[1 + nt:1 + 2 * nt]
    o_ref = refs[1 + 2 * nt]
    lse_ref = refs[2 + 2 * nt] if with_lse else None
    i = pl.program_id(2)
    a = (i * tq) // tk
    lane = lax.broadcasted_iota(I32, (tq, LANES), 1)
    half0 = lane < HEAD_DIM
    row_t = i * tq + (lax.broadcasted_iota(I32, (2 * tq, tk), 0) & (tq - 1))
    col = lax.broadcasted_iota(I32, (2 * tq, tk), 1)
    oks = []
    for jt in range(nt):
        kpos = (a - (nt - 1) + jt) * tk + col
        d = row_t - kpos
        oks.append((d >= 0) & (d <= max_dist) & (kpos >= 0))
    lse_acc = jnp.zeros((tq, LANES), F32)
    for blk in range(Q_NSA // LANES):
        qb = q_ref[0, :, blk * LANES:(blk + 1) * LANES]
        zero = jnp.zeros_like(qb)
        qs = jnp.concatenate([jnp.where(half0, qb, zero), jnp.where(half0, zero, qb)], axis=0)
        kv0 = 0 if shared_kv else blk * LANES
        ss = []
        for jt in range(nt):
            kt = k_refs[jt][0, :, kv0:kv0 + LANES]
            s = lax.dot_general(qs, kt, _NT, preferred_element_type=F32)
            ss.append(jnp.where(oks[jt], s, NEG_INF))
        m = jnp.max(ss[0], axis=-1, keepdims=True)
        for s in ss[1:]:
            m = jnp.maximum(m, jnp.max(s, axis=-1, keepdims=True))
        l = jnp.zeros((2 * tq, 1), F32)
        o = jnp.zeros((2 * tq, LANES), F32)
        for jt in range(nt):
            p = jnp.exp(ss[jt] - m)
            l = l + jnp.sum(p, axis=-1, keepdims=True)
            o = o + jnp.dot(p.astype(BF16), v_refs[jt][0, :, kv0:kv0 + LANES], preferred_element_type=F32)
        o = o / l
        o_ref[0, :, blk * LANES:(blk + 1) * LANES] = jnp.where(half0, o[:tq], o[tq:])
        if with_lse:
            lse = m + jnp.log(l)
            lse_acc = jnp.where(lane == 2 * blk, lse[:tq], lse_acc)
            lse_acc = jnp.where(lane == 2 * blk + 1, lse[tq:], lse_acc)
    if with_lse:
        lse_ref[0] = lse_acc


def _band(q, k, v, *, dil, max_dist, tq, tk, nt, shared_kv, with_lse):
    B, S, _ = q.shape
    m = S // dil
    kvw = k.shape[-1]
    qv = q.reshape(B, m, dil * Q_NSA)
    kv_ = k.reshape(B, m, dil * kvw)
    vv = v.reshape(B, m, dil * kvw)

    def kv_map(jt):
        return lambda b, r, i: (b, jnp.maximum((i * tq) // tk - (nt - 1) + jt, 0), r)

    tok_spec = pl.BlockSpec((1, tq, Q_NSA), lambda b, r, i: (b, i, r))
    in_specs = [tok_spec]
    in_specs += [pl.BlockSpec((1, tk, kvw), kv_map(jt)) for jt in range(nt)]
    in_specs += [pl.BlockSpec((1, tk, kvw), kv_map(jt)) for jt in range(nt)]
    out_specs = [tok_spec]
    out_shape = [jax.ShapeDtypeStruct((B, m, dil * Q_NSA), F32)]
    if with_lse:
        out_specs.append(pl.BlockSpec((1, tq, LANES), lambda b, r, i: (b, i, r)))
        out_shape.append(jax.ShapeDtypeStruct((B, m, dil * LANES), F32))
    res = pl.pallas_call(
        functools.partial(_band_kernel, tq=tq, tk=tk, nt=nt, max_dist=max_dist, shared_kv=shared_kv,
                          with_lse=with_lse),
        grid=(B, dil, m // tq),
        in_specs=in_specs,
        out_specs=out_specs,
        out_shape=out_shape,
        compiler_params=_cparams(("parallel", "parallel", "parallel")),
        name=f"band_d{dil}_w{max_dist}",
    )(qv, *([kv_] * nt), *([vv] * nt))
    o = res[0].reshape(B, S, Q_NSA)
    if with_lse:
        return o, res[1].reshape(B, S, LANES)
    return o


def _out_kernel(ocmp_ref, oslc_ref, owin_ref, gt_ref, od1_ref, od4_ref, od16_ref, l1_ref, l4_ref, l16_ref,
                x_ref, mod_ref, eg_ref, ed_ref, gnsa_ref, gdil_ref, wo_ref, gpost_ref, o_ref):
    sg = jax.nn.sigmoid(gt_ref[0])
    oa = (_split3_dot(sg, eg_ref[0]) * ocmp_ref[0] + _split3_dot(sg, eg_ref[1]) * oslc_ref[0]
          + _split3_dot(sg, eg_ref[2]) * owin_ref[0])
    ya = _rms(oa, gnsa_ref[...])

    l1, l4, l16 = l1_ref[0], l4_ref[0], l16_ref[0]
    mx = jnp.maximum(jnp.maximum(l1, l4), l16)
    e1, e4, e16 = jnp.exp(l1 - mx), jnp.exp(l4 - mx), jnp.exp(l16 - mx)
    den = e1 + e4 + e16
    ed = ed_ref[...]
    ob = (_split3_dot(e1 / den, ed) * od1_ref[0] + _split3_dot(e4 / den, ed) * od4_ref[0]
          + _split3_dot(e16 / den, ed) * od16_ref[0])
    yb = _rms(ob, gdil_ref[...])

    y = jnp.concatenate([ya, yb], axis=1).astype(BF16)
    z = jnp.dot(y, wo_ref[...], preferred_element_type=F32)
    gt_m = mod_ref[0, 2:3, :]
    o_ref[0] = x_ref[0] + gt_m * _rms(z, gpost_ref[...])


def _gate_expanders():
    perm = _nsa_perm()
    eg = np.zeros((3, LANES, Q_NSA), np.float32)
    for lane_out, col in enumerate(perm):
        hq = col // HEAD_DIM
        for c in range(3):
            eg[c, hq * 3 + c, lane_out] = 1.0
    ed = np.zeros((LANES, Q_DIL), np.float32)
    for h in range(N_HEADS_DIL):
        ed[h, h * HEAD_DIM:(h + 1) * HEAD_DIM] = 1.0
    return jnp.asarray(eg, BF16), jnp.asarray(ed, BF16)


def _out(o_cmp, o_slc, o_win, gates, ods, lses, x, mod6, g_out_nsa, g_out_dil, w_o, g_post, tm=256):
    B, S, D = x.shape
    perm = _nsa_perm()
    eg, ed = _gate_expanders()
    w_o_p = jnp.concatenate([w_o[:Q_NSA][perm], w_o[Q_NSA:]], axis=0).astype(BF16)
    g_nsa_p = g_out_nsa[perm].reshape(1, Q_NSA)

    def tok(width):
        return pl.BlockSpec((1, tm, width), lambda b, i: (b, i, 0))

    def const(shape):
        return pl.BlockSpec(shape, lambda b, i: (0,) * len(shape))

    return pl.pallas_call(
        _out_kernel,
        grid=(B, S // tm),
        in_specs=[tok(Q_NSA), tok(Q_NSA), tok(Q_NSA), tok(LANES), tok(Q_DIL), tok(Q_DIL), tok(Q_DIL),
                  tok(LANES), tok(LANES), tok(LANES), tok(D),
                  pl.BlockSpec((1, 6, D), lambda b, i: (b, 0, 0)),
                  const((3, LANES, Q_NSA)), const((LANES, Q_DIL)), const((1, Q_NSA)), const((1, Q_DIL)),
                  const((D, D)), const((1, D))],
        out_specs=tok(D),
        out_shape=jax.ShapeDtypeStruct((B, S, D), F32),
        compiler_params=_cparams(("parallel", "parallel")),
        name="out",
    )(o_cmp, o_slc, o_win, gates, *ods, *lses, x, mod6, eg, ed, g_nsa_p, g_out_dil.reshape(1, Q_DIL),
      w_o_p, g_post.reshape(1, D))


def _route_kernel(x_ref, mod_ref, g_ref, wr_ref, br_ref, h_ref, idx_ref, gate_ref, rank_ref, cnt_ref, run_sc):
    i = pl.program_id(0)

    @pl.when(i == 0)
    def _():
        run_sc[...] = jnp.zeros_like(run_sc)

    sh = mod_ref[0, 3:4, :]
    sc = mod_ref[0, 4:5, :]
    h = _rms(x_ref[...], g_ref[...]) * (1.0 + sc) + sh
    h_ref[...] = h
    logits = jnp.dot(h, wr_ref[...], preferred_element_type=F32,
                     precision=lax.Precision.HIGHEST) + br_ref[...]
    tm = logits.shape[0]
    lane = lax.broadcasted_iota(I32, (tm, LANES), 1)
    work = jnp.where(lane < N_EXPERTS, logits, -jnp.inf)
    idx_out = jnp.zeros((tm, LANES), I32)
    val_out = jnp.full((tm, LANES), -jnp.inf, F32)
    rank_out = jnp.zeros((tm, LANES), F32)
    tri = jnp.where(lax.broadcasted_iota(I32, (tm, tm), 1) < lax.broadcasted_iota(I32, (tm, tm), 0),
                    1.0, 0.0).astype(BF16)
    base = run_sc[0:1, :]
    for k in range(TOP_K):
        mx = jnp.max(work, axis=-1, keepdims=True)
        ix = jnp.min(jnp.where(work == mx, lane, LANES), axis=-1, keepdims=True)
        hit = lane == ix
        idx_out = jnp.where(lane == k, ix, idx_out)
        val_out = jnp.where(lane == k, mx, val_out)
        work = jnp.where(hit, -jnp.inf, work)
        onehot = jnp.where(hit, 1.0, 0.0)
        before = jnp.dot(tri, onehot.astype(BF16), preferred_element_type=F32) + base
        rank_k = jnp.sum(onehot * before, axis=-1, keepdims=True)
        rank_out = jnp.where(lane == k, rank_k, rank_out)
        base = base + jnp.sum(onehot, axis=0, keepdims=True)
    run_sc[...] = jnp.broadcast_to(base, run_sc.shape)
    e = jnp.exp(val_out - val_out[:, 0:1])
    idx_ref[...] = idx_out
    gate_ref[...] = e / jnp.sum(e, axis=-1, keepdims=True)
    rank_ref[...] = rank_out.astype(I32)
    cnt_ref[...] = run_sc[...].astype(I32)


def _route(x1, mod6, g_pre_ffn, w_router, b_router, tm=512):
    B, S, D = x1.shape
    N = B * S
    spb = S // tm
    wr = jnp.pad(w_router, ((0, 0), (0, LANES - N_EXPERTS)))
    br = jnp.pad(b_router, (0, LANES - N_EXPERTS)).reshape(1, LANES)

    def tok(width):
        return pl.BlockSpec((tm, width), lambda i: (i, 0))

    return pl.pallas_call(
        _route_kernel,
        grid=(N // tm,),
        in_specs=[tok(D), pl.BlockSpec((1, 6, D), lambda i: (i // spb, 0, 0)),
                  pl.BlockSpec((1, D), lambda i: (0, 0)),
                  pl.BlockSpec((D, LANES), lambda i: (0, 0)),
                  pl.BlockSpec((1, LANES), lambda i: (0, 0))],
        out_specs=[tok(D), tok(LANES), tok(LANES), tok(LANES), pl.BlockSpec((8, LANES), lambda i: (0, 0))],
        out_shape=[jax.ShapeDtypeStruct((N, D), F32),
                   jax.ShapeDtypeStruct((N, LANES), I32),
                   jax.ShapeDtypeStruct((N, LANES), F32),
                   jax.ShapeDtypeStruct((N, LANES), I32),
                   jax.ShapeDtypeStruct((8, LANES), I32)],
        scratch_shapes=[pltpu.VMEM((8, LANES), F32)],
        compiler_params=_cparams(("arbitrary",)),
        name="route",
    )(x1.reshape(N, D), mod6, g_pre_ffn.reshape(1, D), wr, br)


def _dispatch_kernel(pos_ref, h_ref, xs_in, xs_out, sem, *, tm):
    del xs_in
    i = pl.program_id(0)

    def body(r, carry):
        for k in range(TOP_K):
            dst_row = pos_ref[(i * tm + r) * TOP_K + k]
            pltpu.make_async_copy(h_ref.at[pl.ds(r, 1), :], xs_out.at[pl.ds(dst_row, 1), :], sem).start()
        return carry

    lax.fori_loop(0, tm, body, 0)
    for k in range(TOP_K):
        pltpu.make_async_copy(h_ref, xs_out.at[pl.ds(0, tm), :], sem).wait()


def _dispatch(h2, pos_flat, P, tm=256):
    N, D = h2.shape
    grid_spec = pltpu.PrefetchScalarGridSpec(
        num_scalar_prefetch=1,
        grid=(N // tm,),
        in_specs=[pl.BlockSpec((tm, D), lambda i, p: (i, 0)),
                  pl.BlockSpec(memory_space=pl.ANY)],
        out_specs=pl.BlockSpec(memory_space=pl.ANY),
        scratch_shapes=[pltpu.SemaphoreType.DMA(())],
    )
    return pl.pallas_call(
        functools.partial(_dispatch_kernel, tm=tm),
        grid_spec=grid_spec,
        out_shape=jax.ShapeDtypeStruct((P, D), F32),
        input_output_aliases={2: 0},
        compiler_params=_cparams(("arbitrary",)),
        name="dispatch",
    )(pos_flat, h2, jnp.zeros((P, D), F32))


def _moe_kernel(te_ref, nt_ref, xs_ref, wup_ref, bup_ref, wdn_ref, bdn_ref, pm_ref, y_ref, wup_sc, wdn_sc):
    i = pl.program_id(0)
    n_valid = nt_ref[0]
    new_expert = (i == 0) | (te_ref[i] != te_ref[jnp.maximum(i - 1, 0)])
    n_blk = wup_sc.shape[1] // (2 * LANES)

    @pl.when(new_expert & (i < n_valid))
    def _():
        pm = pm_ref[...]
        for blk in range(n_blk):
            cols = slice(blk * 2 * LANES, (blk + 1) * 2 * LANES)
            w = wup_ref[0, :, cols].astype(BF16)
            wup_sc[:, cols] = jnp.dot(w, pm, preferred_element_type=F32).astype(BF16)
        wdn_sc[...] = wdn_ref[0].astype(BF16)

    @pl.when(i < n_valid)
    def _():
        xs = xs_ref[...].astype(BF16)
        u = jnp.dot(xs, wup_sc[...], preferred_element_type=F32) + bup_ref[0]
        acts = []
        for blk in range(n_blk):
            ug = jnp.minimum(u[:, blk * 2 * LANES:blk * 2 * LANES + LANES], SWIGLU_LIMIT)
            ul = jnp.clip(u[:, blk * 2 * LANES + LANES:(blk + 1) * 2 * LANES], -SWIGLU_LIMIT, SWIGLU_LIMIT)
            acts.append((ug * jax.nn.sigmoid(SWIGLU_ALPHA * ug) * (ul + 1.0)).astype(BF16))
        act = jnp.concatenate(acts, axis=1)
        y_ref[...] = jnp.dot(act, wdn_sc[...], preferred_element_type=F32) + bdn_ref[0]

    @pl.when(i >= n_valid)
    def _():
        y_ref[...] = jnp.zeros_like(y_ref)


def _moe_layout(top_idx, rank, counts, tm):
    N = top_idx.shape[0]
    n_tiles = N * TOP_K // tm + N_EXPERTS
    e_ids = jnp.arange(N_EXPERTS, dtype=I32)
    tiles_e = (counts + tm - 1) // tm
    tile_end = jnp.sum(jnp.where(e_ids[None, :] <= e_ids[:, None], tiles_e[None, :], 0), axis=1)
    start = (tile_end - tiles_e) * tm
    pos = rank + jnp.sum(jnp.where(top_idx[:, :, None] == e_ids[None, None, :], start[None, None, :], 0), axis=-1)
    n_valid = tile_end[N_EXPERTS - 1]
    tile_ids = jnp.arange(n_tiles, dtype=I32)
    tile_e = jnp.sum((jnp.minimum(tile_ids, n_valid - 1)[:, None] >= tile_end[None, :]).astype(I32), axis=1)
    return pos.astype(I32), tile_e.astype(I32), n_valid.astype(I32).reshape(1), n_tiles


def _glu_perm():
    pm = np.zeros((2 * LANES, 2 * LANES), np.float32)
    for j in range(LANES):
        pm[2 * j, j] = 1.0
        pm[2 * j + 1, LANES + j] = 1.0
    return jnp.asarray(pm, BF16)


def _moe(xs, tile_e, n_valid, w_up, b_up, w_down, b_down, tm=256):
    P, D = xs.shape
    n_blk = D_FF // LANES
    b_up_p = b_up.reshape(N_EXPERTS, n_blk, LANES, 2).transpose(0, 1, 3, 2).reshape(N_EXPERTS, 1, 2 * D_FF)
    b_dn = b_down.reshape(N_EXPERTS, 1, D)

    def wmap(i, te, nt):
        return (te[i], 0, 0)

    grid_spec = pltpu.PrefetchScalarGridSpec(
        num_scalar_prefetch=2,
        grid=(P // tm,),
        in_specs=[pl.BlockSpec((tm, D), lambda i, te, nt: (i, 0)),
                  pl.BlockSpec((1, D, 2 * D_FF), wmap), pl.BlockSpec((1, 1, 2 * D_FF), wmap),
                  pl.BlockSpec((1, D_FF, D), wmap), pl.BlockSpec((1, 1, D), wmap),
                  pl.BlockSpec((2 * LANES, 2 * LANES), lambda i, te, nt: (0, 0))],
        out_specs=pl.BlockSpec((tm, D), lambda i, te, nt: (i, 0)),
        scratch_shapes=[pltpu.VMEM((D, 2 * D_FF), BF16), pltpu.VMEM((D_FF, D), BF16)],
    )
    return pl.pallas_call(
        _moe_kernel,
        grid_spec=grid_spec,
        out_shape=jax.ShapeDtypeStruct((P, D), F32),
        compiler_params=_cparams(("arbitrary",)),
        name="moe",
    )(tile_e, n_valid, xs, w_up, b_up_p, w_down, b_dn, _glu_perm())


def _final_kernel(pos_ref, ys_hbm, x_ref, gate_ref, gtf_ref, g_ref, o_ref, buf, sems, *, tm, n_steps):
    i = pl.program_id(0)
    slot = i & 1

    def gather(step, dst, sem):
        def body(r, carry):
            for k in range(TOP_K):
                src_row = pos_ref[(step * tm + r) * TOP_K + k]
                pltpu.make_async_copy(ys_hbm.at[pl.ds(src_row, 1), :], dst.at[pl.ds(k * tm + r, 1), :],
                                      sem).start()
            return carry
        lax.fori_loop(0, tm, body, 0)

    @pl.when(i == 0)
    def _():
        gather(0, buf.at[0], sems.at[0])

    @pl.when(i + 1 < n_steps)
    def _():
        gather(i + 1, buf.at[1 - slot], sems.at[1 - slot])

    pltpu.make_async_copy(ys_hbm.at[pl.ds(0, TOP_K * tm), :], buf.at[slot], sems.at[slot]).wait()
    g = buf[slot]
    gate = gate_ref[...]
    y = gate[:, 0:1] * g[0:tm]
    for k in range(1, TOP_K):
        y = y + gate[:, k:k + 1] * g[k * tm:(k + 1) * tm]
    o_ref[...] = x_ref[...] + gtf_ref[0, 5:6, :] * _rms(y, g_ref[...])


def _final(ys, pos_flat, gate, x1, mod6, g_post_ffn, tm=128):
    B, S, D = x1.shape
    N = B * S
    n_steps = N // tm
    steps_per_b = S // tm
    grid_spec = pltpu.PrefetchScalarGridSpec(
        num_scalar_prefetch=1,
        grid=(n_steps,),
        in_specs=[pl.BlockSpec(memory_space=pl.ANY),
                  pl.BlockSpec((tm, D), lambda i, p: (i, 0)),
                  pl.BlockSpec((tm, LANES), lambda i, p: (i, 0)),
                  pl.BlockSpec((1, 6, D), lambda i, p: (i // steps_per_b, 0, 0)),
                  pl.BlockSpec((1, D), lambda i, p: (0, 0))],
        out_specs=pl.BlockSpec((tm, D), lambda i, p: (i, 0)),
        scratch_shapes=[pltpu.VMEM((2, TOP_K * tm, D), F32), pltpu.SemaphoreType.DMA((2,))],
    )
    out = pl.pallas_call(
        functools.partial(_final_kernel, tm=tm, n_steps=n_steps),
        grid_spec=grid_spec,
        out_shape=jax.ShapeDtypeStruct((N, D), F32),
        compiler_params=_cparams(("arbitrary",)),
        name="final",
    )(pos_flat, ys, x1.reshape(N, D), gate, mod6, g_post_ffn.reshape(1, D))
    return out.reshape(B, S, D)


def _layer(x, c, positions, w_ada, b_ada, g_pre_mix, g_post_mix, g_pre_ffn, g_post_ffn,
           w_in, cmp_pos, w_cmp_k1, w_cmp_k2, w_cmp_v1, w_cmp_v2, g_out_nsa, g_out_dil, w_o,
           w_router, b_router, w_up, b_up, w_down, b_down):
    B, S, D = x.shape
    mod6 = _ada(c, w_ada, b_ada).reshape(B, 6, D)
    (q_raw, q_rot, kc, vc, ks, vs, kw, vw, gates, qb, kb, vb) = _proj(x, mod6, g_pre_mix, positions, w_in)
    kcf, vcf = _cmpmlp(kc, vc, cmp_pos, w_cmp_k1, w_cmp_k2, w_cmp_v1, w_cmp_v2)
    o_cmp, bias = _cmpsel(q_raw, kcf, vcf)
    o_slc = _slc(q_rot, bias, ks, vs)
    o_win = _band(q_rot, kw, vw, dil=1, max_dist=WIN_NSA - 1, tq=128, tk=256, nt=3, shared_kv=True,
                  with_lse=False)
    ods, lses = [], []
    for window, dil in DIL_CONFIGS:
        o, lse = _band(qb, kb, vb, dil=dil, max_dist=window // dil, tq=128, tk=128, nt=2, shared_kv=False,
                       with_lse=True)
        ods.append(o)
        lses.append(lse)
    x1 = _out(o_cmp, o_slc, o_win, gates, ods, lses, x, mod6, g_out_nsa, g_out_dil, w_o, g_post_mix)
    h2, top_idx, gate, rank, counts = _route(x1, mod6, g_pre_ffn, w_router, b_router)
    tm_moe = 256
    pos, tile_e, n_valid, n_tiles = _moe_layout(top_idx[:, :TOP_K], rank[:, :TOP_K], counts[0, :N_EXPERTS], tm_moe)
    pos_flat = pos.reshape(B * S * TOP_K)
    xs = _dispatch(h2, pos_flat, n_tiles * tm_moe)
    ys = _moe(xs, tile_e, n_valid, w_up, b_up, w_down, b_down, tm=tm_moe)
    return _final(ys, pos_flat, gate, x1, mod6, g_post_ffn)


def kernel(x, c, positions, w_ada, b_ada, g_pre_mix, g_post_mix, g_pre_ffn, g_post_ffn, w_in, cmp_pos,
           w_cmp_k1, w_cmp_k2, w_cmp_v1, w_cmp_v2, g_out_nsa, g_out_dil, w_o, w_router, b_router,
           w_up, b_up, w_down, b_down):
    depth = w_ada.shape[0]
    for l in range(depth):
        x = _layer(x, c, positions, w_ada[l], b_ada[l], g_pre_mix[l], g_post_mix[l], g_pre_ffn[l],
                   g_post_ffn[l], w_in[l], cmp_pos[l], w_cmp_k1[l], w_cmp_k2[l], w_cmp_v1[l], w_cmp_v2[l],
                   g_out_nsa[l], g_out_dil[l], w_o[l], w_router[l], b_router[l], w_up[l], b_up[l],
                   w_down[l], b_down[l])
    return x
```

```python
import functools

import numpy as np
import jax
import jax.numpy as jnp
from jax import lax
from jax.experimental import pallas as pl
from jax.experimental.pallas import tpu as pltpu

F32 = jnp.float32
BF16 = jnp.bfloat16
I32 = jnp.int32

D_MODEL = 1024
HEAD_DIM = 64
N_HEADS_NSA = 8
N_KV_NSA = 2
GROUP_NSA = 4
N_HEADS_DIL = 8
ROPE_THETA = 500000.0
ROPE_DIM = 16
ROPE_HALF = 8
CMP_BLOCK = 32
CMP_STRIDE = 16
CMP_HIDDEN = 256
SLC_BLOCK = 64
SLC_SHIFT = 6
SLC_TOPK = 16
WIN_NSA = 512
DIL_CONFIGS = ((128, 1), (512, 4), (2048, 16))
N_EXPERTS = 32
TOP_K = 4
D_FF = 1024
SWIGLU_LIMIT = 7.0
SWIGLU_ALPHA = 1.702
RMS_EPS = 1e-6
NEG_INF = -1e30
FORCE_SCORE = 1e9
SCALE = HEAD_DIM ** -0.5

Q_NSA = 512
KV_NSA = 128
Q_DIL = 512
LANES = 128
VMEM_LIMIT = 56 * 1024 * 1024

_NT = (((1,), (1,)), ((), ()))


def _cparams(sem):
    return pltpu.CompilerParams(dimension_semantics=sem, vmem_limit_bytes=VMEM_LIMIT)


def _rms(x, g):
    return x * lax.rsqrt(jnp.mean(x * x, axis=-1, keepdims=True) + RMS_EPS) * g


def _split3_dot(a, e):
    hi = a.astype(BF16)
    r1 = a - hi.astype(F32)
    mid = r1.astype(BF16)
    lo = (r1 - mid.astype(F32)).astype(BF16)
    return (jnp.dot(hi, e, preferred_element_type=F32) + jnp.dot(mid, e, preferred_element_type=F32)
            + jnp.dot(lo, e, preferred_element_type=F32))


def _ada_kernel(c_ref, w_ref, b_ref, o_ref):
    c = c_ref[...]
    a = c * jax.nn.sigmoid(c)
    o_ref[...] = jnp.dot(a, w_ref[...], preferred_element_type=F32,
                         precision=lax.Precision.HIGHEST) + b_ref[...]


def _ada(c, w_ada, b_ada):
    B, D = c.shape
    n = w_ada.shape[1] // D
    return pl.pallas_call(
        _ada_kernel,
        grid=(n,),
        in_specs=[pl.BlockSpec((B, D), lambda j: (0, 0)),
                  pl.BlockSpec((D, D), lambda j: (0, j)),
                  pl.BlockSpec((1, D), lambda j: (0, j))],
        out_specs=pl.BlockSpec((B, D), lambda j: (0, j)),
        out_shape=jax.ShapeDtypeStruct((B, n * D), F32),
        compiler_params=_cparams(("arbitrary",)),
        name="ada",
    )(c, w_ada, b_ada.reshape(1, -1))


def _rope_tables(pos_col, inv_row, s1_row, s2_row):
    ang = pos_col * inv_row
    cs = jnp.cos(ang)
    sn = jnp.sin(ang)
    return cs, sn * s1_row, sn * s2_row


def _rope_blk(x, cs, s1, s2):
    return x * cs + pltpu.roll(x, LANES - ROPE_HALF, 1) * s1 + pltpu.roll(x, ROPE_HALF, 1) * s2


def _proj_kernel(x_ref, mod_ref, g_ref, pos_ref, tab_ref, wa_ref, wg_ref, wb_ref,
                 qraw_ref, qrot_ref, qrott_ref, kc_ref, vc_ref, ks_ref, vst_ref, kw_ref, vw_ref, gt_ref,
                 qb_ref, kb_ref, vb_ref):
    x = x_ref[0]
    sh = mod_ref[0, 0:1, :]
    sc = mod_ref[0, 1:2, :]
    h = (_rms(x, g_ref[...]) * (1.0 + sc) + sh).astype(BF16)
    cs, s1, s2 = _rope_tables(pos_ref[0], tab_ref[0:1, :], tab_ref[1:2, :], tab_ref[2:3, :])

    pa = jnp.dot(h, wa_ref[...], preferred_element_type=F32)
    for r in range(Q_NSA // LANES):
        blk = pa[:, r * LANES:(r + 1) * LANES]
        qraw_ref[0, :, r * LANES:(r + 1) * LANES] = (blk * SCALE).astype(BF16)
        rot = _rope_blk(blk, cs, s1, s2) * SCALE
        qrot_ref[0, :, r * LANES:(r + 1) * LANES] = rot.astype(BF16)
        qrott_ref[0, r * LANES:(r + 1) * LANES, :] = rot.T.astype(BF16)
    o = Q_NSA
    kc_ref[0] = pa[:, o:o + LANES]
    vc_ref[0] = pa[:, o + LANES:o + 2 * LANES]
    ks_ref[0] = _rope_blk(pa[:, o + 2 * LANES:o + 3 * LANES], cs, s1, s2).astype(BF16)
    vst_ref[0] = pa[:, o + 3 * LANES:o + 4 * LANES].T.astype(BF16)
    kw_ref[0] = _rope_blk(pa[:, o + 4 * LANES:o + 5 * LANES], cs, s1, s2).astype(BF16)
    vw_ref[0] = pa[:, o + 5 * LANES:o + 6 * LANES].astype(BF16)

    gt_ref[0] = jnp.dot(h, wg_ref[...], preferred_element_type=F32)

    pb = jnp.dot(h, wb_ref[...], preferred_element_type=F32)
    for r in range(Q_DIL // LANES):
        qb_ref[0, :, r * LANES:(r + 1) * LANES] = (
            _rope_blk(pb[:, r * LANES:(r + 1) * LANES], cs, s1, s2) * SCALE).astype(BF16)
        kb_ref[0, :, r * LANES:(r + 1) * LANES] = _rope_blk(
            pb[:, Q_DIL + r * LANES:Q_DIL + (r + 1) * LANES], cs, s1, s2).astype(BF16)
    vb_ref[0] = pb[:, 2 * Q_DIL:].astype(BF16)


def _nsa_perm():
    cols = []
    for r in range(GROUP_NSA):
        for hk in range(N_KV_NSA):
            hq = hk * GROUP_NSA + r
            cols.extend(range(hq * HEAD_DIM, (hq + 1) * HEAD_DIM))
    return np.asarray(cols, np.int32)


def _rope_const_table():
    half = ROPE_HALF
    inv = ROPE_THETA ** (-jnp.arange(half, dtype=F32) / half)
    lane = np.arange(LANES) % HEAD_DIM
    inv_row = jnp.where(lane < ROPE_DIM, inv[lane % half], 0.0).astype(F32)
    s1 = np.where(lane < half, -1.0, 0.0).astype(np.float32)
    s2 = np.where((lane >= half) & (lane < ROPE_DIM), 1.0, 0.0).astype(np.float32)
    tab = jnp.zeros((8, LANES), F32).at[0].set(inv_row).at[1].set(s1).at[2].set(s2)
    return tab


def _proj(x, mod6, g_pre, positions, w_in, tm=512):
    B, S, D = x.shape
    perm = _nsa_perm()
    gate_lo = Q_NSA + 6 * KV_NSA
    n_gate = 3 * N_HEADS_NSA
    w_a = jnp.concatenate([w_in[:, :Q_NSA][:, perm], w_in[:, Q_NSA:gate_lo]], axis=1).astype(BF16)
    w_g = jnp.pad(w_in[:, gate_lo:gate_lo + n_gate], ((0, 0), (0, LANES - n_gate))).astype(BF16)
    w_b = w_in[:, gate_lo + n_gate:].astype(BF16)
    pos = positions.astype(F32).reshape(B, S, 1)
    tab = _rope_const_table()
    wa_n, wb_n = w_a.shape[1], w_b.shape[1]

    def tok(width, dtype):
        return (pl.BlockSpec((1, tm, width), lambda b, i: (b, i, 0)),
                jax.ShapeDtypeStruct((B, S, width), dtype))

    def tok_t(width, dtype):
        return (pl.BlockSpec((1, width, tm), lambda b, i: (b, 0, i)),
                jax.ShapeDtypeStruct((B, width, S), dtype))

    outs = [tok(Q_NSA, BF16), tok(Q_NSA, BF16), tok_t(Q_NSA, BF16), tok(LANES, F32), tok(LANES, F32),
            tok(LANES, BF16), tok_t(LANES, BF16), tok(LANES, BF16), tok(LANES, BF16), tok(LANES, F32),
            tok(Q_DIL, BF16), tok(Q_DIL, BF16), tok(Q_DIL, BF16)]
    return pl.pallas_call(
        _proj_kernel,
        grid=(B, S // tm),
        in_specs=[pl.BlockSpec((1, tm, D), lambda b, i: (b, i, 0)),
                  pl.BlockSpec((1, 6, D), lambda b, i: (b, 0, 0)),
                  pl.BlockSpec((1, D), lambda b, i: (0, 0)),
                  pl.BlockSpec((1, tm, 1), lambda b, i: (b, i, 0)),
                  pl.BlockSpec((8, LANES), lambda b, i: (0, 0)),
                  pl.BlockSpec((D, wa_n), lambda b, i: (0, 0)),
                  pl.BlockSpec((D, LANES), lambda b, i: (0, 0)),
                  pl.BlockSpec((D, wb_n), lambda b, i: (0, 0))],
        out_specs=[o[0] for o in outs],
        out_shape=[o[1] for o in outs],
        compiler_params=_cparams(("parallel", "parallel")),
        name="proj",
    )(x, mod6, g_pre.reshape(1, D), pos, tab, w_a, w_g, w_b)


def _cmpmlp_kernel(a_ref, p_ref, w1_ref, w2_ref, o_ref):
    a = a_ref[0, 0].astype(BF16)
    w1 = w1_ref[0]
    half = CMP_STRIDE * HEAD_DIM
    u = jnp.dot(a, w1[:half], preferred_element_type=F32)
    v = jnp.dot(a, w1[half:], preferred_element_type=F32)
    bias = jnp.dot(p_ref[...], w1, preferred_element_type=F32)[0:1]
    n16 = u.shape[0]
    hid = jax.nn.gelu(u + pltpu.roll(v, n16 - 1, 0) + bias)
    o_ref[0, 0] = jnp.dot(hid.astype(BF16), w2_ref[0], preferred_element_type=F32)


def _cmpmlp(kc, vc, cmp_pos, w_k1, w_k2, w_v1, w_v2):
    B, S, _ = kc.shape
    n16 = S // CMP_STRIDE
    seg = CMP_STRIDE * HEAD_DIM

    def segs(a):
        return a.reshape(B, n16, CMP_STRIDE, N_KV_NSA, HEAD_DIM).transpose(0, 3, 1, 2, 4).reshape(
            B * N_KV_NSA, n16, seg)

    a = jnp.stack([segs(kc), segs(vc)], axis=0)
    w1 = jnp.stack([w_k1, w_v1], axis=0).astype(BF16)
    w2 = jnp.stack([w_k2, w_v2], axis=0).astype(BF16)
    p8 = jnp.broadcast_to(cmp_pos.reshape(1, CMP_BLOCK * HEAD_DIM), (8, CMP_BLOCK * HEAD_DIM)).astype(BF16)
    out = pl.pallas_call(
        _cmpmlp_kernel,
        grid=(2, B * N_KV_NSA),
        in_specs=[pl.BlockSpec((1, 1, n16, seg), lambda t, g: (t, g, 0, 0)),
                  pl.BlockSpec((8, 2 * seg), lambda t, g: (0, 0)),
                  pl.BlockSpec((1, 2 * seg, CMP_HIDDEN), lambda t, g: (t, 0, 0)),
                  pl.BlockSpec((1, CMP_HIDDEN, HEAD_DIM), lambda t, g: (t, 0, 0))],
        out_specs=pl.BlockSpec((1, 1, n16, HEAD_DIM), lambda t, g: (t, g, 0, 0)),
        out_shape=jax.ShapeDtypeStruct((2, B * N_KV_NSA, n16, HEAD_DIM), F32),
        compiler_params=_cparams(("parallel", "parallel")),
        name="cmpmlp",
    )(a, p8, w1, w2)
    out = out.reshape(2, B, N_KV_NSA, n16, HEAD_DIM).transpose(0, 1, 3, 2, 4).reshape(2, B, n16, KV_NSA)
    return out[0].astype(BF16), out[1].astype(BF16)


def _stack_heads(q, tq):
    lane = lax.broadcasted_iota(I32, (tq, LANES), 1)
    half0 = lane < HEAD_DIM
    zero = jnp.zeros((tq, LANES), q.dtype)
    rows = []
    for hk in range(N_KV_NSA):
        keep = half0 if hk == 0 else jnp.logical_not(half0)
        for r in range(GROUP_NSA):
            rows.append(jnp.where(keep, q[:, r * LANES:(r + 1) * LANES], zero))
    return jnp.concatenate(rows, axis=0), half0


def _unstack_heads(o, half0, tq, o_ref):
    for r in range(GROUP_NSA):
        o0 = o[r * tq:(r + 1) * tq]
        o1 = o[(GROUP_NSA + r) * tq:(GROUP_NSA + r + 1) * tq]
        o_ref[0, :, r * LANES:(r + 1) * LANES] = jnp.where(half0, o0, o1)


def _cmpsel_kernel(q_ref, kc_ref, vc_ref, ovt_ref, o_ref, bias_ref, *, tq):
    i = pl.program_id(1)
    qs, half0 = _stack_heads(q_ref[0], tq)
    n16 = kc_ref.shape[1]
    rows = 8 * tq
    s = lax.dot_general(qs, kc_ref[0], _NT, preferred_element_type=F32)
    t = i * tq + (lax.broadcasted_iota(I32, (rows, n16), 0) & (tq - 1))
    c = lax.broadcasted_iota(I32, (rows, n16), 1)
    valid = (c * CMP_STRIDE + (CMP_BLOCK - 1)) <= t
    s = jnp.where(valid, s, NEG_INF)
    m = jnp.max(s, axis=-1, keepdims=True)
    e = jnp.exp(s - m)
    l = jnp.sum(e, axis=-1, keepdims=True)
    p = jnp.where(valid, e / l, 0.0)
    o = jnp.dot(p.astype(BF16), vc_ref[0], preferred_element_type=F32)
    _unstack_heads(o, half0, tq, o_ref)

    n_slc = ovt_ref.shape[0]
    j = lax.broadcasted_iota(I32, (n_slc, tq), 0)
    cur = (i * tq + lax.broadcasted_iota(I32, (n_slc, tq), 1)) >> SLC_SHIFT
    forced = (j == 0) | (j == cur) | (j == cur - 1)
    ovt = ovt_ref[...]
    biases = []
    for hk in range(N_KV_NSA):
        ps = p[hk * GROUP_NSA * tq:(hk * GROUP_NSA + 1) * tq]
        for r in range(1, GROUP_NSA):
            ps = ps + p[(hk * GROUP_NSA + r) * tq:(hk * GROUP_NSA + r + 1) * tq]
        hi = ps.astype(BF16)
        lo = (ps - hi.astype(F32)).astype(BF16)
        pslc = (lax.dot_general(ovt, hi, _NT, preferred_element_type=F32)
                + lax.dot_general(ovt, lo, _NT, preferred_element_type=F32))
        score = jnp.where(forced, FORCE_SCORE, jnp.where(j <= cur, pslc, -1.0))
        rank = jnp.zeros((n_slc, tq), I32)
        for ii in range(n_slc):
            ri = score[ii:ii + 1, :]
            beats = (ri > score) | ((ri == score) & (j > ii))
            rank = rank + beats.astype(I32)
        biases.append(jnp.where(rank < SLC_TOPK, 0.0, NEG_INF))
    bias_ref[0] = jnp.concatenate(biases, axis=0).astype(BF16)


def _overlap_t(S):
    n16 = S // CMP_STRIDE
    n_slc = S // SLC_BLOCK
    cs = np.arange(n16) * CMP_STRIDE
    js = np.arange(n_slc) * SLC_BLOCK
    ov = np.clip(np.minimum(cs[:, None] + CMP_BLOCK, js[None, :] + SLC_BLOCK)
                 - np.maximum(cs[:, None], js[None, :]), 0, None).astype(np.float32) / CMP_BLOCK
    ov[n16 - 1] = 0.0
    return jnp.asarray(ov.T, BF16)


def _cmpsel(q_raw, kcf, vcf, tq=128):
    B, S, _ = q_raw.shape
    n16 = S // CMP_STRIDE
    n_slc = S // SLC_BLOCK
    assert n_slc == HEAD_DIM, "selection bias is laid out as one 64-lane half per kv head"
    return pl.pallas_call(
        functools.partial(_cmpsel_kernel, tq=tq),
        grid=(B, S // tq),
        in_specs=[pl.BlockSpec((1, tq, Q_NSA), lambda b, i: (b, i, 0)),
                  pl.BlockSpec((1, n16, KV_NSA), lambda b, i: (b, 0, 0)),
                  pl.BlockSpec((1, n16, KV_NSA), lambda b, i: (b, 0, 0)),
                  pl.BlockSpec((n_slc, n16), lambda b, i: (0, 0))],
        out_specs=[pl.BlockSpec((1, tq, Q_NSA), lambda b, i: (b, i, 0)),
                   pl.BlockSpec((1, 2 * n_slc, tq), lambda b, i: (b, 0, i))],
        out_shape=[jax.ShapeDtypeStruct((B, S, Q_NSA), F32),
                   jax.ShapeDtypeStruct((B, 2 * n_slc, S), BF16)],
        compiler_params=_cparams(("parallel", "parallel")),
        name="cmpsel",
    )(q_raw, kcf, vcf, _overlap_t(S))


def _slc_kernel(qi_ref, kj_ref, qt_ref, bt_ref, k_ref, vt_ref, o_ref, qa_sc, m_sc, l_sc, acc_sc, *, tq, tk):
    step = pl.program_id(1)
    i = qi_ref[step]
    kj = kj_ref[step]
    last = (i * tq + tq - 1) // tk
    cols = 8 * tq

    @pl.when(kj == 0)
    def _():
        half0 = lax.broadcasted_iota(I32, (LANES, tq), 0) < HEAD_DIM
        zero = jnp.zeros((LANES, tq), BF16)
        bt = bt_ref[0]
        groups = []
        for hk in range(N_KV_NSA):
            keep = half0 if hk == 0 else jnp.logical_not(half0)
            bh = jnp.where(keep, bt, zero)
            for r in range(GROUP_NSA):
                qb = qt_ref[0, r * LANES:(r + 1) * LANES, :]
                groups.append(jnp.concatenate([jnp.where(keep, qb, zero), bh], axis=0))
        qa_sc[...] = jnp.concatenate(groups, axis=1)
        m_sc[...] = jnp.full((1, cols), NEG_INF, F32)
        l_sc[...] = jnp.zeros((1, cols), F32)
        acc_sc[...] = jnp.zeros((LANES, cols), F32)

    def update(on_diagonal):
        kblk = (kj * tk + lax.broadcasted_iota(I32, (tk, LANES), 0)) >> SLC_SHIFT
        lane = lax.broadcasted_iota(I32, (tk, LANES), 1) & (HEAD_DIM - 1)
        onehot = jnp.where(kblk == lane, 1.0, 0.0).astype(BF16)
        kaug = jnp.concatenate([k_ref[0], onehot], axis=1)
        st = jnp.dot(kaug, qa_sc[...], preferred_element_type=F32)
        if on_diagonal:
            kpos = kj * tk + lax.broadcasted_iota(I32, (tk, cols), 0)
            t = i * tq + (lax.broadcasted_iota(I32, (tk, cols), 1) & (tq - 1))
            st = jnp.where(kpos <= t, st, NEG_INF)
        m_old = m_sc[...]
        m_new = jnp.maximum(m_old, jnp.max(st, axis=0, keepdims=True))
        alpha = jnp.exp(m_old - m_new)
        p = jnp.exp(st - m_new)
        l_sc[...] = alpha * l_sc[...] + jnp.sum(p, axis=0, keepdims=True)
        acc_sc[...] = alpha * acc_sc[...] + jnp.dot(vt_ref[0], p.astype(BF16), preferred_element_type=F32)
        m_sc[...] = m_new

    @pl.when(kj < last)
    def _():
        update(False)

    @pl.when(kj == last)
    def _():
        update(True)
        ot = acc_sc[...] / l_sc[...]
        half0 = lax.broadcasted_iota(I32, (tq, LANES), 1) < HEAD_DIM
        for r in range(GROUP_NSA):
            o0 = ot[:, r * tq:(r + 1) * tq].T
            o1 = ot[:, (GROUP_NSA + r) * tq:(GROUP_NSA + r + 1) * tq].T
            o_ref[0, :, r * LANES:(r + 1) * LANES] = jnp.where(half0, o0, o1)


def _slc(q_rot_t, bias_t, ks, vs_t, tq=128, tk=512):
    B, _, S = q_rot_t.shape
    assert tk % tq == 0
    qi, kj = [], []
    for i in range(S // tq):
        for j in range((i * tq + tq - 1) // tk + 1):
            qi.append(i)
            kj.append(j)
    grid_spec = pltpu.PrefetchScalarGridSpec(
        num_scalar_prefetch=2,
        grid=(B, len(qi)),
        in_specs=[pl.BlockSpec((1, Q_NSA, tq), lambda b, s, qi, kj: (b, 0, qi[s])),
                  pl.BlockSpec((1, LANES, tq), lambda b, s, qi, kj: (b, 0, qi[s])),
                  pl.BlockSpec((1, tk, KV_NSA), lambda b, s, qi, kj: (b, kj[s], 0)),
                  pl.BlockSpec((1, KV_NSA, tk), lambda b, s, qi, kj: (b, 0, kj[s]))],
        out_specs=pl.BlockSpec((1, tq, Q_NSA), lambda b, s, qi, kj: (b, qi[s], 0)),
        scratch_shapes=[pltpu.VMEM((2 * LANES, 8 * tq), BF16),
                        pltpu.VMEM((1, 8 * tq), F32),
                        pltpu.VMEM((1, 8 * tq), F32),
                        pltpu.VMEM((LANES, 8 * tq), F32)],
    )
    return pl.pallas_call(
        functools.partial(_slc_kernel, tq=tq, tk=tk),
        grid_spec=grid_spec,
        out_shape=jax.ShapeDtypeStruct((B, S, Q_NSA), F32),
        compiler_params=_cparams(("parallel", "arbitrary")),
        name="slc",
    )(jnp.asarray(qi, I32), jnp.asarray(kj, I32), q_rot_t, bias_t, ks, vs_t)


def _band_kernel(*refs, tq, tk, nt, max_dist, shared_kv, with_lse):
    q_ref = refs[0]
    k_refs = refs[1:1 + nt]
    v_refs = refs[1 + nt:1 + 2 * nt]
    o_ref = refs[1 + 2 * nt]
    lse_ref = refs[2 + 2 * nt] if with_lse else None
    i = pl.program_id(2)
    a = (i * tq) // tk
    lane = lax.broadcasted_iota(I32, (tq, LANES), 1)
    half0 = lane < HEAD_DIM
    row_t = i * tq + (lax.broadcasted_iota(I32, (2 * tq, tk), 0) & (tq - 1))
    col = lax.broadcasted_iota(I32, (2 * tq, tk), 1)
    oks = []
    for jt in range(nt):
        kpos = (a - (nt - 1) + jt) * tk + col
        d = row_t - kpos
        oks.append((d >= 0) & (d <= max_dist) & (kpos >= 0))
    lse_acc = jnp.zeros((tq, LANES), F32)
    for blk in range(Q_NSA // LANES):
        qb = q_ref[0, :, blk * LANES:(blk + 1) * LANES]
        zero = jnp.zeros_like(qb)
        qs = jnp.concatenate([jnp.where(half0, qb, zero), jnp.where(half0, zero, qb)], axis=0)
        kv0 = 0 if shared_kv else blk * LANES
        ss = []
        for jt in range(nt):
            kt = k_refs[jt][0, :, kv0:kv0 + LANES]
            s = lax.dot_general(qs, kt, _NT, preferred_element_type=F32)
            ss.append(jnp.where(oks[jt], s, NEG_INF))
        m = jnp.max(ss[0], axis=-1, keepdims=True)
        for s in ss[1:]:
            m = jnp.maximum(m, jnp.max(s, axis=-1, keepdims=True))
        l = jnp.zeros((2 * tq, 1), F32)
        o = jnp.zeros((2 * tq, LANES), F32)
        for jt in range(nt):
            p = jnp.exp(ss[jt] - m)
            l = l + jnp.sum(p, axis=-1, keepdims=True)
            o = o + jnp.dot(p.astype(BF16), v_refs[jt][0, :, kv0:kv0 + LANES], preferred_element_type=F32)
        o = o / l
        o_ref[0, :, blk * LANES:(blk + 1) * LANES] = jnp.where(half0, o[:tq], o[tq:])
        if with_lse:
            lse = m + jnp.log(l)
            lse_acc = jnp.where(lane == 2 * blk, lse[:tq], lse_acc)
            lse_acc = jnp.where(lane == 2 * blk + 1, lse[tq:], lse_acc)
    if with_lse:
        lse_ref[0] = lse_acc


def _band(q, k, v, *, dil, max_dist, tq, tk, nt, shared_kv, with_lse):
    B, S, _ = q.shape
    m = S // dil
    kvw = k.shape[-1]
    qv = q.reshape(B, m, dil * Q_NSA)
    kv_ = k.reshape(B, m, dil * kvw)
    vv = v.reshape(B, m, dil * kvw)

    def kv_map(jt):
        return lambda b, r, i: (b, jnp.maximum((i * tq) // tk - (nt - 1) + jt, 0), r)

    tok_spec = pl.BlockSpec((1, tq, Q_NSA), lambda b, r, i: (b, i, r))
    in_specs = [tok_spec]
    in_specs += [pl.BlockSpec((1, tk, kvw), kv_map(jt)) for jt in range(nt)]
    in_specs += [pl.BlockSpec((1, tk, kvw), kv_map(jt)) for jt in range(nt)]
    out_specs = [tok_spec]
    out_shape = [jax.ShapeDtypeStruct((B, m, dil * Q_NSA), F32)]
    if with_lse:
        out_specs.append(pl.BlockSpec((1, tq, LANES), lambda b, r, i: (b, i, r)))
        out_shape.append(jax.ShapeDtypeStruct((B, m, dil * LANES), F32))
    res = pl.pallas_call(
        functools.partial(_band_kernel, tq=tq, tk=tk, nt=nt, max_dist=max_dist, shared_kv=shared_kv,
                          with_lse=with_lse),
        grid=(B, dil, m // tq),
        in_specs=in_specs,
        out_specs=out_specs,
        out_shape=out_shape,
        compiler_params=_cparams(("parallel", "parallel", "parallel")),
        name=f"band_d{dil}_w{max_dist}",
    )(qv, *([kv_] * nt), *([vv] * nt))
    o = res[0].reshape(B, S, Q_NSA)
    if with_lse:
        return o, res[1].reshape(B, S, LANES)
    return o


def _out_kernel(ocmp_ref, oslc_ref, owin_ref, gt_ref, od1_ref, od4_ref, od16_ref, l1_ref, l4_ref, l16_ref,
                x_ref, mod_ref, eg_ref, ed_ref, gnsa_ref, gdil_ref, wo_ref, gpost_ref, o_ref):
    sg = jax.nn.sigmoid(gt_ref[0])
    oa = (_split3_dot(sg, eg_ref[0]) * ocmp_ref[0] + _split3_dot(sg, eg_ref[1]) * oslc_ref[0]
          + _split3_dot(sg, eg_ref[2]) * owin_ref[0])
    ya = _rms(oa, gnsa_ref[...])

    l1, l4, l16 = l1_ref[0], l4_ref[0], l16_ref[0]
    mx = jnp.maximum(jnp.maximum(l1, l4), l16)
    e1, e4, e16 = jnp.exp(l1 - mx), jnp.exp(l4 - mx), jnp.exp(l16 - mx)
    den = e1 + e4 + e16
    ed = ed_ref[...]
    ob = (_split3_dot(e1 / den, ed) * od1_ref[0] + _split3_dot(e4 / den, ed) * od4_ref[0]
          + _split3_dot(e16 / den, ed) * od16_ref[0])
    yb = _rms(ob, gdil_ref[...])

    y = jnp.concatenate([ya, yb], axis=1).astype(BF16)
    z = jnp.dot(y, wo_ref[...], preferred_element_type=F32)
    gt_m = mod_ref[0, 2:3, :]
    o_ref[0] = x_ref[0] + gt_m * _rms(z, gpost_ref[...])


def _gate_expanders():
    perm = _nsa_perm()
    eg = np.zeros((3, LANES, Q_NSA), np.float32)
    for lane_out, col in enumerate(perm):
        hq = col // HEAD_DIM
        for c in range(3):
            eg[c, hq * 3 + c, lane_out] = 1.0
    ed = np.zeros((LANES, Q_DIL), np.float32)
    for h in range(N_HEADS_DIL):
        ed[h, h * HEAD_DIM:(h + 1) * HEAD_DIM] = 1.0
    return jnp.asarray(eg, BF16), jnp.asarray(ed, BF16)


def _out(o_cmp, o_slc, o_win, gates, ods, lses, x, mod6, g_out_nsa, g_out_dil, w_o, g_post, tm=256):
    B, S, D = x.shape
    perm = _nsa_perm()
    eg, ed = _gate_expanders()
    w_o_p = jnp.concatenate([w_o[:Q_NSA][perm], w_o[Q_NSA:]], axis=0).astype(BF16)
    g_nsa_p = g_out_nsa[perm].reshape(1, Q_NSA)

    def tok(width):
        return pl.BlockSpec((1, tm, width), lambda b, i: (b, i, 0))

    def const(shape):
        return pl.BlockSpec(shape, lambda b, i: (0,) * len(shape))

    return pl.pallas_call(
        _out_kernel,
        grid=(B, S // tm),
        in_specs=[tok(Q_NSA), tok(Q_NSA), tok(Q_NSA), tok(LANES), tok(Q_DIL), tok(Q_DIL), tok(Q_DIL),
                  tok(LANES), tok(LANES), tok(LANES), tok(D),
                  pl.BlockSpec((1, 6, D), lambda b, i: (b, 0, 0)),
                  const((3, LANES, Q_NSA)), const((LANES, Q_DIL)), const((1, Q_NSA)), const((1, Q_DIL)),
                  const((D, D)), const((1, D))],
        out_specs=tok(D),
        out_shape=jax.ShapeDtypeStruct((B, S, D), F32),
        compiler_params=_cparams(("parallel", "parallel")),
        name="out",
    )(o_cmp, o_slc, o_win, gates, *ods, *lses, x, mod6, eg, ed, g_nsa_p, g_out_dil.reshape(1, Q_DIL),
      w_o_p, g_post.reshape(1, D))


def _route_kernel(x_ref, mod_ref, g_ref, wr_ref, br_ref, h_ref, idx_ref, gate_ref, rank_ref, cnt_ref, run_sc):
    i = pl.program_id(0)

    @pl.when(i == 0)
    def _():
        run_sc[...] = jnp.zeros_like(run_sc)

    sh = mod_ref[0, 3:4, :]
    sc = mod_ref[0, 4:5, :]
    h = _rms(x_ref[...], g_ref[...]) * (1.0 + sc) + sh
    h_ref[...] = h
    logits = jnp.dot(h, wr_ref[...], preferred_element_type=F32,
                     precision=lax.Precision.HIGHEST) + br_ref[...]
    tm = logits.shape[0]
    lane = lax.broadcasted_iota(I32, (tm, LANES), 1)
    work = jnp.where(lane < N_EXPERTS, logits, -jnp.inf)
    idx_out = jnp.zeros((tm, LANES), I32)
    val_out = jnp.full((tm, LANES), -jnp.inf, F32)
    rank_out = jnp.zeros((tm, LANES), F32)
    tri = jnp.where(lax.broadcasted_iota(I32, (tm, tm), 1) < lax.broadcasted_iota(I32, (tm, tm), 0),
                    1.0, 0.0).astype(BF16)
    base = run_sc[0:1, :]
    for k in range(TOP_K):
        mx = jnp.max(work, axis=-1, keepdims=True)
        ix = jnp.min(jnp.where(work == mx, lane, LANES), axis=-1, keepdims=True)
        hit = lane == ix
        idx_out = jnp.where(lane == k, ix, idx_out)
        val_out = jnp.where(lane == k, mx, val_out)
        work = jnp.where(hit, -jnp.inf, work)
        onehot = jnp.where(hit, 1.0, 0.0)
        before = jnp.dot(tri, onehot.astype(BF16), preferred_element_type=F32) + base
        rank_k = jnp.sum(onehot * before, axis=-1, keepdims=True)
        rank_out = jnp.where(lane == k, rank_k, rank_out)
        base = base + jnp.sum(onehot, axis=0, keepdims=True)
    run_sc[...] = jnp.broadcast_to(base, run_sc.shape)
    e = jnp.exp(val_out - val_out[:, 0:1])
    idx_ref[...] = idx_out
    gate_ref[...] = e / jnp.sum(e, axis=-1, keepdims=True)
    rank_ref[...] = rank_out.astype(I32)
    cnt_ref[...] = run_sc[...].astype(I32)


def _route(x1, mod6, g_pre_ffn, w_router, b_router, tm=512):
    B, S, D = x1.shape
    N = B * S
    spb = S // tm
    wr = jnp.pad(w_router, ((0, 0), (0, LANES - N_EXPERTS)))
    br = jnp.pad(b_router, (0, LANES - N_EXPERTS)).reshape(1, LANES)

    def tok(width):
        return pl.BlockSpec((tm, width), lambda i: (i, 0))

    return pl.pallas_call(
        _route_kernel,
        grid=(N // tm,),
        in_specs=[tok(D), pl.BlockSpec((1, 6, D), lambda i: (i // spb, 0, 0)),
                  pl.BlockSpec((1, D), lambda i: (0, 0)),
                  pl.BlockSpec((D, LANES), lambda i: (0, 0)),
                  pl.BlockSpec((1, LANES), lambda i: (0, 0))],
        out_specs=[tok(D), tok(LANES), tok(LANES), tok(LANES), pl.BlockSpec((8, LANES), lambda i: (0, 0))],
        out_shape=[jax.ShapeDtypeStruct((N, D), F32),
                   jax.ShapeDtypeStruct((N, LANES), I32),
                   jax.ShapeDtypeStruct((N, LANES), F32),
                   jax.ShapeDtypeStruct((N, LANES), I32),
                   jax.ShapeDtypeStruct((8, LANES), I32)],
        scratch_shapes=[pltpu.VMEM((8, LANES), F32)],
        compiler_params=_cparams(("arbitrary",)),
        name="route",
    )(x1.reshape(N, D), mod6, g_pre_ffn.reshape(1, D), wr, br)


def _dispatch_kernel(pos_ref, h_ref, xs_in, xs_out, sem, *, tm):
    del xs_in
    i = pl.program_id(0)

    def body(r, carry):
        for k in range(TOP_K):
            dst_row = pos_ref[(i * tm + r) * TOP_K + k]
            pltpu.make_async_copy(h_ref.at[pl.ds(r, 1), :], xs_out.at[pl.ds(dst_row, 1), :], sem).start()
        return carry

    lax.fori_loop(0, tm, body, 0)
    for k in range(TOP_K):
        pltpu.make_async_copy(h_ref, xs_out.at[pl.ds(0, tm), :], sem).wait()


def _dispatch(h2, pos_flat, P, tm=256):
    N, D = h2.shape
    grid_spec = pltpu.PrefetchScalarGridSpec(
        num_scalar_prefetch=1,
        grid=(N // tm,),
        in_specs=[pl.BlockSpec((tm, D), lambda i, p: (i, 0)),
                  pl.BlockSpec(memory_space=pl.ANY)],
        out_specs=pl.BlockSpec(memory_space=pl.ANY),
        scratch_shapes=[pltpu.SemaphoreType.DMA(())],
    )
    return pl.pallas_call(
        functools.partial(_dispatch_kernel, tm=tm),
        grid_spec=grid_spec,
        out_shape=jax.ShapeDtypeStruct((P, D), F32),
        input_output_aliases={2: 0},
        compiler_params=_cparams(("arbitrary",)),
        name="dispatch",
    )(pos_flat, h2, jnp.zeros((P, D), F32))


def _moe_kernel(te_ref, nt_ref, xs_ref, wup_ref, bup_ref, wdn_ref, bdn_ref, pm_ref, y_ref, wup_sc, wdn_sc):
    i = pl.program_id(0)
    n_valid = nt_ref[0]
    new_expert = (i == 0) | (te_ref[i] != te_ref[jnp.maximum(i - 1, 0)])
    n_blk = wup_sc.shape[1] // (2 * LANES)

    @pl.when(new_expert & (i < n_valid))
    def _():
        pm = pm_ref[...]
        for blk in range(n_blk):
            cols = slice(blk * 2 * LANES, (blk + 1) * 2 * LANES)
            w = wup_ref[0, :, cols].astype(BF16)
            wup_sc[:, cols] = jnp.dot(w, pm, preferred_element_type=F32).astype(BF16)
        wdn_sc[...] = wdn_ref[0].astype(BF16)

    @pl.when(i < n_valid)
    def _():
        xs = xs_ref[...].astype(BF16)
        u = jnp.dot(xs, wup_sc[...], preferred_element_type=F32) + bup_ref[0]
        acts = []
        for blk in range(n_blk):
            ug = jnp.minimum(u[:, blk * 2 * LANES:blk * 2 * LANES + LANES], SWIGLU_LIMIT)
            ul = jnp.clip(u[:, blk * 2 * LANES + LANES:(blk + 1) * 2 * LANES], -SWIGLU_LIMIT, SWIGLU_LIMIT)
            acts.append((ug * jax.nn.sigmoid(SWIGLU_ALPHA * ug) * (ul + 1.0)).astype(BF16))
        act = jnp.concatenate(acts, axis=1)
        y_ref[...] = jnp.dot(act, wdn_sc[...], preferred_element_type=F32) + bdn_ref[0]

    @pl.when(i >= n_valid)
    def _():
        y_ref[...] = jnp.zeros_like(y_ref)


def _moe_layout(top_idx, rank, counts, tm):
    N = top_idx.shape[0]
    n_tiles = N * TOP_K // tm + N_EXPERTS
    e_ids = jnp.arange(N_EXPERTS, dtype=I32)
    tiles_e = (counts + tm - 1) // tm
    tile_end = jnp.sum(jnp.where(e_ids[None, :] <= e_ids[:, None], tiles_e[None, :], 0), axis=1)
    start = (tile_end - tiles_e) * tm
    pos = rank + jnp.sum(jnp.where(top_idx[:, :, None] == e_ids[None, None, :], start[None, None, :], 0), axis=-1)
    n_valid = tile_end[N_EXPERTS - 1]
    tile_ids = jnp.arange(n_tiles, dtype=I32)
    tile_e = jnp.sum((jnp.minimum(tile_ids, n_valid - 1)[:, None] >= tile_end[None, :]).astype(I32), axis=1)
    return pos.astype(I32), tile_e.astype(I32), n_valid.astype(I32).reshape(1), n_tiles


def _glu_perm():
    pm = np.zeros((2 * LANES, 2 * LANES), np.float32)
    for j in range(LANES):
        pm[2 * j, j] = 1.0
        pm[2 * j + 1, LANES + j] = 1.0
    return jnp.asarray(pm, BF16)


def _moe(xs, tile_e, n_valid, w_up, b_up, w_down, b_down, tm=256):
    P, D = xs.shape
    n_blk = D_FF // LANES
    b_up_p = b_up.reshape(N_EXPERTS, n_blk, LANES, 2).transpose(0, 1, 3, 2).reshape(N_EXPERTS, 1, 2 * D_FF)
    b_dn = b_down.reshape(N_EXPERTS, 1, D)

    def wmap(i, te, nt):
        return (te[i], 0, 0)

    grid_spec = pltpu.PrefetchScalarGridSpec(
        num_scalar_prefetch=2,
        grid=(P // tm,),
        in_specs=[pl.BlockSpec((tm, D), lambda i, te, nt: (i, 0)),
                  pl.BlockSpec((1, D, 2 * D_FF), wmap), pl.BlockSpec((1, 1, 2 * D_FF), wmap),
                  pl.BlockSpec((1, D_FF, D), wmap), pl.BlockSpec((1, 1, D), wmap),
                  pl.BlockSpec((2 * LANES, 2 * LANES), lambda i, te, nt: (0, 0))],
        out_specs=pl.BlockSpec((tm, D), lambda i, te, nt: (i, 0)),
        scratch_shapes=[pltpu.VMEM((D, 2 * D_FF), BF16), pltpu.VMEM((D_FF, D), BF16)],
    )
    return pl.pallas_call(
        _moe_kernel,
        grid_spec=grid_spec,
        out_shape=jax.ShapeDtypeStruct((P, D), F32),
        compiler_params=_cparams(("arbitrary",)),
        name="moe",
    )(tile_e, n_valid, xs, w_up, b_up_p, w_down, b_dn, _glu_perm())


def _final_kernel(pos_ref, ys_hbm, x_ref, gate_ref, gtf_ref, g_ref, o_ref, buf, sems, *, tm, n_steps):
    i = pl.program_id(0)
    slot = i & 1

    def gather(step, dst, sem):
        def body(r, carry):
            for k in range(TOP_K):
                src_row = pos_ref[(step * tm + r) * TOP_K + k]
                pltpu.make_async_copy(ys_hbm.at[pl.ds(src_row, 1), :], dst.at[pl.ds(k * tm + r, 1), :],
                                      sem).start()
            return carry
        lax.fori_loop(0, tm, body, 0)

    @pl.when(i == 0)
    def _():
        gather(0, buf.at[0], sems.at[0])

    @pl.when(i + 1 < n_steps)
    def _():
        gather(i + 1, buf.at[1 - slot], sems.at[1 - slot])

    pltpu.make_async_copy(ys_hbm.at[pl.ds(0, TOP_K * tm), :], buf.at[slot], sems.at[slot]).wait()
    g = buf[slot]
    gate = gate_ref[...]
    y = gate[:, 0:1] * g[0:tm]
    for k in range(1, TOP_K):
        y = y + gate[:, k:k + 1] * g[k * tm:(k + 1) * tm]
    o_ref[...] = x_ref[...] + gtf_ref[0, 5:6, :] * _rms(y, g_ref[...])


def _final(ys, pos_flat, gate, x1, mod6, g_post_ffn, tm=128):
    B, S, D = x1.shape
    N = B * S
    n_steps = N // tm
    steps_per_b = S // tm
    grid_spec = pltpu.PrefetchScalarGridSpec(
        num_scalar_prefetch=1,
        grid=(n_steps,),
        in_specs=[pl.BlockSpec(memory_space=pl.ANY),
                  pl.BlockSpec((tm, D), lambda i, p: (i, 0)),
                  pl.BlockSpec((tm, LANES), lambda i, p: (i, 0)),
                  pl.BlockSpec((1, 6, D), lambda i, p: (i // steps_per_b, 0, 0)),
                  pl.BlockSpec((1, D), lambda i, p: (0, 0))],
        out_specs=pl.BlockSpec((tm, D), lambda i, p: (i, 0)),
        scratch_shapes=[pltpu.VMEM((2, TOP_K * tm, D), F32), pltpu.SemaphoreType.DMA((2,))],
    )
    out = pl.pallas_call(
        functools.partial(_final_kernel, tm=tm, n_steps=n_steps),
        grid_spec=grid_spec,
        out_shape=jax.ShapeDtypeStruct((N, D), F32),
        compiler_params=_cparams(("arbitrary",)),
        name="final",
    )(pos_flat, ys, x1.reshape(N, D), gate, mod6, g_post_ffn.reshape(1, D))
    return out.reshape(B, S, D)


def _layer(x, c, positions, w_ada, b_ada, g_pre_mix, g_post_mix, g_pre_ffn, g_post_ffn,
           w_in, cmp_pos, w_cmp_k1, w_cmp_k2, w_cmp_v1, w_cmp_v2, g_out_nsa, g_out_dil, w_o,
           w_router, b_router, w_up, b_up, w_down, b_down):
    B, S, D = x.shape
    mod6 = _ada(c, w_ada, b_ada).reshape(B, 6, D)
    (q_raw, q_rot, q_rot_t, kc, vc, ks, vs_t, kw, vw, gates, qb, kb, vb) = _proj(x, mod6, g_pre_mix, positions,
                                                                                w_in)
    kcf, vcf = _cmpmlp(kc, vc, cmp_pos, w_cmp_k1, w_cmp_k2, w_cmp_v1, w_cmp_v2)
    o_cmp, bias_t = _cmpsel(q_raw, kcf, vcf)
    o_slc = _slc(q_rot_t, bias_t, ks, vs_t)
    o_win = _band(q_rot, kw, vw, dil=1, max_dist=WIN_NSA - 1, tq=128, tk=256, nt=3, shared_kv=True,
                  with_lse=False)
    ods, lses = [], []
    for window, dil in DIL_CONFIGS:
        o, lse = _band(qb, kb, vb, dil=dil, max_dist=window // dil, tq=128, tk=128, nt=2, shared_kv=False,
                       with_lse=True)
        ods.append(o)
        lses.append(lse)
    x1 = _out(o_cmp, o_slc, o_win, gates, ods, lses, x, mod6, g_out_nsa, g_out_dil, w_o, g_post_mix)
    h2, top_idx, gate, rank, counts = _route(x1, mod6, g_pre_ffn, w_router, b_router)
    tm_moe = 256
    pos, tile_e, n_valid, n_tiles = _moe_layout(top_idx[:, :TOP_K], rank[:, :TOP_K], counts[0, :N_EXPERTS], tm_moe)
    pos_flat = pos.reshape(B * S * TOP_K)
    xs = _dispatch(h2, pos_flat, n_tiles * tm_moe)
    ys = _moe(xs, tile_e, n_valid, w_up, b_up, w_down, b_down, tm=tm_moe)
    return _final(ys, pos_flat, gate, x1, mod6, g_post_ffn)


def kernel(x, c, positions, w_ada, b_ada, g_pre_mix, g_post_mix, g_pre_ffn, g_post_ffn, w_in, cmp_pos,
           w_cmp_k1, w_cmp_k2, w_cmp_v1, w_cmp_v2, g_out_nsa, g_out_dil, w_o, w_router, b_router,
           w_up, b_up, w_down, b_down):
    depth = w_ada.shape[0]
    for l in range(depth):
        x = _layer(x, c, positions, w_ada[l], b_ada[l], g_pre_mix[l], g_post_mix[l], g_pre_ffn[l],
                   g_post_ffn[l], w_in[l], cmp_pos[l], w_cmp_k1[l], w_cmp_k2[l], w_cmp_v1[l], w_cmp_v2[l],
                   g_out_nsa[l], g_out_dil[l], w_o[l], w_router[l], b_router[l], w_up[l], b_up[l],
                   w_down[l], b_down[l])
    return x
```

```python
import functools

import numpy as np
import jax
import jax.numpy as jnp
from jax import lax
from jax.experimental import pallas as pl
from jax.experimental.pallas import tpu as pltpu

F32 = jnp.float32
BF16 = jnp.bfloat16
I32 = jnp.int32

D_MODEL = 1024
HEAD_DIM = 64
N_HEADS_NSA = 8
N_KV_NSA = 2
GROUP_NSA = 4
N_HEADS_DIL = 8
ROPE_THETA = 500000.0
ROPE_DIM = 16
ROPE_HALF = 8
CMP_BLOCK = 32
CMP_STRIDE = 16
CMP_HIDDEN = 256
SLC_BLOCK = 64
SLC_SHIFT = 6
SLC_TOPK = 16
WIN_NSA = 512
DIL_CONFIGS = ((128, 1), (512, 4), (2048, 16))
N_EXPERTS = 32
TOP_K = 4
D_FF = 1024
SWIGLU_LIMIT = 7.0
SWIGLU_ALPHA = 1.702
RMS_EPS = 1e-6
NEG_INF = -1e30
FORCE_SCORE = 1e9
SCALE = HEAD_DIM ** -0.5

Q_NSA = 512
KV_NSA = 128
Q_DIL = 512
LANES = 128
SUBLANES = 8
VMEM_LIMIT = 56 * 1024 * 1024
DMA_UNROLL = 8

_NT = (((1,), (1,)), ((), ()))


def _cparams(sem):
    return pltpu.CompilerParams(dimension_semantics=sem, vmem_limit_bytes=VMEM_LIMIT)


def _rms(x, g):
    return x * lax.rsqrt(jnp.mean(x * x, axis=-1, keepdims=True) + RMS_EPS) * g


def _split3_dot(a, e):
    hi = a.astype(BF16)
    r1 = a - hi.astype(F32)
    mid = r1.astype(BF16)
    lo = (r1 - mid.astype(F32)).astype(BF16)
    return (jnp.dot(hi, e, preferred_element_type=F32) + jnp.dot(mid, e, preferred_element_type=F32)
            + jnp.dot(lo, e, preferred_element_type=F32))


def _ada_kernel(c_ref, w_ref, b_ref, o_ref):
    c = c_ref[...]
    a = c * jax.nn.sigmoid(c)
    o_ref[...] = jnp.dot(a, w_ref[...], preferred_element_type=F32,
                         precision=lax.Precision.HIGHEST) + b_ref[...]


def _ada(c, w_ada, b_ada):
    B, D = c.shape
    n = w_ada.shape[1] // D
    return pl.pallas_call(
        _ada_kernel,
        grid=(n,),
        in_specs=[pl.BlockSpec((B, D), lambda j: (0, 0)),
                  pl.BlockSpec((D, D), lambda j: (0, j)),
                  pl.BlockSpec((1, D), lambda j: (0, j))],
        out_specs=pl.BlockSpec((B, D), lambda j: (0, j)),
        out_shape=jax.ShapeDtypeStruct((B, n * D), F32),
        compiler_params=_cparams(("arbitrary",)),
        name="ada",
    )(c, w_ada, b_ada.reshape(1, -1))


def _rope_tables(pos_col, inv_row, s1_row, s2_row):
    ang = pos_col * inv_row
    cs = jnp.cos(ang)
    sn = jnp.sin(ang)
    return cs, sn * s1_row, sn * s2_row


def _rope_blk(x, cs, s1, s2):
    return x * cs + pltpu.roll(x, LANES - ROPE_HALF, 1) * s1 + pltpu.roll(x, ROPE_HALF, 1) * s2


def _proj_kernel(x_ref, mod_ref, g_ref, pos_ref, tab_ref, wa_ref, wg_ref, wb_ref,
                 qraw_ref, qrot_ref, qrott_ref, kc_ref, vc_ref, ks_ref, vst_ref, kw_ref, vw_ref, gt_ref,
                 qb_ref, kb_ref, vb_ref):
    x = x_ref[0]
    sh = mod_ref[0, 0:1, :]
    sc = mod_ref[0, 1:2, :]
    h = (_rms(x, g_ref[...]) * (1.0 + sc) + sh).astype(BF16)
    cs, s1, s2 = _rope_tables(pos_ref[0], tab_ref[0:1, :], tab_ref[1:2, :], tab_ref[2:3, :])

    pa = jnp.dot(h, wa_ref[...], preferred_element_type=F32)
    for r in range(Q_NSA // LANES):
        blk = pa[:, r * LANES:(r + 1) * LANES]
        qraw_ref[0, :, r * LANES:(r + 1) * LANES] = (blk * SCALE).astype(BF16)
        rot = _rope_blk(blk, cs, s1, s2) * SCALE
        qrot_ref[0, :, r * LANES:(r + 1) * LANES] = rot.astype(BF16)
        qrott_ref[0, r * LANES:(r + 1) * LANES, :] = rot.T.astype(BF16)
    o = Q_NSA
    kc_ref[0] = pa[:, o:o + LANES]
    vc_ref[0] = pa[:, o + LANES:o + 2 * LANES]
    ks_ref[0] = _rope_blk(pa[:, o + 2 * LANES:o + 3 * LANES], cs, s1, s2).astype(BF16)
    vst_ref[0] = pa[:, o + 3 * LANES:o + 4 * LANES].T.astype(BF16)
    kw_ref[0] = _rope_blk(pa[:, o + 4 * LANES:o + 5 * LANES], cs, s1, s2).astype(BF16)
    vw_ref[0] = pa[:, o + 5 * LANES:o + 6 * LANES].astype(BF16)

    gt_ref[0] = jnp.dot(h, wg_ref[...], preferred_element_type=F32)

    pb = jnp.dot(h, wb_ref[...], preferred_element_type=F32)
    for r in range(Q_DIL // LANES):
        qb_ref[0, :, r * LANES:(r + 1) * LANES] = (
            _rope_blk(pb[:, r * LANES:(r + 1) * LANES], cs, s1, s2) * SCALE).astype(BF16)
        kb_ref[0, :, r * LANES:(r + 1) * LANES] = _rope_blk(
            pb[:, Q_DIL + r * LANES:Q_DIL + (r + 1) * LANES], cs, s1, s2).astype(BF16)
    vb_ref[0] = pb[:, 2 * Q_DIL:].astype(BF16)


def _nsa_perm():
    cols = []
    for r in range(GROUP_NSA):
        for hk in range(N_KV_NSA):
            hq = hk * GROUP_NSA + r
            cols.extend(range(hq * HEAD_DIM, (hq + 1) * HEAD_DIM))
    return np.asarray(cols, np.int32)


def _rope_const_table():
    half = ROPE_HALF
    inv = ROPE_THETA ** (-jnp.arange(half, dtype=F32) / half)
    lane = np.arange(LANES) % HEAD_DIM
    inv_row = jnp.where(lane < ROPE_DIM, inv[lane % half], 0.0).astype(F32)
    s1 = np.where(lane < half, -1.0, 0.0).astype(np.float32)
    s2 = np.where((lane >= half) & (lane < ROPE_DIM), 1.0, 0.0).astype(np.float32)
    tab = jnp.zeros((8, LANES), F32).at[0].set(inv_row).at[1].set(s1).at[2].set(s2)
    return tab


def _proj(x, mod6, g_pre, positions, w_in, tm=512):
    B, S, D = x.shape
    perm = _nsa_perm()
    gate_lo = Q_NSA + 6 * KV_NSA
    n_gate = 3 * N_HEADS_NSA
    w_a = jnp.concatenate([w_in[:, :Q_NSA][:, perm], w_in[:, Q_NSA:gate_lo]], axis=1).astype(BF16)
    w_g = jnp.pad(w_in[:, gate_lo:gate_lo + n_gate], ((0, 0), (0, LANES - n_gate))).astype(BF16)
    w_b = w_in[:, gate_lo + n_gate:].astype(BF16)
    pos = positions.astype(F32).reshape(B, S, 1)
    tab = _rope_const_table()
    wa_n, wb_n = w_a.shape[1], w_b.shape[1]

    def tok(width, dtype):
        return (pl.BlockSpec((1, tm, width), lambda b, i: (b, i, 0)),
                jax.ShapeDtypeStruct((B, S, width), dtype))

    def tok_t(width, dtype):
        return (pl.BlockSpec((1, width, tm), lambda b, i: (b, 0, i)),
                jax.ShapeDtypeStruct((B, width, S), dtype))

    outs = [tok(Q_NSA, BF16), tok(Q_NSA, BF16), tok_t(Q_NSA, BF16), tok(LANES, F32), tok(LANES, F32),
            tok(LANES, BF16), tok_t(LANES, BF16), tok(LANES, BF16), tok(LANES, BF16), tok(LANES, F32),
            tok(Q_DIL, BF16), tok(Q_DIL, BF16), tok(Q_DIL, BF16)]
    return pl.pallas_call(
        _proj_kernel,
        grid=(B, S // tm),
        in_specs=[pl.BlockSpec((1, tm, D), lambda b, i: (b, i, 0)),
                  pl.BlockSpec((1, 6, D), lambda b, i: (b, 0, 0)),
                  pl.BlockSpec((1, D), lambda b, i: (0, 0)),
                  pl.BlockSpec((1, tm, 1), lambda b, i: (b, i, 0)),
                  pl.BlockSpec((8, LANES), lambda b, i: (0, 0)),
                  pl.BlockSpec((D, wa_n), lambda b, i: (0, 0)),
                  pl.BlockSpec((D, LANES), lambda b, i: (0, 0)),
                  pl.BlockSpec((D, wb_n), lambda b, i: (0, 0))],
        out_specs=[o[0] for o in outs],
        out_shape=[o[1] for o in outs],
        compiler_params=_cparams(("parallel", "parallel")),
        name="proj",
    )(x, mod6, g_pre.reshape(1, D), pos, tab, w_a, w_g, w_b)


def _cmpmlp_kernel(a_ref, p_ref, w1_ref, w2_ref, o_ref):
    a = a_ref[0, 0].astype(BF16)
    w1 = w1_ref[0]
    half = CMP_STRIDE * HEAD_DIM
    u = jnp.dot(a, w1[:half], preferred_element_type=F32)
    v = jnp.dot(a, w1[half:], preferred_element_type=F32)
    bias = jnp.dot(p_ref[...], w1, preferred_element_type=F32)[0:1]
    n16 = u.shape[0]
    hid = jax.nn.gelu(u + pltpu.roll(v, n16 - 1, 0) + bias)
    o_ref[0, 0] = jnp.dot(hid.astype(BF16), w2_ref[0], preferred_element_type=F32)


def _cmpmlp(kc, vc, cmp_pos, w_k1, w_k2, w_v1, w_v2):
    B, S, _ = kc.shape
    n16 = S // CMP_STRIDE
    seg = CMP_STRIDE * HEAD_DIM

    def segs(a):
        return a.reshape(B, n16, CMP_STRIDE, N_KV_NSA, HEAD_DIM).transpose(0, 3, 1, 2, 4).reshape(
            B * N_KV_NSA, n16, seg)

    a = jnp.stack([segs(kc), segs(vc)], axis=0)
    w1 = jnp.stack([w_k1, w_v1], axis=0).astype(BF16)
    w2 = jnp.stack([w_k2, w_v2], axis=0).astype(BF16)
    p8 = jnp.broadcast_to(cmp_pos.reshape(1, CMP_BLOCK * HEAD_DIM), (8, CMP_BLOCK * HEAD_DIM)).astype(BF16)
    out = pl.pallas_call(
        _cmpmlp_kernel,
        grid=(2, B * N_KV_NSA),
        in_specs=[pl.BlockSpec((1, 1, n16, seg), lambda t, g: (t, g, 0, 0)),
                  pl.BlockSpec((8, 2 * seg), lambda t, g: (0, 0)),
                  pl.BlockSpec((1, 2 * seg, CMP_HIDDEN), lambda t, g: (t, 0, 0)),
                  pl.BlockSpec((1, CMP_HIDDEN, HEAD_DIM), lambda t, g: (t, 0, 0))],
        out_specs=pl.BlockSpec((1, 1, n16, HEAD_DIM), lambda t, g: (t, g, 0, 0)),
        out_shape=jax.ShapeDtypeStruct((2, B * N_KV_NSA, n16, HEAD_DIM), F32),
        compiler_params=_cparams(("parallel", "parallel")),
        name="cmpmlp",
    )(a, p8, w1, w2)
    out = out.reshape(2, B, N_KV_NSA, n16, HEAD_DIM).transpose(0, 1, 3, 2, 4).reshape(2, B, n16, KV_NSA)
    return out[0].astype(BF16), out[1].astype(BF16)


def _stack_heads(q, tq):
    lane = lax.broadcasted_iota(I32, (tq, LANES), 1)
    half0 = lane < HEAD_DIM
    zero = jnp.zeros((tq, LANES), q.dtype)
    rows = []
    for hk in range(N_KV_NSA):
        keep = half0 if hk == 0 else jnp.logical_not(half0)
        for r in range(GROUP_NSA):
            rows.append(jnp.where(keep, q[:, r * LANES:(r + 1) * LANES], zero))
    return jnp.concatenate(rows, axis=0), half0


def _unstack_heads(o, half0, tq, o_ref):
    for r in range(GROUP_NSA):
        o0 = o[r * tq:(r + 1) * tq]
        o1 = o[(GROUP_NSA + r) * tq:(GROUP_NSA + r + 1) * tq]
        o_ref[0, :, r * LANES:(r + 1) * LANES] = jnp.where(half0, o0, o1)


def _cmpsel_kernel(q_ref, kc_ref, vc_ref, ovt_ref, o_ref, bias_ref, *, tq):
    i = pl.program_id(1)
    qs, half0 = _stack_heads(q_ref[0], tq)
    n16 = kc_ref.shape[1]
    rows = 8 * tq
    s = lax.dot_general(qs, kc_ref[0], _NT, preferred_element_type=F32)
    t = i * tq + (lax.broadcasted_iota(I32, (rows, n16), 0) & (tq - 1))
    c = lax.broadcasted_iota(I32, (rows, n16), 1)
    valid = (c * CMP_STRIDE + (CMP_BLOCK - 1)) <= t
    s = jnp.where(valid, s, NEG_INF)
    m = jnp.max(s, axis=-1, keepdims=True)
    e = jnp.exp(s - m)
    l = jnp.sum(e, axis=-1, keepdims=True)
    p = jnp.where(valid, e / l, 0.0)
    o = jnp.dot(p.astype(BF16), vc_ref[0], preferred_element_type=F32)
    _unstack_heads(o, half0, tq, o_ref)

    n_slc = ovt_ref.shape[0]
    j = lax.broadcasted_iota(I32, (n_slc, tq), 0)
    cur = (i * tq + lax.broadcasted_iota(I32, (n_slc, tq), 1)) >> SLC_SHIFT
    forced = (j == 0) | (j == cur) | (j == cur - 1)
    ovt = ovt_ref[...]
    biases = []
    for hk in range(N_KV_NSA):
        ps = p[hk * GROUP_NSA * tq:(hk * GROUP_NSA + 1) * tq]
        for r in range(1, GROUP_NSA):
            ps = ps + p[(hk * GROUP_NSA + r) * tq:(hk * GROUP_NSA + r + 1) * tq]
        hi = ps.astype(BF16)
        lo = (ps - hi.astype(F32)).astype(BF16)
        pslc = (lax.dot_general(ovt, hi, _NT, preferred_element_type=F32)
                + lax.dot_general(ovt, lo, _NT, preferred_element_type=F32))
        score = jnp.where(forced, FORCE_SCORE, jnp.where(j <= cur, pslc, -1.0))
        rank = jnp.zeros((n_slc, tq), I32)
        for ii in range(n_slc):
            ri = score[ii:ii + 1, :]
            beats = (ri > score) | ((ri == score) & (j > ii))
            rank = rank + beats.astype(I32)
        biases.append(jnp.where(rank < SLC_TOPK, 0.0, NEG_INF))
    bias_ref[0] = jnp.concatenate(biases, axis=0).astype(BF16)


def _overlap_t(S):
    n16 = S // CMP_STRIDE
    n_slc = S // SLC_BLOCK
    cs = np.arange(n16) * CMP_STRIDE
    js = np.arange(n_slc) * SLC_BLOCK
    ov = np.clip(np.minimum(cs[:, None] + CMP_BLOCK, js[None, :] + SLC_BLOCK)
                 - np.maximum(cs[:, None], js[None, :]), 0, None).astype(np.float32) / CMP_BLOCK
    ov[n16 - 1] = 0.0
    return jnp.asarray(ov.T, BF16)


def _cmpsel(q_raw, kcf, vcf, tq=128):
    B, S, _ = q_raw.shape
    n16 = S // CMP_STRIDE
    n_slc = S // SLC_BLOCK
    assert n_slc == HEAD_DIM, "selection bias is laid out as one 64-lane half per kv head"
    return pl.pallas_call(
        functools.partial(_cmpsel_kernel, tq=tq),
        grid=(B, S // tq),
        in_specs=[pl.BlockSpec((1, tq, Q_NSA), lambda b, i: (b, i, 0)),
                  pl.BlockSpec((1, n16, KV_NSA), lambda b, i: (b, 0, 0)),
                  pl.BlockSpec((1, n16, KV_NSA), lambda b, i: (b, 0, 0)),
                  pl.BlockSpec((n_slc, n16), lambda b, i: (0, 0))],
        out_specs=[pl.BlockSpec((1, tq, Q_NSA), lambda b, i: (b, i, 0)),
                   pl.BlockSpec((1, 2 * n_slc, tq), lambda b, i: (b, 0, i))],
        out_shape=[jax.ShapeDtypeStruct((B, S, Q_NSA), F32),
                   jax.ShapeDtypeStruct((B, 2 * n_slc, S), BF16)],
        compiler_params=_cparams(("parallel", "parallel")),
        name="cmpsel",
    )(q_raw, kcf, vcf, _overlap_t(S))


def _slc_kernel(qi_ref, kj_ref, qt_ref, bt_ref, k_ref, vt_ref, o_ref, qa_sc, m_sc, l_sc, acc_sc, *, tq, tk):
    step = pl.program_id(1)
    i = qi_ref[step]
    kj = kj_ref[step]
    last = (i * tq + tq - 1) // tk
    cols = 8 * tq

    @pl.when(kj == 0)
    def _():
        half0 = lax.broadcasted_iota(I32, (LANES, tq), 0) < HEAD_DIM
        zero = jnp.zeros((LANES, tq), BF16)
        bt = bt_ref[0]
        groups = []
        for hk in range(N_KV_NSA):
            keep = half0 if hk == 0 else jnp.logical_not(half0)
            bh = jnp.where(keep, bt, zero)
            for r in range(GROUP_NSA):
                qb = qt_ref[0, r * LANES:(r + 1) * LANES, :]
                groups.append(jnp.concatenate([jnp.where(keep, qb, zero), bh], axis=0))
        qa_sc[...] = jnp.concatenate(groups, axis=1)
        m_sc[...] = jnp.full((1, cols), NEG_INF, F32)
        l_sc[...] = jnp.zeros((1, cols), F32)
        acc_sc[...] = jnp.zeros((LANES, cols), F32)

    def update(on_diagonal):
        kblk = (kj * tk + lax.broadcasted_iota(I32, (tk, LANES), 0)) >> SLC_SHIFT
        lane = lax.broadcasted_iota(I32, (tk, LANES), 1) & (HEAD_DIM - 1)
        onehot = jnp.where(kblk == lane, 1.0, 0.0).astype(BF16)
        kaug = jnp.concatenate([k_ref[0], onehot], axis=1)
        st = jnp.dot(kaug, qa_sc[...], preferred_element_type=F32)
        if on_diagonal:
            kpos = kj * tk + lax.broadcasted_iota(I32, (tk, cols), 0)
            t = i * tq + (lax.broadcasted_iota(I32, (tk, cols), 1) & (tq - 1))
            st = jnp.where(kpos <= t, st, NEG_INF)
        m_old = m_sc[...]
        m_new = jnp.maximum(m_old, jnp.max(st, axis=0, keepdims=True))
        alpha = jnp.exp(m_old - m_new)
        p = jnp.exp(st - m_new)
        l_sc[...] = alpha * l_sc[...] + jnp.sum(p, axis=0, keepdims=True)
        acc_sc[...] = alpha * acc_sc[...] + jnp.dot(vt_ref[0], p.astype(BF16), preferred_element_type=F32)
        m_sc[...] = m_new

    @pl.when(kj < last)
    def _():
        update(False)

    @pl.when(kj == last)
    def _():
        update(True)
        ot = acc_sc[...] / l_sc[...]
        half0 = lax.broadcasted_iota(I32, (tq, LANES), 1) < HEAD_DIM
        for r in range(GROUP_NSA):
            o0 = ot[:, r * tq:(r + 1) * tq].T
            o1 = ot[:, (GROUP_NSA + r) * tq:(GROUP_NSA + r + 1) * tq].T
            o_ref[0, :, r * LANES:(r + 1) * LANES] = jnp.where(half0, o0, o1)


def _slc(q_rot_t, bias_t, ks, vs_t, tq=128, tk=512):
    B, _, S = q_rot_t.shape
    assert tk % tq == 0
    qi, kj = [], []
    for i in range(S // tq):
        for j in range((i * tq + tq - 1) // tk + 1):
            qi.append(i)
            kj.append(j)
    grid_spec = pltpu.PrefetchScalarGridSpec(
        num_scalar_prefetch=2,
        grid=(B, len(qi)),
        in_specs=[pl.BlockSpec((1, Q_NSA, tq), lambda b, s, qi, kj: (b, 0, qi[s])),
                  pl.BlockSpec((1, LANES, tq), lambda b, s, qi, kj: (b, 0, qi[s])),
                  pl.BlockSpec((1, tk, KV_NSA), lambda b, s, qi, kj: (b, kj[s], 0)),
                  pl.BlockSpec((1, KV_NSA, tk), lambda b, s, qi, kj: (b, 0, kj[s]))],
        out_specs=pl.BlockSpec((1, tq, Q_NSA), lambda b, s, qi, kj: (b, qi[s], 0)),
        scratch_shapes=[pltpu.VMEM((2 * LANES, 8 * tq), BF16),
                        pltpu.VMEM((1, 8 * tq), F32),
                        pltpu.VMEM((1, 8 * tq), F32),
                        pltpu.VMEM((LANES, 8 * tq), F32)],
    )
    return pl.pallas_call(
        functools.partial(_slc_kernel, tq=tq, tk=tk),
        grid_spec=grid_spec,
        out_shape=jax.ShapeDtypeStruct((B, S, Q_NSA), F32),
        compiler_params=_cparams(("parallel", "arbitrary")),
        name="slc",
    )(jnp.asarray(qi, I32), jnp.asarray(kj, I32), q_rot_t, bias_t, ks, vs_t)


def _band_kernel(*refs, tq, tk, nt, max_dist, shared_kv, with_lse):
    q_ref = refs[0]
    k_refs = refs[1:1 + nt]
    v_refs = refs[1 + nt:1 + 2 * nt]
    o_ref = refs[1 + 2 * nt]
    lse_ref = refs[2 + 2 * nt] if with_lse else None
    i = pl.program_id(2)
    a = (i * tq) // tk
    lane = lax.broadcasted_iota(I32, (tq, LANES), 1)
    half0 = lane < HEAD_DIM
    row_t = i * tq + (lax.broadcasted_iota(I32, (2 * tq, tk), 0) & (tq - 1))
    col = lax.broadcasted_iota(I32, (2 * tq, tk), 1)
    oks = []
    for jt in range(nt):
        kpos = (a - (nt - 1) + jt) * tk + col
        d = row_t - kpos
        oks.append((d >= 0) & (d <= max_dist) & (kpos >= 0))
    lse_acc = jnp.zeros((tq, LANES), F32)
    for blk in range(Q_NSA // LANES):
        qb = q_ref[0, :, blk * LANES:(blk + 1) * LANES]
        zero = jnp.zeros_like(qb)
        qs = jnp.concatenate([jnp.where(half0, qb, zero), jnp.where(half0, zero, qb)], axis=0)
        kv0 = 0 if shared_kv else blk * LANES
        ss = []
        for jt in range(nt):
            kt = k_refs[jt][0, :, kv0:kv0 + LANES]
            s = lax.dot_general(qs, kt, _NT, preferred_element_type=F32)
            ss.append(jnp.where(oks[jt], s, NEG_INF))
        m = jnp.max(ss[0], axis=-1, keepdims=True)
        for s in ss[1:]:
            m = jnp.maximum(m, jnp.max(s, axis=-1, keepdims=True))
        l = jnp.zeros((2 * tq, 1), F32)
        o = jnp.zeros((2 * tq, LANES), F32)
        for jt in range(nt):
            p = jnp.exp(ss[jt] - m)
            l = l + jnp.sum(p, axis=-1, keepdims=True)
            o = o + jnp.dot(p.astype(BF16), v_refs[jt][0, :, kv0:kv0 + LANES], preferred_element_type=F32)
        o = o / l
        o_ref[0, :, blk * LANES:(blk + 1) * LANES] = jnp.where(half0, o[:tq], o[tq:])
        if with_lse:
            lse = m + jnp.log(l)
            lse_acc = jnp.where(lane == 2 * blk, lse[:tq], lse_acc)
            lse_acc = jnp.where(lane == 2 * blk + 1, lse[tq:], lse_acc)
    if with_lse:
        lse_ref[0] = lse_acc


def _band(q, k, v, *, dil, max_dist, tq, tk, nt, shared_kv, with_lse):
    B, S, _ = q.shape
    m = S // dil
    kvw = k.shape[-1]
    qv = q.reshape(B, m, dil * Q_NSA)
    kv_ = k.reshape(B, m, dil * kvw)
    vv = v.reshape(B, m, dil * kvw)

    def kv_map(jt):
        return lambda b, r, i: (b, jnp.maximum((i * tq) // tk - (nt - 1) + jt, 0), r)

    tok_spec = pl.BlockSpec((1, tq, Q_NSA), lambda b, r, i: (b, i, r))
    in_specs = [tok_spec]
    in_specs += [pl.BlockSpec((1, tk, kvw), kv_map(jt)) for jt in range(nt)]
    in_specs += [pl.BlockSpec((1, tk, kvw), kv_map(jt)) for jt in range(nt)]
    out_specs = [tok_spec]
    out_shape = [jax.ShapeDtypeStruct((B, m, dil * Q_NSA), F32)]
    if with_lse:
        out_specs.append(pl.BlockSpec((1, tq, LANES), lambda b, r, i: (b, i, r)))
        out_shape.append(jax.ShapeDtypeStruct((B, m, dil * LANES), F32))
    res = pl.pallas_call(
        functools.partial(_band_kernel, tq=tq, tk=tk, nt=nt, max_dist=max_dist, shared_kv=shared_kv,
                          with_lse=with_lse),
        grid=(B, dil, m // tq),
        in_specs=in_specs,
        out_specs=out_specs,
        out_shape=out_shape,
        compiler_params=_cparams(("parallel", "parallel", "parallel")),
        name=f"band_d{dil}_w{max_dist}",
    )(qv, *([kv_] * nt), *([vv] * nt))
    o = res[0].reshape(B, S, Q_NSA)
    if with_lse:
        return o, res[1].reshape(B, S, LANES)
    return o


def _out_kernel(ocmp_ref, oslc_ref, owin_ref, gt_ref, od1_ref, od4_ref, od16_ref, l1_ref, l4_ref, l16_ref,
                x_ref, mod_ref, eg_ref, ed_ref, gnsa_ref, gdil_ref, wo_ref, gpost_ref, o_ref):
    sg = jax.nn.sigmoid(gt_ref[0])
    oa = (_split3_dot(sg, eg_ref[0]) * ocmp_ref[0] + _split3_dot(sg, eg_ref[1]) * oslc_ref[0]
          + _split3_dot(sg, eg_ref[2]) * owin_ref[0])
    ya = _rms(oa, gnsa_ref[...])

    l1, l4, l16 = l1_ref[0], l4_ref[0], l16_ref[0]
    mx = jnp.maximum(jnp.maximum(l1, l4), l16)
    e1, e4, e16 = jnp.exp(l1 - mx), jnp.exp(l4 - mx), jnp.exp(l16 - mx)
    den = e1 + e4 + e16
    ed = ed_ref[...]
    ob = (_split3_dot(e1 / den, ed) * od1_ref[0] + _split3_dot(e4 / den, ed) * od4_ref[0]
          + _split3_dot(e16 / den, ed) * od16_ref[0])
    yb = _rms(ob, gdil_ref[...])

    y = jnp.concatenate([ya, yb], axis=1).astype(BF16)
    z = jnp.dot(y, wo_ref[...], preferred_element_type=F32)
    gt_m = mod_ref[0, 2:3, :]
    o_ref[0] = x_ref[0] + gt_m * _rms(z, gpost_ref[...])


def _gate_expanders():
    perm = _nsa_perm()
    eg = np.zeros((3, LANES, Q_NSA), np.float32)
    for lane_out, col in enumerate(perm):
        hq = col // HEAD_DIM
        for c in range(3):
            eg[c, hq * 3 + c, lane_out] = 1.0
    ed = np.zeros((LANES, Q_DIL), np.float32)
    for h in range(N_HEADS_DIL):
        ed[h, h * HEAD_DIM:(h + 1) * HEAD_DIM] = 1.0
    return jnp.asarray(eg, BF16), jnp.asarray(ed, BF16)


def _out(o_cmp, o_slc, o_win, gates, ods, lses, x, mod6, g_out_nsa, g_out_dil, w_o, g_post, tm=256):
    B, S, D = x.shape
    perm = _nsa_perm()
    eg, ed = _gate_expanders()
    w_o_p = jnp.concatenate([w_o[:Q_NSA][perm], w_o[Q_NSA:]], axis=0).astype(BF16)
    g_nsa_p = g_out_nsa[perm].reshape(1, Q_NSA)

    def tok(width):
        return pl.BlockSpec((1, tm, width), lambda b, i: (b, i, 0))

    def const(shape):
        return pl.BlockSpec(shape, lambda b, i: (0,) * len(shape))

    return pl.pallas_call(
        _out_kernel,
        grid=(B, S // tm),
        in_specs=[tok(Q_NSA), tok(Q_NSA), tok(Q_NSA), tok(LANES), tok(Q_DIL), tok(Q_DIL), tok(Q_DIL),
                  tok(LANES), tok(LANES), tok(LANES), tok(D),
                  pl.BlockSpec((1, 6, D), lambda b, i: (b, 0, 0)),
                  const((3, LANES, Q_NSA)), const((LANES, Q_DIL)), const((1, Q_NSA)), const((1, Q_DIL)),
                  const((D, D)), const((1, D))],
        out_specs=tok(D),
        out_shape=jax.ShapeDtypeStruct((B, S, D), F32),
        compiler_params=_cparams(("parallel", "parallel")),
        name="out",
    )(o_cmp, o_slc, o_win, gates, *ods, *lses, x, mod6, eg, ed, g_nsa_p, g_out_dil.reshape(1, Q_DIL),
      w_o_p, g_post.reshape(1, D))


def _route_kernel(x_ref, mod_ref, g_ref, wr_ref, br_ref, h_ref, idx_ref, gate_ref, rank_ref, cnt_ref, run_sc):
    i = pl.program_id(0)

    @pl.when(i == 0)
    def _():
        run_sc[...] = jnp.zeros_like(run_sc)

    sh = mod_ref[0, 3:4, :]
    sc = mod_ref[0, 4:5, :]
    h = _rms(x_ref[...], g_ref[...]) * (1.0 + sc) + sh
    h_ref[...] = h
    logits = jnp.dot(h, wr_ref[...], preferred_element_type=F32,
                     precision=lax.Precision.HIGHEST) + br_ref[...]
    tm = logits.shape[0]
    lane = lax.broadcasted_iota(I32, (tm, LANES), 1)
    work = jnp.where(lane < N_EXPERTS, logits, -jnp.inf)
    idx_out = jnp.zeros((tm, LANES), I32)
    val_out = jnp.full((tm, LANES), -jnp.inf, F32)
    rank_out = jnp.zeros((tm, LANES), F32)
    tri = jnp.where(lax.broadcasted_iota(I32, (tm, tm), 1) < lax.broadcasted_iota(I32, (tm, tm), 0),
                    1.0, 0.0).astype(BF16)
    base = run_sc[0:1, :]
    for k in range(TOP_K):
        mx = jnp.max(work, axis=-1, keepdims=True)
        ix = jnp.min(jnp.where(work == mx, lane, LANES), axis=-1, keepdims=True)
        hit = lane == ix
        idx_out = jnp.where(lane == k, ix, idx_out)
        val_out = jnp.where(lane == k, mx, val_out)
        work = jnp.where(hit, -jnp.inf, work)
        onehot = jnp.where(hit, 1.0, 0.0)
        before = jnp.dot(tri, onehot.astype(BF16), preferred_element_type=F32) + base
        rank_k = jnp.sum(onehot * before, axis=-1, keepdims=True)
        rank_out = jnp.where(lane == k, rank_k, rank_out)
        base = base + jnp.sum(onehot, axis=0, keepdims=True)
    run_sc[...] = jnp.broadcast_to(base, run_sc.shape)
    e = jnp.exp(val_out - val_out[:, 0:1])
    idx_ref[...] = idx_out
    gate_ref[...] = e / jnp.sum(e, axis=-1, keepdims=True)
    rank_ref[...] = rank_out.astype(I32)
    cnt_ref[...] = run_sc[...].astype(I32)


def _route(x1, mod6, g_pre_ffn, w_router, b_router, tm=512):
    B, S, D = x1.shape
    N = B * S
    spb = S // tm
    wr = jnp.pad(w_router, ((0, 0), (0, LANES - N_EXPERTS)))
    br = jnp.pad(b_router, (0, LANES - N_EXPERTS)).reshape(1, LANES)

    def tok(width):
        return pl.BlockSpec((tm, width), lambda i: (i, 0))

    return pl.pallas_call(
        _route_kernel,
        grid=(N // tm,),
        in_specs=[tok(D), pl.BlockSpec((1, 6, D), lambda i: (i // spb, 0, 0)),
                  pl.BlockSpec((1, D), lambda i: (0, 0)),
                  pl.BlockSpec((D, LANES), lambda i: (0, 0)),
                  pl.BlockSpec((1, LANES), lambda i: (0, 0))],
        out_specs=[tok(D), tok(LANES), tok(LANES), tok(LANES), pl.BlockSpec((8, LANES), lambda i: (0, 0))],
        out_shape=[jax.ShapeDtypeStruct((N, D), F32),
                   jax.ShapeDtypeStruct((N, LANES), I32),
                   jax.ShapeDtypeStruct((N, LANES), F32),
                   jax.ShapeDtypeStruct((N, LANES), I32),
                   jax.ShapeDtypeStruct((8, LANES), I32)],
        scratch_shapes=[pltpu.VMEM((8, LANES), F32)],
        compiler_params=_cparams(("arbitrary",)),
        name="route",
    )(x1.reshape(N, D), mod6, g_pre_ffn.reshape(1, D), wr, br)


def _dispatch_kernel(pos_ref, pad0_ref, padn_ref, nt_ref, h_ref, xs_out, zbuf, sem, zsem, *, tm, tmx, n_tiles):
    i = pl.program_id(0)
    bits = tmx.bit_length() - 1

    def pad_copies(e, fn):
        p0 = pad0_ref[e]
        head = (-p0) & (SUBLANES - 1)
        head = jnp.minimum(head, padn_ref[e])
        for r in range(SUBLANES - 1):
            @pl.when(r < head)
            def _():
                fn(pltpu.make_async_copy(zbuf.at[pl.ds(0, 1), :], xs_out.at[pl.ds(p0 + r, 1), :], zsem))
        a = p0 + head
        n = padn_ref[e] - head
        for b in range(3, bits):
            size = 1 << b
            off = (n >> (b + 1)) << (b + 1)

            @pl.when((n & size) != 0)
            def _():
                fn(pltpu.make_async_copy(zbuf.at[pl.ds(0, size), :],
                                         xs_out.at[pl.ds(pl.multiple_of(a + off, SUBLANES), size), :], zsem))

    def tail_copy(t, fn):
        fn(pltpu.make_async_copy(zbuf, xs_out.at[pl.ds(pl.multiple_of(t * tmx, tmx), tmx), :], zsem))

    def for_all_fill(fn):
        def per_expert(e, carry):
            pad_copies(e, fn)
            return carry
        lax.fori_loop(0, N_EXPERTS, per_expert, 0)

        def per_tile(t, carry):
            tail_copy(t, fn)
            return carry
        lax.fori_loop(nt_ref[0], n_tiles, per_tile, 0)

    @pl.when(i == 0)
    def _():
        zbuf[...] = jnp.zeros_like(zbuf)
        for_all_fill(lambda cp: cp.start())
        for_all_fill(lambda cp: cp.wait())

    def body(r, carry):
        for k in range(TOP_K):
            dst_row = pos_ref[(i * tm + r) * TOP_K + k]
            pltpu.make_async_copy(h_ref.at[pl.ds(r, 1), :], xs_out.at[pl.ds(dst_row, 1), :], sem).start()
        return carry

    lax.fori_loop(0, tm, body, 0, unroll=DMA_UNROLL // TOP_K)
    for k in range(TOP_K):
        pltpu.make_async_copy(h_ref, xs_out.at[pl.ds(0, tm), :], sem).wait()


def _dispatch(h2, pos_flat, pad_start, pad_len, n_valid, n_tiles, tmx, tm=256):
    N, D = h2.shape
    assert tmx & (tmx - 1) == 0
    grid_spec = pltpu.PrefetchScalarGridSpec(
        num_scalar_prefetch=4,
        grid=(N // tm,),
        in_specs=[pl.BlockSpec((tm, D), lambda i, *_: (i, 0))],
        out_specs=pl.BlockSpec(memory_space=pl.ANY),
        scratch_shapes=[pltpu.VMEM((tmx, D), F32), pltpu.SemaphoreType.DMA(()), pltpu.SemaphoreType.DMA(())],
    )
    return pl.pallas_call(
        functools.partial(_dispatch_kernel, tm=tm, tmx=tmx, n_tiles=n_tiles),
        grid_spec=grid_spec,
        out_shape=jax.ShapeDtypeStruct((n_tiles * tmx, D), F32),
        compiler_params=_cparams(("arbitrary",)),
        name="dispatch",
    )(pos_flat, pad_start, pad_len, n_valid, h2)


def _moe_kernel(te_ref, nt_ref, nx_ref, sl_ref, xs_ref, wup_hbm, bup_ref, wdn_hbm, bdn_ref, pm_ref, y_ref,
                wup_in, wdn_in, wup_sc, wdn_sc, sems):
    i = pl.program_id(0)
    n_valid = nt_ref[0]
    e = te_ref[i]
    new_expert = (i == 0) | (e != te_ref[jnp.maximum(i - 1, 0)])
    slot = sl_ref[i]
    n_blk = wup_sc.shape[1] // (2 * LANES)

    def weight_copies(ex, s):
        return (pltpu.make_async_copy(wup_hbm.at[ex], wup_in.at[s], sems.at[0, s]),
                pltpu.make_async_copy(wdn_hbm.at[ex], wdn_in.at[s], sems.at[1, s]))

    @pl.when(i == 0)
    def _():
        for cp in weight_copies(e, slot):
            cp.start()

    @pl.when(new_expert & (i < n_valid))
    def _():
        for cp in weight_copies(e, slot):
            cp.wait()
        nx = nx_ref[i]

        @pl.when(nx >= 0)
        def _():
            for cp in weight_copies(nx, 1 - slot):
                cp.start()

        pm = pm_ref[...]
        for blk in range(n_blk):
            cols = slice(blk * 2 * LANES, (blk + 1) * 2 * LANES)
            w = wup_in[slot, :, cols].astype(BF16)
            wup_sc[:, cols] = jnp.dot(w, pm, preferred_element_type=F32).astype(BF16)
        wdn_sc[...] = wdn_in[slot].astype(BF16)

    @pl.when(i < n_valid)
    def _():
        xs = xs_ref[...].astype(BF16)
        u = jnp.dot(xs, wup_sc[...], preferred_element_type=F32) + bup_ref[0]
        acts = []
        for blk in range(n_blk):
            ug = jnp.minimum(u[:, blk * 2 * LANES:blk * 2 * LANES + LANES], SWIGLU_LIMIT)
            ul = jnp.clip(u[:, blk * 2 * LANES + LANES:(blk + 1) * 2 * LANES], -SWIGLU_LIMIT, SWIGLU_LIMIT)
            acts.append((ug * jax.nn.sigmoid(SWIGLU_ALPHA * ug) * (ul + 1.0)).astype(BF16))
        act = jnp.concatenate(acts, axis=1)
        y_ref[...] = jnp.dot(act, wdn_sc[...], preferred_element_type=F32) + bdn_ref[0]

    @pl.when(i >= n_valid)
    def _():
        y_ref[...] = jnp.zeros_like(y_ref)


def _moe_layout(top_idx, rank, counts, tm):
    N = top_idx.shape[0]
    n_tiles = N * TOP_K // tm + N_EXPERTS
    e_ids = jnp.arange(N_EXPERTS, dtype=I32)
    tiles_e = (counts + tm - 1) // tm
    tile_end = jnp.sum(jnp.where(e_ids[None, :] <= e_ids[:, None], tiles_e[None, :], 0), axis=1)
    start = (tile_end - tiles_e) * tm
    pos = rank + jnp.sum(jnp.where(top_idx[:, :, None] == e_ids[None, None, :], start[None, None, :], 0), axis=-1)
    n_valid = tile_end[N_EXPERTS - 1]
    tile_ids = jnp.arange(n_tiles, dtype=I32)
    tile_e = jnp.sum((jnp.minimum(tile_ids, n_valid - 1)[:, None] >= tile_end[None, :]).astype(I32), axis=1)
    prev_e = jnp.concatenate([jnp.full((1,), -1, I32), tile_e[:-1]])
    new = (tile_e != prev_e) & (tile_ids < n_valid)
    ordinal = jnp.sum(jnp.where(tile_ids[None, :] <= tile_ids[:, None], new[None, :].astype(I32), 0), axis=1) - 1
    next_first = jnp.sum(jnp.where(tile_e[:, None] == e_ids[None, :], tile_end[None, :], 0), axis=1)
    next_e = jnp.sum(jnp.where(next_first[:, None] == tile_ids[None, :], tile_e[None, :], 0), axis=1)
    next_e = jnp.where(next_first < n_valid, next_e, -1)
    meta = dict(tile_e=tile_e.astype(I32), n_valid=n_valid.astype(I32).reshape(1), next_e=next_e.astype(I32),
                slot=(ordinal & 1).astype(I32), pad_start=(start + counts).astype(I32),
                pad_len=(tiles_e * tm - counts).astype(I32))
    return pos.astype(I32), meta, n_tiles


def _glu_perm():
    pm = np.zeros((2 * LANES, 2 * LANES), np.float32)
    for j in range(LANES):
        pm[2 * j, j] = 1.0
        pm[2 * j + 1, LANES + j] = 1.0
    return jnp.asarray(pm, BF16)


def _moe(xs, meta, w_up, b_up, w_down, b_down, tm):
    P, D = xs.shape
    n_blk = D_FF // LANES
    b_up_p = b_up.reshape(N_EXPERTS, n_blk, LANES, 2).transpose(0, 1, 3, 2).reshape(N_EXPERTS, 1, 2 * D_FF)
    b_dn = b_down.reshape(N_EXPERTS, 1, D)

    def bmap(i, te, *_):
        return (te[i], 0, 0)

    grid_spec = pltpu.PrefetchScalarGridSpec(
        num_scalar_prefetch=4,
        grid=(P // tm,),
        in_specs=[pl.BlockSpec((tm, D), lambda i, *_: (i, 0)),
                  pl.BlockSpec(memory_space=pl.ANY), pl.BlockSpec((1, 1, 2 * D_FF), bmap),
                  pl.BlockSpec(memory_space=pl.ANY), pl.BlockSpec((1, 1, D), bmap),
                  pl.BlockSpec((2 * LANES, 2 * LANES), lambda i, *_: (0, 0))],
        out_specs=pl.BlockSpec((tm, D), lambda i, *_: (i, 0)),
        scratch_shapes=[pltpu.VMEM((2, D, 2 * D_FF), F32), pltpu.VMEM((2, D_FF, D), F32),
                        pltpu.VMEM((D, 2 * D_FF), BF16), pltpu.VMEM((D_FF, D), BF16),
                        pltpu.SemaphoreType.DMA((2, 2))],
    )
    return pl.pallas_call(
        _moe_kernel,
        grid_spec=grid_spec,
        out_shape=jax.ShapeDtypeStruct((P, D), F32),
        compiler_params=_cparams(("arbitrary",)),
        name="moe",
    )(meta["tile_e"], meta["n_valid"], meta["next_e"], meta["slot"], xs, w_up, b_up_p, w_down, b_dn, _glu_perm())


def _final_kernel(pos_ref, ys_hbm, x_ref, gate_ref, gtf_ref, g_ref, o_ref, buf, sems, *, tm, n_steps):
    i = pl.program_id(0)
    slot = i & 1

    n = TOP_K * tm

    def gather(step, dst, sem):
        def body(j, carry):
            pltpu.make_async_copy(ys_hbm.at[pl.ds(pos_ref[step * n + j], 1), :], dst.at[pl.ds(j, 1), :],
                                  sem).start()
            return carry
        lax.fori_loop(0, n, body, 0, unroll=DMA_UNROLL)

    @pl.when(i == 0)
    def _():
        gather(0, buf.at[0], sems.at[0])

    @pl.when(i + 1 < n_steps)
    def _():
        gather(i + 1, buf.at[1 - slot], sems.at[1 - slot])

    pltpu.make_async_copy(ys_hbm.at[pl.ds(0, n), :], buf.at[slot], sems.at[slot]).wait()
    gate = gate_ref[...]
    y = gate[:, 0:1] * buf[slot, 0:tm, :]
    for k in range(1, TOP_K):
        y = y + gate[:, k:k + 1] * buf[slot, k * tm:(k + 1) * tm, :]
    o_ref[...] = x_ref[...] + gtf_ref[0, 5:6, :] * _rms(y, g_ref[...])


def _final(ys, pos_flat, gate, x1, mod6, g_post_ffn, tm=128):
    B, S, D = x1.shape
    N = B * S
    n_steps = N // tm
    steps_per_b = S // tm
    pos_sm = pos_flat.reshape(n_steps, tm, TOP_K).transpose(0, 2, 1).reshape(N * TOP_K)
    grid_spec = pltpu.PrefetchScalarGridSpec(
        num_scalar_prefetch=1,
        grid=(n_steps,),
        in_specs=[pl.BlockSpec(memory_space=pl.ANY),
                  pl.BlockSpec((tm, D), lambda i, p: (i, 0)),
                  pl.BlockSpec((tm, LANES), lambda i, p: (i, 0)),
                  pl.BlockSpec((1, 6, D), lambda i, p: (i // steps_per_b, 0, 0)),
                  pl.BlockSpec((1, D), lambda i, p: (0, 0))],
        out_specs=pl.BlockSpec((tm, D), lambda i, p: (i, 0)),
        scratch_shapes=[pltpu.VMEM((2, TOP_K * tm, D), F32), pltpu.SemaphoreType.DMA((2,))],
    )
    out = pl.pallas_call(
        functools.partial(_final_kernel, tm=tm, n_steps=n_steps),
        grid_spec=grid_spec,
        out_shape=jax.ShapeDtypeStruct((N, D), F32),
        compiler_params=_cparams(("arbitrary",)),
        name="final",
    )(pos_sm, ys, x1.reshape(N, D), gate, mod6, g_post_ffn.reshape(1, D))
    return out.reshape(B, S, D)


def _layer(x, c, positions, w_ada, b_ada, g_pre_mix, g_post_mix, g_pre_ffn, g_post_ffn,
           w_in, cmp_pos, w_cmp_k1, w_cmp_k2, w_cmp_v1, w_cmp_v2, g_out_nsa, g_out_dil, w_o,
           w_router, b_router, w_up, b_up, w_down, b_down):
    B, S, D = x.shape
    mod6 = _ada(c, w_ada, b_ada).reshape(B, 6, D)
    (q_raw, q_rot, q_rot_t, kc, vc, ks, vs_t, kw, vw, gates, qb, kb, vb) = _proj(x, mod6, g_pre_mix, positions,
                                                                                w_in)
    kcf, vcf = _cmpmlp(kc, vc, cmp_pos, w_cmp_k1, w_cmp_k2, w_cmp_v1, w_cmp_v2)
    o_cmp, bias_t = _cmpsel(q_raw, kcf, vcf)
    o_slc = _slc(q_rot_t, bias_t, ks, vs_t)
    o_win = _band(q_rot, kw, vw, dil=1, max_dist=WIN_NSA - 1, tq=128, tk=256, nt=3, shared_kv=True,
                  with_lse=False)
    ods, lses = [], []
    for window, dil in DIL_CONFIGS:
        o, lse = _band(qb, kb, vb, dil=dil, max_dist=window // dil, tq=128, tk=128, nt=2, shared_kv=False,
                       with_lse=True)
        ods.append(o)
        lses.append(lse)
    x1 = _out(o_cmp, o_slc, o_win, gates, ods, lses, x, mod6, g_out_nsa, g_out_dil, w_o, g_post_mix)
    h2, top_idx, gate, rank, counts = _route(x1, mod6, g_pre_ffn, w_router, b_router)
    tm_moe = 512
    pos, meta, n_tiles = _moe_layout(top_idx[:, :TOP_K], rank[:, :TOP_K], counts[0, :N_EXPERTS], tm_moe)
    pos_flat = pos.reshape(B * S * TOP_K)
    xs = _dispatch(h2, pos_flat, meta["pad_start"], meta["pad_len"], meta["n_valid"], n_tiles, tm_moe)
    ys = _moe(xs, meta, w_up, b_up, w_down, b_down, tm_moe)
    return _final(ys, pos_flat, gate, x1, mod6, g_post_ffn)


def kernel(x, c, positions, w_ada, b_ada, g_pre_mix, g_post_mix, g_pre_ffn, g_post_ffn, w_in, cmp_pos,
           w_cmp_k1, w_cmp_k2, w_cmp_v1, w_cmp_v2, g_out_nsa, g_out_dil, w_o, w_router, b_router,
           w_up, b_up, w_down, b_down):
    depth = w_ada.shape[0]
    for l in range(depth):
        x = _layer(x, c, positions, w_ada[l], b_ada[l], g_pre_mix[l], g_post_mix[l], g_pre_ffn[l],
                   g_post_ffn[l], w_in[l], cmp_pos[l], w_cmp_k1[l], w_cmp_k2[l], w_cmp_v1[l], w_cmp_v2[l],
                   g_out_nsa[l], g_out_dil[l], w_o[l], w_router[l], b_router[l], w_up[l], b_up[l],
                   w_down[l], b_down[l])
    return x
```

```python
import functools

import numpy as np
import jax
import jax.numpy as jnp
from jax import lax
from jax.experimental import pallas as pl
from jax.experimental.pallas import tpu as pltpu

F32 = jnp.float32
BF16 = jnp.bfloat16
I32 = jnp.int32

D_MODEL = 1024
HEAD_DIM = 64
N_HEADS_NSA = 8
N_KV_NSA = 2
GROUP_NSA = 4
N_HEADS_DIL = 8
ROPE_THETA = 500000.0
ROPE_DIM = 16
ROPE_HALF = 8
CMP_BLOCK = 32
CMP_STRIDE = 16
CMP_HIDDEN = 256
SLC_BLOCK = 64
SLC_SHIFT = 6
SLC_TOPK = 16
WIN_NSA = 512
DIL_CONFIGS = ((128, 1), (512, 4), (2048, 16))
N_EXPERTS = 32
TOP_K = 4
D_FF = 1024
SWIGLU_LIMIT = 7.0
SWIGLU_ALPHA = 1.702
RMS_EPS = 1e-6
NEG_INF = -1e30
FORCE_SCORE = 1e9
SCALE = HEAD_DIM ** -0.5

Q_NSA = 512
KV_NSA = 128
Q_DIL = 512
LANES = 128
SUBLANES = 8
VMEM_LIMIT = 56 * 1024 * 1024
DMA_UNROLL = 8

_NT = (((1,), (1,)), ((), ()))


def _cparams(sem):
    return pltpu.CompilerParams(dimension_semantics=sem, vmem_limit_bytes=VMEM_LIMIT)


def _rms(x, g):
    return x * lax.rsqrt(jnp.mean(x * x, axis=-1, keepdims=True) + RMS_EPS) * g


def _split3_dot(a, e):
    hi = a.astype(BF16)
    r1 = a - hi.astype(F32)
    mid = r1.astype(BF16)
    lo = (r1 - mid.astype(F32)).astype(BF16)
    return (jnp.dot(hi, e, preferred_element_type=F32) + jnp.dot(mid, e, preferred_element_type=F32)
            + jnp.dot(lo, e, preferred_element_type=F32))


def _ada_kernel(c_ref, w_ref, b_ref, o_ref):
    c = c_ref[...]
    a = c * jax.nn.sigmoid(c)
    o_ref[...] = jnp.dot(a, w_ref[...], preferred_element_type=F32,
                         precision=lax.Precision.HIGHEST) + b_ref[...]


def _ada(c, w_ada, b_ada):
    B, D = c.shape
    n = w_ada.shape[1] // D
    return pl.pallas_call(
        _ada_kernel,
        grid=(n,),
        in_specs=[pl.BlockSpec((B, D), lambda j: (0, 0)),
                  pl.BlockSpec((D, D), lambda j: (0, j)),
                  pl.BlockSpec((1, D), lambda j: (0, j))],
        out_specs=pl.BlockSpec((B, D), lambda j: (0, j)),
        out_shape=jax.ShapeDtypeStruct((B, n * D), F32),
        compiler_params=_cparams(("arbitrary",)),
        name="ada",
    )(c, w_ada, b_ada.reshape(1, -1))


def _rope_tables(pos_col, inv_row, s1_row, s2_row):
    ang = pos_col * inv_row
    cs = jnp.cos(ang)
    sn = jnp.sin(ang)
    return cs, sn * s1_row, sn * s2_row


def _rope_blk(x, cs, s1, s2):
    return x * cs + pltpu.roll(x, LANES - ROPE_HALF, 1) * s1 + pltpu.roll(x, ROPE_HALF, 1) * s2


def _proj_kernel(x_ref, mod_ref, g_ref, pos_ref, tab_ref, wa_ref, wg_ref, wb_ref,
                 qraw_ref, qrot_ref, qrott_ref, kc_ref, vc_ref, ks_ref, vst_ref, kw_ref, vw_ref, gt_ref,
                 *dil_refs_and_scratch):
    n_cfg = len(DIL_CONFIGS)
    qb_refs = dil_refs_and_scratch[0:n_cfg]
    kb_refs = dil_refs_and_scratch[n_cfg:2 * n_cfg]
    vb_refs = dil_refs_and_scratch[2 * n_cfg:3 * n_cfg]
    st_sc = dil_refs_and_scratch[3 * n_cfg]
    tm = x_ref.shape[1]

    def emit_dilated(val, blk, refs):
        st_sc[...] = val
        for (_, d), ref in zip(DIL_CONFIGS, refs):
            for r in range(d):
                piece = val if d == 1 else st_sc[pl.ds(r, tm // d, stride=d), :]
                lo = r * Q_DIL + blk * LANES
                ref[0, :, lo:lo + LANES] = piece.astype(BF16)

    x = x_ref[0]
    sh = mod_ref[0, 0:1, :]
    sc = mod_ref[0, 1:2, :]
    h = (_rms(x, g_ref[...]) * (1.0 + sc) + sh).astype(BF16)
    cs, s1, s2 = _rope_tables(pos_ref[0], tab_ref[0:1, :], tab_ref[1:2, :], tab_ref[2:3, :])

    pa = jnp.dot(h, wa_ref[...], preferred_element_type=F32)
    for r in range(Q_NSA // LANES):
        blk = pa[:, r * LANES:(r + 1) * LANES]
        qraw_ref[0, :, r * LANES:(r + 1) * LANES] = (blk * SCALE).astype(BF16)
        rot = _rope_blk(blk, cs, s1, s2) * SCALE
        qrot_ref[0, :, r * LANES:(r + 1) * LANES] = rot.astype(BF16)
        qrott_ref[0, r * LANES:(r + 1) * LANES, :] = rot.T.astype(BF16)
    o = Q_NSA
    kc_ref[0] = pa[:, o:o + LANES]
    vc_ref[0] = pa[:, o + LANES:o + 2 * LANES]
    ks_ref[0] = _rope_blk(pa[:, o + 2 * LANES:o + 3 * LANES], cs, s1, s2).astype(BF16)
    vst_ref[0] = pa[:, o + 3 * LANES:o + 4 * LANES].T.astype(BF16)
    kw_ref[0] = _rope_blk(pa[:, o + 4 * LANES:o + 5 * LANES], cs, s1, s2).astype(BF16)
    vw_ref[0] = pa[:, o + 5 * LANES:o + 6 * LANES].astype(BF16)

    gt_ref[0] = jnp.dot(h, wg_ref[...], preferred_element_type=F32)

    pb = jnp.dot(h, wb_ref[...], preferred_element_type=F32)
    for blk in range(Q_DIL // LANES):
        lanes = slice(blk * LANES, (blk + 1) * LANES)
        emit_dilated(_rope_blk(pb[:, lanes], cs, s1, s2) * SCALE, blk, qb_refs)
        emit_dilated(_rope_blk(pb[:, Q_DIL + blk * LANES:Q_DIL + (blk + 1) * LANES], cs, s1, s2), blk, kb_refs)
        emit_dilated(pb[:, 2 * Q_DIL + blk * LANES:2 * Q_DIL + (blk + 1) * LANES], blk, vb_refs)


def _nsa_perm():
    cols = []
    for r in range(GROUP_NSA):
        for hk in range(N_KV_NSA):
            hq = hk * GROUP_NSA + r
            cols.extend(range(hq * HEAD_DIM, (hq + 1) * HEAD_DIM))
    return np.asarray(cols, np.int32)


def _rope_const_table():
    half = ROPE_HALF
    inv = ROPE_THETA ** (-jnp.arange(half, dtype=F32) / half)
    lane = np.arange(LANES) % HEAD_DIM
    inv_row = jnp.where(lane < ROPE_DIM, inv[lane % half], 0.0).astype(F32)
    s1 = np.where(lane < half, -1.0, 0.0).astype(np.float32)
    s2 = np.where((lane >= half) & (lane < ROPE_DIM), 1.0, 0.0).astype(np.float32)
    tab = jnp.zeros((8, LANES), F32).at[0].set(inv_row).at[1].set(s1).at[2].set(s2)
    return tab


def _proj(x, mod6, g_pre, positions, w_in, tm=512):
    B, S, D = x.shape
    perm = _nsa_perm()
    gate_lo = Q_NSA + 6 * KV_NSA
    n_gate = 3 * N_HEADS_NSA
    w_a = jnp.concatenate([w_in[:, :Q_NSA][:, perm], w_in[:, Q_NSA:gate_lo]], axis=1).astype(BF16)
    w_g = jnp.pad(w_in[:, gate_lo:gate_lo + n_gate], ((0, 0), (0, LANES - n_gate))).astype(BF16)
    w_b = w_in[:, gate_lo + n_gate:].astype(BF16)
    pos = positions.astype(F32).reshape(B, S, 1)
    tab = _rope_const_table()
    wa_n, wb_n = w_a.shape[1], w_b.shape[1]

    def tok(width, dtype):
        return (pl.BlockSpec((1, tm, width), lambda b, i: (b, i, 0)),
                jax.ShapeDtypeStruct((B, S, width), dtype))

    def tok_t(width, dtype):
        return (pl.BlockSpec((1, width, tm), lambda b, i: (b, 0, i)),
                jax.ShapeDtypeStruct((B, width, S), dtype))

    def dil_view(d):
        return (pl.BlockSpec((1, tm // d, d * Q_DIL), lambda b, i: (b, i, 0)),
                jax.ShapeDtypeStruct((B, S // d, d * Q_DIL), BF16))

    outs = [tok(Q_NSA, BF16), tok(Q_NSA, BF16), tok_t(Q_NSA, BF16), tok(LANES, F32), tok(LANES, F32),
            tok(LANES, BF16), tok_t(LANES, BF16), tok(LANES, BF16), tok(LANES, BF16), tok(LANES, F32)]
    outs += [dil_view(d) for _ in range(3) for _, d in DIL_CONFIGS]
    res = pl.pallas_call(
        _proj_kernel,
        grid=(B, S // tm),
        in_specs=[pl.BlockSpec((1, tm, D), lambda b, i: (b, i, 0)),
                  pl.BlockSpec((1, 6, D), lambda b, i: (b, 0, 0)),
                  pl.BlockSpec((1, D), lambda b, i: (0, 0)),
                  pl.BlockSpec((1, tm, 1), lambda b, i: (b, i, 0)),
                  pl.BlockSpec((8, LANES), lambda b, i: (0, 0)),
                  pl.BlockSpec((D, wa_n), lambda b, i: (0, 0)),
                  pl.BlockSpec((D, LANES), lambda b, i: (0, 0)),
                  pl.BlockSpec((D, wb_n), lambda b, i: (0, 0))],
        out_specs=[o[0] for o in outs],
        out_shape=[o[1] for o in outs],
        scratch_shapes=[pltpu.VMEM((tm, LANES), F32)],
        compiler_params=_cparams(("parallel", "parallel")),
        name="proj",
    )(x, mod6, g_pre.reshape(1, D), pos, tab, w_a, w_g, w_b)
    n_cfg = len(DIL_CONFIGS)
    return tuple(res[:10]) + (res[10:10 + n_cfg], res[10 + n_cfg:10 + 2 * n_cfg], res[10 + 2 * n_cfg:])


def _cmpmlp_kernel(a_ref, p_ref, w1_ref, w2_ref, o_ref):
    a = a_ref[0, 0].astype(BF16)
    w1 = w1_ref[0]
    half = CMP_STRIDE * HEAD_DIM
    u = jnp.dot(a, w1[:half], preferred_element_type=F32)
    v = jnp.dot(a, w1[half:], preferred_element_type=F32)
    bias = jnp.dot(p_ref[...], w1, preferred_element_type=F32)[0:1]
    n16 = u.shape[0]
    hid = jax.nn.gelu(u + pltpu.roll(v, n16 - 1, 0) + bias)
    o_ref[0, 0] = jnp.dot(hid.astype(BF16), w2_ref[0], preferred_element_type=F32)


def _cmpmlp(kc, vc, cmp_pos, w_k1, w_k2, w_v1, w_v2):
    B, S, _ = kc.shape
    n16 = S // CMP_STRIDE
    seg = CMP_STRIDE * HEAD_DIM

    def segs(a):
        return a.reshape(B, n16, CMP_STRIDE, N_KV_NSA, HEAD_DIM).transpose(0, 3, 1, 2, 4).reshape(
            B * N_KV_NSA, n16, seg)

    a = jnp.stack([segs(kc), segs(vc)], axis=0)
    w1 = jnp.stack([w_k1, w_v1], axis=0).astype(BF16)
    w2 = jnp.stack([w_k2, w_v2], axis=0).astype(BF16)
    p8 = jnp.broadcast_to(cmp_pos.reshape(1, CMP_BLOCK * HEAD_DIM), (8, CMP_BLOCK * HEAD_DIM)).astype(BF16)
    out = pl.pallas_call(
        _cmpmlp_kernel,
        grid=(2, B * N_KV_NSA),
        in_specs=[pl.BlockSpec((1, 1, n16, seg), lambda t, g: (t, g, 0, 0)),
                  pl.BlockSpec((8, 2 * seg), lambda t, g: (0, 0)),
                  pl.BlockSpec((1, 2 * seg, CMP_HIDDEN), lambda t, g: (t, 0, 0)),
                  pl.BlockSpec((1, CMP_HIDDEN, HEAD_DIM), lambda t, g: (t, 0, 0))],
        out_specs=pl.BlockSpec((1, 1, n16, HEAD_DIM), lambda t, g: (t, g, 0, 0)),
        out_shape=jax.ShapeDtypeStruct((2, B * N_KV_NSA, n16, HEAD_DIM), F32),
        compiler_params=_cparams(("parallel", "parallel")),
        name="cmpmlp",
    )(a, p8, w1, w2)
    out = out.reshape(2, B, N_KV_NSA, n16, HEAD_DIM).transpose(0, 1, 3, 2, 4).reshape(2, B, n16, KV_NSA)
    return out[0].astype(BF16), out[1].astype(BF16)


def _stack_heads(q, tq):
    lane = lax.broadcasted_iota(I32, (tq, LANES), 1)
    half0 = lane < HEAD_DIM
    zero = jnp.zeros((tq, LANES), q.dtype)
    rows = []
    for hk in range(N_KV_NSA):
        keep = half0 if hk == 0 else jnp.logical_not(half0)
        for r in range(GROUP_NSA):
            rows.append(jnp.where(keep, q[:, r * LANES:(r + 1) * LANES], zero))
    return jnp.concatenate(rows, axis=0), half0


def _unstack_heads(o, half0, tq, o_ref):
    for r in range(GROUP_NSA):
        o0 = o[r * tq:(r + 1) * tq]
        o1 = o[(GROUP_NSA + r) * tq:(GROUP_NSA + r + 1) * tq]
        o_ref[0, :, r * LANES:(r + 1) * LANES] = jnp.where(half0, o0, o1)


def _cmpsel_kernel(q_ref, kc_ref, vc_ref, ovt_ref, o_ref, bias_ref, *, tq):
    i = pl.program_id(1)
    qs, half0 = _stack_heads(q_ref[0], tq)
    n16 = kc_ref.shape[1]
    rows = 8 * tq
    s = lax.dot_general(qs, kc_ref[0], _NT, preferred_element_type=F32)
    t = i * tq + (lax.broadcasted_iota(I32, (rows, n16), 0) & (tq - 1))
    c = lax.broadcasted_iota(I32, (rows, n16), 1)
    valid = (c * CMP_STRIDE + (CMP_BLOCK - 1)) <= t
    s = jnp.where(valid, s, NEG_INF)
    m = jnp.max(s, axis=-1, keepdims=True)
    e = jnp.exp(s - m)
    l = jnp.sum(e, axis=-1, keepdims=True)
    p = jnp.where(valid, e / l, 0.0)
    o = jnp.dot(p.astype(BF16), vc_ref[0], preferred_element_type=F32)
    _unstack_heads(o, half0, tq, o_ref)

    n_slc = ovt_ref.shape[0]
    j = lax.broadcasted_iota(I32, (n_slc, tq), 0)
    cur = (i * tq + lax.broadcasted_iota(I32, (n_slc, tq), 1)) >> SLC_SHIFT
    forced = (j == 0) | (j == cur) | (j == cur - 1)
    ovt = ovt_ref[...]
    biases = []
    for hk in range(N_KV_NSA):
        ps = p[hk * GROUP_NSA * tq:(hk * GROUP_NSA + 1) * tq]
        for r in range(1, GROUP_NSA):
            ps = ps + p[(hk * GROUP_NSA + r) * tq:(hk * GROUP_NSA + r + 1) * tq]
        hi = ps.astype(BF16)
        lo = (ps - hi.astype(F32)).astype(BF16)
        pslc = (lax.dot_general(ovt, hi, _NT, preferred_element_type=F32)
                + lax.dot_general(ovt, lo, _NT, preferred_element_type=F32))
        score = jnp.where(forced, FORCE_SCORE, jnp.where(j <= cur, pslc, -1.0))
        rank = jnp.zeros((n_slc, tq), I32)
        for ii in range(n_slc):
            ri = score[ii:ii + 1, :]
            beats = (ri > score) | ((ri == score) & (j > ii))
            rank = rank + beats.astype(I32)
        biases.append(jnp.where(rank < SLC_TOPK, 0.0, NEG_INF))
    bias_ref[0] = jnp.concatenate(biases, axis=0).astype(BF16)


def _overlap_t(S):
    n16 = S // CMP_STRIDE
    n_slc = S // SLC_BLOCK
    cs = np.arange(n16) * CMP_STRIDE
    js = np.arange(n_slc) * SLC_BLOCK
    ov = np.clip(np.minimum(cs[:, None] + CMP_BLOCK, js[None, :] + SLC_BLOCK)
                 - np.maximum(cs[:, None], js[None, :]), 0, None).astype(np.float32) / CMP_BLOCK
    ov[n16 - 1] = 0.0
    return jnp.asarray(ov.T, BF16)


def _cmpsel(q_raw, kcf, vcf, tq=128):
    B, S, _ = q_raw.shape
    n16 = S // CMP_STRIDE
    n_slc = S // SLC_BLOCK
    assert n_slc == HEAD_DIM, "selection bias is laid out as one 64-lane half per kv head"
    return pl.pallas_call(
        functools.partial(_cmpsel_kernel, tq=tq),
        grid=(B, S // tq),
        in_specs=[pl.BlockSpec((1, tq, Q_NSA), lambda b, i: (b, i, 0)),
                  pl.BlockSpec((1, n16, KV_NSA), lambda b, i: (b, 0, 0)),
                  pl.BlockSpec((1, n16, KV_NSA), lambda b, i: (b, 0, 0)),
                  pl.BlockSpec((n_slc, n16), lambda b, i: (0, 0))],
        out_specs=[pl.BlockSpec((1, tq, Q_NSA), lambda b, i: (b, i, 0)),
                   pl.BlockSpec((1, 2 * n_slc, tq), lambda b, i: (b, 0, i))],
        out_shape=[jax.ShapeDtypeStruct((B, S, Q_NSA), F32),
                   jax.ShapeDtypeStruct((B, 2 * n_slc, S), BF16)],
        compiler_params=_cparams(("parallel", "parallel")),
        name="cmpsel",
    )(q_raw, kcf, vcf, _overlap_t(S))


def _slc_kernel(qi_ref, kj_ref, qt_ref, bt_ref, k_ref, vt_ref, o_ref, qa_sc, m_sc, l_sc, acc_sc, *, tq, tk):
    step = pl.program_id(1)
    i = qi_ref[step]
    kj = kj_ref[step]
    last = (i * tq + tq - 1) // tk
    cols = 8 * tq

    @pl.when(kj == 0)
    def _():
        half0 = lax.broadcasted_iota(I32, (LANES, tq), 0) < HEAD_DIM
        zero = jnp.zeros((LANES, tq), BF16)
        bt = bt_ref[0]
        groups = []
        for hk in range(N_KV_NSA):
            keep = half0 if hk == 0 else jnp.logical_not(half0)
            bh = jnp.where(keep, bt, zero)
            for r in range(GROUP_NSA):
                qb = qt_ref[0, r * LANES:(r + 1) * LANES, :]
                groups.append(jnp.concatenate([jnp.where(keep, qb, zero), bh], axis=0))
        qa_sc[...] = jnp.concatenate(groups, axis=1)
        m_sc[...] = jnp.full((1, cols), NEG_INF, F32)
        l_sc[...] = jnp.zeros((1, cols), F32)
        acc_sc[...] = jnp.zeros((LANES, cols), F32)

    def update(on_diagonal):
        kblk = (kj * tk + lax.broadcasted_iota(I32, (tk, LANES), 0)) >> SLC_SHIFT
        lane = lax.broadcasted_iota(I32, (tk, LANES), 1) & (HEAD_DIM - 1)
        onehot = jnp.where(kblk == lane, 1.0, 0.0).astype(BF16)
        kaug = jnp.concatenate([k_ref[0], onehot], axis=1)
        st = jnp.dot(kaug, qa_sc[...], preferred_element_type=F32)
        if on_diagonal:
            kpos = kj * tk + lax.broadcasted_iota(I32, (tk, cols), 0)
            t = i * tq + (lax.broadcasted_iota(I32, (tk, cols), 1) & (tq - 1))
            st = jnp.where(kpos <= t, st, NEG_INF)
        m_old = m_sc[...]
        m_new = jnp.maximum(m_old, jnp.max(st, axis=0, keepdims=True))
        alpha = jnp.exp(m_old - m_new)
        p = jnp.exp(st - m_new)
        l_sc[...] = alpha * l_sc[...] + jnp.sum(p, axis=0, keepdims=True)
        acc_sc[...] = alpha * acc_sc[...] + jnp.dot(vt_ref[0], p.astype(BF16), preferred_element_type=F32)
        m_sc[...] = m_new

    @pl.when(kj < last)
    def _():
        update(False)

    @pl.when(kj == last)
    def _():
        update(True)
        ot = acc_sc[...] / l_sc[...]
        half0 = lax.broadcasted_iota(I32, (tq, LANES), 1) < HEAD_DIM
        for r in range(GROUP_NSA):
            o0 = ot[:, r * tq:(r + 1) * tq].T
            o1 = ot[:, (GROUP_NSA + r) * tq:(GROUP_NSA + r + 1) * tq].T
            o_ref[0, :, r * LANES:(r + 1) * LANES] = jnp.where(half0, o0, o1)


def _slc(q_rot_t, bias_t, ks, vs_t, tq=128, tk=512):
    B, _, S = q_rot_t.shape
    assert tk % tq == 0
    qi, kj = [], []
    for i in range(S // tq):
        for j in range((i * tq + tq - 1) // tk + 1):
            qi.append(i)
            kj.append(j)
    grid_spec = pltpu.PrefetchScalarGridSpec(
        num_scalar_prefetch=2,
        grid=(B, len(qi)),
        in_specs=[pl.BlockSpec((1, Q_NSA, tq), lambda b, s, qi, kj: (b, 0, qi[s])),
                  pl.BlockSpec((1, LANES, tq), lambda b, s, qi, kj: (b, 0, qi[s])),
                  pl.BlockSpec((1, tk, KV_NSA), lambda b, s, qi, kj: (b, kj[s], 0)),
                  pl.BlockSpec((1, KV_NSA, tk), lambda b, s, qi, kj: (b, 0, kj[s]))],
        out_specs=pl.BlockSpec((1, tq, Q_NSA), lambda b, s, qi, kj: (b, qi[s], 0)),
        scratch_shapes=[pltpu.VMEM((2 * LANES, 8 * tq), BF16),
                        pltpu.VMEM((1, 8 * tq), F32),
                        pltpu.VMEM((1, 8 * tq), F32),
                        pltpu.VMEM((LANES, 8 * tq), F32)],
    )
    return pl.pallas_call(
        functools.partial(_slc_kernel, tq=tq, tk=tk),
        grid_spec=grid_spec,
        out_shape=jax.ShapeDtypeStruct((B, S, Q_NSA), F32),
        compiler_params=_cparams(("parallel", "arbitrary")),
        name="slc",
    )(jnp.asarray(qi, I32), jnp.asarray(kj, I32), q_rot_t, bias_t, ks, vs_t)


def _band_kernel(*refs, tq, tk, nt, max_dist, shared_kv, with_lse):
    q_ref = refs[0]
    k_refs = refs[1:1 + nt]
    v_refs = refs[1 + nt:1 + 2 * nt]
    o_ref = refs[1 + 2 * nt]
    lse_ref = refs[2 + 2 * nt] if with_lse else None
    i = pl.program_id(2)
    a = (i * tq) // tk
    lane = lax.broadcasted_iota(I32, (tq, LANES), 1)
    half0 = lane < HEAD_DIM
    row_t = i * tq + (lax.broadcasted_iota(I32, (2 * tq, tk), 0) & (tq - 1))
    col = lax.broadcasted_iota(I32, (2 * tq, tk), 1)
    oks = []
    for jt in range(nt):
        kpos = (a - (nt - 1) + jt) * tk + col
        d = row_t - kpos
        oks.append((d >= 0) & (d <= max_dist) & (kpos >= 0))
    lse_acc = jnp.zeros((tq, LANES), F32)
    for blk in range(Q_NSA // LANES):
        qb = q_ref[0, :, blk * LANES:(blk + 1) * LANES]
        zero = jnp.zeros_like(qb)
        qs = jnp.concatenate([jnp.where(half0, qb, zero), jnp.where(half0, zero, qb)], axis=0)
        kv0 = 0 if shared_kv else blk * LANES
        ss = []
        for jt in range(nt):
            kt = k_refs[jt][0, :, kv0:kv0 + LANES]
            s = lax.dot_general(qs, kt, _NT, preferred_element_type=F32)
            ss.append(jnp.where(oks[jt], s, NEG_INF))
        m = jnp.max(ss[0], axis=-1, keepdims=True)
        for s in ss[1:]:
            m = jnp.maximum(m, jnp.max(s, axis=-1, keepdims=True))
        l = jnp.zeros((2 * tq, 1), F32)
        o = jnp.zeros((2 * tq, LANES), F32)
        for jt in range(nt):
            p = jnp.exp(ss[jt] - m)
            l = l + jnp.sum(p, axis=-1, keepdims=True)
            o = o + jnp.dot(p.astype(BF16), v_refs[jt][0, :, kv0:kv0 + LANES], preferred_element_type=F32)
        o = o / l
        o_ref[0, :, blk * LANES:(blk + 1) * LANES] = jnp.where(half0, o[:tq], o[tq:])
        if with_lse:
            lse = m + jnp.log(l)
            lse_acc = jnp.where(lane == 2 * blk, lse[:tq], lse_acc)
            lse_acc = jnp.where(lane == 2 * blk + 1, lse[tq:], lse_acc)
    if with_lse:
        lse_ref[0] = lse_acc


def _band(q, k, v, *, dil, max_dist, tq, tk, nt, shared_kv, with_lse):
    B, m, _ = q.shape
    kvw = k.shape[-1] // dil
    qv, kv_, vv = q, k, v

    def kv_map(jt):
        return lambda b, r, i: (b, jnp.maximum((i * tq) // tk - (nt - 1) + jt, 0), r)

    tok_spec = pl.BlockSpec((1, tq, Q_NSA), lambda b, r, i: (b, i, r))
    in_specs = [tok_spec]
    in_specs += [pl.BlockSpec((1, tk, kvw), kv_map(jt)) for jt in range(nt)]
    in_specs += [pl.BlockSpec((1, tk, kvw), kv_map(jt)) for jt in range(nt)]
    out_specs = [tok_spec]
    out_shape = [jax.ShapeDtypeStruct((B, m, dil * Q_NSA), F32)]
    if with_lse:
        out_specs.append(pl.BlockSpec((1, tq, LANES), lambda b, r, i: (b, i, r)))
        out_shape.append(jax.ShapeDtypeStruct((B, m, dil * LANES), F32))
    res = pl.pallas_call(
        functools.partial(_band_kernel, tq=tq, tk=tk, nt=nt, max_dist=max_dist, shared_kv=shared_kv,
                          with_lse=with_lse),
        grid=(B, dil, m // tq),
        in_specs=in_specs,
        out_specs=out_specs,
        out_shape=out_shape,
        compiler_params=_cparams(("parallel", "parallel", "parallel")),
        name=f"band_d{dil}_w{max_dist}",
    )(qv, *([kv_] * nt), *([vv] * nt))
    return (res[0], res[1]) if with_lse else res[0]


def _out_kernel(ocmp_ref, oslc_ref, owin_ref, gt_ref, od1_ref, od4_ref, od16_ref, l1_ref, l4_ref, l16_ref,
                x_ref, mod_ref, eg_ref, ed_ref, gnsa_ref, gdil_ref, wo_ref, gpost_ref, o_ref, st_sc):
    tm = x_ref.shape[1]

    def token_order(ref, d, n_blk):
        if d == 1:
            return ref[0]
        blocks = []
        for blk in range(n_blk):
            for r in range(d):
                lo = (r * n_blk + blk) * LANES
                st_sc[pl.ds(r, tm // d, stride=d), :] = ref[0, :, lo:lo + LANES]
            blocks.append(st_sc[...])
        return blocks[0] if n_blk == 1 else jnp.concatenate(blocks, axis=1)

    sg = jax.nn.sigmoid(gt_ref[0])
    oa = (_split3_dot(sg, eg_ref[0]) * ocmp_ref[0] + _split3_dot(sg, eg_ref[1]) * oslc_ref[0]
          + _split3_dot(sg, eg_ref[2]) * owin_ref[0])
    ya = _rms(oa, gnsa_ref[...])

    dils = [d for _, d in DIL_CONFIGS]
    n_blk = Q_DIL // LANES
    l1, l4, l16 = [token_order(ref, d, 1) for ref, d in zip((l1_ref, l4_ref, l16_ref), dils)]
    mx = jnp.maximum(jnp.maximum(l1, l4), l16)
    e1, e4, e16 = jnp.exp(l1 - mx), jnp.exp(l4 - mx), jnp.exp(l16 - mx)
    den = e1 + e4 + e16
    ed = ed_ref[...]
    od1, od4, od16 = [token_order(ref, d, n_blk) for ref, d in zip((od1_ref, od4_ref, od16_ref), dils)]
    ob = (_split3_dot(e1 / den, ed) * od1 + _split3_dot(e4 / den, ed) * od4 + _split3_dot(e16 / den, ed) * od16)
    yb = _rms(ob, gdil_ref[...])

    y = jnp.concatenate([ya, yb], axis=1).astype(BF16)
    z = jnp.dot(y, wo_ref[...], preferred_element_type=F32)
    gt_m = mod_ref[0, 2:3, :]
    o_ref[0] = x_ref[0] + gt_m * _rms(z, gpost_ref[...])


def _gate_expanders():
    perm = _nsa_perm()
    eg = np.zeros((3, LANES, Q_NSA), np.float32)
    for lane_out, col in enumerate(perm):
        hq = col // HEAD_DIM
        for c in range(3):
            eg[c, hq * 3 + c, lane_out] = 1.0
    ed = np.zeros((LANES, Q_DIL), np.float32)
    for h in range(N_HEADS_DIL):
        ed[h, h * HEAD_DIM:(h + 1) * HEAD_DIM] = 1.0
    return jnp.asarray(eg, BF16), jnp.asarray(ed, BF16)


def _out(o_cmp, o_slc, o_win, gates, ods, lses, x, mod6, g_out_nsa, g_out_dil, w_o, g_post, tm=256):
    B, S, D = x.shape
    perm = _nsa_perm()
    eg, ed = _gate_expanders()
    w_o_p = jnp.concatenate([w_o[:Q_NSA][perm], w_o[Q_NSA:]], axis=0).astype(BF16)
    g_nsa_p = g_out_nsa[perm].reshape(1, Q_NSA)

    def tok(width):
        return pl.BlockSpec((1, tm, width), lambda b, i: (b, i, 0))

    def const(shape):
        return pl.BlockSpec(shape, lambda b, i: (0,) * len(shape))

    def view(d, width):
        return pl.BlockSpec((1, tm // d, d * width), lambda b, i: (b, i, 0))

    dils = [d for _, d in DIL_CONFIGS]
    return pl.pallas_call(
        _out_kernel,
        grid=(B, S // tm),
        in_specs=[tok(Q_NSA), tok(Q_NSA), tok(Q_NSA), tok(LANES)]
        + [view(d, Q_DIL) for d in dils] + [view(d, LANES) for d in dils]
        + [tok(D), pl.BlockSpec((1, 6, D), lambda b, i: (b, 0, 0)),
           const((3, LANES, Q_NSA)), const((LANES, Q_DIL)), const((1, Q_NSA)), const((1, Q_DIL)),
           const((D, D)), const((1, D))],
        out_specs=tok(D),
        out_shape=jax.ShapeDtypeStruct((B, S, D), F32),
        scratch_shapes=[pltpu.VMEM((tm, LANES), F32)],
        compiler_params=_cparams(("parallel", "parallel")),
        name="out",
    )(o_cmp, o_slc, o_win, gates, *ods, *lses, x, mod6, eg, ed, g_nsa_p, g_out_dil.reshape(1, Q_DIL),
      w_o_p, g_post.reshape(1, D))


def _route_kernel(x_ref, mod_ref, g_ref, wr_ref, br_ref, h_ref, idx_ref, gate_ref, rank_ref, cnt_ref, run_sc):
    i = pl.program_id(0)

    @pl.when(i == 0)
    def _():
        run_sc[...] = jnp.zeros_like(run_sc)

    sh = mod_ref[0, 3:4, :]
    sc = mod_ref[0, 4:5, :]
    h = _rms(x_ref[...], g_ref[...]) * (1.0 + sc) + sh
    h_ref[...] = h
    logits = jnp.dot(h, wr_ref[...], preferred_element_type=F32,
                     precision=lax.Precision.HIGHEST) + br_ref[...]
    tm = logits.shape[0]
    lane = lax.broadcasted_iota(I32, (tm, LANES), 1)
    work = jnp.where(lane < N_EXPERTS, logits, -jnp.inf)
    idx_out = jnp.zeros((tm, LANES), I32)
    val_out = jnp.full((tm, LANES), -jnp.inf, F32)
    rank_out = jnp.zeros((tm, LANES), F32)
    tri = jnp.where(lax.broadcasted_iota(I32, (tm, tm), 1) < lax.broadcasted_iota(I32, (tm, tm), 0),
                    1.0, 0.0).astype(BF16)
    base = run_sc[0:1, :]
    for k in range(TOP_K):
        mx = jnp.max(work, axis=-1, keepdims=True)
        ix = jnp.min(jnp.where(work == mx, lane, LANES), axis=-1, keepdims=True)
        hit = lane == ix
        idx_out = jnp.where(lane == k, ix, idx_out)
        val_out = jnp.where(lane == k, mx, val_out)
        work = jnp.where(hit, -jnp.inf, work)
        onehot = jnp.where(hit, 1.0, 0.0)
        before = jnp.dot(tri, onehot.astype(BF16), preferred_element_type=F32) + base
        rank_k = jnp.sum(onehot * before, axis=-1, keepdims=True)
        rank_out = jnp.where(lane == k, rank_k, rank_out)
        base = base + jnp.sum(onehot, axis=0, keepdims=True)
    run_sc[...] = jnp.broadcast_to(base, run_sc.shape)
    e = jnp.exp(val_out - val_out[:, 0:1])
    idx_ref[...] = idx_out
    gate_ref[...] = e / jnp.sum(e, axis=-1, keepdims=True)
    rank_ref[...] = rank_out.astype(I32)
    cnt_ref[...] = run_sc[...].astype(I32)


def _route(x1, mod6, g_pre_ffn, w_router, b_router, tm=512):
    B, S, D = x1.shape
    N = B * S
    spb = S // tm
    wr = jnp.pad(w_router, ((0, 0), (0, LANES - N_EXPERTS)))
    br = jnp.pad(b_router, (0, LANES - N_EXPERTS)).reshape(1, LANES)

    def tok(width):
        return pl.BlockSpec((tm, width), lambda i: (i, 0))

    return pl.pallas_call(
        _route_kernel,
        grid=(N // tm,),
        in_specs=[tok(D), pl.BlockSpec((1, 6, D), lambda i: (i // spb, 0, 0)),
                  pl.BlockSpec((1, D), lambda i: (0, 0)),
                  pl.BlockSpec((D, LANES), lambda i: (0, 0)),
                  pl.BlockSpec((1, LANES), lambda i: (0, 0))],
        out_specs=[tok(D), tok(LANES), tok(LANES), tok(LANES), pl.BlockSpec((8, LANES), lambda i: (0, 0))],
        out_shape=[jax.ShapeDtypeStruct((N, D), F32),
                   jax.ShapeDtypeStruct((N, LANES), I32),
                   jax.ShapeDtypeStruct((N, LANES), F32),
                   jax.ShapeDtypeStruct((N, LANES), I32),
                   jax.ShapeDtypeStruct((8, LANES), I32)],
        scratch_shapes=[pltpu.VMEM((8, LANES), F32)],
        compiler_params=_cparams(("arbitrary",)),
        name="route",
    )(x1.reshape(N, D), mod6, g_pre_ffn.reshape(1, D), wr, br)


def _dispatch_kernel(pos_ref, pad0_ref, padn_ref, nt_ref, h_ref, xs_out, zbuf, sem, zsem, *, tm, tmx, n_tiles):
    i = pl.program_id(0)
    bits = tmx.bit_length() - 1

    def pad_copies(e, fn):
        p0 = pad0_ref[e]
        head = (-p0) & (SUBLANES - 1)
        head = jnp.minimum(head, padn_ref[e])
        for r in range(SUBLANES - 1):
            @pl.when(r < head)
            def _():
                fn(pltpu.make_async_copy(zbuf.at[pl.ds(0, 1), :], xs_out.at[pl.ds(p0 + r, 1), :], zsem))
        a = p0 + head
        n = padn_ref[e] - head
        for b in range(3, bits):
            size = 1 << b
            off = (n >> (b + 1)) << (b + 1)

            @pl.when((n & size) != 0)
            def _():
                fn(pltpu.make_async_copy(zbuf.at[pl.ds(0, size), :],
                                         xs_out.at[pl.ds(pl.multiple_of(a + off, SUBLANES), size), :], zsem))

    def tail_copy(t, fn):
        fn(pltpu.make_async_copy(zbuf, xs_out.at[pl.ds(pl.multiple_of(t * tmx, tmx), tmx), :], zsem))

    def for_all_fill(fn):
        def per_expert(e, carry):
            pad_copies(e, fn)
            return carry
        lax.fori_loop(0, N_EXPERTS, per_expert, 0)

        def per_tile(t, carry):
            tail_copy(t, fn)
            return carry
        lax.fori_loop(nt_ref[0], n_tiles, per_tile, 0)

    @pl.when(i == 0)
    def _():
        zbuf[...] = jnp.zeros_like(zbuf)
        for_all_fill(lambda cp: cp.start())
        for_all_fill(lambda cp: cp.wait())

    def body(r, carry):
        for k in range(TOP_K):
            dst_row = pos_ref[(i * tm + r) * TOP_K + k]
            pltpu.make_async_copy(h_ref.at[pl.ds(r, 1), :], xs_out.at[pl.ds(dst_row, 1), :], sem).start()
        return carry

    lax.fori_loop(0, tm, body, 0, unroll=DMA_UNROLL // TOP_K)
    for k in range(TOP_K):
        pltpu.make_async_copy(h_ref, xs_out.at[pl.ds(0, tm), :], sem).wait()


def _dispatch(h2, pos_flat, pad_start, pad_len, n_valid, n_tiles, tmx, tm=256):
    N, D = h2.shape
    assert tmx & (tmx - 1) == 0
    grid_spec = pltpu.PrefetchScalarGridSpec(
        num_scalar_prefetch=4,
        grid=(N // tm,),
        in_specs=[pl.BlockSpec((tm, D), lambda i, *_: (i, 0))],
        out_specs=pl.BlockSpec(memory_space=pl.ANY),
        scratch_shapes=[pltpu.VMEM((tmx, D), F32), pltpu.SemaphoreType.DMA(()), pltpu.SemaphoreType.DMA(())],
    )
    return pl.pallas_call(
        functools.partial(_dispatch_kernel, tm=tm, tmx=tmx, n_tiles=n_tiles),
        grid_spec=grid_spec,
        out_shape=jax.ShapeDtypeStruct((n_tiles * tmx, D), F32),
        compiler_params=_cparams(("arbitrary",)),
        name="dispatch",
    )(pos_flat, pad_start, pad_len, n_valid, h2)


def _moe_kernel(te_ref, nt_ref, nx_ref, sl_ref, xs_ref, wup_hbm, bup_ref, wdn_hbm, bdn_ref, pm_ref, y_ref,
                wup_in, wdn_in, wup_sc, wdn_sc, sems):
    i = pl.program_id(0)
    n_valid = nt_ref[0]
    e = te_ref[i]
    new_expert = (i == 0) | (e != te_ref[jnp.maximum(i - 1, 0)])
    slot = sl_ref[i]
    n_blk = wup_sc.shape[1] // (2 * LANES)

    def weight_copies(ex, s):
        return (pltpu.make_async_copy(wup_hbm.at[ex], wup_in.at[s], sems.at[0, s]),
                pltpu.make_async_copy(wdn_hbm.at[ex], wdn_in.at[s], sems.at[1, s]))

    @pl.when(i == 0)
    def _():
        for cp in weight_copies(e, slot):
            cp.start()

    @pl.when(new_expert & (i < n_valid))
    def _():
        for cp in weight_copies(e, slot):
            cp.wait()
        nx = nx_ref[i]

        @pl.when(nx >= 0)
        def _():
            for cp in weight_copies(nx, 1 - slot):
                cp.start()

        pm = pm_ref[...]
        for blk in range(n_blk):
            cols = slice(blk * 2 * LANES, (blk + 1) * 2 * LANES)
            w = wup_in[slot, :, cols].astype(BF16)
            wup_sc[:, cols] = jnp.dot(w, pm, preferred_element_type=F32).astype(BF16)
        wdn_sc[...] = wdn_in[slot].astype(BF16)

    @pl.when(i < n_valid)
    def _():
        xs = xs_ref[...].astype(BF16)
        u = jnp.dot(xs, wup_sc[...], preferred_element_type=F32) + bup_ref[0]
        acts = []
        for blk in range(n_blk):
            ug = jnp.minimum(u[:, blk * 2 * LANES:blk * 2 * LANES + LANES], SWIGLU_LIMIT)
            ul = jnp.clip(u[:, blk * 2 * LANES + LANES:(blk + 1) * 2 * LANES], -SWIGLU_LIMIT, SWIGLU_LIMIT)
            acts.append((ug * jax.nn.sigmoid(SWIGLU_ALPHA * ug) * (ul + 1.0)).astype(BF16))
        act = jnp.concatenate(acts, axis=1)
        y_ref[...] = jnp.dot(act, wdn_sc[...], preferred_element_type=F32) + bdn_ref[0]

    @pl.when(i >= n_valid)
    def _():
        y_ref[...] = jnp.zeros_like(y_ref)


def _moe_layout(top_idx, rank, counts, tm):
    N = top_idx.shape[0]
    n_tiles = N * TOP_K // tm + N_EXPERTS
    e_ids = jnp.arange(N_EXPERTS, dtype=I32)
    tiles_e = (counts + tm - 1) // tm
    tile_end = jnp.sum(jnp.where(e_ids[None, :] <= e_ids[:, None], tiles_e[None, :], 0), axis=1)
    start = (tile_end - tiles_e) * tm
    pos = rank + jnp.sum(jnp.where(top_idx[:, :, None] == e_ids[None, None, :], start[None, None, :], 0), axis=-1)
    n_valid = tile_end[N_EXPERTS - 1]
    tile_ids = jnp.arange(n_tiles, dtype=I32)
    tile_e = jnp.sum((jnp.minimum(tile_ids, n_valid - 1)[:, None] >= tile_end[None, :]).astype(I32), axis=1)
    prev_e = jnp.concatenate([jnp.full((1,), -1, I32), tile_e[:-1]])
    new = (tile_e != prev_e) & (tile_ids < n_valid)
    ordinal = jnp.sum(jnp.where(tile_ids[None, :] <= tile_ids[:, None], new[None, :].astype(I32), 0), axis=1) - 1
    next_first = jnp.sum(jnp.where(tile_e[:, None] == e_ids[None, :], tile_end[None, :], 0), axis=1)
    next_e = jnp.sum(jnp.where(next_first[:, None] == tile_ids[None, :], tile_e[None, :], 0), axis=1)
    next_e = jnp.where(next_first < n_valid, next_e, -1)
    meta = dict(tile_e=tile_e.astype(I32), n_valid=n_valid.astype(I32).reshape(1), next_e=next_e.astype(I32),
                slot=(ordinal & 1).astype(I32), pad_start=(start + counts).astype(I32),
                pad_len=(tiles_e * tm - counts).astype(I32))
    return pos.astype(I32), meta, n_tiles


def _glu_perm():
    pm = np.zeros((2 * LANES, 2 * LANES), np.float32)
    for j in range(LANES):
        pm[2 * j, j] = 1.0
        pm[2 * j + 1, LANES + j] = 1.0
    return jnp.asarray(pm, BF16)


def _moe(xs, meta, w_up, b_up, w_down, b_down, tm):
    P, D = xs.shape
    n_blk = D_FF // LANES
    b_up_p = b_up.reshape(N_EXPERTS, n_blk, LANES, 2).transpose(0, 1, 3, 2).reshape(N_EXPERTS, 1, 2 * D_FF)
    b_dn = b_down.reshape(N_EXPERTS, 1, D)

    def bmap(i, te, *_):
        return (te[i], 0, 0)

    grid_spec = pltpu.PrefetchScalarGridSpec(
        num_scalar_prefetch=4,
        grid=(P // tm,),
        in_specs=[pl.BlockSpec((tm, D), lambda i, *_: (i, 0)),
                  pl.BlockSpec(memory_space=pl.ANY), pl.BlockSpec((1, 1, 2 * D_FF), bmap),
                  pl.BlockSpec(memory_space=pl.ANY), pl.BlockSpec((1, 1, D), bmap),
                  pl.BlockSpec((2 * LANES, 2 * LANES), lambda i, *_: (0, 0))],
        out_specs=pl.BlockSpec((tm, D), lambda i, *_: (i, 0)),
        scratch_shapes=[pltpu.VMEM((2, D, 2 * D_FF), F32), pltpu.VMEM((2, D_FF, D), F32),
                        pltpu.VMEM((D, 2 * D_FF), BF16), pltpu.VMEM((D_FF, D), BF16),
                        pltpu.SemaphoreType.DMA((2, 2))],
    )
    return pl.pallas_call(
        _moe_kernel,
        grid_spec=grid_spec,
        out_shape=jax.ShapeDtypeStruct((P, D), F32),
        compiler_params=_cparams(("arbitrary",)),
        name="moe",
    )(meta["tile_e"], meta["n_valid"], meta["next_e"], meta["slot"], xs, w_up, b_up_p, w_down, b_dn, _glu_perm())


def _final_kernel(pos_ref, ys_hbm, x_ref, gate_ref, gtf_ref, g_ref, o_ref, buf, sems, *, tm, n_steps):
    i = pl.program_id(0)
    slot = i & 1

    n = TOP_K * tm

    def gather(step, dst, sem):
        def body(j, carry):
            pltpu.make_async_copy(ys_hbm.at[pl.ds(pos_ref[step * n + j], 1), :], dst.at[pl.ds(j, 1), :],
                                  sem).start()
            return carry
        lax.fori_loop(0, n, body, 0, unroll=DMA_UNROLL)

    @pl.when(i == 0)
    def _():
        gather(0, buf.at[0], sems.at[0])

    @pl.when(i + 1 < n_steps)
    def _():
        gather(i + 1, buf.at[1 - slot], sems.at[1 - slot])

    pltpu.make_async_copy(ys_hbm.at[pl.ds(0, n), :], buf.at[slot], sems.at[slot]).wait()
    gate = gate_ref[...]
    y = gate[:, 0:1] * buf[slot, 0:tm, :]
    for k in range(1, TOP_K):
        y = y + gate[:, k:k + 1] * buf[slot, k * tm:(k + 1) * tm, :]
    o_ref[...] = x_ref[...] + gtf_ref[0, 5:6, :] * _rms(y, g_ref[...])


def _final(ys, pos_flat, gate, x1, mod6, g_post_ffn, tm=128):
    B, S, D = x1.shape
    N = B * S
    n_steps = N // tm
    steps_per_b = S // tm
    pos_sm = pos_flat.reshape(n_steps, tm, TOP_K).transpose(0, 2, 1).reshape(N * TOP_K)
    grid_spec = pltpu.PrefetchScalarGridSpec(
        num_scalar_prefetch=1,
        grid=(n_steps,),
        in_specs=[pl.BlockSpec(memory_space=pl.ANY),
                  pl.BlockSpec((tm, D), lambda i, p: (i, 0)),
                  pl.BlockSpec((tm, LANES), lambda i, p: (i, 0)),
                  pl.BlockSpec((1, 6, D), lambda i, p: (i // steps_per_b, 0, 0)),
                  pl.BlockSpec((1, D), lambda i, p: (0, 0))],
        out_specs=pl.BlockSpec((tm, D), lambda i, p: (i, 0)),
        scratch_shapes=[pltpu.VMEM((2, TOP_K * tm, D), F32), pltpu.SemaphoreType.DMA((2,))],
    )
    out = pl.pallas_call(
        functools.partial(_final_kernel, tm=tm, n_steps=n_steps),
        grid_spec=grid_spec,
        out_shape=jax.ShapeDtypeStruct((N, D), F32),
        compiler_params=_cparams(("arbitrary",)),
        name="final",
    )(pos_sm, ys, x1.reshape(N, D), gate, mod6, g_post_ffn.reshape(1, D))
    return out.reshape(B, S, D)


def _layer(x, c, positions, w_ada, b_ada, g_pre_mix, g_post_mix, g_pre_ffn, g_post_ffn,
           w_in, cmp_pos, w_cmp_k1, w_cmp_k2, w_cmp_v1, w_cmp_v2, g_out_nsa, g_out_dil, w_o,
           w_router, b_router, w_up, b_up, w_down, b_down):
    B, S, D = x.shape
    mod6 = _ada(c, w_ada, b_ada).reshape(B, 6, D)
    (q_raw, q_rot, q_rot_t, kc, vc, ks, vs_t, kw, vw, gates, qbs, kbs, vbs) = _proj(x, mod6, g_pre_mix, positions,
                                                                                   w_in)
    kcf, vcf = _cmpmlp(kc, vc, cmp_pos, w_cmp_k1, w_cmp_k2, w_cmp_v1, w_cmp_v2)
    o_cmp, bias_t = _cmpsel(q_raw, kcf, vcf)
    o_slc = _slc(q_rot_t, bias_t, ks, vs_t)
    o_win = _band(q_rot, kw, vw, dil=1, max_dist=WIN_NSA - 1, tq=128, tk=256, nt=3, shared_kv=True,
                  with_lse=False)
    ods, lses = [], []
    for (window, dil), qb, kb, vb in zip(DIL_CONFIGS, qbs, kbs, vbs):
        o, lse = _band(qb, kb, vb, dil=dil, max_dist=window // dil, tq=128, tk=128, nt=2, shared_kv=False,
                       with_lse=True)
        ods.append(o)
        lses.append(lse)
    x1 = _out(o_cmp, o_slc, o_win, gates, ods, lses, x, mod6, g_out_nsa, g_out_dil, w_o, g_post_mix)
    h2, top_idx, gate, rank, counts = _route(x1, mod6, g_pre_ffn, w_router, b_router)
    tm_moe = 512
    pos, meta, n_tiles = _moe_layout(top_idx[:, :TOP_K], rank[:, :TOP_K], counts[0, :N_EXPERTS], tm_moe)
    pos_flat = pos.reshape(B * S * TOP_K)
    xs = _dispatch(h2, pos_flat, meta["pad_start"], meta["pad_len"], meta["n_valid"], n_tiles, tm_moe)
    ys = _moe(xs, meta, w_up, b_up, w_down, b_down, tm_moe)
    return _final(ys, pos_flat, gate, x1, mod6, g_post_ffn)


def kernel(x, c, positions, w_ada, b_ada, g_pre_mix, g_post_mix, g_pre_ffn, g_post_ffn, w_in, cmp_pos,
           w_cmp_k1, w_cmp_k2, w_cmp_v1, w_cmp_v2, g_out_nsa, g_out_dil, w_o, w_router, b_router,
           w_up, b_up, w_down, b_down):
    depth = w_ada.shape[0]
    for l in range(depth):
        x = _layer(x, c, positions, w_ada[l], b_ada[l], g_pre_mix[l], g_post_mix[l], g_pre_ffn[l],
                   g_post_ffn[l], w_in[l], cmp_pos[l], w_cmp_k1[l], w_cmp_k2[l], w_cmp_v1[l], w_cmp_v2[l],
                   g_out_nsa[l], g_out_dil[l], w_o[l], w_router[l], b_router[l], w_up[l], b_up[l],
                   w_down[l], b_down[l])
    return x
```

```python
import functools

import numpy as np
import jax
import jax.numpy as jnp
from jax import lax
from jax.experimental import pallas as pl
from jax.experimental.pallas import tpu as pltpu

F32 = jnp.float32
BF16 = jnp.bfloat16
I32 = jnp.int32

D_MODEL = 1024
HEAD_DIM = 64
N_HEADS_NSA = 8
N_KV_NSA = 2
GROUP_NSA = 4
N_HEADS_DIL = 8
ROPE_THETA = 500000.0
ROPE_DIM = 16
ROPE_HALF = 8
CMP_BLOCK = 32
CMP_STRIDE = 16
CMP_HIDDEN = 256
SLC_BLOCK = 64
SLC_SHIFT = 6
SLC_TOPK = 16
WIN_NSA = 512
DIL_CONFIGS = ((128, 1), (512, 4), (2048, 16))
N_EXPERTS = 32
TOP_K = 4
D_FF = 1024
SWIGLU_LIMIT = 7.0
SWIGLU_ALPHA = 1.702
RMS_EPS = 1e-6
NEG_INF = -1e30
FORCE_SCORE = 1e9
SCALE = HEAD_DIM ** -0.5
LOG2E = 1.4426950408889634
DEN_ROWS = 16

Q_NSA = 512
KV_NSA = 128
Q_DIL = 512
LANES = 128
SUBLANES = 8
VMEM_LIMIT = 56 * 1024 * 1024
DMA_UNROLL = 8

_NT = (((1,), (1,)), ((), ()))


def _cparams(sem):
    return pltpu.CompilerParams(dimension_semantics=sem, vmem_limit_bytes=VMEM_LIMIT)


def _rms(x, g):
    return x * lax.rsqrt(jnp.mean(x * x, axis=-1, keepdims=True) + RMS_EPS) * g


def _split3_dot(a, e):
    hi = a.astype(BF16)
    r1 = a - hi.astype(F32)
    mid = r1.astype(BF16)
    lo = (r1 - mid.astype(F32)).astype(BF16)
    return (jnp.dot(hi, e, preferred_element_type=F32) + jnp.dot(mid, e, preferred_element_type=F32)
            + jnp.dot(lo, e, preferred_element_type=F32))


def _ada_kernel(c_ref, w_ref, b_ref, o_ref):
    c = c_ref[...]
    a = c * jax.nn.sigmoid(c)
    o_ref[...] = jnp.dot(a, w_ref[...], preferred_element_type=F32,
                         precision=lax.Precision.HIGHEST) + b_ref[...]


def _ada(c, w_ada, b_ada):
    B, D = c.shape
    n = w_ada.shape[1] // D
    return pl.pallas_call(
        _ada_kernel,
        grid=(n,),
        in_specs=[pl.BlockSpec((B, D), lambda j: (0, 0)),
                  pl.BlockSpec((D, D), lambda j: (0, j)),
                  pl.BlockSpec((1, D), lambda j: (0, j))],
        out_specs=pl.BlockSpec((B, D), lambda j: (0, j)),
        out_shape=jax.ShapeDtypeStruct((B, n * D), F32),
        compiler_params=_cparams(("arbitrary",)),
        name="ada",
    )(c, w_ada, b_ada.reshape(1, -1))


def _rope_tables(pos_col, inv_row, s1_row, s2_row):
    ang = pos_col * inv_row
    cs = jnp.cos(ang)
    sn = jnp.sin(ang)
    return cs, sn * s1_row, sn * s2_row


def _rope_blk(x, cs, s1, s2):
    return x * cs + pltpu.roll(x, LANES - ROPE_HALF, 1) * s1 + pltpu.roll(x, ROPE_HALF, 1) * s2


def _proj_kernel(x_ref, mod_ref, g_ref, pos_ref, tab_ref, wa_ref, wg_ref, wb_ref,
                 qraw_ref, qrott_ref, kc_ref, vc_ref, ks_ref, vst_ref, kw_ref, vwt_ref, gt_ref,
                 *dil_refs_and_scratch):
    n_cfg = len(DIL_CONFIGS)
    qb_refs = dil_refs_and_scratch[0:n_cfg]
    kb_refs = dil_refs_and_scratch[n_cfg:2 * n_cfg]
    vb_refs = dil_refs_and_scratch[2 * n_cfg:3 * n_cfg]
    st_sc = dil_refs_and_scratch[3 * n_cfg]
    tm = x_ref.shape[1]

    def emit_dilated(val, blk, refs):
        st_sc[...] = val
        for (_, d), ref in zip(DIL_CONFIGS, refs):
            for r in range(d):
                piece = val if d == 1 else st_sc[pl.ds(r, tm // d, stride=d), :]
                lo = r * Q_DIL + blk * LANES
                ref[0, :, lo:lo + LANES] = piece.astype(BF16)

    x = x_ref[0]
    sh = mod_ref[0, 0:1, :]
    sc = mod_ref[0, 1:2, :]
    h = (_rms(x, g_ref[...]) * (1.0 + sc) + sh).astype(BF16)
    cs, s1, s2 = _rope_tables(pos_ref[0], tab_ref[0:1, :], tab_ref[1:2, :], tab_ref[2:3, :])

    pa = jnp.dot(h, wa_ref[...], preferred_element_type=F32)
    for r in range(Q_NSA // LANES):
        blk = pa[:, r * LANES:(r + 1) * LANES]
        qraw_ref[0, :, r * LANES:(r + 1) * LANES] = (blk * SCALE).astype(BF16)
        rot = _rope_blk(blk, cs, s1, s2) * (SCALE * LOG2E)
        qrott_ref[0, r * LANES:(r + 1) * LANES, :] = rot.T.astype(BF16)
    o = Q_NSA
    kc_ref[0] = pa[:, o:o + LANES]
    vc_ref[0] = pa[:, o + LANES:o + 2 * LANES]
    ks_ref[0] = _rope_blk(pa[:, o + 2 * LANES:o + 3 * LANES], cs, s1, s2).astype(BF16)
    vst_ref[0] = pa[:, o + 3 * LANES:o + 4 * LANES].T.astype(BF16)
    kw_ref[0] = _rope_blk(pa[:, o + 4 * LANES:o + 5 * LANES], cs, s1, s2).astype(BF16)
    vwt_ref[0] = pa[:, o + 5 * LANES:o + 6 * LANES].T.astype(BF16)

    gt_ref[0] = jnp.dot(h, wg_ref[...], preferred_element_type=F32)

    pb = jnp.dot(h, wb_ref[...], preferred_element_type=F32)
    for blk in range(Q_DIL // LANES):
        lanes = slice(blk * LANES, (blk + 1) * LANES)
        emit_dilated(_rope_blk(pb[:, lanes], cs, s1, s2) * SCALE, blk, qb_refs)
        emit_dilated(_rope_blk(pb[:, Q_DIL + blk * LANES:Q_DIL + (blk + 1) * LANES], cs, s1, s2), blk, kb_refs)
        emit_dilated(pb[:, 2 * Q_DIL + blk * LANES:2 * Q_DIL + (blk + 1) * LANES], blk, vb_refs)


def _nsa_perm():
    cols = []
    for r in range(GROUP_NSA):
        for hk in range(N_KV_NSA):
            hq = hk * GROUP_NSA + r
            cols.extend(range(hq * HEAD_DIM, (hq + 1) * HEAD_DIM))
    return np.asarray(cols, np.int32)


def _rope_const_table():
    half = ROPE_HALF
    inv = ROPE_THETA ** (-jnp.arange(half, dtype=F32) / half)
    lane = np.arange(LANES) % HEAD_DIM
    inv_row = jnp.where(lane < ROPE_DIM, inv[lane % half], 0.0).astype(F32)
    s1 = np.where(lane < half, -1.0, 0.0).astype(np.float32)
    s2 = np.where((lane >= half) & (lane < ROPE_DIM), 1.0, 0.0).astype(np.float32)
    tab = jnp.zeros((8, LANES), F32).at[0].set(inv_row).at[1].set(s1).at[2].set(s2)
    return tab


def _proj(x, mod6, g_pre, positions, w_in, tm=512):
    B, S, D = x.shape
    perm = _nsa_perm()
    gate_lo = Q_NSA + 6 * KV_NSA
    n_gate = 3 * N_HEADS_NSA
    w_a = jnp.concatenate([w_in[:, :Q_NSA][:, perm], w_in[:, Q_NSA:gate_lo]], axis=1).astype(BF16)
    w_g = jnp.pad(w_in[:, gate_lo:gate_lo + n_gate], ((0, 0), (0, LANES - n_gate))).astype(BF16)
    w_b = w_in[:, gate_lo + n_gate:].astype(BF16)
    pos = positions.astype(F32).reshape(B, S, 1)
    tab = _rope_const_table()
    wa_n, wb_n = w_a.shape[1], w_b.shape[1]

    def tok(width, dtype):
        return (pl.BlockSpec((1, tm, width), lambda b, i: (b, i, 0)),
                jax.ShapeDtypeStruct((B, S, width), dtype))

    def tok_t(width, dtype):
        return (pl.BlockSpec((1, width, tm), lambda b, i: (b, 0, i)),
                jax.ShapeDtypeStruct((B, width, S), dtype))

    def dil_view(d):
        return (pl.BlockSpec((1, tm // d, d * Q_DIL), lambda b, i: (b, i, 0)),
                jax.ShapeDtypeStruct((B, S // d, d * Q_DIL), BF16))

    outs = [tok(Q_NSA, BF16), tok_t(Q_NSA, BF16), tok(LANES, F32), tok(LANES, F32),
            tok(LANES, BF16), tok_t(LANES, BF16), tok(LANES, BF16), tok_t(LANES, BF16), tok(LANES, F32)]
    outs += [dil_view(d) for _ in range(3) for _, d in DIL_CONFIGS]
    res = pl.pallas_call(
        _proj_kernel,
        grid=(B, S // tm),
        in_specs=[pl.BlockSpec((1, tm, D), lambda b, i: (b, i, 0)),
                  pl.BlockSpec((1, 6, D), lambda b, i: (b, 0, 0)),
                  pl.BlockSpec((1, D), lambda b, i: (0, 0)),
                  pl.BlockSpec((1, tm, 1), lambda b, i: (b, i, 0)),
                  pl.BlockSpec((8, LANES), lambda b, i: (0, 0)),
                  pl.BlockSpec((D, wa_n), lambda b, i: (0, 0)),
                  pl.BlockSpec((D, LANES), lambda b, i: (0, 0)),
                  pl.BlockSpec((D, wb_n), lambda b, i: (0, 0))],
        out_specs=[o[0] for o in outs],
        out_shape=[o[1] for o in outs],
        scratch_shapes=[pltpu.VMEM((tm, LANES), F32)],
        compiler_params=_cparams(("parallel", "parallel")),
        name="proj",
    )(x, mod6, g_pre.reshape(1, D), pos, tab, w_a, w_g, w_b)
    n_cfg = len(DIL_CONFIGS)
    n0 = len(outs) - 3 * n_cfg
    return tuple(res[:n0]) + (res[n0:n0 + n_cfg], res[n0 + n_cfg:n0 + 2 * n_cfg], res[n0 + 2 * n_cfg:])


def _cmpmlp_kernel(a_ref, p_ref, w1_ref, w2_ref, o_ref):
    a = a_ref[0, 0].astype(BF16)
    w1 = w1_ref[0]
    half = CMP_STRIDE * HEAD_DIM
    u = jnp.dot(a, w1[:half], preferred_element_type=F32)
    v = jnp.dot(a, w1[half:], preferred_element_type=F32)
    bias = jnp.dot(p_ref[...], w1, preferred_element_type=F32)[0:1]
    n16 = u.shape[0]
    hid = jax.nn.gelu(u + pltpu.roll(v, n16 - 1, 0) + bias)
    o_ref[0, 0] = jnp.dot(hid.astype(BF16), w2_ref[0], preferred_element_type=F32)


def _cmpmlp(kc, vc, cmp_pos, w_k1, w_k2, w_v1, w_v2):
    B, S, _ = kc.shape
    n16 = S // CMP_STRIDE
    seg = CMP_STRIDE * HEAD_DIM

    def segs(a):
        return a.reshape(B, n16, CMP_STRIDE, N_KV_NSA, HEAD_DIM).transpose(0, 3, 1, 2, 4).reshape(
            B * N_KV_NSA, n16, seg)

    a = jnp.stack([segs(kc), segs(vc)], axis=0)
    w1 = jnp.stack([w_k1, w_v1], axis=0).astype(BF16)
    w2 = jnp.stack([w_k2, w_v2], axis=0).astype(BF16)
    p8 = jnp.broadcast_to(cmp_pos.reshape(1, CMP_BLOCK * HEAD_DIM), (8, CMP_BLOCK * HEAD_DIM)).astype(BF16)
    out = pl.pallas_call(
        _cmpmlp_kernel,
        grid=(2, B * N_KV_NSA),
        in_specs=[pl.BlockSpec((1, 1, n16, seg), lambda t, g: (t, g, 0, 0)),
                  pl.BlockSpec((8, 2 * seg), lambda t, g: (0, 0)),
                  pl.BlockSpec((1, 2 * seg, CMP_HIDDEN), lambda t, g: (t, 0, 0)),
                  pl.BlockSpec((1, CMP_HIDDEN, HEAD_DIM), lambda t, g: (t, 0, 0))],
        out_specs=pl.BlockSpec((1, 1, n16, HEAD_DIM), lambda t, g: (t, g, 0, 0)),
        out_shape=jax.ShapeDtypeStruct((2, B * N_KV_NSA, n16, HEAD_DIM), F32),
        compiler_params=_cparams(("parallel", "parallel")),
        name="cmpmlp",
    )(a, p8, w1, w2)
    out = out.reshape(2, B, N_KV_NSA, n16, HEAD_DIM).transpose(0, 1, 3, 2, 4).reshape(2, B, n16, KV_NSA)
    return out[0].astype(BF16), out[1].astype(BF16)


def _stack_heads(q, tq):
    lane = lax.broadcasted_iota(I32, (tq, LANES), 1)
    half0 = lane < HEAD_DIM
    zero = jnp.zeros((tq, LANES), q.dtype)
    rows = []
    for hk in range(N_KV_NSA):
        keep = half0 if hk == 0 else jnp.logical_not(half0)
        for r in range(GROUP_NSA):
            rows.append(jnp.where(keep, q[:, r * LANES:(r + 1) * LANES], zero))
    return jnp.concatenate(rows, axis=0), half0


def _unstack_heads(o, half0, tq, o_ref):
    for r in range(GROUP_NSA):
        o0 = o[r * tq:(r + 1) * tq]
        o1 = o[(GROUP_NSA + r) * tq:(GROUP_NSA + r + 1) * tq]
        o_ref[0, :, r * LANES:(r + 1) * LANES] = jnp.where(half0, o0, o1)


def _cmpsel_kernel(q_ref, kc_ref, vc_ref, ovt_ref, o_ref, bias_ref, *, tq):
    i = pl.program_id(1)
    qs, half0 = _stack_heads(q_ref[0], tq)
    n16 = kc_ref.shape[1]
    rows = 8 * tq
    s = lax.dot_general(qs, kc_ref[0], _NT, preferred_element_type=F32)
    t = i * tq + (lax.broadcasted_iota(I32, (rows, n16), 0) & (tq - 1))
    c = lax.broadcasted_iota(I32, (rows, n16), 1)
    valid = (c * CMP_STRIDE + (CMP_BLOCK - 1)) <= t
    s = jnp.where(valid, s, NEG_INF)
    m = jnp.max(s, axis=-1, keepdims=True)
    e = jnp.exp(s - m)
    l = jnp.sum(e, axis=-1, keepdims=True)
    p = jnp.where(valid, e / l, 0.0)
    o = jnp.dot(p.astype(BF16), vc_ref[0], preferred_element_type=F32)
    _unstack_heads(o, half0, tq, o_ref)

    n_slc = ovt_ref.shape[0]
    j = lax.broadcasted_iota(I32, (n_slc, tq), 0)
    cur = (i * tq + lax.broadcasted_iota(I32, (n_slc, tq), 1)) >> SLC_SHIFT
    forced = (j == 0) | (j == cur) | (j == cur - 1)
    ovt = ovt_ref[...]
    biases = []
    for hk in range(N_KV_NSA):
        ps = p[hk * GROUP_NSA * tq:(hk * GROUP_NSA + 1) * tq]
        for r in range(1, GROUP_NSA):
            ps = ps + p[(hk * GROUP_NSA + r) * tq:(hk * GROUP_NSA + r + 1) * tq]
        hi = ps.astype(BF16)
        lo = (ps - hi.astype(F32)).astype(BF16)
        pslc = (lax.dot_general(ovt, hi, _NT, preferred_element_type=F32)
                + lax.dot_general(ovt, lo, _NT, preferred_element_type=F32))
        score = jnp.where(forced, FORCE_SCORE, jnp.where(j <= cur, pslc, -1.0))
        rank = jnp.zeros((n_slc, tq), I32)
        for ii in range(n_slc):
            ri = score[ii:ii + 1, :]
            beats = (ri > score) | ((ri == score) & (j > ii))
            rank = rank + beats.astype(I32)
        biases.append(jnp.where(rank < SLC_TOPK, 0.0, NEG_INF))
    bias_ref[0] = jnp.concatenate(biases, axis=0).astype(BF16)


def _overlap_t(S):
    n16 = S // CMP_STRIDE
    n_slc = S // SLC_BLOCK
    cs = np.arange(n16) * CMP_STRIDE
    js = np.arange(n_slc) * SLC_BLOCK
    ov = np.clip(np.minimum(cs[:, None] + CMP_BLOCK, js[None, :] + SLC_BLOCK)
                 - np.maximum(cs[:, None], js[None, :]), 0, None).astype(np.float32) / CMP_BLOCK
    ov[n16 - 1] = 0.0
    return jnp.asarray(ov.T, BF16)


def _cmpsel(q_raw, kcf, vcf, tq=128):
    B, S, _ = q_raw.shape
    n16 = S // CMP_STRIDE
    n_slc = S // SLC_BLOCK
    assert n_slc == HEAD_DIM, "selection bias is laid out as one 64-lane half per kv head"
    return pl.pallas_call(
        functools.partial(_cmpsel_kernel, tq=tq),
        grid=(B, S // tq),
        in_specs=[pl.BlockSpec((1, tq, Q_NSA), lambda b, i: (b, i, 0)),
                  pl.BlockSpec((1, n16, KV_NSA), lambda b, i: (b, 0, 0)),
                  pl.BlockSpec((1, n16, KV_NSA), lambda b, i: (b, 0, 0)),
                  pl.BlockSpec((n_slc, n16), lambda b, i: (0, 0))],
        out_specs=[pl.BlockSpec((1, tq, Q_NSA), lambda b, i: (b, i, 0)),
                   pl.BlockSpec((1, 2 * n_slc, tq), lambda b, i: (b, 0, i))],
        out_shape=[jax.ShapeDtypeStruct((B, S, Q_NSA), F32),
                   jax.ShapeDtypeStruct((B, 2 * n_slc, S), BF16)],
        compiler_params=_cparams(("parallel", "parallel")),
        name="cmpsel",
    )(q_raw, kcf, vcf, _overlap_t(S))


def _slc_kernel(qi_ref, kj_ref, qt_ref, bt_ref, k_ref, vt_ref, o_ref, qa_sc, m_sc, acc_sc, *, tq, tk):
    step = pl.program_id(1)
    i = qi_ref[step]
    kj = kj_ref[step]
    last = (i * tq + tq - 1) // tk
    cols = 8 * tq

    @pl.when(kj == 0)
    def _():
        half0 = lax.broadcasted_iota(I32, (LANES, tq), 0) < HEAD_DIM
        zero = jnp.zeros((LANES, tq), BF16)
        bt = bt_ref[0]
        groups = []
        for hk in range(N_KV_NSA):
            keep = half0 if hk == 0 else jnp.logical_not(half0)
            bh = jnp.where(keep, bt, zero)
            for r in range(GROUP_NSA):
                qb = qt_ref[0, r * LANES:(r + 1) * LANES, :]
                groups.append(jnp.concatenate([jnp.where(keep, qb, zero), bh], axis=0))
        qa_sc[...] = jnp.concatenate(groups, axis=1)
        m_sc[...] = jnp.full((1, cols), NEG_INF, F32)
        acc_sc[...] = jnp.zeros(acc_sc.shape, F32)

    def update(on_diagonal):
        kblk = (kj * tk + lax.broadcasted_iota(I32, (tk, LANES), 0)) >> SLC_SHIFT
        lane = lax.broadcasted_iota(I32, (tk, LANES), 1) & (HEAD_DIM - 1)
        onehot = jnp.where(kblk == lane, 1.0, 0.0).astype(BF16)
        kaug = jnp.concatenate([k_ref[0], onehot], axis=1)
        st = jnp.dot(kaug, qa_sc[...], preferred_element_type=F32)
        if on_diagonal:
            kpos = kj * tk + lax.broadcasted_iota(I32, (tk, cols), 0)
            t = i * tq + (lax.broadcasted_iota(I32, (tk, cols), 1) & (tq - 1))
            st = jnp.where(kpos <= t, st, NEG_INF)
        m_old = m_sc[...]
        m_new = jnp.maximum(m_old, jnp.max(st, axis=0, keepdims=True))
        alpha = jnp.exp2(m_old - m_new)
        p = jnp.exp2(st - m_new).astype(BF16)
        vt_ones = jnp.concatenate([vt_ref[0], jnp.ones((DEN_ROWS, tk), BF16)], axis=0)
        acc_sc[...] = alpha * acc_sc[...] + jnp.dot(vt_ones, p, preferred_element_type=F32)
        m_sc[...] = m_new

    @pl.when(kj < last)
    def _():
        update(False)

    @pl.when(kj == last)
    def _():
        update(True)
        acc = acc_sc[...]
        ot = acc[:LANES] / acc[LANES:LANES + 1]
        half0 = lax.broadcasted_iota(I32, (tq, LANES), 1) < HEAD_DIM
        for r in range(GROUP_NSA):
            o0 = ot[:, r * tq:(r + 1) * tq].T
            o1 = ot[:, (GROUP_NSA + r) * tq:(GROUP_NSA + r + 1) * tq].T
            o_ref[0, :, r * LANES:(r + 1) * LANES] = jnp.where(half0, o0, o1)


def _slc(q_rot_t, bias_t, ks, vs_t, tq=128, tk=512):
    B, _, S = q_rot_t.shape
    assert tk % tq == 0
    qi, kj = [], []
    for i in range(S // tq):
        for j in range((i * tq + tq - 1) // tk + 1):
            qi.append(i)
            kj.append(j)
    grid_spec = pltpu.PrefetchScalarGridSpec(
        num_scalar_prefetch=2,
        grid=(B, len(qi)),
        in_specs=[pl.BlockSpec((1, Q_NSA, tq), lambda b, s, qi, kj: (b, 0, qi[s])),
                  pl.BlockSpec((1, LANES, tq), lambda b, s, qi, kj: (b, 0, qi[s])),
                  pl.BlockSpec((1, tk, KV_NSA), lambda b, s, qi, kj: (b, kj[s], 0)),
                  pl.BlockSpec((1, KV_NSA, tk), lambda b, s, qi, kj: (b, 0, kj[s]))],
        out_specs=pl.BlockSpec((1, tq, Q_NSA), lambda b, s, qi, kj: (b, qi[s], 0)),
        scratch_shapes=[pltpu.VMEM((2 * LANES, 8 * tq), BF16),
                        pltpu.VMEM((1, 8 * tq), F32),
                        pltpu.VMEM((LANES + DEN_ROWS, 8 * tq), F32)],
    )
    return pl.pallas_call(
        functools.partial(_slc_kernel, tq=tq, tk=tk),
        grid_spec=grid_spec,
        out_shape=jax.ShapeDtypeStruct((B, S, Q_NSA), F32),
        compiler_params=_cparams(("parallel", "arbitrary")),
        name="slc",
    )(jnp.asarray(qi, I32), jnp.asarray(kj, I32), q_rot_t, bias_t, ks, vs_t)


def _win_kernel(qt_ref, k0_ref, k1_ref, k2_ref, v0_ref, v1_ref, v2_ref, o_ref, *, tq, tk, max_dist):
    i = pl.program_id(1)
    a = (i * tq) // tk
    cols = 8 * tq
    half0 = lax.broadcasted_iota(I32, (LANES, tq), 0) < HEAD_DIM
    zero = jnp.zeros((LANES, tq), BF16)
    groups = []
    for hk in range(N_KV_NSA):
        keep = half0 if hk == 0 else jnp.logical_not(half0)
        for r in range(GROUP_NSA):
            groups.append(jnp.where(keep, qt_ref[0, r * LANES:(r + 1) * LANES, :], zero))
    qa = jnp.concatenate(groups, axis=1)
    t = i * tq + (lax.broadcasted_iota(I32, (tk, cols), 1) & (tq - 1))
    row = lax.broadcasted_iota(I32, (tk, cols), 0)
    sts = []
    for jt, k_ref in enumerate((k0_ref, k1_ref, k2_ref)):
        st = jnp.dot(k_ref[0], qa, preferred_element_type=F32)
        kpos = (a - 2 + jt) * tk + row
        if jt == 0:
            st = jnp.where((t - kpos <= max_dist) & (kpos >= 0), st, NEG_INF)
        elif jt == 1:
            st = jnp.where(kpos >= 0, st, NEG_INF)
        else:
            st = jnp.where(kpos <= t, st, NEG_INF)
        sts.append(st)
    m = jnp.max(sts[0], axis=0, keepdims=True)
    for st in sts[1:]:
        m = jnp.maximum(m, jnp.max(st, axis=0, keepdims=True))
    acc = jnp.zeros((LANES + DEN_ROWS, cols), F32)
    for st, v_ref in zip(sts, (v0_ref, v1_ref, v2_ref)):
        vt_ones = jnp.concatenate([v_ref[0], jnp.ones((DEN_ROWS, tk), BF16)], axis=0)
        acc = acc + jnp.dot(vt_ones, jnp.exp2(st - m).astype(BF16), preferred_element_type=F32)
    ot = acc[:LANES] / acc[LANES:LANES + 1]
    lane_half0 = lax.broadcasted_iota(I32, (tq, LANES), 1) < HEAD_DIM
    for r in range(GROUP_NSA):
        o0 = ot[:, r * tq:(r + 1) * tq].T
        o1 = ot[:, (GROUP_NSA + r) * tq:(GROUP_NSA + r + 1) * tq].T
        o_ref[0, :, r * LANES:(r + 1) * LANES] = jnp.where(lane_half0, o0, o1)


def _win(q_rot_t, kw, vw_t, tq=128, tk=256):
    B, _, S = q_rot_t.shape
    max_dist = WIN_NSA - 1
    assert tk % tq == 0 and 2 * tk == max_dist + 1

    def k_map(jt):
        return lambda b, i: (b, jnp.maximum((i * tq) // tk - 2 + jt, 0), 0)

    def v_map(jt):
        return lambda b, i: (b, 0, jnp.maximum((i * tq) // tk - 2 + jt, 0))

    return pl.pallas_call(
        functools.partial(_win_kernel, tq=tq, tk=tk, max_dist=max_dist),
        grid=(B, S // tq),
        in_specs=[pl.BlockSpec((1, Q_NSA, tq), lambda b, i: (b, 0, i))]
        + [pl.BlockSpec((1, tk, KV_NSA), k_map(jt)) for jt in range(3)]
        + [pl.BlockSpec((1, KV_NSA, tk), v_map(jt)) for jt in range(3)],
        out_specs=pl.BlockSpec((1, tq, Q_NSA), lambda b, i: (b, i, 0)),
        out_shape=jax.ShapeDtypeStruct((B, S, Q_NSA), F32),
        compiler_params=_cparams(("parallel", "parallel")),
        name="win",
    )(q_rot_t, kw, kw, kw, vw_t, vw_t, vw_t)


def _band_kernel(*refs, tq, tk, nt, max_dist, shared_kv, with_lse):
    q_ref = refs[0]
    k_refs = refs[1:1 + nt]
    v_refs = refs[1 + nt:1 + 2 * nt]
    o_ref = refs[1 + 2 * nt]
    lse_ref = refs[2 + 2 * nt] if with_lse else None
    i = pl.program_id(2)
    a = (i * tq) // tk
    lane = lax.broadcasted_iota(I32, (tq, LANES), 1)
    half0 = lane < HEAD_DIM
    row_t = i * tq + (lax.broadcasted_iota(I32, (2 * tq, tk), 0) & (tq - 1))
    col = lax.broadcasted_iota(I32, (2 * tq, tk), 1)
    oks = []
    for jt in range(nt):
        kpos = (a - (nt - 1) + jt) * tk + col
        d = row_t - kpos
        oks.append((d >= 0) & (d <= max_dist) & (kpos >= 0))
    lse_acc = jnp.zeros((tq, LANES), F32)
    for blk in range(Q_NSA // LANES):
        qb = q_ref[0, :, blk * LANES:(blk + 1) * LANES]
        zero = jnp.zeros_like(qb)
        qs = jnp.concatenate([jnp.where(half0, qb, zero), jnp.where(half0, zero, qb)], axis=0)
        kv0 = 0 if shared_kv else blk * LANES
        ss = []
        for jt in range(nt):
            kt = k_refs[jt][0, :, kv0:kv0 + LANES]
            s = lax.dot_general(qs, kt, _NT, preferred_element_type=F32)
            ss.append(jnp.where(oks[jt], s, NEG_INF))
        m = jnp.max(ss[0], axis=-1, keepdims=True)
        for s in ss[1:]:
            m = jnp.maximum(m, jnp.max(s, axis=-1, keepdims=True))
        l = jnp.zeros((2 * tq, 1), F32)
        o = jnp.zeros((2 * tq, LANES), F32)
        for jt in range(nt):
            p = jnp.exp(ss[jt] - m)
            l = l + jnp.sum(p, axis=-1, keepdims=True)
            o = o + jnp.dot(p.astype(BF16), v_refs[jt][0, :, kv0:kv0 + LANES], preferred_element_type=F32)
        o = o / l
        o_ref[0, :, blk * LANES:(blk + 1) * LANES] = jnp.where(half0, o[:tq], o[tq:])
        if with_lse:
            lse = m + jnp.log(l)
            lse_acc = jnp.where(lane == 2 * blk, lse[:tq], lse_acc)
            lse_acc = jnp.where(lane == 2 * blk + 1, lse[tq:], lse_acc)
    if with_lse:
        lse_ref[0] = lse_acc


def _band(q, k, v, *, dil, max_dist, tq, tk, nt, shared_kv, with_lse):
    B, m, _ = q.shape
    kvw = k.shape[-1] // dil
    qv, kv_, vv = q, k, v

    def kv_map(jt):
        return lambda b, r, i: (b, jnp.maximum((i * tq) // tk - (nt - 1) + jt, 0), r)

    tok_spec = pl.BlockSpec((1, tq, Q_NSA), lambda b, r, i: (b, i, r))
    in_specs = [tok_spec]
    in_specs += [pl.BlockSpec((1, tk, kvw), kv_map(jt)) for jt in range(nt)]
    in_specs += [pl.BlockSpec((1, tk, kvw), kv_map(jt)) for jt in range(nt)]
    out_specs = [tok_spec]
    out_shape = [jax.ShapeDtypeStruct((B, m, dil * Q_NSA), F32)]
    if with_lse:
        out_specs.append(pl.BlockSpec((1, tq, LANES), lambda b, r, i: (b, i, r)))
        out_shape.append(jax.ShapeDtypeStruct((B, m, dil * LANES), F32))
    res = pl.pallas_call(
        functools.partial(_band_kernel, tq=tq, tk=tk, nt=nt, max_dist=max_dist, shared_kv=shared_kv,
                          with_lse=with_lse),
        grid=(B, dil, m // tq),
        in_specs=in_specs,
        out_specs=out_specs,
        out_shape=out_shape,
        compiler_params=_cparams(("parallel", "parallel", "parallel")),
        name=f"band_d{dil}_w{max_dist}",
    )(qv, *([kv_] * nt), *([vv] * nt))
    return (res[0], res[1]) if with_lse else res[0]


def _out_kernel(ocmp_ref, oslc_ref, owin_ref, gt_ref, od1_ref, od4_ref, od16_ref, l1_ref, l4_ref, l16_ref,
                x_ref, mod_ref, eg_ref, ed_ref, gnsa_ref, gdil_ref, wo_ref, gpost_ref, o_ref, st_sc):
    tm = x_ref.shape[1]

    def token_order(ref, d, n_blk):
        if d == 1:
            return ref[0]
        blocks = []
        for blk in range(n_blk):
            for r in range(d):
                lo = (r * n_blk + blk) * LANES
                st_sc[pl.ds(r, tm // d, stride=d), :] = ref[0, :, lo:lo + LANES]
            blocks.append(st_sc[...])
        return blocks[0] if n_blk == 1 else jnp.concatenate(blocks, axis=1)

    sg = jax.nn.sigmoid(gt_ref[0])
    oa = (_split3_dot(sg, eg_ref[0]) * ocmp_ref[0] + _split3_dot(sg, eg_ref[1]) * oslc_ref[0]
          + _split3_dot(sg, eg_ref[2]) * owin_ref[0])
    ya = _rms(oa, gnsa_ref[...])

    dils = [d for _, d in DIL_CONFIGS]
    n_blk = Q_DIL // LANES
    l1, l4, l16 = [token_order(ref, d, 1) for ref, d in zip((l1_ref, l4_ref, l16_ref), dils)]
    mx = jnp.maximum(jnp.maximum(l1, l4), l16)
    e1, e4, e16 = jnp.exp(l1 - mx), jnp.exp(l4 - mx), jnp.exp(l16 - mx)
    den = e1 + e4 + e16
    ed = ed_ref[...]
    od1, od4, od16 = [token_order(ref, d, n_blk) for ref, d in zip((od1_ref, od4_ref, od16_ref), dils)]
    ob = (_split3_dot(e1 / den, ed) * od1 + _split3_dot(e4 / den, ed) * od4 + _split3_dot(e16 / den, ed) * od16)
    yb = _rms(ob, gdil_ref[...])

    y = jnp.concatenate([ya, yb], axis=1).astype(BF16)
    z = jnp.dot(y, wo_ref[...], preferred_element_type=F32)
    gt_m = mod_ref[0, 2:3, :]
    o_ref[0] = x_ref[0] + gt_m * _rms(z, gpost_ref[...])


def _gate_expanders():
    perm = _nsa_perm()
    eg = np.zeros((3, LANES, Q_NSA), np.float32)
    for lane_out, col in enumerate(perm):
        hq = col // HEAD_DIM
        for c in range(3):
            eg[c, hq * 3 + c, lane_out] = 1.0
    ed = np.zeros((LANES, Q_DIL), np.float32)
    for h in range(N_HEADS_DIL):
        ed[h, h * HEAD_DIM:(h + 1) * HEAD_DIM] = 1.0
    return jnp.asarray(eg, BF16), jnp.asarray(ed, BF16)


def _out(o_cmp, o_slc, o_win, gates, ods, lses, x, mod6, g_out_nsa, g_out_dil, w_o, g_post, tm=256):
    B, S, D = x.shape
    perm = _nsa_perm()
    eg, ed = _gate_expanders()
    w_o_p = jnp.concatenate([w_o[:Q_NSA][perm], w_o[Q_NSA:]], axis=0).astype(BF16)
    g_nsa_p = g_out_nsa[perm].reshape(1, Q_NSA)

    def tok(width):
        return pl.BlockSpec((1, tm, width), lambda b, i: (b, i, 0))

    def const(shape):
        return pl.BlockSpec(shape, lambda b, i: (0,) * len(shape))

    def view(d, width):
        return pl.BlockSpec((1, tm // d, d * width), lambda b, i: (b, i, 0))

    dils = [d for _, d in DIL_CONFIGS]
    return pl.pallas_call(
        _out_kernel,
        grid=(B, S // tm),
        in_specs=[tok(Q_NSA), tok(Q_NSA), tok(Q_NSA), tok(LANES)]
        + [view(d, Q_DIL) for d in dils] + [view(d, LANES) for d in dils]
        + [tok(D), pl.BlockSpec((1, 6, D), lambda b, i: (b, 0, 0)),
           const((3, LANES, Q_NSA)), const((LANES, Q_DIL)), const((1, Q_NSA)), const((1, Q_DIL)),
           const((D, D)), const((1, D))],
        out_specs=tok(D),
        out_shape=jax.ShapeDtypeStruct((B, S, D), F32),
        scratch_shapes=[pltpu.VMEM((tm, LANES), F32)],
        compiler_params=_cparams(("parallel", "parallel")),
        name="out",
    )(o_cmp, o_slc, o_win, gates, *ods, *lses, x, mod6, eg, ed, g_nsa_p, g_out_dil.reshape(1, Q_DIL),
      w_o_p, g_post.reshape(1, D))


def _route_kernel(x_ref, mod_ref, g_ref, wr_ref, br_ref, h_ref, idx_ref, gate_ref, rank_ref, cnt_ref, run_sc):
    i = pl.program_id(0)

    @pl.when(i == 0)
    def _():
        run_sc[...] = jnp.zeros_like(run_sc)

    sh = mod_ref[0, 3:4, :]
    sc = mod_ref[0, 4:5, :]
    h = _rms(x_ref[...], g_ref[...]) * (1.0 + sc) + sh
    h_ref[...] = h
    logits = jnp.dot(h, wr_ref[...], preferred_element_type=F32,
                     precision=lax.Precision.HIGHEST) + br_ref[...]
    tm = logits.shape[0]
    lane = lax.broadcasted_iota(I32, (tm, LANES), 1)
    work = jnp.where(lane < N_EXPERTS, logits, -jnp.inf)
    idx_out = jnp.zeros((tm, LANES), I32)
    val_out = jnp.full((tm, LANES), -jnp.inf, F32)
    rank_out = jnp.zeros((tm, LANES), F32)
    tri = jnp.where(lax.broadcasted_iota(I32, (tm, tm), 1) < lax.broadcasted_iota(I32, (tm, tm), 0),
                    1.0, 0.0).astype(BF16)
    base = run_sc[0:1, :]
    for k in range(TOP_K):
        mx = jnp.max(work, axis=-1, keepdims=True)
        ix = jnp.min(jnp.where(work == mx, lane, LANES), axis=-1, keepdims=True)
        hit = lane == ix
        idx_out = jnp.where(lane == k, ix, idx_out)
        val_out = jnp.where(lane == k, mx, val_out)
        work = jnp.where(hit, -jnp.inf, work)
        onehot = jnp.where(hit, 1.0, 0.0)
        before = jnp.dot(tri, onehot.astype(BF16), preferred_element_type=F32) + base
        rank_k = jnp.sum(onehot * before, axis=-1, keepdims=True)
        rank_out = jnp.where(lane == k, rank_k, rank_out)
        base = base + jnp.sum(onehot, axis=0, keepdims=True)
    run_sc[...] = jnp.broadcast_to(base, run_sc.shape)
    e = jnp.exp(val_out - val_out[:, 0:1])
    idx_ref[...] = idx_out
    gate_ref[...] = e / jnp.sum(e, axis=-1, keepdims=True)
    rank_ref[...] = rank_out.astype(I32)
    cnt_ref[...] = run_sc[...].astype(I32)


def _route(x1, mod6, g_pre_ffn, w_router, b_router, tm=512):
    B, S, D = x1.shape
    N = B * S
    spb = S // tm
    wr = jnp.pad(w_router, ((0, 0), (0, LANES - N_EXPERTS)))
    br = jnp.pad(b_router, (0, LANES - N_EXPERTS)).reshape(1, LANES)

    def tok(width):
        return pl.BlockSpec((tm, width), lambda i: (i, 0))

    return pl.pallas_call(
        _route_kernel,
        grid=(N // tm,),
        in_specs=[tok(D), pl.BlockSpec((1, 6, D), lambda i: (i // spb, 0, 0)),
                  pl.BlockSpec((1, D), lambda i: (0, 0)),
                  pl.BlockSpec((D, LANES), lambda i: (0, 0)),
                  pl.BlockSpec((1, LANES), lambda i: (0, 0))],
        out_specs=[tok(D), tok(LANES), tok(LANES), tok(LANES), pl.BlockSpec((8, LANES), lambda i: (0, 0))],
        out_shape=[jax.ShapeDtypeStruct((N, D), F32),
                   jax.ShapeDtypeStruct((N, LANES), I32),
                   jax.ShapeDtypeStruct((N, LANES), F32),
                   jax.ShapeDtypeStruct((N, LANES), I32),
                   jax.ShapeDtypeStruct((8, LANES), I32)],
        scratch_shapes=[pltpu.VMEM((8, LANES), F32)],
        compiler_params=_cparams(("arbitrary",)),
        name="route",
    )(x1.reshape(N, D), mod6, g_pre_ffn.reshape(1, D), wr, br)


def _dispatch_kernel(pos_ref, pad0_ref, padn_ref, nt_ref, h_ref, xs_out, zbuf, sem, zsem, *, tm, tmx, n_tiles):
    i = pl.program_id(0)
    bits = tmx.bit_length() - 1

    def pad_copies(e, fn):
        p0 = pad0_ref[e]
        head = (-p0) & (SUBLANES - 1)
        head = jnp.minimum(head, padn_ref[e])
        for r in range(SUBLANES - 1):
            @pl.when(r < head)
            def _():
                fn(pltpu.make_async_copy(zbuf.at[pl.ds(0, 1), :], xs_out.at[pl.ds(p0 + r, 1), :], zsem))
        a = p0 + head
        n = padn_ref[e] - head
        for b in range(3, bits):
            size = 1 << b
            off = (n >> (b + 1)) << (b + 1)

            @pl.when((n & size) != 0)
            def _():
                fn(pltpu.make_async_copy(zbuf.at[pl.ds(0, size), :],
                                         xs_out.at[pl.ds(pl.multiple_of(a + off, SUBLANES), size), :], zsem))

    def tail_copy(t, fn):
        fn(pltpu.make_async_copy(zbuf, xs_out.at[pl.ds(pl.multiple_of(t * tmx, tmx), tmx), :], zsem))

    def for_all_fill(fn):
        def per_expert(e, carry):
            pad_copies(e, fn)
            return carry
        lax.fori_loop(0, N_EXPERTS, per_expert, 0)

        def per_tile(t, carry):
            tail_copy(t, fn)
            return carry
        lax.fori_loop(nt_ref[0], n_tiles, per_tile, 0)

    @pl.when(i == 0)
    def _():
        zbuf[...] = jnp.zeros_like(zbuf)
        for_all_fill(lambda cp: cp.start())
        for_all_fill(lambda cp: cp.wait())

    def body(r, carry):
        for k in range(TOP_K):
            dst_row = pos_ref[(i * tm + r) * TOP_K + k]
            pltpu.make_async_copy(h_ref.at[pl.ds(r, 1), :], xs_out.at[pl.ds(dst_row, 1), :], sem).start()
        return carry

    lax.fori_loop(0, tm, body, 0, unroll=DMA_UNROLL // TOP_K)
    for k in range(TOP_K):
        pltpu.make_async_copy(h_ref, xs_out.at[pl.ds(0, tm), :], sem).wait()


def _dispatch(h2, pos_flat, pad_start, pad_len, n_valid, n_tiles, tmx, tm=256):
    N, D = h2.shape
    assert tmx & (tmx - 1) == 0
    grid_spec = pltpu.PrefetchScalarGridSpec(
        num_scalar_prefetch=4,
        grid=(N // tm,),
        in_specs=[pl.BlockSpec((tm, D), lambda i, *_: (i, 0))],
        out_specs=pl.BlockSpec(memory_space=pl.ANY),
        scratch_shapes=[pltpu.VMEM((tmx, D), F32), pltpu.SemaphoreType.DMA(()), pltpu.SemaphoreType.DMA(())],
    )
    return pl.pallas_call(
        functools.partial(_dispatch_kernel, tm=tm, tmx=tmx, n_tiles=n_tiles),
        grid_spec=grid_spec,
        out_shape=jax.ShapeDtypeStruct((n_tiles * tmx, D), F32),
        compiler_params=_cparams(("arbitrary",)),
        name="dispatch",
    )(pos_flat, pad_start, pad_len, n_valid, h2)


def _moe_kernel(te_ref, nt_ref, nx_ref, sl_ref, xs_ref, wup_hbm, bup_ref, wdn_hbm, bdn_ref, pm_ref, y_ref,
                wup_in, wdn_in, wup_sc, wdn_sc, sems):
    i = pl.program_id(0)
    n_valid = nt_ref[0]
    e = te_ref[i]
    new_expert = (i == 0) | (e != te_ref[jnp.maximum(i - 1, 0)])
    slot = sl_ref[i]
    n_blk = wup_sc.shape[1] // (2 * LANES)

    def weight_copies(ex, s):
        return (pltpu.make_async_copy(wup_hbm.at[ex], wup_in.at[s], sems.at[0, s]),
                pltpu.make_async_copy(wdn_hbm.at[ex], wdn_in.at[s], sems.at[1, s]))

    @pl.when(i == 0)
    def _():
        for cp in weight_copies(e, slot):
            cp.start()

    @pl.when(new_expert & (i < n_valid))
    def _():
        for cp in weight_copies(e, slot):
            cp.wait()
        nx = nx_ref[i]

        @pl.when(nx >= 0)
        def _():
            for cp in weight_copies(nx, 1 - slot):
                cp.start()

        pm = pm_ref[...]
        for blk in range(n_blk):
            cols = slice(blk * 2 * LANES, (blk + 1) * 2 * LANES)
            w = wup_in[slot, :, cols].astype(BF16)
            wup_sc[:, cols] = jnp.dot(w, pm, preferred_element_type=F32).astype(BF16)
        wdn_sc[...] = wdn_in[slot].astype(BF16)

    @pl.when(i < n_valid)
    def _():
        xs = xs_ref[...].astype(BF16)
        u = jnp.dot(xs, wup_sc[...], preferred_element_type=F32) + bup_ref[0]
        acts = []
        for blk in range(n_blk):
            ug = jnp.minimum(u[:, blk * 2 * LANES:blk * 2 * LANES + LANES], SWIGLU_LIMIT)
            ul = jnp.clip(u[:, blk * 2 * LANES + LANES:(blk + 1) * 2 * LANES], -SWIGLU_LIMIT, SWIGLU_LIMIT)
            acts.append((ug * jax.nn.sigmoid(SWIGLU_ALPHA * ug) * (ul + 1.0)).astype(BF16))
        act = jnp.concatenate(acts, axis=1)
        y_ref[...] = jnp.dot(act, wdn_sc[...], preferred_element_type=F32) + bdn_ref[0]

    @pl.when(i >= n_valid)
    def _():
        y_ref[...] = jnp.zeros_like(y_ref)


def _moe_layout(top_idx, rank, counts, tm):
    N = top_idx.shape[0]
    n_tiles = N * TOP_K // tm + N_EXPERTS
    e_ids = jnp.arange(N_EXPERTS, dtype=I32)
    tiles_e = (counts + tm - 1) // tm
    tile_end = jnp.sum(jnp.where(e_ids[None, :] <= e_ids[:, None], tiles_e[None, :], 0), axis=1)
    start = (tile_end - tiles_e) * tm
    pos = rank + jnp.sum(jnp.where(top_idx[:, :, None] == e_ids[None, None, :], start[None, None, :], 0), axis=-1)
    n_valid = tile_end[N_EXPERTS - 1]
    tile_ids = jnp.arange(n_tiles, dtype=I32)
    tile_e = jnp.sum((jnp.minimum(tile_ids, n_valid - 1)[:, None] >= tile_end[None, :]).astype(I32), axis=1)
    prev_e = jnp.concatenate([jnp.full((1,), -1, I32), tile_e[:-1]])
    new = (tile_e != prev_e) & (tile_ids < n_valid)
    ordinal = jnp.sum(jnp.where(tile_ids[None, :] <= tile_ids[:, None], new[None, :].astype(I32), 0), axis=1) - 1
    next_first = jnp.sum(jnp.where(tile_e[:, None] == e_ids[None, :], tile_end[None, :], 0), axis=1)
    next_e = jnp.sum(jnp.where(next_first[:, None] == tile_ids[None, :], tile_e[None, :], 0), axis=1)
    next_e = jnp.where(next_first < n_valid, next_e, -1)
    meta = dict(tile_e=tile_e.astype(I32), n_valid=n_valid.astype(I32).reshape(1), next_e=next_e.astype(I32),
                slot=(ordinal & 1).astype(I32), pad_start=(start + counts).astype(I32),
                pad_len=(tiles_e * tm - counts).astype(I32))
    return pos.astype(I32), meta, n_tiles


def _glu_perm():
    pm = np.zeros((2 * LANES, 2 * LANES), np.float32)
    for j in range(LANES):
        pm[2 * j, j] = 1.0
        pm[2 * j + 1, LANES + j] = 1.0
    return jnp.asarray(pm, BF16)


def _moe(xs, meta, w_up, b_up, w_down, b_down, tm):
    P, D = xs.shape
    n_blk = D_FF // LANES
    b_up_p = b_up.reshape(N_EXPERTS, n_blk, LANES, 2).transpose(0, 1, 3, 2).reshape(N_EXPERTS, 1, 2 * D_FF)
    b_dn = b_down.reshape(N_EXPERTS, 1, D)

    def bmap(i, te, *_):
        return (te[i], 0, 0)

    grid_spec = pltpu.PrefetchScalarGridSpec(
        num_scalar_prefetch=4,
        grid=(P // tm,),
        in_specs=[pl.BlockSpec((tm, D), lambda i, *_: (i, 0)),
                  pl.BlockSpec(memory_space=pl.ANY), pl.BlockSpec((1, 1, 2 * D_FF), bmap),
                  pl.BlockSpec(memory_space=pl.ANY), pl.BlockSpec((1, 1, D), bmap),
                  pl.BlockSpec((2 * LANES, 2 * LANES), lambda i, *_: (0, 0))],
        out_specs=pl.BlockSpec((tm, D), lambda i, *_: (i, 0)),
        scratch_shapes=[pltpu.VMEM((2, D, 2 * D_FF), F32), pltpu.VMEM((2, D_FF, D), F32),
                        pltpu.VMEM((D, 2 * D_FF), BF16), pltpu.VMEM((D_FF, D), BF16),
                        pltpu.SemaphoreType.DMA((2, 2))],
    )
    return pl.pallas_call(
        _moe_kernel,
        grid_spec=grid_spec,
        out_shape=jax.ShapeDtypeStruct((P, D), F32),
        compiler_params=_cparams(("arbitrary",)),
        name="moe",
    )(meta["tile_e"], meta["n_valid"], meta["next_e"], meta["slot"], xs, w_up, b_up_p, w_down, b_dn, _glu_perm())


def _final_kernel(pos_ref, ys_hbm, x_ref, gate_ref, gtf_ref, g_ref, o_ref, buf, sems, *, tm, n_steps):
    i = pl.program_id(0)
    slot = i & 1

    n = TOP_K * tm

    def gather(step, dst, sem):
        def body(j, carry):
            pltpu.make_async_copy(ys_hbm.at[pl.ds(pos_ref[step * n + j], 1), :], dst.at[pl.ds(j, 1), :],
                                  sem).start()
            return carry
        lax.fori_loop(0, n, body, 0, unroll=DMA_UNROLL)

    @pl.when(i == 0)
    def _():
        gather(0, buf.at[0], sems.at[0])

    @pl.when(i + 1 < n_steps)
    def _():
        gather(i + 1, buf.at[1 - slot], sems.at[1 - slot])

    pltpu.make_async_copy(ys_hbm.at[pl.ds(0, n), :], buf.at[slot], sems.at[slot]).wait()
    gate = gate_ref[...]
    y = gate[:, 0:1] * buf[slot, 0:tm, :]
    for k in range(1, TOP_K):
        y = y + gate[:, k:k + 1] * buf[slot, k * tm:(k + 1) * tm, :]
    o_ref[...] = x_ref[...] + gtf_ref[0, 5:6, :] * _rms(y, g_ref[...])


def _final(ys, pos_flat, gate, x1, mod6, g_post_ffn, tm=128):
    B, S, D = x1.shape
    N = B * S
    n_steps = N // tm
    steps_per_b = S // tm
    pos_sm = pos_flat.reshape(n_steps, tm, TOP_K).transpose(0, 2, 1).reshape(N * TOP_K)
    grid_spec = pltpu.PrefetchScalarGridSpec(
        num_scalar_prefetch=1,
        grid=(n_steps,),
        in_specs=[pl.BlockSpec(memory_space=pl.ANY),
                  pl.BlockSpec((tm, D), lambda i, p: (i, 0)),
                  pl.BlockSpec((tm, LANES), lambda i, p: (i, 0)),
                  pl.BlockSpec((1, 6, D), lambda i, p: (i // steps_per_b, 0, 0)),
                  pl.BlockSpec((1, D), lambda i, p: (0, 0))],
        out_specs=pl.BlockSpec((tm, D), lambda i, p: (i, 0)),
        scratch_shapes=[pltpu.VMEM((2, TOP_K * tm, D), F32), pltpu.SemaphoreType.DMA((2,))],
    )
    out = pl.pallas_call(
        functools.partial(_final_kernel, tm=tm, n_steps=n_steps),
        grid_spec=grid_spec,
        out_shape=jax.ShapeDtypeStruct((N, D), F32),
        compiler_params=_cparams(("arbitrary",)),
        name="final",
    )(pos_sm, ys, x1.reshape(N, D), gate, mod6, g_post_ffn.reshape(1, D))
    return out.reshape(B, S, D)


def _layer(x, c, positions, w_ada, b_ada, g_pre_mix, g_post_mix, g_pre_ffn, g_post_ffn,
           w_in, cmp_pos, w_cmp_k1, w_cmp_k2, w_cmp_v1, w_cmp_v2, g_out_nsa, g_out_dil, w_o,
           w_router, b_router, w_up, b_up, w_down, b_down):
    B, S, D = x.shape
    mod6 = _ada(c, w_ada, b_ada).reshape(B, 6, D)
    (q_raw, q_rot_t, kc, vc, ks, vs_t, kw, vw_t, gates, qbs, kbs, vbs) = _proj(x, mod6, g_pre_mix, positions, w_in)
    kcf, vcf = _cmpmlp(kc, vc, cmp_pos, w_cmp_k1, w_cmp_k2, w_cmp_v1, w_cmp_v2)
    o_cmp, bias_t = _cmpsel(q_raw, kcf, vcf)
    o_slc = _slc(q_rot_t, bias_t, ks, vs_t)
    o_win = _win(q_rot_t, kw, vw_t)
    ods, lses = [], []
    for (window, dil), qb, kb, vb in zip(DIL_CONFIGS, qbs, kbs, vbs):
        o, lse = _band(qb, kb, vb, dil=dil, max_dist=window // dil, tq=128, tk=128, nt=2, shared_kv=False,
                       with_lse=True)
        ods.append(o)
        lses.append(lse)
    x1 = _out(o_cmp, o_slc, o_win, gates, ods, lses, x, mod6, g_out_nsa, g_out_dil, w_o, g_post_mix)
    h2, top_idx, gate, rank, counts = _route(x1, mod6, g_pre_ffn, w_router, b_router)
    tm_moe = 512
    pos, meta, n_tiles = _moe_layout(top_idx[:, :TOP_K], rank[:, :TOP_K], counts[0, :N_EXPERTS], tm_moe)
    pos_flat = pos.reshape(B * S * TOP_K)
    xs = _dispatch(h2, pos_flat, meta["pad_start"], meta["pad_len"], meta["n_valid"], n_tiles, tm_moe)
    ys = _moe(xs, meta, w_up, b_up, w_down, b_down, tm_moe)
    return _final(ys, pos_flat, gate, x1, mod6, g_post_ffn)


def kernel(x, c, positions, w_ada, b_ada, g_pre_mix, g_post_mix, g_pre_ffn, g_post_ffn, w_in, cmp_pos,
           w_cmp_k1, w_cmp_k2, w_cmp_v1, w_cmp_v2, g_out_nsa, g_out_dil, w_o, w_router, b_router,
           w_up, b_up, w_down, b_down):
    depth = w_ada.shape[0]
    for l in range(depth):
        x = _layer(x, c, positions, w_ada[l], b_ada[l], g_pre_mix[l], g_post_mix[l], g_pre_ffn[l],
                   g_post_ffn[l], w_in[l], cmp_pos[l], w_cmp_k1[l], w_cmp_k2[l], w_cmp_v1[l], w_cmp_v2[l],
                   g_out_nsa[l], g_out_dil[l], w_o[l], w_router[l], b_router[l], w_up[l], b_up[l],
                   w_down[l], b_down[l])
    return x
```

```python
import functools

import numpy as np
import jax
import jax.numpy as jnp
from jax import lax
from jax.experimental import pallas as pl
from jax.experimental.pallas import tpu as pltpu

F32 = jnp.float32
BF16 = jnp.bfloat16
I32 = jnp.int32

D_MODEL = 1024
HEAD_DIM = 64
N_HEADS_NSA = 8
N_KV_NSA = 2
GROUP_NSA = 4
N_HEADS_DIL = 8
ROPE_THETA = 500000.0
ROPE_DIM = 16
ROPE_HALF = 8
CMP_BLOCK = 32
CMP_STRIDE = 16
CMP_HIDDEN = 256
SLC_BLOCK = 64
SLC_SHIFT = 6
SLC_TOPK = 16
WIN_NSA = 512
DIL_CONFIGS = ((128, 1), (512, 4), (2048, 16))
N_EXPERTS = 32
TOP_K = 4
D_FF = 1024
SWIGLU_LIMIT = 7.0
SWIGLU_ALPHA = 1.702
RMS_EPS = 1e-6
NEG_INF = -1e30
FORCE_SCORE = 1e9
SCALE = HEAD_DIM ** -0.5
LOG2E = 1.4426950408889634
DEN_ROWS = 16

Q_NSA = 512
KV_NSA = 128
Q_DIL = 512
LANES = 128
SUBLANES = 8
VMEM_LIMIT = 56 * 1024 * 1024
DMA_UNROLL = 8

_NT = (((1,), (1,)), ((), ()))


def _cparams(sem):
    return pltpu.CompilerParams(dimension_semantics=sem, vmem_limit_bytes=VMEM_LIMIT)


def _rms(x, g):
    return x * lax.rsqrt(jnp.mean(x * x, axis=-1, keepdims=True) + RMS_EPS) * g


def _split3_dot(a, e):
    hi = a.astype(BF16)
    r1 = a - hi.astype(F32)
    mid = r1.astype(BF16)
    lo = (r1 - mid.astype(F32)).astype(BF16)
    return (jnp.dot(hi, e, preferred_element_type=F32) + jnp.dot(mid, e, preferred_element_type=F32)
            + jnp.dot(lo, e, preferred_element_type=F32))


def _ada_kernel(c_ref, w_ref, b_ref, o_ref):
    c = c_ref[...]
    a = c * jax.nn.sigmoid(c)
    o_ref[...] = jnp.dot(a, w_ref[...], preferred_element_type=F32,
                         precision=lax.Precision.HIGHEST) + b_ref[...]


def _ada(c, w_ada, b_ada):
    B, D = c.shape
    n = w_ada.shape[1] // D
    return pl.pallas_call(
        _ada_kernel,
        grid=(n,),
        in_specs=[pl.BlockSpec((B, D), lambda j: (0, 0)),
                  pl.BlockSpec((D, D), lambda j: (0, j)),
                  pl.BlockSpec((1, D), lambda j: (0, j))],
        out_specs=pl.BlockSpec((B, D), lambda j: (0, j)),
        out_shape=jax.ShapeDtypeStruct((B, n * D), F32),
        compiler_params=_cparams(("arbitrary",)),
        name="ada",
    )(c, w_ada, b_ada.reshape(1, -1))


def _rope_tables(pos_col, inv_row, s1_row, s2_row):
    ang = pos_col * inv_row
    cs = jnp.cos(ang)
    sn = jnp.sin(ang)
    return cs, sn * s1_row, sn * s2_row


def _rope_blk(x, cs, s1, s2):
    return x * cs + pltpu.roll(x, LANES - ROPE_HALF, 1) * s1 + pltpu.roll(x, ROPE_HALF, 1) * s2


def _proj_kernel(x_ref, mod_ref, g_ref, pos_ref, tab_ref, wa_ref, wg_ref, wb_ref,
                 qrawt_ref, qrott_ref, kv16_ref, ks_ref, vst_ref, kw_ref, vwt_ref, gt_ref,
                 *dil_refs_and_scratch):
    n_cfg = len(DIL_CONFIGS)
    qb_refs = dil_refs_and_scratch[0:n_cfg]
    kb_refs = dil_refs_and_scratch[n_cfg:2 * n_cfg]
    vb_refs = dil_refs_and_scratch[2 * n_cfg:3 * n_cfg]
    st_sc = dil_refs_and_scratch[3 * n_cfg]
    tm = x_ref.shape[1]

    def emit_dilated(val, blk, refs):
        st_sc[...] = val
        for (_, d), ref in zip(DIL_CONFIGS, refs):
            for r in range(d):
                piece = val if d == 1 else st_sc[pl.ds(r, tm // d, stride=d), :]
                lo = r * Q_DIL + blk * LANES
                ref[0, :, lo:lo + LANES] = piece.astype(BF16)

    x = x_ref[0]
    sh = mod_ref[0, 0:1, :]
    sc = mod_ref[0, 1:2, :]
    h = (_rms(x, g_ref[...]) * (1.0 + sc) + sh).astype(BF16)
    cs, s1, s2 = _rope_tables(pos_ref[0], tab_ref[0:1, :], tab_ref[1:2, :], tab_ref[2:3, :])

    pa = jnp.dot(h, wa_ref[...], preferred_element_type=F32)
    for r in range(Q_NSA // LANES):
        blk = pa[:, r * LANES:(r + 1) * LANES]
        qrawt_ref[0, r * LANES:(r + 1) * LANES, :] = (blk * (SCALE * LOG2E)).T.astype(BF16)
        rot = _rope_blk(blk, cs, s1, s2) * (SCALE * LOG2E)
        qrott_ref[0, r * LANES:(r + 1) * LANES, :] = rot.T.astype(BF16)
    o = Q_NSA
    for t in range(2):
        st_sc[...] = pa[:, o + t * LANES:o + (t + 1) * LANES]
        for j in range(CMP_STRIDE):
            kv16_ref[t, 0, :, j * LANES:(j + 1) * LANES] = st_sc[pl.ds(j, tm // CMP_STRIDE, stride=CMP_STRIDE), :]
    ks_ref[0] = _rope_blk(pa[:, o + 2 * LANES:o + 3 * LANES], cs, s1, s2).astype(BF16)
    vst_ref[0] = pa[:, o + 3 * LANES:o + 4 * LANES].T.astype(BF16)
    kw_ref[0] = _rope_blk(pa[:, o + 4 * LANES:o + 5 * LANES], cs, s1, s2).astype(BF16)
    vwt_ref[0] = pa[:, o + 5 * LANES:o + 6 * LANES].T.astype(BF16)

    gt_ref[0] = jnp.dot(h, wg_ref[...], preferred_element_type=F32)

    pb = jnp.dot(h, wb_ref[...], preferred_element_type=F32)
    for blk in range(Q_DIL // LANES):
        lanes = slice(blk * LANES, (blk + 1) * LANES)
        emit_dilated(_rope_blk(pb[:, lanes], cs, s1, s2) * SCALE, blk, qb_refs)
        emit_dilated(_rope_blk(pb[:, Q_DIL + blk * LANES:Q_DIL + (blk + 1) * LANES], cs, s1, s2), blk, kb_refs)
        emit_dilated(pb[:, 2 * Q_DIL + blk * LANES:2 * Q_DIL + (blk + 1) * LANES], blk, vb_refs)


def _nsa_perm():
    cols = []
    for r in range(GROUP_NSA):
        for hk in range(N_KV_NSA):
            hq = hk * GROUP_NSA + r
            cols.extend(range(hq * HEAD_DIM, (hq + 1) * HEAD_DIM))
    return np.asarray(cols, np.int32)


def _rope_const_table():
    half = ROPE_HALF
    inv = ROPE_THETA ** (-jnp.arange(half, dtype=F32) / half)
    lane = np.arange(LANES) % HEAD_DIM
    inv_row = jnp.where(lane < ROPE_DIM, inv[lane % half], 0.0).astype(F32)
    s1 = np.where(lane < half, -1.0, 0.0).astype(np.float32)
    s2 = np.where((lane >= half) & (lane < ROPE_DIM), 1.0, 0.0).astype(np.float32)
    tab = jnp.zeros((8, LANES), F32).at[0].set(inv_row).at[1].set(s1).at[2].set(s2)
    return tab


def _proj(x, mod6, g_pre, positions, w_in, tm=512):
    B, S, D = x.shape
    perm = _nsa_perm()
    gate_lo = Q_NSA + 6 * KV_NSA
    n_gate = 3 * N_HEADS_NSA
    w_a = jnp.concatenate([w_in[:, :Q_NSA][:, perm], w_in[:, Q_NSA:gate_lo]], axis=1).astype(BF16)
    w_g = jnp.pad(w_in[:, gate_lo:gate_lo + n_gate], ((0, 0), (0, LANES - n_gate))).astype(BF16)
    w_b = w_in[:, gate_lo + n_gate:].astype(BF16)
    pos = positions.astype(F32).reshape(B, S, 1)
    tab = _rope_const_table()
    wa_n, wb_n = w_a.shape[1], w_b.shape[1]

    def tok(width, dtype):
        return (pl.BlockSpec((1, tm, width), lambda b, i: (b, i, 0)),
                jax.ShapeDtypeStruct((B, S, width), dtype))

    def tok_t(width, dtype):
        return (pl.BlockSpec((1, width, tm), lambda b, i: (b, 0, i)),
                jax.ShapeDtypeStruct((B, width, S), dtype))

    def dil_view(d):
        return (pl.BlockSpec((1, tm // d, d * Q_DIL), lambda b, i: (b, i, 0)),
                jax.ShapeDtypeStruct((B, S // d, d * Q_DIL), BF16))

    kv16 = (pl.BlockSpec((2, 1, tm // CMP_STRIDE, CMP_STRIDE * KV_NSA), lambda b, i: (0, b, i, 0)),
            jax.ShapeDtypeStruct((2, B, S // CMP_STRIDE, CMP_STRIDE * KV_NSA), F32))
    outs = [tok_t(Q_NSA, BF16), tok_t(Q_NSA, BF16), kv16,
            tok(LANES, BF16), tok_t(LANES, BF16), tok(LANES, BF16), tok_t(LANES, BF16), tok(LANES, F32)]
    outs += [dil_view(d) for _ in range(3) for _, d in DIL_CONFIGS]
    res = pl.pallas_call(
        _proj_kernel,
        grid=(B, S // tm),
        in_specs=[pl.BlockSpec((1, tm, D), lambda b, i: (b, i, 0)),
                  pl.BlockSpec((1, 6, D), lambda b, i: (b, 0, 0)),
                  pl.BlockSpec((1, D), lambda b, i: (0, 0)),
                  pl.BlockSpec((1, tm, 1), lambda b, i: (b, i, 0)),
                  pl.BlockSpec((8, LANES), lambda b, i: (0, 0)),
                  pl.BlockSpec((D, wa_n), lambda b, i: (0, 0)),
                  pl.BlockSpec((D, LANES), lambda b, i: (0, 0)),
                  pl.BlockSpec((D, wb_n), lambda b, i: (0, 0))],
        out_specs=[o[0] for o in outs],
        out_shape=[o[1] for o in outs],
        scratch_shapes=[pltpu.VMEM((tm, LANES), F32)],
        compiler_params=_cparams(("parallel", "parallel")),
        name="proj",
    )(x, mod6, g_pre.reshape(1, D), pos, tab, w_a, w_g, w_b)
    n_cfg = len(DIL_CONFIGS)
    n0 = len(outs) - 3 * n_cfg
    return tuple(res[:n0]) + (res[n0:n0 + n_cfg], res[n0 + n_cfg:n0 + 2 * n_cfg], res[n0 + 2 * n_cfg:])


def _cmpmlp_kernel(a_ref, p_ref, w1_ref, w1x_ref, w2x_ref, o_ref):
    a = a_ref[0, 0].astype(BF16)
    bias = jnp.dot(p_ref[...], w1_ref[0], preferred_element_type=F32)[0:1]
    n16 = a.shape[0]
    out = jnp.zeros((n16, KV_NSA), F32)
    for h in range(N_KV_NSA):
        u = jnp.dot(a, w1x_ref[0, h, 0], preferred_element_type=F32)
        v = jnp.dot(a, w1x_ref[0, h, 1], preferred_element_type=F32)
        hid = jax.nn.gelu(u + pltpu.roll(v, n16 - 1, 0) + bias)
        out = out + jnp.dot(hid.astype(BF16), w2x_ref[0, h], preferred_element_type=F32)
    o_ref[0, 0] = out.astype(BF16)


def _cmpmlp(a, cmp_pos, w_k1, w_k2, w_v1, w_v2):
    _, B, n16, _ = a.shape
    seg = CMP_STRIDE * HEAD_DIM

    def expand1(w):
        w = w.reshape(2, CMP_STRIDE, HEAD_DIM, CMP_HIDDEN)
        z = jnp.zeros_like(w)
        per_head = [jnp.concatenate([w, z] if h == 0 else [z, w], axis=2) for h in range(N_KV_NSA)]
        return jnp.stack(per_head, axis=0).reshape(N_KV_NSA, 2, CMP_STRIDE * KV_NSA, CMP_HIDDEN)

    def expand2(w):
        z = jnp.zeros_like(w)
        return jnp.stack([jnp.concatenate([w, z] if h == 0 else [z, w], axis=1) for h in range(N_KV_NSA)], axis=0)

    w1 = jnp.stack([w_k1, w_v1], axis=0).astype(BF16)
    w1x = jnp.stack([expand1(w_k1), expand1(w_v1)], axis=0).astype(BF16)
    w2x = jnp.stack([expand2(w_k2), expand2(w_v2)], axis=0).astype(BF16)
    p8 = jnp.broadcast_to(cmp_pos.reshape(1, CMP_BLOCK * HEAD_DIM), (8, CMP_BLOCK * HEAD_DIM)).astype(BF16)
    out = pl.pallas_call(
        _cmpmlp_kernel,
        grid=(2, B),
        in_specs=[pl.BlockSpec((1, 1, n16, CMP_STRIDE * KV_NSA), lambda t, b: (t, b, 0, 0)),
                  pl.BlockSpec((8, 2 * seg), lambda t, b: (0, 0)),
                  pl.BlockSpec((1, 2 * seg, CMP_HIDDEN), lambda t, b: (t, 0, 0)),
                  pl.BlockSpec((1, N_KV_NSA, 2, CMP_STRIDE * KV_NSA, CMP_HIDDEN), lambda t, b: (t, 0, 0, 0, 0)),
                  pl.BlockSpec((1, N_KV_NSA, CMP_HIDDEN, KV_NSA), lambda t, b: (t, 0, 0, 0))],
        out_specs=pl.BlockSpec((1, 1, n16, KV_NSA), lambda t, b: (t, b, 0, 0)),
        out_shape=jax.ShapeDtypeStruct((2, B, n16, KV_NSA), BF16),
        compiler_params=_cparams(("parallel", "parallel")),
        name="cmpmlp",
    )(a, p8, w1, w1x, w2x)
    return out[0], out[1]


def _query_groups_t(qt_ref, tq):
    half0 = lax.broadcasted_iota(I32, (LANES, tq), 0) < HEAD_DIM
    zero = jnp.zeros((LANES, tq), BF16)
    groups = []
    for hk in range(N_KV_NSA):
        keep = half0 if hk == 0 else jnp.logical_not(half0)
        for r in range(GROUP_NSA):
            groups.append(jnp.where(keep, qt_ref[0, r * LANES:(r + 1) * LANES, :], zero))
    return groups


def _store_heads_t(ot, tq, o_ref):
    half0 = lax.broadcasted_iota(I32, (tq, LANES), 1) < HEAD_DIM
    for r in range(GROUP_NSA):
        o0 = ot[:, r * tq:(r + 1) * tq].T
        o1 = ot[:, (GROUP_NSA + r) * tq:(GROUP_NSA + r + 1) * tq].T
        o_ref[0, :, r * LANES:(r + 1) * LANES] = jnp.where(half0, o0, o1)


def _cmpsel_kernel(qt_ref, kc_ref, vct_ref, ovt_ref, o_ref, bias_ref, *, tq):
    i = pl.program_id(1)
    n16 = kc_ref.shape[1]
    cols = 8 * tq
    qa = jnp.concatenate(_query_groups_t(qt_ref, tq), axis=1)
    st = jnp.dot(kc_ref[0], qa, preferred_element_type=F32)
    t = i * tq + (lax.broadcasted_iota(I32, (n16, cols), 1) & (tq - 1))
    c = lax.broadcasted_iota(I32, (n16, cols), 0)
    valid = (c * CMP_STRIDE + (CMP_BLOCK - 1)) <= t
    st = jnp.where(valid, st, NEG_INF)
    m = jnp.max(st, axis=0, keepdims=True)
    e = jnp.exp2(st - m)
    l = jnp.sum(e, axis=0, keepdims=True)
    p = jnp.where(valid, e * (1.0 / l), 0.0)
    ot = jnp.dot(vct_ref[0], p.astype(BF16), preferred_element_type=F32)
    _store_heads_t(ot, tq, o_ref)

    n_slc = ovt_ref.shape[0]
    j = lax.broadcasted_iota(I32, (n_slc, tq), 0)
    cur = (i * tq + lax.broadcasted_iota(I32, (n_slc, tq), 1)) >> SLC_SHIFT
    forced = (j == 0) | (j == cur) | (j == cur - 1)
    ovt = ovt_ref[...]
    biases = []
    for hk in range(N_KV_NSA):
        ps = p[:, hk * GROUP_NSA * tq:(hk * GROUP_NSA + 1) * tq]
        for r in range(1, GROUP_NSA):
            ps = ps + p[:, (hk * GROUP_NSA + r) * tq:(hk * GROUP_NSA + r + 1) * tq]
        hi = ps.astype(BF16)
        lo = (ps - hi.astype(F32)).astype(BF16)
        pslc = (jnp.dot(ovt, hi, preferred_element_type=F32)
                + jnp.dot(ovt, lo, preferred_element_type=F32))
        score = jnp.where(forced, FORCE_SCORE, jnp.where(j <= cur, pslc, -1.0))
        rank = jnp.zeros((n_slc, tq), I32)
        for ii in range(n_slc):
            ri = score[ii:ii + 1, :]
            beats = (ri > score) | ((ri == score) & (j > ii))
            rank = rank + beats.astype(I32)
        biases.append(jnp.where(rank < SLC_TOPK, 0.0, NEG_INF))
    bias_ref[0] = jnp.concatenate(biases, axis=0).astype(BF16)


def _overlap_t(S):
    n16 = S // CMP_STRIDE
    n_slc = S // SLC_BLOCK
    cs = np.arange(n16) * CMP_STRIDE
    js = np.arange(n_slc) * SLC_BLOCK
    ov = np.clip(np.minimum(cs[:, None] + CMP_BLOCK, js[None, :] + SLC_BLOCK)
                 - np.maximum(cs[:, None], js[None, :]), 0, None).astype(np.float32) / CMP_BLOCK
    ov[n16 - 1] = 0.0
    return jnp.asarray(ov.T, BF16)


def _cmpsel(q_raw_t, kcf, vcf, tq=256):
    B, _, S = q_raw_t.shape
    n16 = S // CMP_STRIDE
    n_slc = S // SLC_BLOCK
    assert n_slc == HEAD_DIM, "selection bias is laid out as one 64-lane half per kv head"
    return pl.pallas_call(
        functools.partial(_cmpsel_kernel, tq=tq),
        grid=(B, S // tq),
        in_specs=[pl.BlockSpec((1, Q_NSA, tq), lambda b, i: (b, 0, i)),
                  pl.BlockSpec((1, n16, KV_NSA), lambda b, i: (b, 0, 0)),
                  pl.BlockSpec((1, KV_NSA, n16), lambda b, i: (b, 0, 0)),
                  pl.BlockSpec((n_slc, n16), lambda b, i: (0, 0))],
        out_specs=[pl.BlockSpec((1, tq, Q_NSA), lambda b, i: (b, i, 0)),
                   pl.BlockSpec((1, 2 * n_slc, tq), lambda b, i: (b, 0, i))],
        out_shape=[jax.ShapeDtypeStruct((B, S, Q_NSA), F32),
                   jax.ShapeDtypeStruct((B, 2 * n_slc, S), BF16)],
        compiler_params=_cparams(("parallel", "parallel")),
        name="cmpsel",
    )(q_raw_t, kcf, vcf.transpose(0, 2, 1), _overlap_t(S))


def _slc_kernel(qi_ref, kj_ref, qt_ref, bt_ref, k_ref, vt_ref, o_ref, qa_sc, m_sc, acc_sc, *, tq, tk):
    step = pl.program_id(1)
    i = qi_ref[step]
    kj = kj_ref[step]
    last = (i * tq + tq - 1) // tk
    cols = 8 * tq

    @pl.when(kj == 0)
    def _():
        half0 = lax.broadcasted_iota(I32, (LANES, tq), 0) < HEAD_DIM
        zero = jnp.zeros((LANES, tq), BF16)
        bt = bt_ref[0]
        groups = []
        for hk in range(N_KV_NSA):
            keep = half0 if hk == 0 else jnp.logical_not(half0)
            bh = jnp.where(keep, bt, zero)
            for r in range(GROUP_NSA):
                qb = qt_ref[0, r * LANES:(r + 1) * LANES, :]
                groups.append(jnp.concatenate([jnp.where(keep, qb, zero), bh], axis=0))
        qa_sc[...] = jnp.concatenate(groups, axis=1)
        m_sc[...] = jnp.full((1, cols), NEG_INF, F32)
        acc_sc[...] = jnp.zeros(acc_sc.shape, F32)

    def update(on_diagonal):
        kblk = (kj * tk + lax.broadcasted_iota(I32, (tk, LANES), 0)) >> SLC_SHIFT
        lane = lax.broadcasted_iota(I32, (tk, LANES), 1) & (HEAD_DIM - 1)
        onehot = jnp.where(kblk == lane, 1.0, 0.0).astype(BF16)
        kaug = jnp.concatenate([k_ref[0], onehot], axis=1)
        st = jnp.dot(kaug, qa_sc[...], preferred_element_type=F32)
        if on_diagonal:
            kpos = kj * tk + lax.broadcasted_iota(I32, (tk, cols), 0)
            t = i * tq + (lax.broadcasted_iota(I32, (tk, cols), 1) & (tq - 1))
            st = jnp.where(kpos <= t, st, NEG_INF)
        m_old = m_sc[...]
        m_new = jnp.maximum(m_old, jnp.max(st, axis=0, keepdims=True))
        alpha = jnp.exp2(m_old - m_new)
        p = jnp.exp2(st - m_new).astype(BF16)
        vt_ones = jnp.concatenate([vt_ref[0], jnp.ones((DEN_ROWS, tk), BF16)], axis=0)
        acc_sc[...] = alpha * acc_sc[...] + jnp.dot(vt_ones, p, preferred_element_type=F32)
        m_sc[...] = m_new

    @pl.when(kj < last)
    def _():
        update(False)

    @pl.when(kj == last)
    def _():
        update(True)
        acc = acc_sc[...]
        _store_heads_t(acc[:LANES] * (1.0 / acc[LANES:LANES + 1]), tq, o_ref)


def _slc(q_rot_t, bias_t, ks, vs_t, tq=512, tk=512):
    B, _, S = q_rot_t.shape
    assert tk % tq == 0
    qi, kj = [], []
    for i in range(S // tq):
        for j in range((i * tq + tq - 1) // tk + 1):
            qi.append(i)
            kj.append(j)
    grid_spec = pltpu.PrefetchScalarGridSpec(
        num_scalar_prefetch=2,
        grid=(B, len(qi)),
        in_specs=[pl.BlockSpec((1, Q_NSA, tq), lambda b, s, qi, kj: (b, 0, qi[s])),
                  pl.BlockSpec((1, LANES, tq), lambda b, s, qi, kj: (b, 0, qi[s])),
                  pl.BlockSpec((1, tk, KV_NSA), lambda b, s, qi, kj: (b, kj[s], 0)),
                  pl.BlockSpec((1, KV_NSA, tk), lambda b, s, qi, kj: (b, 0, kj[s]))],
        out_specs=pl.BlockSpec((1, tq, Q_NSA), lambda b, s, qi, kj: (b, qi[s], 0)),
        scratch_shapes=[pltpu.VMEM((2 * LANES, 8 * tq), BF16),
                        pltpu.VMEM((1, 8 * tq), F32),
                        pltpu.VMEM((LANES + DEN_ROWS, 8 * tq), F32)],
    )
    return pl.pallas_call(
        functools.partial(_slc_kernel, tq=tq, tk=tk),
        grid_spec=grid_spec,
        out_shape=jax.ShapeDtypeStruct((B, S, Q_NSA), F32),
        compiler_params=_cparams(("parallel", "arbitrary")),
        name="slc",
    )(jnp.asarray(qi, I32), jnp.asarray(kj, I32), q_rot_t, bias_t, ks, vs_t)


def _win_kernel(qt_ref, k0_ref, k1_ref, k2_ref, v0_ref, v1_ref, v2_ref, o_ref, *, tq, tk, max_dist):
    i = pl.program_id(1)
    a = (i * tq) // tk
    cols = 8 * tq
    qa = jnp.concatenate(_query_groups_t(qt_ref, tq), axis=1)
    t = i * tq + (lax.broadcasted_iota(I32, (tk, cols), 1) & (tq - 1))
    row = lax.broadcasted_iota(I32, (tk, cols), 0)
    sts = []
    for jt, k_ref in enumerate((k0_ref, k1_ref, k2_ref)):
        st = jnp.dot(k_ref[0], qa, preferred_element_type=F32)
        kpos = (a - 2 + jt) * tk + row
        if jt == 0:
            st = jnp.where((t - kpos <= max_dist) & (kpos >= 0), st, NEG_INF)
        elif jt == 1:
            st = jnp.where(kpos >= 0, st, NEG_INF)
        else:
            st = jnp.where(kpos <= t, st, NEG_INF)
        sts.append(st)
    m = jnp.max(sts[0], axis=0, keepdims=True)
    for st in sts[1:]:
        m = jnp.maximum(m, jnp.max(st, axis=0, keepdims=True))
    acc = jnp.zeros((LANES + DEN_ROWS, cols), F32)
    for st, v_ref in zip(sts, (v0_ref, v1_ref, v2_ref)):
        vt_ones = jnp.concatenate([v_ref[0], jnp.ones((DEN_ROWS, tk), BF16)], axis=0)
        acc = acc + jnp.dot(vt_ones, jnp.exp2(st - m).astype(BF16), preferred_element_type=F32)
    _store_heads_t(acc[:LANES] * (1.0 / acc[LANES:LANES + 1]), tq, o_ref)


def _win(q_rot_t, kw, vw_t, tq=256, tk=256):
    B, _, S = q_rot_t.shape
    max_dist = WIN_NSA - 1
    assert tk % tq == 0 and 2 * tk == max_dist + 1

    def k_map(jt):
        return lambda b, i: (b, jnp.maximum((i * tq) // tk - 2 + jt, 0), 0)

    def v_map(jt):
        return lambda b, i: (b, 0, jnp.maximum((i * tq) // tk - 2 + jt, 0))

    return pl.pallas_call(
        functools.partial(_win_kernel, tq=tq, tk=tk, max_dist=max_dist),
        grid=(B, S // tq),
        in_specs=[pl.BlockSpec((1, Q_NSA, tq), lambda b, i: (b, 0, i))]
        + [pl.BlockSpec((1, tk, KV_NSA), k_map(jt)) for jt in range(3)]
        + [pl.BlockSpec((1, KV_NSA, tk), v_map(jt)) for jt in range(3)],
        out_specs=pl.BlockSpec((1, tq, Q_NSA), lambda b, i: (b, i, 0)),
        out_shape=jax.ShapeDtypeStruct((B, S, Q_NSA), F32),
        compiler_params=_cparams(("parallel", "parallel")),
        name="win",
    )(q_rot_t, kw, kw, kw, vw_t, vw_t, vw_t)


def _band_kernel(*refs, tq, tk, nt, max_dist):
    q_ref = refs[0]
    k_refs = refs[1:1 + nt]
    v_refs = refs[1 + nt:1 + 2 * nt]
    o_ref = refs[1 + 2 * nt]
    lse_ref = refs[2 + 2 * nt]
    i = pl.program_id(2)
    a = (i * tq) // tk
    lane = lax.broadcasted_iota(I32, (tq, LANES), 1)
    half0 = lane < HEAD_DIM
    row_t = i * tq + (lax.broadcasted_iota(I32, (2 * tq, tk), 0) & (tq - 1))
    col = lax.broadcasted_iota(I32, (2 * tq, tk), 1)
    oks = []
    for jt in range(nt):
        kpos = (a - (nt - 1) + jt) * tk + col
        d = row_t - kpos
        oks.append((d >= 0) & (d <= max_dist) & (kpos >= 0))
    lse_acc = jnp.zeros((tq, LANES), F32)
    for blk in range(Q_DIL // LANES):
        qb = q_ref[0, :, blk * LANES:(blk + 1) * LANES]
        zero = jnp.zeros_like(qb)
        qs = jnp.concatenate([jnp.where(half0, qb, zero), jnp.where(half0, zero, qb)], axis=0)
        kv0 = blk * LANES
        ss = []
        for jt in range(nt):
            kt = k_refs[jt][0, :, kv0:kv0 + LANES]
            s = lax.dot_general(qs, kt, _NT, preferred_element_type=F32)
            ss.append(jnp.where(oks[jt], s, NEG_INF))
        m = jnp.max(ss[0], axis=-1, keepdims=True)
        for s in ss[1:]:
            m = jnp.maximum(m, jnp.max(s, axis=-1, keepdims=True))
        l = jnp.zeros((2 * tq, 1), F32)
        o = jnp.zeros((2 * tq, LANES), F32)
        for jt in range(nt):
            p = jnp.exp(ss[jt] - m)
            l = l + jnp.sum(p, axis=-1, keepdims=True)
            o = o + jnp.dot(p.astype(BF16), v_refs[jt][0, :, kv0:kv0 + LANES], preferred_element_type=F32)
        o = o * (1.0 / l)
        o_ref[0, :, blk * LANES:(blk + 1) * LANES] = jnp.where(half0, o[:tq], o[tq:])
        lse = m + jnp.log(l)
        lse_acc = jnp.where(lane == 2 * blk, lse[:tq], lse_acc)
        lse_acc = jnp.where(lane == 2 * blk + 1, lse[tq:], lse_acc)
    lse_ref[0] = lse_acc


def _band(q, k, v, *, dil, max_dist, tq, tk, nt):
    B, m, _ = q.shape

    def kv_map(jt):
        return lambda b, r, i: (b, jnp.maximum((i * tq) // tk - (nt - 1) + jt, 0), r)

    tok_spec = pl.BlockSpec((1, tq, Q_DIL), lambda b, r, i: (b, i, r))
    kv_specs = [pl.BlockSpec((1, tk, Q_DIL), kv_map(jt)) for jt in range(nt)]
    return pl.pallas_call(
        functools.partial(_band_kernel, tq=tq, tk=tk, nt=nt, max_dist=max_dist),
        grid=(B, dil, m // tq),
        in_specs=[tok_spec] + kv_specs + kv_specs,
        out_specs=[tok_spec, pl.BlockSpec((1, tq, LANES), lambda b, r, i: (b, i, r))],
        out_shape=[jax.ShapeDtypeStruct((B, m, dil * Q_DIL), F32),
                   jax.ShapeDtypeStruct((B, m, dil * LANES), F32)],
        compiler_params=_cparams(("parallel", "parallel", "parallel")),
        name=f"band_d{dil}_w{max_dist}",
    )(q, *([k] * nt), *([v] * nt))


def _out_kernel(ocmp_ref, oslc_ref, owin_ref, gt_ref, od1_ref, od4_ref, od16_ref, l1_ref, l4_ref, l16_ref,
                x_ref, mod_ref, eg_ref, ed_ref, gnsa_ref, gdil_ref, wo_ref, gpost_ref, o_ref, st_sc):
    tm = x_ref.shape[1]

    def token_order(ref, d, n_blk):
        if d == 1:
            return ref[0]
        blocks = []
        for blk in range(n_blk):
            for r in range(d):
                lo = (r * n_blk + blk) * LANES
                st_sc[pl.ds(r, tm // d, stride=d), :] = ref[0, :, lo:lo + LANES]
            blocks.append(st_sc[...])
        return blocks[0] if n_blk == 1 else jnp.concatenate(blocks, axis=1)

    sg = jax.nn.sigmoid(gt_ref[0])
    oa = (_split3_dot(sg, eg_ref[0]) * ocmp_ref[0] + _split3_dot(sg, eg_ref[1]) * oslc_ref[0]
          + _split3_dot(sg, eg_ref[2]) * owin_ref[0])
    ya = _rms(oa, gnsa_ref[...])

    dils = [d for _, d in DIL_CONFIGS]
    n_blk = Q_DIL // LANES
    l1, l4, l16 = [token_order(ref, d, 1) for ref, d in zip((l1_ref, l4_ref, l16_ref), dils)]
    mx = jnp.maximum(jnp.maximum(l1, l4), l16)
    e1, e4, e16 = jnp.exp(l1 - mx), jnp.exp(l4 - mx), jnp.exp(l16 - mx)
    den = e1 + e4 + e16
    ed = ed_ref[...]
    od1, od4, od16 = [token_order(ref, d, n_blk) for ref, d in zip((od1_ref, od4_ref, od16_ref), dils)]
    inv = 1.0 / den
    ob = (_split3_dot(e1 * inv, ed) * od1 + _split3_dot(e4 * inv, ed) * od4 + _split3_dot(e16 * inv, ed) * od16)
    yb = _rms(ob, gdil_ref[...])

    y = jnp.concatenate([ya, yb], axis=1).astype(BF16)
    z = jnp.dot(y, wo_ref[...], preferred_element_type=F32)
    gt_m = mod_ref[0, 2:3, :]
    o_ref[0] = x_ref[0] + gt_m * _rms(z, gpost_ref[...])


def _gate_expanders():
    perm = _nsa_perm()
    eg = np.zeros((3, LANES, Q_NSA), np.float32)
    for lane_out, col in enumerate(perm):
        hq = col // HEAD_DIM
        for c in range(3):
            eg[c, hq * 3 + c, lane_out] = 1.0
    ed = np.zeros((LANES, Q_DIL), np.float32)
    for h in range(N_HEADS_DIL):
        ed[h, h * HEAD_DIM:(h + 1) * HEAD_DIM] = 1.0
    return jnp.asarray(eg, BF16), jnp.asarray(ed, BF16)


def _out(o_cmp, o_slc, o_win, gates, ods, lses, x, mod6, g_out_nsa, g_out_dil, w_o, g_post, tm=512):
    B, S, D = x.shape
    perm = _nsa_perm()
    eg, ed = _gate_expanders()
    w_o_p = jnp.concatenate([w_o[:Q_NSA][perm], w_o[Q_NSA:]], axis=0).astype(BF16)
    g_nsa_p = g_out_nsa[perm].reshape(1, Q_NSA)

    def tok(width):
        return pl.BlockSpec((1, tm, width), lambda b, i: (b, i, 0))

    def const(shape):
        return pl.BlockSpec(shape, lambda b, i: (0,) * len(shape))

    def view(d, width):
        return pl.BlockSpec((1, tm // d, d * width), lambda b, i: (b, i, 0))

    dils = [d for _, d in DIL_CONFIGS]
    return pl.pallas_call(
        _out_kernel,
        grid=(B, S // tm),
        in_specs=[tok(Q_NSA), tok(Q_NSA), tok(Q_NSA), tok(LANES)]
        + [view(d, Q_DIL) for d in dils] + [view(d, LANES) for d in dils]
        + [tok(D), pl.BlockSpec((1, 6, D), lambda b, i: (b, 0, 0)),
           const((3, LANES, Q_NSA)), const((LANES, Q_DIL)), const((1, Q_NSA)), const((1, Q_DIL)),
           const((D, D)), const((1, D))],
        out_specs=tok(D),
        out_shape=jax.ShapeDtypeStruct((B, S, D), F32),
        scratch_shapes=[pltpu.VMEM((tm, LANES), F32)],
        compiler_params=_cparams(("parallel", "parallel")),
        name="out",
    )(o_cmp, o_slc, o_win, gates, *ods, *lses, x, mod6, eg, ed, g_nsa_p, g_out_dil.reshape(1, Q_DIL),
      w_o_p, g_post.reshape(1, D))


def _route_kernel(x_ref, mod_ref, g_ref, wr_ref, br_ref, h_ref, idx_ref, gate_ref, rank_ref, cnt_ref, run_sc):
    i = pl.program_id(0)

    @pl.when(i == 0)
    def _():
        run_sc[...] = jnp.zeros_like(run_sc)

    sh = mod_ref[0, 3:4, :]
    sc = mod_ref[0, 4:5, :]
    h = _rms(x_ref[...], g_ref[...]) * (1.0 + sc) + sh
    h_ref[...] = h
    logits = jnp.dot(h, wr_ref[...], preferred_element_type=F32,
                     precision=lax.Precision.HIGHEST) + br_ref[...]
    tm = logits.shape[0]
    lane = lax.broadcasted_iota(I32, (tm, LANES), 1)
    work = jnp.where(lane < N_EXPERTS, logits, -jnp.inf)
    idx_out = jnp.zeros((tm, LANES), I32)
    val_out = jnp.full((tm, LANES), -jnp.inf, F32)
    rank_out = jnp.zeros((tm, LANES), F32)
    tri = jnp.where(lax.broadcasted_iota(I32, (tm, tm), 1) < lax.broadcasted_iota(I32, (tm, tm), 0),
                    1.0, 0.0).astype(BF16)
    base = run_sc[0:1, :]
    for k in range(TOP_K):
        mx = jnp.max(work, axis=-1, keepdims=True)
        ix = jnp.min(jnp.where(work == mx, lane, LANES), axis=-1, keepdims=True)
        hit = lane == ix
        idx_out = jnp.where(lane == k, ix, idx_out)
        val_out = jnp.where(lane == k, mx, val_out)
        work = jnp.where(hit, -jnp.inf, work)
        onehot = jnp.where(hit, 1.0, 0.0)
        before = jnp.dot(tri, onehot.astype(BF16), preferred_element_type=F32) + base
        rank_k = jnp.sum(onehot * before, axis=-1, keepdims=True)
        rank_out = jnp.where(lane == k, rank_k, rank_out)
        base = base + jnp.sum(onehot, axis=0, keepdims=True)
    run_sc[...] = jnp.broadcast_to(base, run_sc.shape)
    e = jnp.exp(val_out - val_out[:, 0:1])
    idx_ref[...] = idx_out
    gate_ref[...] = e / jnp.sum(e, axis=-1, keepdims=True)
    rank_ref[...] = rank_out.astype(I32)
    cnt_ref[...] = run_sc[...].astype(I32)


def _route(x1, mod6, g_pre_ffn, w_router, b_router, tm=512):
    B, S, D = x1.shape
    N = B * S
    spb = S // tm
    wr = jnp.pad(w_router, ((0, 0), (0, LANES - N_EXPERTS)))
    br = jnp.pad(b_router, (0, LANES - N_EXPERTS)).reshape(1, LANES)

    def tok(width):
        return pl.BlockSpec((tm, width), lambda i: (i, 0))

    return pl.pallas_call(
        _route_kernel,
        grid=(N // tm,),
        in_specs=[tok(D), pl.BlockSpec((1, 6, D), lambda i: (i // spb, 0, 0)),
                  pl.BlockSpec((1, D), lambda i: (0, 0)),
                  pl.BlockSpec((D, LANES), lambda i: (0, 0)),
                  pl.BlockSpec((1, LANES), lambda i: (0, 0))],
        out_specs=[tok(D), tok(LANES), tok(LANES), tok(LANES), pl.BlockSpec((8, LANES), lambda i: (0, 0))],
        out_shape=[jax.ShapeDtypeStruct((N, D), F32),
                   jax.ShapeDtypeStruct((N, LANES), I32),
                   jax.ShapeDtypeStruct((N, LANES), F32),
                   jax.ShapeDtypeStruct((N, LANES), I32),
                   jax.ShapeDtypeStruct((8, LANES), I32)],
        scratch_shapes=[pltpu.VMEM((8, LANES), F32)],
        compiler_params=_cparams(("arbitrary",)),
        name="route",
    )(x1.reshape(N, D), mod6, g_pre_ffn.reshape(1, D), wr, br)


def _dispatch_kernel(pos_ref, pad0_ref, padn_ref, nt_ref, h_ref, xs_out, zbuf, sem, zsem, *, tm, tmx, n_tiles):
    i = pl.program_id(0)
    bits = tmx.bit_length() - 1

    def pad_copies(e, fn):
        p0 = pad0_ref[e]
        head = (-p0) & (SUBLANES - 1)
        head = jnp.minimum(head, padn_ref[e])
        for r in range(SUBLANES - 1):
            @pl.when(r < head)
            def _():
                fn(pltpu.make_async_copy(zbuf.at[pl.ds(0, 1), :], xs_out.at[pl.ds(p0 + r, 1), :], zsem))
        a = p0 + head
        n = padn_ref[e] - head
        for b in range(3, bits):
            size = 1 << b
            off = (n >> (b + 1)) << (b + 1)

            @pl.when((n & size) != 0)
            def _():
                fn(pltpu.make_async_copy(zbuf.at[pl.ds(0, size), :],
                                         xs_out.at[pl.ds(pl.multiple_of(a + off, SUBLANES), size), :], zsem))

    def tail_copy(t, fn):
        fn(pltpu.make_async_copy(zbuf, xs_out.at[pl.ds(pl.multiple_of(t * tmx, tmx), tmx), :], zsem))

    def for_all_fill(fn):
        def per_expert(e, carry):
            pad_copies(e, fn)
            return carry
        lax.fori_loop(0, N_EXPERTS, per_expert, 0)

        def per_tile(t, carry):
            tail_copy(t, fn)
            return carry
        lax.fori_loop(nt_ref[0], n_tiles, per_tile, 0)

    @pl.when(i == 0)
    def _():
        zbuf[...] = jnp.zeros_like(zbuf)
        for_all_fill(lambda cp: cp.start())
        for_all_fill(lambda cp: cp.wait())

    def body(r, carry):
        for k in range(TOP_K):
            dst_row = pos_ref[(i * tm + r) * TOP_K + k]
            pltpu.make_async_copy(h_ref.at[pl.ds(r, 1), :], xs_out.at[pl.ds(dst_row, 1), :], sem).start()
        return carry

    lax.fori_loop(0, tm, body, 0, unroll=DMA_UNROLL // TOP_K)
    for k in range(TOP_K):
        pltpu.make_async_copy(h_ref, xs_out.at[pl.ds(0, tm), :], sem).wait()


def _dispatch(h2, pos_flat, pad_start, pad_len, n_valid, n_tiles, tmx, tm=512):
    N, D = h2.shape
    assert tmx & (tmx - 1) == 0
    grid_spec = pltpu.PrefetchScalarGridSpec(
        num_scalar_prefetch=4,
        grid=(N // tm,),
        in_specs=[pl.BlockSpec((tm, D), lambda i, *_: (i, 0))],
        out_specs=pl.BlockSpec(memory_space=pl.ANY),
        scratch_shapes=[pltpu.VMEM((tmx, D), F32), pltpu.SemaphoreType.DMA(()), pltpu.SemaphoreType.DMA(())],
    )
    return pl.pallas_call(
        functools.partial(_dispatch_kernel, tm=tm, tmx=tmx, n_tiles=n_tiles),
        grid_spec=grid_spec,
        out_shape=jax.ShapeDtypeStruct((n_tiles * tmx, D), F32),
        compiler_params=_cparams(("arbitrary",)),
        name="dispatch",
    )(pos_flat, pad_start, pad_len, n_valid, h2)


def _moe_kernel(te_ref, nt_ref, nx_ref, sl_ref, xs_ref, wup_hbm, bup_ref, wdn_hbm, bdn_ref, pm_ref, y_ref,
                wup_in, wdn_in, wup_sc, wdn_sc, sems):
    i = pl.program_id(0)
    n_valid = nt_ref[0]
    e = te_ref[i]
    new_expert = (i == 0) | (e != te_ref[jnp.maximum(i - 1, 0)])
    slot = sl_ref[i]
    n_blk = wup_sc.shape[1] // (2 * LANES)

    def weight_copies(ex, s):
        return (pltpu.make_async_copy(wup_hbm.at[ex], wup_in.at[s], sems.at[0, s]),
                pltpu.make_async_copy(wdn_hbm.at[ex], wdn_in.at[s], sems.at[1, s]))

    @pl.when(i == 0)
    def _():
        for cp in weight_copies(e, slot):
            cp.start()

    @pl.when(new_expert & (i < n_valid))
    def _():
        for cp in weight_copies(e, slot):
            cp.wait()
        nx = nx_ref[i]

        @pl.when(nx >= 0)
        def _():
            for cp in weight_copies(nx, 1 - slot):
                cp.start()

        pm = pm_ref[...]
        for blk in range(n_blk):
            cols = slice(blk * 2 * LANES, (blk + 1) * 2 * LANES)
            w = wup_in[slot, :, cols].astype(BF16)
            wup_sc[:, cols] = jnp.dot(w, pm, preferred_element_type=F32).astype(BF16)
        wdn_sc[...] = wdn_in[slot].astype(BF16)

    @pl.when(i < n_valid)
    def _():
        xs = xs_ref[...].astype(BF16)
        u = jnp.dot(xs, wup_sc[...], preferred_element_type=F32) + bup_ref[0]
        acts = []
        for blk in range(n_blk):
            ug = jnp.minimum(u[:, blk * 2 * LANES:blk * 2 * LANES + LANES], SWIGLU_LIMIT)
            ul = jnp.clip(u[:, blk * 2 * LANES + LANES:(blk + 1) * 2 * LANES], -SWIGLU_LIMIT, SWIGLU_LIMIT)
            acts.append((ug * jax.nn.sigmoid(SWIGLU_ALPHA * ug) * (ul + 1.0)).astype(BF16))
        act = jnp.concatenate(acts, axis=1)
        y_ref[...] = jnp.dot(act, wdn_sc[...], preferred_element_type=F32) + bdn_ref[0]

    @pl.when(i >= n_valid)
    def _():
        y_ref[...] = jnp.zeros_like(y_ref)


def _moe_layout(top_idx, rank, counts, tm):
    N = top_idx.shape[0]
    n_tiles = N * TOP_K // tm + N_EXPERTS
    e_ids = jnp.arange(N_EXPERTS, dtype=I32)
    tiles_e = (counts + tm - 1) // tm
    tile_end = jnp.sum(jnp.where(e_ids[None, :] <= e_ids[:, None], tiles_e[None, :], 0), axis=1)
    start = (tile_end - tiles_e) * tm
    pos = rank + jnp.sum(jnp.where(top_idx[:, :, None] == e_ids[None, None, :], start[None, None, :], 0), axis=-1)
    n_valid = tile_end[N_EXPERTS - 1]
    tile_ids = jnp.arange(n_tiles, dtype=I32)
    tile_e = jnp.sum((jnp.minimum(tile_ids, n_valid - 1)[:, None] >= tile_end[None, :]).astype(I32), axis=1)
    prev_e = jnp.concatenate([jnp.full((1,), -1, I32), tile_e[:-1]])
    new = (tile_e != prev_e) & (tile_ids < n_valid)
    ordinal = jnp.sum(jnp.where(tile_ids[None, :] <= tile_ids[:, None], new[None, :].astype(I32), 0), axis=1) - 1
    next_first = jnp.sum(jnp.where(tile_e[:, None] == e_ids[None, :], tile_end[None, :], 0), axis=1)
    next_e = jnp.sum(jnp.where(next_first[:, None] == tile_ids[None, :], tile_e[None, :], 0), axis=1)
    next_e = jnp.where(next_first < n_valid, next_e, -1)
    meta = dict(tile_e=tile_e.astype(I32), n_valid=n_valid.astype(I32).reshape(1), next_e=next_e.astype(I32),
                slot=(ordinal & 1).astype(I32), pad_start=(start + counts).astype(I32),
                pad_len=(tiles_e * tm - counts).astype(I32))
    return pos.astype(I32), meta, n_tiles


def _glu_perm():
    pm = np.zeros((2 * LANES, 2 * LANES), np.float32)
    for j in range(LANES):
        pm[2 * j, j] = 1.0
        pm[2 * j + 1, LANES + j] = 1.0
    return jnp.asarray(pm, BF16)


def _moe(xs, meta, w_up, b_up, w_down, b_down, tm):
    P, D = xs.shape
    n_blk = D_FF // LANES
    b_up_p = b_up.reshape(N_EXPERTS, n_blk, LANES, 2).transpose(0, 1, 3, 2).reshape(N_EXPERTS, 1, 2 * D_FF)
    b_dn = b_down.reshape(N_EXPERTS, 1, D)

    def bmap(i, te, *_):
        return (te[i], 0, 0)

    grid_spec = pltpu.PrefetchScalarGridSpec(
        num_scalar_prefetch=4,
        grid=(P // tm,),
        in_specs=[pl.BlockSpec((tm, D), lambda i, *_: (i, 0)),
                  pl.BlockSpec(memory_space=pl.ANY), pl.BlockSpec((1, 1, 2 * D_FF), bmap),
                  pl.BlockSpec(memory_space=pl.ANY), pl.BlockSpec((1, 1, D), bmap),
                  pl.BlockSpec((2 * LANES, 2 * LANES), lambda i, *_: (0, 0))],
        out_specs=pl.BlockSpec((tm, D), lambda i, *_: (i, 0)),
        scratch_shapes=[pltpu.VMEM((2, D, 2 * D_FF), F32), pltpu.VMEM((2, D_FF, D), F32),
                        pltpu.VMEM((D, 2 * D_FF), BF16), pltpu.VMEM((D_FF, D), BF16),
                        pltpu.SemaphoreType.DMA((2, 2))],
    )
    return pl.pallas_call(
        _moe_kernel,
        grid_spec=grid_spec,
        out_shape=jax.ShapeDtypeStruct((P, D), F32),
        compiler_params=_cparams(("arbitrary",)),
        name="moe",
    )(meta["tile_e"], meta["n_valid"], meta["next_e"], meta["slot"], xs, w_up, b_up_p, w_down, b_dn, _glu_perm())


def _final_kernel(pos_ref, ys_hbm, x_ref, gate_ref, gtf_ref, g_ref, o_ref, buf, sems, *, tm, n_steps):
    i = pl.program_id(0)
    slot = i & 1

    n = TOP_K * tm

    def gather(step, dst, sem):
        def body(j, carry):
            pltpu.make_async_copy(ys_hbm.at[pl.ds(pos_ref[step * n + j], 1), :], dst.at[pl.ds(j, 1), :],
                                  sem).start()
            return carry
        lax.fori_loop(0, n, body, 0, unroll=DMA_UNROLL)

    @pl.when(i == 0)
    def _():
        gather(0, buf.at[0], sems.at[0])

    @pl.when(i + 1 < n_steps)
    def _():
        gather(i + 1, buf.at[1 - slot], sems.at[1 - slot])

    pltpu.make_async_copy(ys_hbm.at[pl.ds(0, n), :], buf.at[slot], sems.at[slot]).wait()
    gate = gate_ref[...]
    y = gate[:, 0:1] * buf[slot, 0:tm, :]
    for k in range(1, TOP_K):
        y = y + gate[:, k:k + 1] * buf[slot, k * tm:(k + 1) * tm, :]
    o_ref[...] = x_ref[...] + gtf_ref[0, 5:6, :] * _rms(y, g_ref[...])


def _final(ys, pos_flat, gate, x1, mod6, g_post_ffn, tm=256):
    B, S, D = x1.shape
    N = B * S
    n_steps = N // tm
    steps_per_b = S // tm
    pos_sm = pos_flat.reshape(n_steps, tm, TOP_K).transpose(0, 2, 1).reshape(N * TOP_K)
    grid_spec = pltpu.PrefetchScalarGridSpec(
        num_scalar_prefetch=1,
        grid=(n_steps,),
        in_specs=[pl.BlockSpec(memory_space=pl.ANY),
                  pl.BlockSpec((tm, D), lambda i, p: (i, 0)),
                  pl.BlockSpec((tm, LANES), lambda i, p: (i, 0)),
                  pl.BlockSpec((1, 6, D), lambda i, p: (i // steps_per_b, 0, 0)),
                  pl.BlockSpec((1, D), lambda i, p: (0, 0))],
        out_specs=pl.BlockSpec((tm, D), lambda i, p: (i, 0)),
        scratch_shapes=[pltpu.VMEM((2, TOP_K * tm, D), F32), pltpu.SemaphoreType.DMA((2,))],
    )
    out = pl.pallas_call(
        functools.partial(_final_kernel, tm=tm, n_steps=n_steps),
        grid_spec=grid_spec,
        out_shape=jax.ShapeDtypeStruct((N, D), F32),
        compiler_params=_cparams(("arbitrary",)),
        name="final",
    )(pos_sm, ys, x1.reshape(N, D), gate, mod6, g_post_ffn.reshape(1, D))
    return out.reshape(B, S, D)


def _layer(x, c, positions, w_ada, b_ada, g_pre_mix, g_post_mix, g_pre_ffn, g_post_ffn,
           w_in, cmp_pos, w_cmp_k1, w_cmp_k2, w_cmp_v1, w_cmp_v2, g_out_nsa, g_out_dil, w_o,
           w_router, b_router, w_up, b_up, w_down, b_down):
    B, S, D = x.shape
    mod6 = _ada(c, w_ada, b_ada).reshape(B, 6, D)
    (q_raw_t, q_rot_t, kv16, ks, vs_t, kw, vw_t, gates, qbs, kbs, vbs) = _proj(x, mod6, g_pre_mix, positions, w_in)
    kcf, vcf = _cmpmlp(kv16, cmp_pos, w_cmp_k1, w_cmp_k2, w_cmp_v1, w_cmp_v2)
    o_cmp, bias_t = _cmpsel(q_raw_t, kcf, vcf)
    o_slc = _slc(q_rot_t, bias_t, ks, vs_t)
    o_win = _win(q_rot_t, kw, vw_t)
    ods, lses = [], []
    for (window, dil), qb, kb, vb in zip(DIL_CONFIGS, qbs, kbs, vbs):
        o, lse = _band(qb, kb, vb, dil=dil, max_dist=window // dil, tq=128, tk=128, nt=2)
        ods.append(o)
        lses.append(lse)
    x1 = _out(o_cmp, o_slc, o_win, gates, ods, lses, x, mod6, g_out_nsa, g_out_dil, w_o, g_post_mix)
    h2, top_idx, gate, rank, counts = _route(x1, mod6, g_pre_ffn, w_router, b_router)
    tm_moe = 512
    pos, meta, n_tiles = _moe_layout(top_idx[:, :TOP_K], rank[:, :TOP_K], counts[0, :N_EXPERTS], tm_moe)
    pos_flat = pos.reshape(B * S * TOP_K)
    xs = _dispatch(h2, pos_flat, meta["pad_start"], meta["pad_len"], meta["n_valid"], n_tiles, tm_moe)
    ys = _moe(xs, meta, w_up, b_up, w_down, b_down, tm_moe)
    return _final(ys, pos_flat, gate, x1, mod6, g_post_ffn)


def kernel(x, c, positions, w_ada, b_ada, g_pre_mix, g_post_mix, g_pre_ffn, g_post_ffn, w_in, cmp_pos,
           w_cmp_k1, w_cmp_k2, w_cmp_v1, w_cmp_v2, g_out_nsa, g_out_dil, w_o, w_router, b_router,
           w_up, b_up, w_down, b_down):
    depth = w_ada.shape[0]
    for l in range(depth):
        x = _layer(x, c, positions, w_ada[l], b_ada[l], g_pre_mix[l], g_post_mix[l], g_pre_ffn[l],
                   g_post_ffn[l], w_in[l], cmp_pos[l], w_cmp_k1[l], w_cmp_k2[l], w_cmp_v1[l], w_cmp_v2[l],
                   g_out_nsa[l], g_out_dil[l], w_o[l], w_router[l], b_router[l], w_up[l], b_up[l],
                   w_down[l], b_down[l])
    return x
```

```python
import functools

import numpy as np
import jax
import jax.numpy as jnp
from jax import lax
from jax.experimental import pallas as pl
from jax.experimental.pallas import tpu as pltpu

F32 = jnp.float32
BF16 = jnp.bfloat16
I32 = jnp.int32

D_MODEL = 1024
HEAD_DIM = 64
N_HEADS_NSA = 8
N_KV_NSA = 2
GROUP_NSA = 4
N_HEADS_DIL = 8
ROPE_THETA = 500000.0
ROPE_DIM = 16
ROPE_HALF = 8
CMP_BLOCK = 32
CMP_STRIDE = 16
CMP_HIDDEN = 256
SLC_BLOCK = 64
SLC_SHIFT = 6
SLC_TOPK = 16
WIN_NSA = 512
DIL_CONFIGS = ((128, 1), (512, 4), (2048, 16))
N_EXPERTS = 32
TOP_K = 4
D_FF = 1024
SWIGLU_LIMIT = 7.0
SWIGLU_ALPHA = 1.702
RMS_EPS = 1e-6
NEG_INF = -1e30
FORCE_SCORE = 1e9
SCALE = HEAD_DIM ** -0.5
LOG2E = 1.4426950408889634
DEN_ROWS = 16

Q_NSA = 512
KV_NSA = 128
Q_DIL = 512
LANES = 128
SUBLANES = 8
VMEM_LIMIT = 56 * 1024 * 1024
DMA_UNROLL = 8

_NT = (((1,), (1,)), ((), ()))


def _cparams(sem):
    return pltpu.CompilerParams(dimension_semantics=sem, vmem_limit_bytes=VMEM_LIMIT)


def _rms(x, g):
    return x * lax.rsqrt(jnp.mean(x * x, axis=-1, keepdims=True) + RMS_EPS) * g


def _split3_dot(a, e):
    hi = a.astype(BF16)
    r1 = a - hi.astype(F32)
    mid = r1.astype(BF16)
    lo = (r1 - mid.astype(F32)).astype(BF16)
    return (jnp.dot(hi, e, preferred_element_type=F32) + jnp.dot(mid, e, preferred_element_type=F32)
            + jnp.dot(lo, e, preferred_element_type=F32))


def _ada_kernel(c_ref, w_ref, b_ref, o_ref):
    c = c_ref[...]
    a = c * jax.nn.sigmoid(c)
    o_ref[...] = jnp.dot(a, w_ref[...], preferred_element_type=F32,
                         precision=lax.Precision.HIGHEST) + b_ref[...]


def _ada(c, w_ada, b_ada):
    B, D = c.shape
    n = w_ada.shape[1] // D
    return pl.pallas_call(
        _ada_kernel,
        grid=(n,),
        in_specs=[pl.BlockSpec((B, D), lambda j: (0, 0)),
                  pl.BlockSpec((D, D), lambda j: (0, j)),
                  pl.BlockSpec((1, D), lambda j: (0, j))],
        out_specs=pl.BlockSpec((B, D), lambda j: (0, j)),
        out_shape=jax.ShapeDtypeStruct((B, n * D), F32),
        compiler_params=_cparams(("arbitrary",)),
        name="ada",
    )(c, w_ada, b_ada.reshape(1, -1))


def _rope_tables(pos_row, inv_col, pass_row, expand):
    ang = inv_col * pos_row
    parts = []
    for v in (jnp.cos(ang), jnp.sin(ang)):
        hi = v.astype(BF16)
        parts += [hi, (v - hi.astype(F32)).astype(BF16)]
    tab_t = jnp.concatenate(parts, axis=0)
    out = lax.dot_general(tab_t, expand, (((0,), (0,)), ((), ())), preferred_element_type=F32)
    return out[:, :LANES] + pass_row, out[:, LANES:2 * LANES], out[:, 2 * LANES:]


def _rope_blk(x, cs, s1, s2):
    return x * cs + pltpu.roll(x, LANES - ROPE_HALF, 1) * s1 + pltpu.roll(x, ROPE_HALF, 1) * s2


def _proj_kernel(x_ref, mod_ref, g_ref, pos_ref, tab_ref, exp_ref, wa_ref, wg_ref, wb_ref,
                 qrawt_ref, qrott_ref, kv16_ref, ks_ref, vst_ref, kw_ref, vwt_ref, gt_ref,
                 *dil_refs_and_scratch):
    n_cfg = len(DIL_CONFIGS)
    qb_refs = dil_refs_and_scratch[0:n_cfg]
    kb_refs = dil_refs_and_scratch[n_cfg:2 * n_cfg]
    vb_refs = dil_refs_and_scratch[2 * n_cfg:3 * n_cfg]
    st_sc = dil_refs_and_scratch[3 * n_cfg]
    tm = x_ref.shape[1]

    def emit_dilated(val, blk, refs):
        st_sc[...] = val
        for (_, d), ref in zip(DIL_CONFIGS, refs):
            for r in range(d):
                piece = val if d == 1 else st_sc[pl.ds(r, tm // d, stride=d), :]
                lo = r * Q_DIL + blk * LANES
                ref[0, :, lo:lo + LANES] = piece.astype(BF16)

    x = x_ref[0]
    sh = mod_ref[0, 0:1, :]
    sc = mod_ref[0, 1:2, :]
    h = (_rms(x, g_ref[...]) * (1.0 + sc) + sh).astype(BF16)
    cs, s1, s2 = _rope_tables(pos_ref[0], tab_ref[0:ROPE_HALF, 0:1], tab_ref[ROPE_HALF:ROPE_HALF + 1, :],
                              exp_ref[...])

    pa = jnp.dot(h, wa_ref[...], preferred_element_type=F32)
    for r in range(Q_NSA // LANES):
        blk = pa[:, r * LANES:(r + 1) * LANES]
        qrawt_ref[0, r * LANES:(r + 1) * LANES, :] = (blk * (SCALE * LOG2E)).T.astype(BF16)
        rot = _rope_blk(blk, cs, s1, s2) * (SCALE * LOG2E)
        qrott_ref[0, r * LANES:(r + 1) * LANES, :] = rot.T.astype(BF16)
    o = Q_NSA
    for t in range(2):
        st_sc[...] = pa[:, o + t * LANES:o + (t + 1) * LANES]
        for j in range(CMP_STRIDE):
            kv16_ref[t, 0, :, j * LANES:(j + 1) * LANES] = st_sc[pl.ds(j, tm // CMP_STRIDE, stride=CMP_STRIDE), :]
    ks_ref[0] = _rope_blk(pa[:, o + 2 * LANES:o + 3 * LANES], cs, s1, s2).astype(BF16)
    vst_ref[0] = pa[:, o + 3 * LANES:o + 4 * LANES].T.astype(BF16)
    kw_ref[0] = _rope_blk(pa[:, o + 4 * LANES:o + 5 * LANES], cs, s1, s2).astype(BF16)
    vwt_ref[0] = pa[:, o + 5 * LANES:o + 6 * LANES].T.astype(BF16)

    gt_ref[0] = jnp.dot(h, wg_ref[...], preferred_element_type=F32)

    pb = jnp.dot(h, wb_ref[...], preferred_element_type=F32)
    for blk in range(Q_DIL // LANES):
        lanes = slice(blk * LANES, (blk + 1) * LANES)
        emit_dilated(_rope_blk(pb[:, lanes], cs, s1, s2) * SCALE, blk, qb_refs)
        emit_dilated(_rope_blk(pb[:, Q_DIL + blk * LANES:Q_DIL + (blk + 1) * LANES], cs, s1, s2), blk, kb_refs)
        emit_dilated(pb[:, 2 * Q_DIL + blk * LANES:2 * Q_DIL + (blk + 1) * LANES], blk, vb_refs)


def _nsa_perm():
    cols = []
    for r in range(GROUP_NSA):
        for hk in range(N_KV_NSA):
            hq = hk * GROUP_NSA + r
            cols.extend(range(hq * HEAD_DIM, (hq + 1) * HEAD_DIM))
    return np.asarray(cols, np.int32)


def _rope_const_tables():
    half = ROPE_HALF
    inv = ROPE_THETA ** (-jnp.arange(half, dtype=F32) / half)
    lane = np.arange(LANES) % HEAD_DIM
    passthrough = (lane >= ROPE_DIM).astype(np.float32)
    tab = jnp.zeros((2 * half, LANES), F32).at[:half, 0].set(inv).at[half].set(passthrough)
    expand = np.zeros((4 * half, 3 * LANES), np.float32)
    for l in range(LANES):
        f = lane[l] % half
        if lane[l] < ROPE_DIM:
            expand[f, l] = expand[half + f, l] = 1.0
        if lane[l] < half:
            expand[2 * half + f, LANES + l] = expand[3 * half + f, LANES + l] = -1.0
        elif lane[l] < ROPE_DIM:
            expand[2 * half + f, 2 * LANES + l] = expand[3 * half + f, 2 * LANES + l] = 1.0
    return tab, jnp.asarray(expand, BF16)


def _proj(x, mod6, g_pre, positions, w_in, tm=512):
    B, S, D = x.shape
    perm = _nsa_perm()
    gate_lo = Q_NSA + 6 * KV_NSA
    n_gate = 3 * N_HEADS_NSA
    w_a = jnp.concatenate([w_in[:, :Q_NSA][:, perm], w_in[:, Q_NSA:gate_lo]], axis=1).astype(BF16)
    w_g = jnp.pad(w_in[:, gate_lo:gate_lo + n_gate], ((0, 0), (0, LANES - n_gate))).astype(BF16)
    w_b = w_in[:, gate_lo + n_gate:].astype(BF16)
    pos = positions.astype(F32).reshape(B, 1, S)
    tab, expand = _rope_const_tables()
    wa_n, wb_n = w_a.shape[1], w_b.shape[1]

    def tok(width, dtype):
        return (pl.BlockSpec((1, tm, width), lambda b, i: (b, i, 0)),
                jax.ShapeDtypeStruct((B, S, width), dtype))

    def tok_t(width, dtype):
        return (pl.BlockSpec((1, width, tm), lambda b, i: (b, 0, i)),
                jax.ShapeDtypeStruct((B, width, S), dtype))

    def dil_view(d):
        return (pl.BlockSpec((1, tm // d, d * Q_DIL), lambda b, i: (b, i, 0)),
                jax.ShapeDtypeStruct((B, S // d, d * Q_DIL), BF16))

    kv16 = (pl.BlockSpec((2, 1, tm // CMP_STRIDE, CMP_STRIDE * KV_NSA), lambda b, i: (0, b, i, 0)),
            jax.ShapeDtypeStruct((2, B, S // CMP_STRIDE, CMP_STRIDE * KV_NSA), F32))
    outs = [tok_t(Q_NSA, BF16), tok_t(Q_NSA, BF16), kv16,
            tok(LANES, BF16), tok_t(LANES, BF16), tok(LANES, BF16), tok_t(LANES, BF16), tok(LANES, F32)]
    outs += [dil_view(d) for _ in range(3) for _, d in DIL_CONFIGS]
    res = pl.pallas_call(
        _proj_kernel,
        grid=(B, S // tm),
        in_specs=[pl.BlockSpec((1, tm, D), lambda b, i: (b, i, 0)),
                  pl.BlockSpec((1, 6, D), lambda b, i: (b, 0, 0)),
                  pl.BlockSpec((1, D), lambda b, i: (0, 0)),
                  pl.BlockSpec((1, 1, tm), lambda b, i: (b, 0, i)),
                  pl.BlockSpec(tab.shape, lambda b, i: (0, 0)),
                  pl.BlockSpec(expand.shape, lambda b, i: (0, 0)),
                  pl.BlockSpec((D, wa_n), lambda b, i: (0, 0)),
                  pl.BlockSpec((D, LANES), lambda b, i: (0, 0)),
                  pl.BlockSpec((D, wb_n), lambda b, i: (0, 0))],
        out_specs=[o[0] for o in outs],
        out_shape=[o[1] for o in outs],
        scratch_shapes=[pltpu.VMEM((tm, LANES), F32)],
        compiler_params=_cparams(("parallel", "parallel")),
        name="proj",
    )(x, mod6, g_pre.reshape(1, D), pos, tab, expand, w_a, w_g, w_b)
    n_cfg = len(DIL_CONFIGS)
    n0 = len(outs) - 3 * n_cfg
    return tuple(res[:n0]) + (res[n0:n0 + n_cfg], res[n0 + n_cfg:n0 + 2 * n_cfg], res[n0 + 2 * n_cfg:])


def _cmpmlp_kernel(a_ref, p_ref, w1_ref, w1x_ref, w2x_ref, o_ref):
    a = a_ref[0, 0].astype(BF16)
    bias = jnp.dot(p_ref[...], w1_ref[0], preferred_element_type=F32)[0:1]
    n16 = a.shape[0]
    out = jnp.zeros((n16, KV_NSA), F32)
    for h in range(N_KV_NSA):
        u = jnp.dot(a, w1x_ref[0, h, 0], preferred_element_type=F32)
        v = jnp.dot(a, w1x_ref[0, h, 1], preferred_element_type=F32)
        hid = jax.nn.gelu(u + pltpu.roll(v, n16 - 1, 0) + bias)
        out = out + jnp.dot(hid.astype(BF16), w2x_ref[0, h], preferred_element_type=F32)
    o_ref[0, 0] = out.astype(BF16)


def _cmpmlp(a, cmp_pos, w_k1, w_k2, w_v1, w_v2):
    _, B, n16, _ = a.shape
    seg = CMP_STRIDE * HEAD_DIM

    def expand1(w):
        w = w.reshape(2, CMP_STRIDE, HEAD_DIM, CMP_HIDDEN)
        z = jnp.zeros_like(w)
        per_head = [jnp.concatenate([w, z] if h == 0 else [z, w], axis=2) for h in range(N_KV_NSA)]
        return jnp.stack(per_head, axis=0).reshape(N_KV_NSA, 2, CMP_STRIDE * KV_NSA, CMP_HIDDEN)

    def expand2(w):
        z = jnp.zeros_like(w)
        return jnp.stack([jnp.concatenate([w, z] if h == 0 else [z, w], axis=1) for h in range(N_KV_NSA)], axis=0)

    w1 = jnp.stack([w_k1, w_v1], axis=0).astype(BF16)
    w1x = jnp.stack([expand1(w_k1), expand1(w_v1)], axis=0).astype(BF16)
    w2x = jnp.stack([expand2(w_k2), expand2(w_v2)], axis=0).astype(BF16)
    p8 = jnp.broadcast_to(cmp_pos.reshape(1, CMP_BLOCK * HEAD_DIM), (8, CMP_BLOCK * HEAD_DIM)).astype(BF16)
    out = pl.pallas_call(
        _cmpmlp_kernel,
        grid=(2, B),
        in_specs=[pl.BlockSpec((1, 1, n16, CMP_STRIDE * KV_NSA), lambda t, b: (t, b, 0, 0)),
                  pl.BlockSpec((8, 2 * seg), lambda t, b: (0, 0)),
                  pl.BlockSpec((1, 2 * seg, CMP_HIDDEN), lambda t, b: (t, 0, 0)),
                  pl.BlockSpec((1, N_KV_NSA, 2, CMP_STRIDE * KV_NSA, CMP_HIDDEN), lambda t, b: (t, 0, 0, 0, 0)),
                  pl.BlockSpec((1, N_KV_NSA, CMP_HIDDEN, KV_NSA), lambda t, b: (t, 0, 0, 0))],
        out_specs=pl.BlockSpec((1, 1, n16, KV_NSA), lambda t, b: (t, b, 0, 0)),
        out_shape=jax.ShapeDtypeStruct((2, B, n16, KV_NSA), BF16),
        compiler_params=_cparams(("parallel", "parallel")),
        name="cmpmlp",
    )(a, p8, w1, w1x, w2x)
    return out[0], out[1]


def _query_groups_t(qt_ref, tq):
    half0 = lax.broadcasted_iota(I32, (LANES, tq), 0) < HEAD_DIM
    zero = jnp.zeros((LANES, tq), BF16)
    groups = []
    for hk in range(N_KV_NSA):
        keep = half0 if hk == 0 else jnp.logical_not(half0)
        for r in range(GROUP_NSA):
            groups.append(jnp.where(keep, qt_ref[0, r * LANES:(r + 1) * LANES, :], zero))
    return groups


def _store_heads_t(ot, tq, o_ref):
    half0 = lax.broadcasted_iota(I32, (tq, LANES), 1) < HEAD_DIM
    for r in range(GROUP_NSA):
        o0 = ot[:, r * tq:(r + 1) * tq].T
        o1 = ot[:, (GROUP_NSA + r) * tq:(GROUP_NSA + r + 1) * tq].T
        o_ref[0, :, r * LANES:(r + 1) * LANES] = jnp.where(half0, o0, o1)


def _cmpsel_kernel(qt_ref, kc_ref, vct_ref, ovt_ref, o_ref, bias_ref, *, tq):
    i = pl.program_id(1)
    n16 = kc_ref.shape[1]
    cols = 8 * tq
    qa = jnp.concatenate(_query_groups_t(qt_ref, tq), axis=1)
    st = jnp.dot(kc_ref[0], qa, preferred_element_type=F32)
    t = i * tq + (lax.broadcasted_iota(I32, (n16, cols), 1) & (tq - 1))
    c = lax.broadcasted_iota(I32, (n16, cols), 0)
    valid = (c * CMP_STRIDE + (CMP_BLOCK - 1)) <= t
    st = jnp.where(valid, st, NEG_INF)
    m = jnp.max(st, axis=0, keepdims=True)
    e = jnp.exp2(st - m)
    l = jnp.sum(e, axis=0, keepdims=True)
    p = jnp.where(valid, e * (1.0 / l), 0.0)
    ot = jnp.dot(vct_ref[0], p.astype(BF16), preferred_element_type=F32)
    _store_heads_t(ot, tq, o_ref)

    n_slc = ovt_ref.shape[0]
    j = lax.broadcasted_iota(I32, (n_slc, tq), 0)
    cur = (i * tq + lax.broadcasted_iota(I32, (n_slc, tq), 1)) >> SLC_SHIFT
    forced = (j == 0) | (j == cur) | (j == cur - 1)
    ovt = ovt_ref[...]
    biases = []
    for hk in range(N_KV_NSA):
        ps = p[:, hk * GROUP_NSA * tq:(hk * GROUP_NSA + 1) * tq]
        for r in range(1, GROUP_NSA):
            ps = ps + p[:, (hk * GROUP_NSA + r) * tq:(hk * GROUP_NSA + r + 1) * tq]
        hi = ps.astype(BF16)
        lo = (ps - hi.astype(F32)).astype(BF16)
        pslc = (jnp.dot(ovt, hi, preferred_element_type=F32)
                + jnp.dot(ovt, lo, preferred_element_type=F32))
        score = jnp.where(forced, FORCE_SCORE, jnp.where(j <= cur, pslc, -1.0))
        rank = jnp.zeros((n_slc, tq), I32)
        for ii in range(n_slc):
            ri = score[ii:ii + 1, :]
            beats = (ri > score) | ((ri == score) & (j > ii))
            rank = rank + beats.astype(I32)
        biases.append(jnp.where(rank < SLC_TOPK, 0.0, NEG_INF))
    bias_ref[0] = jnp.concatenate(biases, axis=0).astype(BF16)


def _overlap_t(S):
    n16 = S // CMP_STRIDE
    n_slc = S // SLC_BLOCK
    cs = np.arange(n16) * CMP_STRIDE
    js = np.arange(n_slc) * SLC_BLOCK
    ov = np.clip(np.minimum(cs[:, None] + CMP_BLOCK, js[None, :] + SLC_BLOCK)
                 - np.maximum(cs[:, None], js[None, :]), 0, None).astype(np.float32) / CMP_BLOCK
    ov[n16 - 1] = 0.0
    return jnp.asarray(ov.T, BF16)


def _cmpsel(q_raw_t, kcf, vcf, tq=256):
    B, _, S = q_raw_t.shape
    n16 = S // CMP_STRIDE
    n_slc = S // SLC_BLOCK
    assert n_slc == HEAD_DIM, "selection bias is laid out as one 64-lane half per kv head"
    return pl.pallas_call(
        functools.partial(_cmpsel_kernel, tq=tq),
        grid=(B, S // tq),
        in_specs=[pl.BlockSpec((1, Q_NSA, tq), lambda b, i: (b, 0, i)),
                  pl.BlockSpec((1, n16, KV_NSA), lambda b, i: (b, 0, 0)),
                  pl.BlockSpec((1, KV_NSA, n16), lambda b, i: (b, 0, 0)),
                  pl.BlockSpec((n_slc, n16), lambda b, i: (0, 0))],
        out_specs=[pl.BlockSpec((1, tq, Q_NSA), lambda b, i: (b, i, 0)),
                   pl.BlockSpec((1, 2 * n_slc, tq), lambda b, i: (b, 0, i))],
        out_shape=[jax.ShapeDtypeStruct((B, S, Q_NSA), F32),
                   jax.ShapeDtypeStruct((B, 2 * n_slc, S), BF16)],
        compiler_params=_cparams(("parallel", "parallel")),
        name="cmpsel",
    )(q_raw_t, kcf, vcf.transpose(0, 2, 1), _overlap_t(S))


def _slc_kernel(qi_ref, kj_ref, qt_ref, bt_ref, k_ref, vt_ref, o_ref, qa_sc, m_sc, acc_sc, *, tq, tk):
    step = pl.program_id(1)
    i = qi_ref[step]
    kj = kj_ref[step]
    last = (i * tq + tq - 1) // tk
    cols = 8 * tq

    @pl.when(kj == 0)
    def _():
        half0 = lax.broadcasted_iota(I32, (LANES, tq), 0) < HEAD_DIM
        zero = jnp.zeros((LANES, tq), BF16)
        bt = bt_ref[0]
        groups = []
        for hk in range(N_KV_NSA):
            keep = half0 if hk == 0 else jnp.logical_not(half0)
            bh = jnp.where(keep, bt, zero)
            for r in range(GROUP_NSA):
                qb = qt_ref[0, r * LANES:(r + 1) * LANES, :]
                groups.append(jnp.concatenate([jnp.where(keep, qb, zero), bh], axis=0))
        qa_sc[...] = jnp.concatenate(groups, axis=1)
        m_sc[...] = jnp.full((1, cols), NEG_INF, F32)
        acc_sc[...] = jnp.zeros(acc_sc.shape, F32)

    def update(on_diagonal):
        kblk = (kj * tk + lax.broadcasted_iota(I32, (tk, LANES), 0)) >> SLC_SHIFT
        lane = lax.broadcasted_iota(I32, (tk, LANES), 1) & (HEAD_DIM - 1)
        onehot = jnp.where(kblk == lane, 1.0, 0.0).astype(BF16)
        kaug = jnp.concatenate([k_ref[0], onehot], axis=1)
        st = jnp.dot(kaug, qa_sc[...], preferred_element_type=F32)
        if on_diagonal:
            kpos = kj * tk + lax.broadcasted_iota(I32, (tk, cols), 0)
            t = i * tq + (lax.broadcasted_iota(I32, (tk, cols), 1) & (tq - 1))
            st = jnp.where(kpos <= t, st, NEG_INF)
        m_old = m_sc[...]
        m_new = jnp.maximum(m_old, jnp.max(st, axis=0, keepdims=True))
        alpha = jnp.exp2(m_old - m_new)
        p = jnp.exp2(st - m_new).astype(BF16)
        vt_ones = jnp.concatenate([vt_ref[0], jnp.ones((DEN_ROWS, tk), BF16)], axis=0)
        acc_sc[...] = alpha * acc_sc[...] + jnp.dot(vt_ones, p, preferred_element_type=F32)
        m_sc[...] = m_new

    @pl.when(kj < last)
    def _():
        update(False)

    @pl.when(kj == last)
    def _():
        update(True)
        acc = acc_sc[...]
        _store_heads_t(acc[:LANES] * (1.0 / acc[LANES:LANES + 1]), tq, o_ref)


def _slc(q_rot_t, bias_t, ks, vs_t, tq=512, tk=512):
    B, _, S = q_rot_t.shape
    assert tk % tq == 0
    qi, kj = [], []
    for i in range(S // tq):
        for j in range((i * tq + tq - 1) // tk + 1):
            qi.append(i)
            kj.append(j)
    grid_spec = pltpu.PrefetchScalarGridSpec(
        num_scalar_prefetch=2,
        grid=(B, len(qi)),
        in_specs=[pl.BlockSpec((1, Q_NSA, tq), lambda b, s, qi, kj: (b, 0, qi[s])),
                  pl.BlockSpec((1, LANES, tq), lambda b, s, qi, kj: (b, 0, qi[s])),
                  pl.BlockSpec((1, tk, KV_NSA), lambda b, s, qi, kj: (b, kj[s], 0)),
                  pl.BlockSpec((1, KV_NSA, tk), lambda b, s, qi, kj: (b, 0, kj[s]))],
        out_specs=pl.BlockSpec((1, tq, Q_NSA), lambda b, s, qi, kj: (b, qi[s], 0)),
        scratch_shapes=[pltpu.VMEM((2 * LANES, 8 * tq), BF16),
                        pltpu.VMEM((1, 8 * tq), F32),
                        pltpu.VMEM((LANES + DEN_ROWS, 8 * tq), F32)],
    )
    return pl.pallas_call(
        functools.partial(_slc_kernel, tq=tq, tk=tk),
        grid_spec=grid_spec,
        out_shape=jax.ShapeDtypeStruct((B, S, Q_NSA), F32),
        compiler_params=_cparams(("parallel", "arbitrary")),
        name="slc",
    )(jnp.asarray(qi, I32), jnp.asarray(kj, I32), q_rot_t, bias_t, ks, vs_t)


def _win_kernel(qt_ref, k0_ref, k1_ref, k2_ref, v0_ref, v1_ref, v2_ref, o_ref, *, tq, tk, max_dist):
    i = pl.program_id(1)
    a = (i * tq) // tk
    cols = 8 * tq
    qa = jnp.concatenate(_query_groups_t(qt_ref, tq), axis=1)
    t = i * tq + (lax.broadcasted_iota(I32, (tk, cols), 1) & (tq - 1))
    row = lax.broadcasted_iota(I32, (tk, cols), 0)
    sts = []
    for jt, k_ref in enumerate((k0_ref, k1_ref, k2_ref)):
        st = jnp.dot(k_ref[0], qa, preferred_element_type=F32)
        kpos = (a - 2 + jt) * tk + row
        if jt == 0:
            st = jnp.where((t - kpos <= max_dist) & (kpos >= 0), st, NEG_INF)
        elif jt == 1:
            st = jnp.where(kpos >= 0, st, NEG_INF)
        else:
            st = jnp.where(kpos <= t, st, NEG_INF)
        sts.append(st)
    m = jnp.max(sts[0], axis=0, keepdims=True)
    for st in sts[1:]:
        m = jnp.maximum(m, jnp.max(st, axis=0, keepdims=True))
    acc = jnp.zeros((LANES + DEN_ROWS, cols), F32)
    for st, v_ref in zip(sts, (v0_ref, v1_ref, v2_ref)):
        vt_ones = jnp.concatenate([v_ref[0], jnp.ones((DEN_ROWS, tk), BF16)], axis=0)
        acc = acc + jnp.dot(vt_ones, jnp.exp2(st - m).astype(BF16), preferred_element_type=F32)
    _store_heads_t(acc[:LANES] * (1.0 / acc[LANES:LANES + 1]), tq, o_ref)


def _win(q_rot_t, kw, vw_t, tq=256, tk=256):
    B, _, S = q_rot_t.shape
    max_dist = WIN_NSA - 1
    assert tk % tq == 0 and 2 * tk == max_dist + 1

    def k_map(jt):
        return lambda b, i: (b, jnp.maximum((i * tq) // tk - 2 + jt, 0), 0)

    def v_map(jt):
        return lambda b, i: (b, 0, jnp.maximum((i * tq) // tk - 2 + jt, 0))

    return pl.pallas_call(
        functools.partial(_win_kernel, tq=tq, tk=tk, max_dist=max_dist),
        grid=(B, S // tq),
        in_specs=[pl.BlockSpec((1, Q_NSA, tq), lambda b, i: (b, 0, i))]
        + [pl.BlockSpec((1, tk, KV_NSA), k_map(jt)) for jt in range(3)]
        + [pl.BlockSpec((1, KV_NSA, tk), v_map(jt)) for jt in range(3)],
        out_specs=pl.BlockSpec((1, tq, Q_NSA), lambda b, i: (b, i, 0)),
        out_shape=jax.ShapeDtypeStruct((B, S, Q_NSA), F32),
        compiler_params=_cparams(("parallel", "parallel")),
        name="win",
    )(q_rot_t, kw, kw, kw, vw_t, vw_t, vw_t)


def _band_kernel(*refs, tq, tk, nt, max_dist):
    q_ref = refs[0]
    k_refs = refs[1:1 + nt]
    v_refs = refs[1 + nt:1 + 2 * nt]
    o_ref = refs[1 + 2 * nt]
    lse_ref = refs[2 + 2 * nt]
    i = pl.program_id(2)
    a = (i * tq) // tk
    lane = lax.broadcasted_iota(I32, (tq, LANES), 1)
    half0 = lane < HEAD_DIM
    row_t = i * tq + (lax.broadcasted_iota(I32, (2 * tq, nt * tk), 0) & (tq - 1))
    kpos = (a - (nt - 1)) * tk + lax.broadcasted_iota(I32, (2 * tq, nt * tk), 1)
    dist = row_t - kpos
    ok = (dist >= 0) & (dist <= max_dist) & (kpos >= 0)
    ones = jnp.ones((nt * tk, LANES), BF16)
    lse_acc = jnp.zeros((tq, LANES), F32)
    for blk in range(Q_DIL // LANES):
        lanes = slice(blk * LANES, (blk + 1) * LANES)
        qb = q_ref[0, :, lanes]
        zero = jnp.zeros_like(qb)
        qs = jnp.concatenate([jnp.where(half0, qb, zero), jnp.where(half0, zero, qb)], axis=0)
        kcat = jnp.concatenate([k_ref[0, :, lanes] for k_ref in k_refs], axis=0)
        vcat = jnp.concatenate([v_ref[0, :, lanes] for v_ref in v_refs], axis=0)
        s = lax.dot_general(qs, kcat, _NT, preferred_element_type=F32)
        s = jnp.where(ok, s, NEG_INF)
        m = jnp.max(s, axis=-1, keepdims=True)
        p = jnp.exp(s - m).astype(BF16)
        ol = jnp.dot(p, jnp.concatenate([vcat, ones], axis=1), preferred_element_type=F32)
        den = ol[:, LANES:]
        o = ol[:, :LANES] * (1.0 / den)
        o_ref[0, :, lanes] = jnp.where(half0, o[:tq], o[tq:])
        lse = m + jnp.log(den[:, 0:1])
        lse_acc = jnp.where(lane == 2 * blk, lse[:tq], lse_acc)
        lse_acc = jnp.where(lane == 2 * blk + 1, lse[tq:], lse_acc)
    lse_ref[0] = lse_acc


def _band(q, k, v, *, dil, max_dist, tq, tk, nt):
    B, m, _ = q.shape

    def kv_map(jt):
        return lambda b, r, i: (b, jnp.maximum((i * tq) // tk - (nt - 1) + jt, 0), r)

    tok_spec = pl.BlockSpec((1, tq, Q_DIL), lambda b, r, i: (b, i, r))
    kv_specs = [pl.BlockSpec((1, tk, Q_DIL), kv_map(jt)) for jt in range(nt)]
    return pl.pallas_call(
        functools.partial(_band_kernel, tq=tq, tk=tk, nt=nt, max_dist=max_dist),
        grid=(B, dil, m // tq),
        in_specs=[tok_spec] + kv_specs + kv_specs,
        out_specs=[tok_spec, pl.BlockSpec((1, tq, LANES), lambda b, r, i: (b, i, r))],
        out_shape=[jax.ShapeDtypeStruct((B, m, dil * Q_DIL), F32),
                   jax.ShapeDtypeStruct((B, m, dil * LANES), F32)],
        compiler_params=_cparams(("parallel", "parallel", "parallel")),
        name=f"band_d{dil}_w{max_dist}",
    )(q, *([k] * nt), *([v] * nt))


def _out_kernel(ocmp_ref, oslc_ref, owin_ref, gt_ref, od1_ref, od4_ref, od16_ref, l1_ref, l4_ref, l16_ref,
                x_ref, mod_ref, eg_ref, ed_ref, gnsa_ref, gdil_ref, wo_ref, gpost_ref, o_ref, st_sc):
    tm = x_ref.shape[1]

    def token_order(ref, d, n_blk):
        if d == 1:
            return ref[0]
        blocks = []
        for blk in range(n_blk):
            for r in range(d):
                lo = (r * n_blk + blk) * LANES
                st_sc[pl.ds(r, tm // d, stride=d), :] = ref[0, :, lo:lo + LANES]
            blocks.append(st_sc[...])
        return blocks[0] if n_blk == 1 else jnp.concatenate(blocks, axis=1)

    sg = jax.nn.sigmoid(gt_ref[0])
    oa = (_split3_dot(sg, eg_ref[0]) * ocmp_ref[0] + _split3_dot(sg, eg_ref[1]) * oslc_ref[0]
          + _split3_dot(sg, eg_ref[2]) * owin_ref[0])
    ya = _rms(oa, gnsa_ref[...])

    dils = [d for _, d in DIL_CONFIGS]
    n_blk = Q_DIL // LANES
    l1, l4, l16 = [token_order(ref, d, 1) for ref, d in zip((l1_ref, l4_ref, l16_ref), dils)]
    mx = jnp.maximum(jnp.maximum(l1, l4), l16)
    e1, e4, e16 = jnp.exp(l1 - mx), jnp.exp(l4 - mx), jnp.exp(l16 - mx)
    den = e1 + e4 + e16
    ed = ed_ref[...]
    od1, od4, od16 = [token_order(ref, d, n_blk) for ref, d in zip((od1_ref, od4_ref, od16_ref), dils)]
    inv = 1.0 / den
    ob = (_split3_dot(e1 * inv, ed) * od1 + _split3_dot(e4 * inv, ed) * od4 + _split3_dot(e16 * inv, ed) * od16)
    yb = _rms(ob, gdil_ref[...])

    y = jnp.concatenate([ya, yb], axis=1).astype(BF16)
    z = jnp.dot(y, wo_ref[...], preferred_element_type=F32)
    gt_m = mod_ref[0, 2:3, :]
    o_ref[0] = x_ref[0] + gt_m * _rms(z, gpost_ref[...])


def _gate_expanders():
    perm = _nsa_perm()
    eg = np.zeros((3, LANES, Q_NSA), np.float32)
    for lane_out, col in enumerate(perm):
        hq = col // HEAD_DIM
        for c in range(3):
            eg[c, hq * 3 + c, lane_out] = 1.0
    ed = np.zeros((LANES, Q_DIL), np.float32)
    for h in range(N_HEADS_DIL):
        ed[h, h * HEAD_DIM:(h + 1) * HEAD_DIM] = 1.0
    return jnp.asarray(eg, BF16), jnp.asarray(ed, BF16)


def _out(o_cmp, o_slc, o_win, gates, ods, lses, x, mod6, g_out_nsa, g_out_dil, w_o, g_post, tm=512):
    B, S, D = x.shape
    perm = _nsa_perm()
    eg, ed = _gate_expanders()
    w_o_p = jnp.concatenate([w_o[:Q_NSA][perm], w_o[Q_NSA:]], axis=0).astype(BF16)
    g_nsa_p = g_out_nsa[perm].reshape(1, Q_NSA)

    def tok(width):
        return pl.BlockSpec((1, tm, width), lambda b, i: (b, i, 0))

    def const(shape):
        return pl.BlockSpec(shape, lambda b, i: (0,) * len(shape))

    def view(d, width):
        return pl.BlockSpec((1, tm // d, d * width), lambda b, i: (b, i, 0))

    dils = [d for _, d in DIL_CONFIGS]
    return pl.pallas_call(
        _out_kernel,
        grid=(B, S // tm),
        in_specs=[tok(Q_NSA), tok(Q_NSA), tok(Q_NSA), tok(LANES)]
        + [view(d, Q_DIL) for d in dils] + [view(d, LANES) for d in dils]
        + [tok(D), pl.BlockSpec((1, 6, D), lambda b, i: (b, 0, 0)),
           const((3, LANES, Q_NSA)), const((LANES, Q_DIL)), const((1, Q_NSA)), const((1, Q_DIL)),
           const((D, D)), const((1, D))],
        out_specs=tok(D),
        out_shape=jax.ShapeDtypeStruct((B, S, D), F32),
        scratch_shapes=[pltpu.VMEM((tm, LANES), F32)],
        compiler_params=_cparams(("parallel", "parallel")),
        name="out",
    )(o_cmp, o_slc, o_win, gates, *ods, *lses, x, mod6, eg, ed, g_nsa_p, g_out_dil.reshape(1, Q_DIL),
      w_o_p, g_post.reshape(1, D))


def _route_kernel(x_ref, mod_ref, g_ref, wr_ref, br_ref, h_ref, idx_ref, gate_ref, rank_ref, cnt_ref, run_sc):
    i = pl.program_id(0)

    @pl.when(i == 0)
    def _():
        run_sc[...] = jnp.zeros_like(run_sc)

    sh = mod_ref[0, 3:4, :]
    sc = mod_ref[0, 4:5, :]
    h = _rms(x_ref[...], g_ref[...]) * (1.0 + sc) + sh
    h_ref[...] = h
    logits = jnp.dot(h, wr_ref[...], preferred_element_type=F32,
                     precision=lax.Precision.HIGHEST) + br_ref[...]
    tm = logits.shape[0]
    lane = lax.broadcasted_iota(I32, (tm, LANES), 1)
    work = jnp.where(lane < N_EXPERTS, logits, -jnp.inf)
    idx_out = jnp.zeros((tm, LANES), I32)
    val_out = jnp.full((tm, LANES), -jnp.inf, F32)
    rank_out = jnp.zeros((tm, LANES), F32)
    tri = jnp.where(lax.broadcasted_iota(I32, (tm, tm), 1) < lax.broadcasted_iota(I32, (tm, tm), 0),
                    1.0, 0.0).astype(BF16)
    base = run_sc[0:1, :]
    for k in range(TOP_K):
        mx = jnp.max(work, axis=-1, keepdims=True)
        ix = jnp.min(jnp.where(work == mx, lane, LANES), axis=-1, keepdims=True)
        hit = lane == ix
        idx_out = jnp.where(lane == k, ix, idx_out)
        val_out = jnp.where(lane == k, mx, val_out)
        work = jnp.where(hit, -jnp.inf, work)
        onehot = jnp.where(hit, 1.0, 0.0)
        before = jnp.dot(tri, onehot.astype(BF16), preferred_element_type=F32) + base
        rank_k = jnp.sum(onehot * before, axis=-1, keepdims=True)
        rank_out = jnp.where(lane == k, rank_k, rank_out)
        base = base + jnp.sum(onehot, axis=0, keepdims=True)
    run_sc[...] = jnp.broadcast_to(base, run_sc.shape)
    e = jnp.exp(val_out - val_out[:, 0:1])
    idx_ref[...] = idx_out
    gate_ref[...] = e / jnp.sum(e, axis=-1, keepdims=True)
    rank_ref[...] = rank_out.astype(I32)
    cnt_ref[...] = run_sc[...].astype(I32)


def _route(x1, mod6, g_pre_ffn, w_router, b_router, tm=512):
    B, S, D = x1.shape
    N = B * S
    spb = S // tm
    wr = jnp.pad(w_router, ((0, 0), (0, LANES - N_EXPERTS)))
    br = jnp.pad(b_router, (0, LANES - N_EXPERTS)).reshape(1, LANES)

    def tok(width):
        return pl.BlockSpec((tm, width), lambda i: (i, 0))

    return pl.pallas_call(
        _route_kernel,
        grid=(N // tm,),
        in_specs=[tok(D), pl.BlockSpec((1, 6, D), lambda i: (i // spb, 0, 0)),
                  pl.BlockSpec((1, D), lambda i: (0, 0)),
                  pl.BlockSpec((D, LANES), lambda i: (0, 0)),
                  pl.BlockSpec((1, LANES), lambda i: (0, 0))],
        out_specs=[tok(D), tok(LANES), tok(LANES), tok(LANES), pl.BlockSpec((8, LANES), lambda i: (0, 0))],
        out_shape=[jax.ShapeDtypeStruct((N, D), F32),
                   jax.ShapeDtypeStruct((N, LANES), I32),
                   jax.ShapeDtypeStruct((N, LANES), F32),
                   jax.ShapeDtypeStruct((N, LANES), I32),
                   jax.ShapeDtypeStruct((8, LANES), I32)],
        scratch_shapes=[pltpu.VMEM((8, LANES), F32)],
        compiler_params=_cparams(("arbitrary",)),
        name="route",
    )(x1.reshape(N, D), mod6, g_pre_ffn.reshape(1, D), wr, br)


def _dispatch_kernel(pos_ref, pad0_ref, padn_ref, nt_ref, h_ref, xs_out, zbuf, sem, zsem, *, tm, tmx, n_tiles):
    i = pl.program_id(0)
    bits = tmx.bit_length() - 1

    def pad_copies(e, fn):
        p0 = pad0_ref[e]
        head = (-p0) & (SUBLANES - 1)
        head = jnp.minimum(head, padn_ref[e])
        for r in range(SUBLANES - 1):
            @pl.when(r < head)
            def _():
                fn(pltpu.make_async_copy(zbuf.at[pl.ds(0, 1), :], xs_out.at[pl.ds(p0 + r, 1), :], zsem))
        a = p0 + head
        n = padn_ref[e] - head
        for b in range(3, bits):
            size = 1 << b
            off = (n >> (b + 1)) << (b + 1)

            @pl.when((n & size) != 0)
            def _():
                fn(pltpu.make_async_copy(zbuf.at[pl.ds(0, size), :],
                                         xs_out.at[pl.ds(pl.multiple_of(a + off, SUBLANES), size), :], zsem))

    def tail_copy(t, fn):
        fn(pltpu.make_async_copy(zbuf, xs_out.at[pl.ds(pl.multiple_of(t * tmx, tmx), tmx), :], zsem))

    def for_all_fill(fn):
        def per_expert(e, carry):
            pad_copies(e, fn)
            return carry
        lax.fori_loop(0, N_EXPERTS, per_expert, 0)

        def per_tile(t, carry):
            tail_copy(t, fn)
            return carry
        lax.fori_loop(nt_ref[0], n_tiles, per_tile, 0)

    @pl.when(i == 0)
    def _():
        zbuf[...] = jnp.zeros_like(zbuf)
        for_all_fill(lambda cp: cp.start())
        for_all_fill(lambda cp: cp.wait())

    def body(r, carry):
        for k in range(TOP_K):
            dst_row = pos_ref[(i * tm + r) * TOP_K + k]
            pltpu.make_async_copy(h_ref.at[pl.ds(r, 1), :], xs_out.at[pl.ds(dst_row, 1), :], sem).start()
        return carry

    lax.fori_loop(0, tm, body, 0, unroll=DMA_UNROLL // TOP_K)
    for k in range(TOP_K):
        pltpu.make_async_copy(h_ref, xs_out.at[pl.ds(0, tm), :], sem).wait()


def _dispatch(h2, pos_flat, pad_start, pad_len, n_valid, n_tiles, tmx, tm=512):
    N, D = h2.shape
    assert tmx & (tmx - 1) == 0
    grid_spec = pltpu.PrefetchScalarGridSpec(
        num_scalar_prefetch=4,
        grid=(N // tm,),
        in_specs=[pl.BlockSpec((tm, D), lambda i, *_: (i, 0))],
        out_specs=pl.BlockSpec(memory_space=pl.ANY),
        scratch_shapes=[pltpu.VMEM((tmx, D), F32), pltpu.SemaphoreType.DMA(()), pltpu.SemaphoreType.DMA(())],
    )
    return pl.pallas_call(
        functools.partial(_dispatch_kernel, tm=tm, tmx=tmx, n_tiles=n_tiles),
        grid_spec=grid_spec,
        out_shape=jax.ShapeDtypeStruct((n_tiles * tmx, D), F32),
        compiler_params=_cparams(("arbitrary",)),
        name="dispatch",
    )(pos_flat, pad_start, pad_len, n_valid, h2)


def _moe_kernel(te_ref, nt_ref, nx_ref, sl_ref, xs_ref, wup_hbm, bup_ref, wdn_hbm, bdn_ref, pm_ref, y_ref,
                wup_in, wdn_in, wup_sc, wdn_sc, sems):
    i = pl.program_id(0)
    n_valid = nt_ref[0]
    e = te_ref[i]
    new_expert = (i == 0) | (e != te_ref[jnp.maximum(i - 1, 0)])
    slot = sl_ref[i]
    n_blk = wup_sc.shape[1] // (2 * LANES)

    def weight_copies(ex, s):
        return (pltpu.make_async_copy(wup_hbm.at[ex], wup_in.at[s], sems.at[0, s]),
                pltpu.make_async_copy(wdn_hbm.at[ex], wdn_in.at[s], sems.at[1, s]))

    @pl.when(i == 0)
    def _():
        for cp in weight_copies(e, slot):
            cp.start()

    @pl.when(new_expert & (i < n_valid))
    def _():
        for cp in weight_copies(e, slot):
            cp.wait()
        nx = nx_ref[i]

        @pl.when(nx >= 0)
        def _():
            for cp in weight_copies(nx, 1 - slot):
                cp.start()

        pm = pm_ref[...]
        for blk in range(n_blk):
            cols = slice(blk * 2 * LANES, (blk + 1) * 2 * LANES)
            w = wup_in[slot, :, cols].astype(BF16)
            wup_sc[:, cols] = jnp.dot(w, pm, preferred_element_type=F32).astype(BF16)
        wdn_sc[...] = wdn_in[slot].astype(BF16)

    @pl.when(i < n_valid)
    def _():
        xs = xs_ref[...].astype(BF16)
        u = jnp.dot(xs, wup_sc[...], preferred_element_type=F32) + bup_ref[0]
        acts = []
        for blk in range(n_blk):
            ug = jnp.minimum(u[:, blk * 2 * LANES:blk * 2 * LANES + LANES], SWIGLU_LIMIT)
            ul = jnp.clip(u[:, blk * 2 * LANES + LANES:(blk + 1) * 2 * LANES], -SWIGLU_LIMIT, SWIGLU_LIMIT)
            acts.append((ug * jax.nn.sigmoid(SWIGLU_ALPHA * ug) * (ul + 1.0)).astype(BF16))
        act = jnp.concatenate(acts, axis=1)
        y_ref[...] = jnp.dot(act, wdn_sc[...], preferred_element_type=F32) + bdn_ref[0]

    @pl.when(i >= n_valid)
    def _():
        y_ref[...] = jnp.zeros_like(y_ref)


def _moe_layout(top_idx, rank, counts, tm):
    N = top_idx.shape[0]
    n_tiles = N * TOP_K // tm + N_EXPERTS
    e_ids = jnp.arange(N_EXPERTS, dtype=I32)
    tiles_e = (counts + tm - 1) // tm
    tile_end = jnp.sum(jnp.where(e_ids[None, :] <= e_ids[:, None], tiles_e[None, :], 0), axis=1)
    start = (tile_end - tiles_e) * tm
    pos = rank + jnp.sum(jnp.where(top_idx[:, :, None] == e_ids[None, None, :], start[None, None, :], 0), axis=-1)
    n_valid = tile_end[N_EXPERTS - 1]
    tile_ids = jnp.arange(n_tiles, dtype=I32)
    tile_e = jnp.sum((jnp.minimum(tile_ids, n_valid - 1)[:, None] >= tile_end[None, :]).astype(I32), axis=1)
    prev_e = jnp.concatenate([jnp.full((1,), -1, I32), tile_e[:-1]])
    new = (tile_e != prev_e) & (tile_ids < n_valid)
    ordinal = jnp.sum(jnp.where(tile_ids[None, :] <= tile_ids[:, None], new[None, :].astype(I32), 0), axis=1) - 1
    next_first = jnp.sum(jnp.where(tile_e[:, None] == e_ids[None, :], tile_end[None, :], 0), axis=1)
    next_e = jnp.sum(jnp.where(next_first[:, None] == tile_ids[None, :], tile_e[None, :], 0), axis=1)
    next_e = jnp.where(next_first < n_valid, next_e, -1)
    meta = dict(tile_e=tile_e.astype(I32), n_valid=n_valid.astype(I32).reshape(1), next_e=next_e.astype(I32),
                slot=(ordinal & 1).astype(I32), pad_start=(start + counts).astype(I32),
                pad_len=(tiles_e * tm - counts).astype(I32))
    return pos.astype(I32), meta, n_tiles


def _glu_perm():
    pm = np.zeros((2 * LANES, 2 * LANES), np.float32)
    for j in range(LANES):
        pm[2 * j, j] = 1.0
        pm[2 * j + 1, LANES + j] = 1.0
    return jnp.asarray(pm, BF16)


def _moe(xs, meta, w_up, b_up, w_down, b_down, tm):
    P, D = xs.shape
    n_blk = D_FF // LANES
    b_up_p = b_up.reshape(N_EXPERTS, n_blk, LANES, 2).transpose(0, 1, 3, 2).reshape(N_EXPERTS, 1, 2 * D_FF)
    b_dn = b_down.reshape(N_EXPERTS, 1, D)

    def bmap(i, te, *_):
        return (te[i], 0, 0)

    grid_spec = pltpu.PrefetchScalarGridSpec(
        num_scalar_prefetch=4,
        grid=(P // tm,),
        in_specs=[pl.BlockSpec((tm, D), lambda i, *_: (i, 0)),
                  pl.BlockSpec(memory_space=pl.ANY), pl.BlockSpec((1, 1, 2 * D_FF), bmap),
                  pl.BlockSpec(memory_space=pl.ANY), pl.BlockSpec((1, 1, D), bmap),
                  pl.BlockSpec((2 * LANES, 2 * LANES), lambda i, *_: (0, 0))],
        out_specs=pl.BlockSpec((tm, D), lambda i, *_: (i, 0)),
        scratch_shapes=[pltpu.VMEM((2, D, 2 * D_FF), F32), pltpu.VMEM((2, D_FF, D), F32),
                        pltpu.VMEM((D, 2 * D_FF), BF16), pltpu.VMEM((D_FF, D), BF16),
                        pltpu.SemaphoreType.DMA((2, 2))],
    )
    return pl.pallas_call(
        _moe_kernel,
        grid_spec=grid_spec,
        out_shape=jax.ShapeDtypeStruct((P, D), F32),
        compiler_params=_cparams(("arbitrary",)),
        name="moe",
    )(meta["tile_e"], meta["n_valid"], meta["next_e"], meta["slot"], xs, w_up, b_up_p, w_down, b_dn, _glu_perm())


def _final_kernel(pos_ref, ys_hbm, x_ref, gate_ref, gtf_ref, g_ref, o_ref, buf, sems, *, tm, n_steps):
    i = pl.program_id(0)
    slot = i & 1

    n = TOP_K * tm

    def gather(step, dst, sem):
        def body(j, carry):
            pltpu.make_async_copy(ys_hbm.at[pl.ds(pos_ref[step * n + j], 1), :], dst.at[pl.ds(j, 1), :],
                                  sem).start()
            return carry
        lax.fori_loop(0, n, body, 0, unroll=DMA_UNROLL)

    @pl.when(i == 0)
    def _():
        gather(0, buf.at[0], sems.at[0])

    @pl.when(i + 1 < n_steps)
    def _():
        gather(i + 1, buf.at[1 - slot], sems.at[1 - slot])

    pltpu.make_async_copy(ys_hbm.at[pl.ds(0, n), :], buf.at[slot], sems.at[slot]).wait()
    gate = gate_ref[...]
    y = gate[:, 0:1] * buf[slot, 0:tm, :]
    for k in range(1, TOP_K):
        y = y + gate[:, k:k + 1] * buf[slot, k * tm:(k + 1) * tm, :]
    o_ref[...] = x_ref[...] + gtf_ref[0, 5:6, :] * _rms(y, g_ref[...])


def _final(ys, pos_flat, gate, x1, mod6, g_post_ffn, tm=512):
    B, S, D = x1.shape
    N = B * S
    n_steps = N // tm
    steps_per_b = S // tm
    pos_sm = pos_flat.reshape(n_steps, tm, TOP_K).transpose(0, 2, 1).reshape(N * TOP_K)
    grid_spec = pltpu.PrefetchScalarGridSpec(
        num_scalar_prefetch=1,
        grid=(n_steps,),
        in_specs=[pl.BlockSpec(memory_space=pl.ANY),
                  pl.BlockSpec((tm, D), lambda i, p: (i, 0)),
                  pl.BlockSpec((tm, LANES), lambda i, p: (i, 0)),
                  pl.BlockSpec((1, 6, D), lambda i, p: (i // steps_per_b, 0, 0)),
                  pl.BlockSpec((1, D), lambda i, p: (0, 0))],
        out_specs=pl.BlockSpec((tm, D), lambda i, p: (i, 0)),
        scratch_shapes=[pltpu.VMEM((2, TOP_K * tm, D), F32), pltpu.SemaphoreType.DMA((2,))],
    )
    out = pl.pallas_call(
        functools.partial(_final_kernel, tm=tm, n_steps=n_steps),
        grid_spec=grid_spec,
        out_shape=jax.ShapeDtypeStruct((N, D), F32),
        compiler_params=_cparams(("arbitrary",)),
        name="final",
    )(pos_sm, ys, x1.reshape(N, D), gate, mod6, g_post_ffn.reshape(1, D))
    return out.reshape(B, S, D)


def _layer(x, c, positions, w_ada, b_ada, g_pre_mix, g_post_mix, g_pre_ffn, g_post_ffn,
           w_in, cmp_pos, w_cmp_k1, w_cmp_k2, w_cmp_v1, w_cmp_v2, g_out_nsa, g_out_dil, w_o,
           w_router, b_router, w_up, b_up, w_down, b_down):
    B, S, D = x.shape
    mod6 = _ada(c, w_ada, b_ada).reshape(B, 6, D)
    (q_raw_t, q_rot_t, kv16, ks, vs_t, kw, vw_t, gates, qbs, kbs, vbs) = _proj(x, mod6, g_pre_mix, positions, w_in)
    kcf, vcf = _cmpmlp(kv16, cmp_pos, w_cmp_k1, w_cmp_k2, w_cmp_v1, w_cmp_v2)
    o_cmp, bias_t = _cmpsel(q_raw_t, kcf, vcf)
    o_slc = _slc(q_rot_t, bias_t, ks, vs_t)
    o_win = _win(q_rot_t, kw, vw_t)
    ods, lses = [], []
    for (window, dil), qb, kb, vb in zip(DIL_CONFIGS, qbs, kbs, vbs):
        o, lse = _band(qb, kb, vb, dil=dil, max_dist=window // dil, tq=128, tk=128, nt=2)
        ods.append(o)
        lses.append(lse)
    x1 = _out(o_cmp, o_slc, o_win, gates, ods, lses, x, mod6, g_out_nsa, g_out_dil, w_o, g_post_mix)
    h2, top_idx, gate, rank, counts = _route(x1, mod6, g_pre_ffn, w_router, b_router)
    tm_moe = 512
    pos, meta, n_tiles = _moe_layout(top_idx[:, :TOP_K], rank[:, :TOP_K], counts[0, :N_EXPERTS], tm_moe)
    pos_flat = pos.reshape(B * S * TOP_K)
    xs = _dispatch(h2, pos_flat, meta["pad_start"], meta["pad_len"], meta["n_valid"], n_tiles, tm_moe)
    ys = _moe(xs, meta, w_up, b_up, w_down, b_down, tm_moe)
    return _final(ys, pos_flat, gate, x1, mod6, g_post_ffn)


def kernel(x, c, positions, w_ada, b_ada, g_pre_mix, g_post_mix, g_pre_ffn, g_post_ffn, w_in, cmp_pos,
           w_cmp_k1, w_cmp_k2, w_cmp_v1, w_cmp_v2, g_out_nsa, g_out_dil, w_o, w_router, b_router,
           w_up, b_up, w_down, b_down):
    depth = w_ada.shape[0]
    for l in range(depth):
        x = _layer(x, c, positions, w_ada[l], b_ada[l], g_pre_mix[l], g_post_mix[l], g_pre_ffn[l],
                   g_post_ffn[l], w_in[l], cmp_pos[l], w_cmp_k1[l], w_cmp_k2[l], w_cmp_v1[l], w_cmp_v2[l],
                   g_out_nsa[l], g_out_dil[l], w_o[l], w_router[l], b_router[l], w_up[l], b_up[l],
                   w_down[l], b_down[l])
    return x
```

```python
import functools

import numpy as np
import jax
import jax.numpy as jnp
from jax import lax
from jax.experimental import pallas as pl
from jax.experimental.pallas import tpu as pltpu

F32 = jnp.float32
BF16 = jnp.bfloat16
I32 = jnp.int32

D_MODEL = 1024
HEAD_DIM = 64
N_HEADS_NSA = 8
N_KV_NSA = 2
GROUP_NSA = 4
N_HEADS_DIL = 8
ROPE_THETA = 500000.0
ROPE_DIM = 16
ROPE_HALF = 8
CMP_BLOCK = 32
CMP_STRIDE = 16
CMP_HIDDEN = 256
SLC_BLOCK = 64
SLC_SHIFT = 6
SLC_TOPK = 16
WIN_NSA = 512
DIL_CONFIGS = ((128, 1), (512, 4), (2048, 16))
N_EXPERTS = 32
TOP_K = 4
D_FF = 1024
SWIGLU_LIMIT = 7.0
SWIGLU_ALPHA = 1.702
RMS_EPS = 1e-6
NEG_INF = -1e30
FORCE_SCORE = 1e9
SCALE = HEAD_DIM ** -0.5
LOG2E = 1.4426950408889634
DEN_ROWS = 16

Q_NSA = 512
KV_NSA = 128
Q_DIL = 512
LANES = 128
SUBLANES = 8
VMEM_LIMIT = 56 * 1024 * 1024
DMA_UNROLL = 32

_NT = (((1,), (1,)), ((), ()))


def _cparams(sem):
    return pltpu.CompilerParams(dimension_semantics=sem, vmem_limit_bytes=VMEM_LIMIT)


def _rms(x, g):
    return x * lax.rsqrt(jnp.mean(x * x, axis=-1, keepdims=True) + RMS_EPS) * g


def _hi_lo(a):
    hi = a.astype(BF16)
    return hi, (a - hi.astype(F32)).astype(BF16)


def _expand_dot(a, e2):
    return jnp.dot(jnp.concatenate(_hi_lo(a), axis=1), e2, preferred_element_type=F32)


def _dot3(a, b):
    ah, al = _hi_lo(a)
    bh, bl = _hi_lo(b)
    return (jnp.dot(ah, bh, preferred_element_type=F32) + jnp.dot(ah, bl, preferred_element_type=F32)
            + jnp.dot(al, bh, preferred_element_type=F32))


def _ada_kernel(c_ref, w_ref, b_ref, o_ref):
    c = c_ref[...]
    a = c * jax.nn.sigmoid(c)
    o_ref[...] = _dot3(a, w_ref[...]) + b_ref[...]


def _ada(c, w_ada, b_ada):
    B, D = c.shape
    n = w_ada.shape[1] // D
    return pl.pallas_call(
        _ada_kernel,
        grid=(n,),
        in_specs=[pl.BlockSpec((B, D), lambda j: (0, 0)),
                  pl.BlockSpec((D, D), lambda j: (0, j)),
                  pl.BlockSpec((1, D), lambda j: (0, j))],
        out_specs=pl.BlockSpec((B, D), lambda j: (0, j)),
        out_shape=jax.ShapeDtypeStruct((B, n * D), F32),
        compiler_params=_cparams(("arbitrary",)),
        name="ada",
    )(c, w_ada, b_ada.reshape(1, -1))


def _rope_tables(pos_row, inv_col, pass_row, expand):
    ang = inv_col * pos_row
    parts = []
    for v in (jnp.cos(ang), jnp.sin(ang)):
        hi = v.astype(BF16)
        parts += [hi, (v - hi.astype(F32)).astype(BF16)]
    tab_t = jnp.concatenate(parts, axis=0)
    out = lax.dot_general(tab_t, expand, (((0,), (0,)), ((), ())), preferred_element_type=F32)
    return out[:, :LANES] + pass_row, out[:, LANES:2 * LANES], out[:, 2 * LANES:]


def _rope_blk(x, cs, s1, s2):
    return x * cs + pltpu.roll(x, LANES - ROPE_HALF, 1) * s1 + pltpu.roll(x, ROPE_HALF, 1) * s2


def _proj_kernel(x_ref, mod_ref, g_ref, pos_ref, tab_ref, exp_ref, wa_ref, wg_ref, wb_ref,
                 qrawt_ref, qrott_ref, kv16_ref, ks_ref, vst_ref, kw_ref, vwt_ref, gt_ref,
                 *dil_refs_and_scratch):
    n_cfg = len(DIL_CONFIGS)
    qb_refs = dil_refs_and_scratch[0:n_cfg]
    kb_refs = dil_refs_and_scratch[n_cfg:2 * n_cfg]
    vb_refs = dil_refs_and_scratch[2 * n_cfg:3 * n_cfg]
    st_sc = dil_refs_and_scratch[3 * n_cfg]
    tm = x_ref.shape[1]

    def emit_dilated(val, blk, refs):
        st_sc[...] = val
        for (_, d), ref in zip(DIL_CONFIGS, refs):
            for r in range(d):
                piece = val if d == 1 else st_sc[pl.ds(r, tm // d, stride=d), :]
                lo = r * Q_DIL + blk * LANES
                ref[0, :, lo:lo + LANES] = piece.astype(BF16)

    x = x_ref[0]
    sh = mod_ref[0, 0:1, :]
    sc = mod_ref[0, 1:2, :]
    h = (_rms(x, g_ref[...]) * (1.0 + sc) + sh).astype(BF16)
    cs, s1, s2 = _rope_tables(pos_ref[0], tab_ref[0:ROPE_HALF, 0:1], tab_ref[ROPE_HALF:ROPE_HALF + 1, :],
                              exp_ref[...])

    pa = jnp.dot(h, wa_ref[...], preferred_element_type=F32)
    for r in range(Q_NSA // LANES):
        blk = pa[:, r * LANES:(r + 1) * LANES]
        qrawt_ref[0, r * LANES:(r + 1) * LANES, :] = (blk * (SCALE * LOG2E)).T.astype(BF16)
        rot = _rope_blk(blk, cs, s1, s2) * (SCALE * LOG2E)
        qrott_ref[0, r * LANES:(r + 1) * LANES, :] = rot.T.astype(BF16)
    o = Q_NSA
    for t in range(2):
        st_sc[...] = pa[:, o + t * LANES:o + (t + 1) * LANES]
        for j in range(CMP_STRIDE):
            kv16_ref[t, 0, :, j * LANES:(j + 1) * LANES] = st_sc[pl.ds(j, tm // CMP_STRIDE, stride=CMP_STRIDE), :]
    ks_ref[0] = _rope_blk(pa[:, o + 2 * LANES:o + 3 * LANES], cs, s1, s2).astype(BF16)
    vst_ref[0] = pa[:, o + 3 * LANES:o + 4 * LANES].T.astype(BF16)
    kw_ref[0] = _rope_blk(pa[:, o + 4 * LANES:o + 5 * LANES], cs, s1, s2).astype(BF16)
    vwt_ref[0] = pa[:, o + 5 * LANES:o + 6 * LANES].T.astype(BF16)

    gt_ref[0] = jnp.dot(h, wg_ref[...], preferred_element_type=F32)

    pb = jnp.dot(h, wb_ref[...], preferred_element_type=F32)
    for blk in range(Q_DIL // LANES):
        lanes = slice(blk * LANES, (blk + 1) * LANES)
        emit_dilated(_rope_blk(pb[:, lanes], cs, s1, s2) * SCALE, blk, qb_refs)
        emit_dilated(_rope_blk(pb[:, Q_DIL + blk * LANES:Q_DIL + (blk + 1) * LANES], cs, s1, s2), blk, kb_refs)
        emit_dilated(pb[:, 2 * Q_DIL + blk * LANES:2 * Q_DIL + (blk + 1) * LANES], blk, vb_refs)


def _nsa_perm():
    cols = []
    for r in range(GROUP_NSA):
        for hk in range(N_KV_NSA):
            hq = hk * GROUP_NSA + r
            cols.extend(range(hq * HEAD_DIM, (hq + 1) * HEAD_DIM))
    return np.asarray(cols, np.int32)


def _rope_const_tables():
    half = ROPE_HALF
    inv = ROPE_THETA ** (-jnp.arange(half, dtype=F32) / half)
    lane = np.arange(LANES) % HEAD_DIM
    passthrough = (lane >= ROPE_DIM).astype(np.float32)
    tab = jnp.zeros((2 * half, LANES), F32).at[:half, 0].set(inv).at[half].set(passthrough)
    expand = np.zeros((4 * half, 3 * LANES), np.float32)
    for l in range(LANES):
        f = lane[l] % half
        if lane[l] < ROPE_DIM:
            expand[f, l] = expand[half + f, l] = 1.0
        if lane[l] < half:
            expand[2 * half + f, LANES + l] = expand[3 * half + f, LANES + l] = -1.0
        elif lane[l] < ROPE_DIM:
            expand[2 * half + f, 2 * LANES + l] = expand[3 * half + f, 2 * LANES + l] = 1.0
    return tab, jnp.asarray(expand, BF16)


def _proj(x, mod6, g_pre, positions, w_in, tm=512):
    B, S, D = x.shape
    perm = _nsa_perm()
    gate_lo = Q_NSA + 6 * KV_NSA
    n_gate = 3 * N_HEADS_NSA
    w_a = jnp.concatenate([w_in[:, :Q_NSA][:, perm], w_in[:, Q_NSA:gate_lo]], axis=1).astype(BF16)
    w_g = jnp.pad(w_in[:, gate_lo:gate_lo + n_gate], ((0, 0), (0, LANES - n_gate))).astype(BF16)
    w_b = w_in[:, gate_lo + n_gate:].astype(BF16)
    pos = positions.astype(F32).reshape(B, 1, S)
    tab, expand = _rope_const_tables()
    wa_n, wb_n = w_a.shape[1], w_b.shape[1]

    def tok(width, dtype):
        return (pl.BlockSpec((1, tm, width), lambda b, i: (b, i, 0)),
                jax.ShapeDtypeStruct((B, S, width), dtype))

    def tok_t(width, dtype):
        return (pl.BlockSpec((1, width, tm), lambda b, i: (b, 0, i)),
                jax.ShapeDtypeStruct((B, width, S), dtype))

    def dil_view(d):
        return (pl.BlockSpec((1, tm // d, d * Q_DIL), lambda b, i: (b, i, 0)),
                jax.ShapeDtypeStruct((B, S // d, d * Q_DIL), BF16))

    kv16 = (pl.BlockSpec((2, 1, tm // CMP_STRIDE, CMP_STRIDE * KV_NSA), lambda b, i: (0, b, i, 0)),
            jax.ShapeDtypeStruct((2, B, S // CMP_STRIDE, CMP_STRIDE * KV_NSA), F32))
    outs = [tok_t(Q_NSA, BF16), tok_t(Q_NSA, BF16), kv16,
            tok(LANES, BF16), tok_t(LANES, BF16), tok(LANES, BF16), tok_t(LANES, BF16), tok(LANES, F32)]
    outs += [dil_view(d) for _ in range(3) for _, d in DIL_CONFIGS]
    res = pl.pallas_call(
        _proj_kernel,
        grid=(B, S // tm),
        in_specs=[pl.BlockSpec((1, tm, D), lambda b, i: (b, i, 0)),
                  pl.BlockSpec((1, 6, D), lambda b, i: (b, 0, 0)),
                  pl.BlockSpec((1, D), lambda b, i: (0, 0)),
                  pl.BlockSpec((1, 1, tm), lambda b, i: (b, 0, i)),
                  pl.BlockSpec(tab.shape, lambda b, i: (0, 0)),
                  pl.BlockSpec(expand.shape, lambda b, i: (0, 0)),
                  pl.BlockSpec((D, wa_n), lambda b, i: (0, 0)),
                  pl.BlockSpec((D, LANES), lambda b, i: (0, 0)),
                  pl.BlockSpec((D, wb_n), lambda b, i: (0, 0))],
        out_specs=[o[0] for o in outs],
        out_shape=[o[1] for o in outs],
        scratch_shapes=[pltpu.VMEM((tm, LANES), F32)],
        compiler_params=_cparams(("parallel", "parallel")),
        name="proj",
    )(x, mod6, g_pre.reshape(1, D), pos, tab, expand, w_a, w_g, w_b)
    n_cfg = len(DIL_CONFIGS)
    n0 = len(outs) - 3 * n_cfg
    return tuple(res[:n0]) + (res[n0:n0 + n_cfg], res[n0 + n_cfg:n0 + 2 * n_cfg], res[n0 + 2 * n_cfg:])


def _cmpmlp_kernel(a_ref, p_ref, w1_ref, w1x_ref, w2x_ref, o_ref):
    a = a_ref[0, 0].astype(BF16)
    bias = jnp.dot(p_ref[...], w1_ref[0], preferred_element_type=F32)[0:1]
    n16 = a.shape[0]
    out = jnp.zeros((n16, KV_NSA), F32)
    for h in range(N_KV_NSA):
        u = jnp.dot(a, w1x_ref[0, h, 0], preferred_element_type=F32)
        v = jnp.dot(a, w1x_ref[0, h, 1], preferred_element_type=F32)
        hid = jax.nn.gelu(u + pltpu.roll(v, n16 - 1, 0) + bias)
        out = out + jnp.dot(hid.astype(BF16), w2x_ref[0, h], preferred_element_type=F32)
    o_ref[0, 0] = out.astype(BF16)


def _cmpmlp(a, cmp_pos, w_k1, w_k2, w_v1, w_v2):
    _, B, n16, _ = a.shape
    seg = CMP_STRIDE * HEAD_DIM

    def expand1(w):
        w = w.reshape(2, CMP_STRIDE, HEAD_DIM, CMP_HIDDEN)
        z = jnp.zeros_like(w)
        per_head = [jnp.concatenate([w, z] if h == 0 else [z, w], axis=2) for h in range(N_KV_NSA)]
        return jnp.stack(per_head, axis=0).reshape(N_KV_NSA, 2, CMP_STRIDE * KV_NSA, CMP_HIDDEN)

    def expand2(w):
        z = jnp.zeros_like(w)
        return jnp.stack([jnp.concatenate([w, z] if h == 0 else [z, w], axis=1) for h in range(N_KV_NSA)], axis=0)

    w1 = jnp.stack([w_k1, w_v1], axis=0).astype(BF16)
    w1x = jnp.stack([expand1(w_k1), expand1(w_v1)], axis=0).astype(BF16)
    w2x = jnp.stack([expand2(w_k2), expand2(w_v2)], axis=0).astype(BF16)
    p8 = jnp.broadcast_to(cmp_pos.reshape(1, CMP_BLOCK * HEAD_DIM), (8, CMP_BLOCK * HEAD_DIM)).astype(BF16)
    out = pl.pallas_call(
        _cmpmlp_kernel,
        grid=(2, B),
        in_specs=[pl.BlockSpec((1, 1, n16, CMP_STRIDE * KV_NSA), lambda t, b: (t, b, 0, 0)),
                  pl.BlockSpec((8, 2 * seg), lambda t, b: (0, 0)),
                  pl.BlockSpec((1, 2 * seg, CMP_HIDDEN), lambda t, b: (t, 0, 0)),
                  pl.BlockSpec((1, N_KV_NSA, 2, CMP_STRIDE * KV_NSA, CMP_HIDDEN), lambda t, b: (t, 0, 0, 0, 0)),
                  pl.BlockSpec((1, N_KV_NSA, CMP_HIDDEN, KV_NSA), lambda t, b: (t, 0, 0, 0))],
        out_specs=pl.BlockSpec((1, 1, n16, KV_NSA), lambda t, b: (t, b, 0, 0)),
        out_shape=jax.ShapeDtypeStruct((2, B, n16, KV_NSA), BF16),
        compiler_params=_cparams(("parallel", "parallel")),
        name="cmpmlp",
    )(a, p8, w1, w1x, w2x)
    return out[0], out[1]


def _query_groups_t(qt_ref, tq):
    half0 = lax.broadcasted_iota(I32, (LANES, tq), 0) < HEAD_DIM
    zero = jnp.zeros((LANES, tq), BF16)
    groups = []
    for hk in range(N_KV_NSA):
        keep = half0 if hk == 0 else jnp.logical_not(half0)
        for r in range(GROUP_NSA):
            groups.append(jnp.where(keep, qt_ref[0, r * LANES:(r + 1) * LANES, :], zero))
    return groups


def _store_heads_t(ot, tq, o_ref):
    half0 = lax.broadcasted_iota(I32, (tq, LANES), 1) < HEAD_DIM
    for r in range(GROUP_NSA):
        o0 = ot[:, r * tq:(r + 1) * tq].T
        o1 = ot[:, (GROUP_NSA + r) * tq:(GROUP_NSA + r + 1) * tq].T
        o_ref[0, :, r * LANES:(r + 1) * LANES] = jnp.where(half0, o0, o1)


def _cmpsel_kernel(qt_ref, kc_ref, vct_ref, ovt_ref, o_ref, bias_ref, *, tq):
    i = pl.program_id(1)
    n16 = kc_ref.shape[1]
    cols = 8 * tq
    qa = jnp.concatenate(_query_groups_t(qt_ref, tq), axis=1)
    st = jnp.dot(kc_ref[0], qa, preferred_element_type=F32)
    t = i * tq + (lax.broadcasted_iota(I32, (n16, cols), 1) & (tq - 1))
    c = lax.broadcasted_iota(I32, (n16, cols), 0)
    valid = (c * CMP_STRIDE + (CMP_BLOCK - 1)) <= t
    st = jnp.where(valid, st, NEG_INF)
    m = jnp.max(st, axis=0, keepdims=True)
    e = jnp.exp2(st - m)
    l = jnp.sum(e, axis=0, keepdims=True)
    p = jnp.where(valid, e * (1.0 / l), 0.0)
    ot = jnp.dot(vct_ref[0], p.astype(BF16), preferred_element_type=F32)
    _store_heads_t(ot, tq, o_ref)

    n_slc = ovt_ref.shape[0]
    j = lax.broadcasted_iota(I32, (n_slc, tq), 0)
    cur = (i * tq + lax.broadcasted_iota(I32, (n_slc, tq), 1)) >> SLC_SHIFT
    forced = (j == 0) | (j == cur) | (j == cur - 1)
    ovt = ovt_ref[...]
    biases = []
    for hk in range(N_KV_NSA):
        ps = p[:, hk * GROUP_NSA * tq:(hk * GROUP_NSA + 1) * tq]
        for r in range(1, GROUP_NSA):
            ps = ps + p[:, (hk * GROUP_NSA + r) * tq:(hk * GROUP_NSA + r + 1) * tq]
        hi = ps.astype(BF16)
        lo = (ps - hi.astype(F32)).astype(BF16)
        pslc = (jnp.dot(ovt, hi, preferred_element_type=F32)
                + jnp.dot(ovt, lo, preferred_element_type=F32))
        score = jnp.where(forced, FORCE_SCORE, jnp.where(j <= cur, pslc, -1.0))
        rank = jnp.zeros((n_slc, tq), I32)
        for ii in range(n_slc):
            ri = score[ii:ii + 1, :]
            beats = (ri > score) | ((ri == score) & (j > ii))
            rank = rank + beats.astype(I32)
        biases.append(jnp.where(rank < SLC_TOPK, 0.0, NEG_INF))
    bias_ref[0] = jnp.concatenate(biases, axis=0).astype(BF16)


def _overlap_t(S):
    n16 = S // CMP_STRIDE
    n_slc = S // SLC_BLOCK
    cs = np.arange(n16) * CMP_STRIDE
    js = np.arange(n_slc) * SLC_BLOCK
    ov = np.clip(np.minimum(cs[:, None] + CMP_BLOCK, js[None, :] + SLC_BLOCK)
                 - np.maximum(cs[:, None], js[None, :]), 0, None).astype(np.float32) / CMP_BLOCK
    ov[n16 - 1] = 0.0
    return jnp.asarray(ov.T, BF16)


def _cmpsel(q_raw_t, kcf, vcf, tq=256):
    B, _, S = q_raw_t.shape
    n16 = S // CMP_STRIDE
    n_slc = S // SLC_BLOCK
    assert n_slc == HEAD_DIM, "selection bias is laid out as one 64-lane half per kv head"
    return pl.pallas_call(
        functools.partial(_cmpsel_kernel, tq=tq),
        grid=(B, S // tq),
        in_specs=[pl.BlockSpec((1, Q_NSA, tq), lambda b, i: (b, 0, i)),
                  pl.BlockSpec((1, n16, KV_NSA), lambda b, i: (b, 0, 0)),
                  pl.BlockSpec((1, KV_NSA, n16), lambda b, i: (b, 0, 0)),
                  pl.BlockSpec((n_slc, n16), lambda b, i: (0, 0))],
        out_specs=[pl.BlockSpec((1, tq, Q_NSA), lambda b, i: (b, i, 0)),
                   pl.BlockSpec((1, 2 * n_slc, tq), lambda b, i: (b, 0, i))],
        out_shape=[jax.ShapeDtypeStruct((B, S, Q_NSA), F32),
                   jax.ShapeDtypeStruct((B, 2 * n_slc, S), BF16)],
        compiler_params=_cparams(("parallel", "parallel")),
        name="cmpsel",
    )(q_raw_t, kcf, vcf.transpose(0, 2, 1), _overlap_t(S))


def _slc_kernel(qi_ref, kj_ref, qt_ref, bt_ref, k_ref, vt_ref, o_ref, qa_sc, m_sc, acc_sc, *, tq, tk):
    step = pl.program_id(1)
    i = qi_ref[step]
    kj = kj_ref[step]
    last = (i * tq + tq - 1) // tk
    cols = 8 * tq

    @pl.when(kj == 0)
    def _():
        half0 = lax.broadcasted_iota(I32, (LANES, tq), 0) < HEAD_DIM
        zero = jnp.zeros((LANES, tq), BF16)
        bt = bt_ref[0]
        groups = []
        for hk in range(N_KV_NSA):
            keep = half0 if hk == 0 else jnp.logical_not(half0)
            bh = jnp.where(keep, bt, zero)
            for r in range(GROUP_NSA):
                qb = qt_ref[0, r * LANES:(r + 1) * LANES, :]
                groups.append(jnp.concatenate([jnp.where(keep, qb, zero), bh], axis=0))
        qa_sc[...] = jnp.concatenate(groups, axis=1)
        m_sc[...] = jnp.full((1, cols), NEG_INF, F32)
        acc_sc[...] = jnp.zeros(acc_sc.shape, F32)

    def update(on_diagonal):
        kblk = (kj * tk + lax.broadcasted_iota(I32, (tk, LANES), 0)) >> SLC_SHIFT
        lane = lax.broadcasted_iota(I32, (tk, LANES), 1) & (HEAD_DIM - 1)
        onehot = jnp.where(kblk == lane, 1.0, 0.0).astype(BF16)
        kaug = jnp.concatenate([k_ref[0], onehot], axis=1)
        st = jnp.dot(kaug, qa_sc[...], preferred_element_type=F32)
        if on_diagonal:
            kpos = kj * tk + lax.broadcasted_iota(I32, (tk, cols), 0)
            t = i * tq + (lax.broadcasted_iota(I32, (tk, cols), 1) & (tq - 1))
            st = jnp.where(kpos <= t, st, NEG_INF)
        m_old = m_sc[...]
        m_new = jnp.maximum(m_old, jnp.max(st, axis=0, keepdims=True))
        alpha = jnp.exp2(m_old - m_new)
        p = jnp.exp2(st - m_new).astype(BF16)
        vt_ones = jnp.concatenate([vt_ref[0], jnp.ones((DEN_ROWS, tk), BF16)], axis=0)
        acc_sc[...] = alpha * acc_sc[...] + jnp.dot(vt_ones, p, preferred_element_type=F32)
        m_sc[...] = m_new

    @pl.when(kj < last)
    def _():
        update(False)

    @pl.when(kj == last)
    def _():
        update(True)
        acc = acc_sc[...]
        _store_heads_t(acc[:LANES] * (1.0 / acc[LANES:LANES + 1]), tq, o_ref)


def _slc(q_rot_t, bias_t, ks, vs_t, tq=512, tk=512):
    B, _, S = q_rot_t.shape
    assert tk % tq == 0
    qi, kj = [], []
    for i in range(S // tq):
        for j in range((i * tq + tq - 1) // tk + 1):
            qi.append(i)
            kj.append(j)
    grid_spec = pltpu.PrefetchScalarGridSpec(
        num_scalar_prefetch=2,
        grid=(B, len(qi)),
        in_specs=[pl.BlockSpec((1, Q_NSA, tq), lambda b, s, qi, kj: (b, 0, qi[s])),
                  pl.BlockSpec((1, LANES, tq), lambda b, s, qi, kj: (b, 0, qi[s])),
                  pl.BlockSpec((1, tk, KV_NSA), lambda b, s, qi, kj: (b, kj[s], 0)),
                  pl.BlockSpec((1, KV_NSA, tk), lambda b, s, qi, kj: (b, 0, kj[s]))],
        out_specs=pl.BlockSpec((1, tq, Q_NSA), lambda b, s, qi, kj: (b, qi[s], 0)),
        scratch_shapes=[pltpu.VMEM((2 * LANES, 8 * tq), BF16),
                        pltpu.VMEM((1, 8 * tq), F32),
                        pltpu.VMEM((LANES + DEN_ROWS, 8 * tq), F32)],
    )
    return pl.pallas_call(
        functools.partial(_slc_kernel, tq=tq, tk=tk),
        grid_spec=grid_spec,
        out_shape=jax.ShapeDtypeStruct((B, S, Q_NSA), F32),
        compiler_params=_cparams(("parallel", "arbitrary")),
        name="slc",
    )(jnp.asarray(qi, I32), jnp.asarray(kj, I32), q_rot_t, bias_t, ks, vs_t)


def _win_kernel(qt_ref, k0_ref, k1_ref, k2_ref, v0_ref, v1_ref, v2_ref, o_ref, *, tq, tk, max_dist):
    i = pl.program_id(1)
    a = (i * tq) // tk
    cols = 8 * tq
    qa = jnp.concatenate(_query_groups_t(qt_ref, tq), axis=1)
    t = i * tq + (lax.broadcasted_iota(I32, (tk, cols), 1) & (tq - 1))
    row = lax.broadcasted_iota(I32, (tk, cols), 0)
    sts = []
    for jt, k_ref in enumerate((k0_ref, k1_ref, k2_ref)):
        st = jnp.dot(k_ref[0], qa, preferred_element_type=F32)
        kpos = (a - 2 + jt) * tk + row
        if jt == 0:
            st = jnp.where((t - kpos <= max_dist) & (kpos >= 0), st, NEG_INF)
        elif jt == 1:
            st = jnp.where(kpos >= 0, st, NEG_INF)
        else:
            st = jnp.where(kpos <= t, st, NEG_INF)
        sts.append(st)
    m = jnp.max(sts[0], axis=0, keepdims=True)
    for st in sts[1:]:
        m = jnp.maximum(m, jnp.max(st, axis=0, keepdims=True))
    acc = jnp.zeros((LANES + DEN_ROWS, cols), F32)
    for st, v_ref in zip(sts, (v0_ref, v1_ref, v2_ref)):
        vt_ones = jnp.concatenate([v_ref[0], jnp.ones((DEN_ROWS, tk), BF16)], axis=0)
        acc = acc + jnp.dot(vt_ones, jnp.exp2(st - m).astype(BF16), preferred_element_type=F32)
    _store_heads_t(acc[:LANES] * (1.0 / acc[LANES:LANES + 1]), tq, o_ref)


def _win(q_rot_t, kw, vw_t, tq=256, tk=256):
    B, _, S = q_rot_t.shape
    max_dist = WIN_NSA - 1
    assert tk % tq == 0 and 2 * tk == max_dist + 1

    def k_map(jt):
        return lambda b, i: (b, jnp.maximum((i * tq) // tk - 2 + jt, 0), 0)

    def v_map(jt):
        return lambda b, i: (b, 0, jnp.maximum((i * tq) // tk - 2 + jt, 0))

    return pl.pallas_call(
        functools.partial(_win_kernel, tq=tq, tk=tk, max_dist=max_dist),
        grid=(B, S // tq),
        in_specs=[pl.BlockSpec((1, Q_NSA, tq), lambda b, i: (b, 0, i))]
        + [pl.BlockSpec((1, tk, KV_NSA), k_map(jt)) for jt in range(3)]
        + [pl.BlockSpec((1, KV_NSA, tk), v_map(jt)) for jt in range(3)],
        out_specs=pl.BlockSpec((1, tq, Q_NSA), lambda b, i: (b, i, 0)),
        out_shape=jax.ShapeDtypeStruct((B, S, Q_NSA), F32),
        compiler_params=_cparams(("parallel", "parallel")),
        name="win",
    )(q_rot_t, kw, kw, kw, vw_t, vw_t, vw_t)


def _band_kernel(*refs, tq, tk, nt, max_dist):
    q_ref = refs[0]
    k_refs = refs[1:1 + nt]
    v_refs = refs[1 + nt:1 + 2 * nt]
    o_ref = refs[1 + 2 * nt]
    lse_ref = refs[2 + 2 * nt]
    i = pl.program_id(2)
    a = (i * tq) // tk
    lane = lax.broadcasted_iota(I32, (tq, LANES), 1)
    half0 = lane < HEAD_DIM
    row_t = i * tq + (lax.broadcasted_iota(I32, (2 * tq, nt * tk), 0) & (tq - 1))
    kpos = (a - (nt - 1)) * tk + lax.broadcasted_iota(I32, (2 * tq, nt * tk), 1)
    dist = row_t - kpos
    ok = (dist >= 0) & (dist <= max_dist) & (kpos >= 0)
    ones = jnp.ones((nt * tk, LANES), BF16)
    lse_acc = jnp.zeros((tq, LANES), F32)
    for blk in range(Q_DIL // LANES):
        lanes = slice(blk * LANES, (blk + 1) * LANES)
        qb = q_ref[0, :, lanes]
        zero = jnp.zeros_like(qb)
        qs = jnp.concatenate([jnp.where(half0, qb, zero), jnp.where(half0, zero, qb)], axis=0)
        kcat = jnp.concatenate([k_ref[0, :, lanes] for k_ref in k_refs], axis=0)
        vcat = jnp.concatenate([v_ref[0, :, lanes] for v_ref in v_refs], axis=0)
        s = lax.dot_general(qs, kcat, _NT, preferred_element_type=F32)
        s = jnp.where(ok, s, NEG_INF)
        m = jnp.max(s, axis=-1, keepdims=True)
        p = jnp.exp(s - m).astype(BF16)
        ol = jnp.dot(p, jnp.concatenate([vcat, ones], axis=1), preferred_element_type=F32)
        den = ol[:, LANES:]
        o = ol[:, :LANES] * (1.0 / den)
        o_ref[0, :, lanes] = jnp.where(half0, o[:tq], o[tq:])
        lse = m + jnp.log(den[:, 0:1])
        lse_acc = jnp.where(lane == 2 * blk, lse[:tq], lse_acc)
        lse_acc = jnp.where(lane == 2 * blk + 1, lse[tq:], lse_acc)
    lse_ref[0] = lse_acc


def _band(q, k, v, *, dil, max_dist, tq, tk, nt):
    B, m, _ = q.shape

    def kv_map(jt):
        return lambda b, r, i: (b, jnp.maximum((i * tq) // tk - (nt - 1) + jt, 0), r)

    tok_spec = pl.BlockSpec((1, tq, Q_DIL), lambda b, r, i: (b, i, r))
    kv_specs = [pl.BlockSpec((1, tk, Q_DIL), kv_map(jt)) for jt in range(nt)]
    return pl.pallas_call(
        functools.partial(_band_kernel, tq=tq, tk=tk, nt=nt, max_dist=max_dist),
        grid=(B, dil, m // tq),
        in_specs=[tok_spec] + kv_specs + kv_specs,
        out_specs=[tok_spec, pl.BlockSpec((1, tq, LANES), lambda b, r, i: (b, i, r))],
        out_shape=[jax.ShapeDtypeStruct((B, m, dil * Q_DIL), F32),
                   jax.ShapeDtypeStruct((B, m, dil * LANES), F32)],
        compiler_params=_cparams(("parallel", "parallel", "parallel")),
        name=f"band_d{dil}_w{max_dist}",
    )(q, *([k] * nt), *([v] * nt))


def _out_kernel(ocmp_ref, oslc_ref, owin_ref, gt_ref, od1_ref, od4_ref, od16_ref, l1_ref, l4_ref, l16_ref,
                x_ref, mod_ref, eg_ref, ed_ref, gnsa_ref, gdil_ref, wo_ref, gpost_ref, o_ref, st_sc):
    tm = x_ref.shape[1]

    def token_order(ref, d, n_blk):
        if d == 1:
            return ref[0]
        blocks = []
        for blk in range(n_blk):
            for r in range(d):
                lo = (r * n_blk + blk) * LANES
                st_sc[pl.ds(r, tm // d, stride=d), :] = ref[0, :, lo:lo + LANES]
            blocks.append(st_sc[...])
        return blocks[0] if n_blk == 1 else jnp.concatenate(blocks, axis=1)

    sg = jax.nn.sigmoid(gt_ref[0])
    g3 = _expand_dot(sg, eg_ref[...])
    oa = (g3[:, :Q_NSA] * ocmp_ref[0] + g3[:, Q_NSA:2 * Q_NSA] * oslc_ref[0] + g3[:, 2 * Q_NSA:] * owin_ref[0])
    ya = _rms(oa, gnsa_ref[...])

    dils = [d for _, d in DIL_CONFIGS]
    n_blk = Q_DIL // LANES
    l1, l4, l16 = [token_order(ref, d, 1) for ref, d in zip((l1_ref, l4_ref, l16_ref), dils)]
    mx = jnp.maximum(jnp.maximum(l1, l4), l16)
    e1, e4, e16 = jnp.exp(l1 - mx), jnp.exp(l4 - mx), jnp.exp(l16 - mx)
    den = e1 + e4 + e16
    ed = ed_ref[...]
    od1, od4, od16 = [token_order(ref, d, n_blk) for ref, d in zip((od1_ref, od4_ref, od16_ref), dils)]
    inv = 1.0 / den
    ob = (_expand_dot(e1 * inv, ed) * od1 + _expand_dot(e4 * inv, ed) * od4 + _expand_dot(e16 * inv, ed) * od16)
    yb = _rms(ob, gdil_ref[...])

    y = jnp.concatenate([ya, yb], axis=1).astype(BF16)
    z = jnp.dot(y, wo_ref[...], preferred_element_type=F32)
    gt_m = mod_ref[0, 2:3, :]
    o_ref[0] = x_ref[0] + gt_m * _rms(z, gpost_ref[...])


def _gate_expanders():
    perm = _nsa_perm()
    eg = np.zeros((LANES, 3 * Q_NSA), np.float32)
    for lane_out, col in enumerate(perm):
        hq = col // HEAD_DIM
        for c in range(3):
            eg[hq * 3 + c, c * Q_NSA + lane_out] = 1.0
    ed = np.zeros((LANES, Q_DIL), np.float32)
    for h in range(N_HEADS_DIL):
        ed[h, h * HEAD_DIM:(h + 1) * HEAD_DIM] = 1.0
    return jnp.asarray(np.concatenate([eg, eg]), BF16), jnp.asarray(np.concatenate([ed, ed]), BF16)


def _out(o_cmp, o_slc, o_win, gates, ods, lses, x, mod6, g_out_nsa, g_out_dil, w_o, g_post, tm=512):
    B, S, D = x.shape
    perm = _nsa_perm()
    eg, ed = _gate_expanders()
    w_o_p = jnp.concatenate([w_o[:Q_NSA][perm], w_o[Q_NSA:]], axis=0).astype(BF16)
    g_nsa_p = g_out_nsa[perm].reshape(1, Q_NSA)

    def tok(width):
        return pl.BlockSpec((1, tm, width), lambda b, i: (b, i, 0))

    def const(shape):
        return pl.BlockSpec(shape, lambda b, i: (0,) * len(shape))

    def view(d, width):
        return pl.BlockSpec((1, tm // d, d * width), lambda b, i: (b, i, 0))

    dils = [d for _, d in DIL_CONFIGS]
    return pl.pallas_call(
        _out_kernel,
        grid=(B, S // tm),
        in_specs=[tok(Q_NSA), tok(Q_NSA), tok(Q_NSA), tok(LANES)]
        + [view(d, Q_DIL) for d in dils] + [view(d, LANES) for d in dils]
        + [tok(D), pl.BlockSpec((1, 6, D), lambda b, i: (b, 0, 0)),
           const(eg.shape), const(ed.shape), const((1, Q_NSA)), const((1, Q_DIL)),
           const((D, D)), const((1, D))],
        out_specs=tok(D),
        out_shape=jax.ShapeDtypeStruct((B, S, D), F32),
        scratch_shapes=[pltpu.VMEM((tm, LANES), F32)],
        compiler_params=_cparams(("parallel", "parallel")),
        name="out",
    )(o_cmp, o_slc, o_win, gates, *ods, *lses, x, mod6, eg, ed, g_nsa_p, g_out_dil.reshape(1, Q_DIL),
      w_o_p, g_post.reshape(1, D))


def _route_kernel(x_ref, mod_ref, g_ref, wr_ref, br_ref, h_ref, idx_ref, gate_ref, rank_ref, cnt_ref, run_sc):
    i = pl.program_id(0)

    @pl.when(i == 0)
    def _():
        run_sc[...] = jnp.zeros_like(run_sc)

    sh = mod_ref[0, 3:4, :]
    sc = mod_ref[0, 4:5, :]
    h = _rms(x_ref[...], g_ref[...]) * (1.0 + sc) + sh
    h_ref[...] = h
    logits = _dot3(h, wr_ref[...]) + br_ref[...]
    tm = logits.shape[0]
    lane = lax.broadcasted_iota(I32, (tm, LANES), 1)
    work = jnp.where(lane < N_EXPERTS, logits, -jnp.inf)
    idx_out = jnp.zeros((tm, LANES), I32)
    val_out = jnp.full((tm, LANES), -jnp.inf, F32)
    rank_out = jnp.zeros((tm, LANES), F32)
    tri = jnp.where(lax.broadcasted_iota(I32, (tm, tm), 1) < lax.broadcasted_iota(I32, (tm, tm), 0),
                    1.0, 0.0).astype(BF16)
    onehots = []
    for k in range(TOP_K):
        mx = jnp.max(work, axis=-1, keepdims=True)
        ix = jnp.min(jnp.where(work == mx, lane, LANES), axis=-1, keepdims=True)
        hit = lane == ix
        idx_out = jnp.where(lane == k, ix, idx_out)
        val_out = jnp.where(lane == k, mx, val_out)
        work = jnp.where(hit, -jnp.inf, work)
        onehots.append(jnp.where(hit, 1.0, 0.0))
    earlier = jnp.dot(tri, jnp.concatenate(onehots, axis=1).astype(BF16), preferred_element_type=F32)
    base = run_sc[0:1, :]
    for k, onehot in enumerate(onehots):
        before = earlier[:, k * LANES:(k + 1) * LANES] + base
        rank_k = jnp.sum(onehot * before, axis=-1, keepdims=True)
        rank_out = jnp.where(lane == k, rank_k, rank_out)
        base = base + jnp.sum(onehot, axis=0, keepdims=True)
    run_sc[...] = jnp.broadcast_to(base, run_sc.shape)
    e = jnp.exp(val_out - val_out[:, 0:1])
    idx_ref[...] = idx_out
    gate_ref[...] = e / jnp.sum(e, axis=-1, keepdims=True)
    rank_ref[...] = rank_out.astype(I32)
    cnt_ref[...] = run_sc[...].astype(I32)


def _route(x1, mod6, g_pre_ffn, w_router, b_router, tm=512):
    B, S, D = x1.shape
    N = B * S
    spb = S // tm
    wr = jnp.pad(w_router, ((0, 0), (0, LANES - N_EXPERTS)))
    br = jnp.pad(b_router, (0, LANES - N_EXPERTS)).reshape(1, LANES)

    def tok(width):
        return pl.BlockSpec((tm, width), lambda i: (i, 0))

    return pl.pallas_call(
        _route_kernel,
        grid=(N // tm,),
        in_specs=[tok(D), pl.BlockSpec((1, 6, D), lambda i: (i // spb, 0, 0)),
                  pl.BlockSpec((1, D), lambda i: (0, 0)),
                  pl.BlockSpec((D, LANES), lambda i: (0, 0)),
                  pl.BlockSpec((1, LANES), lambda i: (0, 0))],
        out_specs=[tok(D), tok(LANES), tok(LANES), tok(LANES), pl.BlockSpec((8, LANES), lambda i: (0, 0))],
        out_shape=[jax.ShapeDtypeStruct((N, D), F32),
                   jax.ShapeDtypeStruct((N, LANES), I32),
                   jax.ShapeDtypeStruct((N, LANES), F32),
                   jax.ShapeDtypeStruct((N, LANES), I32),
                   jax.ShapeDtypeStruct((8, LANES), I32)],
        scratch_shapes=[pltpu.VMEM((8, LANES), F32)],
        compiler_params=_cparams(("arbitrary",)),
        name="route",
    )(x1.reshape(N, D), mod6, g_pre_ffn.reshape(1, D), wr, br)


def _dispatch_kernel(pos_ref, pad0_ref, padn_ref, nt_ref, h_ref, xs_out, zbuf, sem, zsem, *, tm, tmx, n_tiles):
    i = pl.program_id(0)
    bits = tmx.bit_length() - 1

    def pad_copies(e, fn):
        p0 = pad0_ref[e]
        head = (-p0) & (SUBLANES - 1)
        head = jnp.minimum(head, padn_ref[e])
        for r in range(SUBLANES - 1):
            @pl.when(r < head)
            def _():
                fn(pltpu.make_async_copy(zbuf.at[pl.ds(0, 1), :], xs_out.at[pl.ds(p0 + r, 1), :], zsem))
        a = p0 + head
        n = padn_ref[e] - head
        for b in range(3, bits):
            size = 1 << b
            off = (n >> (b + 1)) << (b + 1)

            @pl.when((n & size) != 0)
            def _():
                fn(pltpu.make_async_copy(zbuf.at[pl.ds(0, size), :],
                                         xs_out.at[pl.ds(pl.multiple_of(a + off, SUBLANES), size), :], zsem))

    def tail_copy(t, fn):
        fn(pltpu.make_async_copy(zbuf, xs_out.at[pl.ds(pl.multiple_of(t * tmx, tmx), tmx), :], zsem))

    def for_all_fill(fn):
        def per_expert(e, carry):
            pad_copies(e, fn)
            return carry
        lax.fori_loop(0, N_EXPERTS, per_expert, 0)

        def per_tile(t, carry):
            tail_copy(t, fn)
            return carry
        lax.fori_loop(nt_ref[0], n_tiles, per_tile, 0)

    @pl.when(i == 0)
    def _():
        zbuf[...] = jnp.zeros_like(zbuf)
        for_all_fill(lambda cp: cp.start())
        for_all_fill(lambda cp: cp.wait())

    def body(r, carry):
        for k in range(TOP_K):
            dst_row = pos_ref[(i * tm + r) * TOP_K + k]
            pltpu.make_async_copy(h_ref.at[pl.ds(r, 1), :], xs_out.at[pl.ds(dst_row, 1), :], sem).start()
        return carry

    lax.fori_loop(0, tm, body, 0, unroll=DMA_UNROLL // TOP_K)
    for k in range(TOP_K):
        pltpu.make_async_copy(h_ref, xs_out.at[pl.ds(0, tm), :], sem).wait()


def _dispatch(h2, pos_flat, pad_start, pad_len, n_valid, n_tiles, tmx, tm=512):
    N, D = h2.shape
    assert tmx & (tmx - 1) == 0
    grid_spec = pltpu.PrefetchScalarGridSpec(
        num_scalar_prefetch=4,
        grid=(N // tm,),
        in_specs=[pl.BlockSpec((tm, D), lambda i, *_: (i, 0))],
        out_specs=pl.BlockSpec(memory_space=pl.ANY),
        scratch_shapes=[pltpu.VMEM((tmx, D), F32), pltpu.SemaphoreType.DMA(()), pltpu.SemaphoreType.DMA(())],
    )
    return pl.pallas_call(
        functools.partial(_dispatch_kernel, tm=tm, tmx=tmx, n_tiles=n_tiles),
        grid_spec=grid_spec,
        out_shape=jax.ShapeDtypeStruct((n_tiles * tmx, D), F32),
        compiler_params=_cparams(("arbitrary",)),
        name="dispatch",
    )(pos_flat, pad_start, pad_len, n_valid, h2)


def _moe_kernel(te_ref, nt_ref, nx_ref, sl_ref, xs_ref, wup_hbm, bup_ref, wdn_hbm, bdn_ref, pm_ref, y_ref,
                wup_in, wdn_in, wup_sc, wdn_sc, sems):
    i = pl.program_id(0)
    n_valid = nt_ref[0]
    e = te_ref[i]
    new_expert = (i == 0) | (e != te_ref[jnp.maximum(i - 1, 0)])
    slot = sl_ref[i]
    n_blk = wup_sc.shape[1] // (2 * LANES)

    def weight_copies(ex, s):
        return (pltpu.make_async_copy(wup_hbm.at[ex], wup_in.at[s], sems.at[0, s]),
                pltpu.make_async_copy(wdn_hbm.at[ex], wdn_in.at[s], sems.at[1, s]))

    @pl.when(i == 0)
    def _():
        for cp in weight_copies(e, slot):
            cp.start()

    @pl.when(new_expert & (i < n_valid))
    def _():
        for cp in weight_copies(e, slot):
            cp.wait()
        nx = nx_ref[i]

        @pl.when(nx >= 0)
        def _():
            for cp in weight_copies(nx, 1 - slot):
                cp.start()

        pm = pm_ref[...]
        for blk in range(n_blk):
            cols = slice(blk * 2 * LANES, (blk + 1) * 2 * LANES)
            w = wup_in[slot, :, cols].astype(BF16)
            wup_sc[:, cols] = jnp.dot(w, pm, preferred_element_type=F32).astype(BF16)
        wdn_sc[...] = wdn_in[slot].astype(BF16)

    @pl.when(i < n_valid)
    def _():
        xs = xs_ref[...].astype(BF16)
        u = jnp.dot(xs, wup_sc[...], preferred_element_type=F32) + bup_ref[0]
        acts = []
        for blk in range(n_blk):
            ug = jnp.minimum(u[:, blk * 2 * LANES:blk * 2 * LANES + LANES], SWIGLU_LIMIT)
            ul = jnp.clip(u[:, blk * 2 * LANES + LANES:(blk + 1) * 2 * LANES], -SWIGLU_LIMIT, SWIGLU_LIMIT)
            acts.append((ug * jax.nn.sigmoid(SWIGLU_ALPHA * ug) * (ul + 1.0)).astype(BF16))
        act = jnp.concatenate(acts, axis=1)
        y_ref[...] = jnp.dot(act, wdn_sc[...], preferred_element_type=F32) + bdn_ref[0]

    @pl.when(i >= n_valid)
    def _():
        y_ref[...] = jnp.zeros_like(y_ref)


def _moe_layout(top_idx, rank, counts, tm):
    N = top_idx.shape[0]
    n_tiles = N * TOP_K // tm + N_EXPERTS
    e_ids = jnp.arange(N_EXPERTS, dtype=I32)
    tiles_e = (counts + tm - 1) // tm
    tile_end = jnp.sum(jnp.where(e_ids[None, :] <= e_ids[:, None], tiles_e[None, :], 0), axis=1)
    start = (tile_end - tiles_e) * tm
    pos = rank + jnp.sum(jnp.where(top_idx[:, :, None] == e_ids[None, None, :], start[None, None, :], 0), axis=-1)
    n_valid = tile_end[N_EXPERTS - 1]
    tile_ids = jnp.arange(n_tiles, dtype=I32)
    tile_e = jnp.sum((jnp.minimum(tile_ids, n_valid - 1)[:, None] >= tile_end[None, :]).astype(I32), axis=1)
    prev_e = jnp.concatenate([jnp.full((1,), -1, I32), tile_e[:-1]])
    new = (tile_e != prev_e) & (tile_ids < n_valid)
    ordinal = jnp.sum(jnp.where(tile_ids[None, :] <= tile_ids[:, None], new[None, :].astype(I32), 0), axis=1) - 1
    next_first = jnp.sum(jnp.where(tile_e[:, None] == e_ids[None, :], tile_end[None, :], 0), axis=1)
    next_e = jnp.sum(jnp.where(next_first[:, None] == tile_ids[None, :], tile_e[None, :], 0), axis=1)
    next_e = jnp.where(next_first < n_valid, next_e, -1)
    meta = dict(tile_e=tile_e.astype(I32), n_valid=n_valid.astype(I32).reshape(1), next_e=next_e.astype(I32),
                slot=(ordinal & 1).astype(I32), pad_start=(start + counts).astype(I32),
                pad_len=(tiles_e * tm - counts).astype(I32))
    return pos.astype(I32), meta, n_tiles


def _glu_perm():
    pm = np.zeros((2 * LANES, 2 * LANES), np.float32)
    for j in range(LANES):
        pm[2 * j, j] = 1.0
        pm[2 * j + 1, LANES + j] = 1.0
    return jnp.asarray(pm, BF16)


def _moe(xs, meta, w_up, b_up, w_down, b_down, tm):
    P, D = xs.shape
    n_blk = D_FF // LANES
    b_up_p = b_up.reshape(N_EXPERTS, n_blk, LANES, 2).transpose(0, 1, 3, 2).reshape(N_EXPERTS, 1, 2 * D_FF)
    b_dn = b_down.reshape(N_EXPERTS, 1, D)

    def bmap(i, te, *_):
        return (te[i], 0, 0)

    grid_spec = pltpu.PrefetchScalarGridSpec(
        num_scalar_prefetch=4,
        grid=(P // tm,),
        in_specs=[pl.BlockSpec((tm, D), lambda i, *_: (i, 0)),
                  pl.BlockSpec(memory_space=pl.ANY), pl.BlockSpec((1, 1, 2 * D_FF), bmap),
                  pl.BlockSpec(memory_space=pl.ANY), pl.BlockSpec((1, 1, D), bmap),
                  pl.BlockSpec((2 * LANES, 2 * LANES), lambda i, *_: (0, 0))],
        out_specs=pl.BlockSpec((tm, D), lambda i, *_: (i, 0)),
        scratch_shapes=[pltpu.VMEM((2, D, 2 * D_FF), F32), pltpu.VMEM((2, D_FF, D), F32),
                        pltpu.VMEM((D, 2 * D_FF), BF16), pltpu.VMEM((D_FF, D), BF16),
                        pltpu.SemaphoreType.DMA((2, 2))],
    )
    return pl.pallas_call(
        _moe_kernel,
        grid_spec=grid_spec,
        out_shape=jax.ShapeDtypeStruct((P, D), F32),
        compiler_params=_cparams(("arbitrary",)),
        name="moe",
    )(meta["tile_e"], meta["n_valid"], meta["next_e"], meta["slot"], xs, w_up, b_up_p, w_down, b_dn, _glu_perm())


def _final_kernel(pos_ref, ys_hbm, x_ref, gate_ref, gtf_ref, g_ref, o_ref, buf, sems, *, tm, n_steps):
    i = pl.program_id(0)
    slot = i & 1

    n = TOP_K * tm

    def gather(step, dst, sem):
        def body(j, carry):
            pltpu.make_async_copy(ys_hbm.at[pl.ds(pos_ref[step * n + j], 1), :], dst.at[pl.ds(j, 1), :],
                                  sem).start()
            return carry
        lax.fori_loop(0, n, body, 0, unroll=DMA_UNROLL)

    @pl.when(i == 0)
    def _():
        gather(0, buf.at[0], sems.at[0])

    @pl.when(i + 1 < n_steps)
    def _():
        gather(i + 1, buf.at[1 - slot], sems.at[1 - slot])

    pltpu.make_async_copy(ys_hbm.at[pl.ds(0, n), :], buf.at[slot], sems.at[slot]).wait()
    gate = gate_ref[...]
    y = gate[:, 0:1] * buf[slot, 0:tm, :]
    for k in range(1, TOP_K):
        y = y + gate[:, k:k + 1] * buf[slot, k * tm:(k + 1) * tm, :]
    o_ref[...] = x_ref[...] + gtf_ref[0, 5:6, :] * _rms(y, g_ref[...])


def _final(ys, pos_flat, gate, x1, mod6, g_post_ffn, tm=512):
    B, S, D = x1.shape
    N = B * S
    n_steps = N // tm
    steps_per_b = S // tm
    pos_sm = pos_flat.reshape(n_steps, tm, TOP_K).transpose(0, 2, 1).reshape(N * TOP_K)
    grid_spec = pltpu.PrefetchScalarGridSpec(
        num_scalar_prefetch=1,
        grid=(n_steps,),
        in_specs=[pl.BlockSpec(memory_space=pl.ANY),
                  pl.BlockSpec((tm, D), lambda i, p: (i, 0)),
                  pl.BlockSpec((tm, LANES), lambda i, p: (i, 0)),
                  pl.BlockSpec((1, 6, D), lambda i, p: (i // steps_per_b, 0, 0)),
                  pl.BlockSpec((1, D), lambda i, p: (0, 0))],
        out_specs=pl.BlockSpec((tm, D), lambda i, p: (i, 0)),
        scratch_shapes=[pltpu.VMEM((2, TOP_K * tm, D), F32), pltpu.SemaphoreType.DMA((2,))],
    )
    out = pl.pallas_call(
        functools.partial(_final_kernel, tm=tm, n_steps=n_steps),
        grid_spec=grid_spec,
        out_shape=jax.ShapeDtypeStruct((N, D), F32),
        compiler_params=_cparams(("arbitrary",)),
        name="final",
    )(pos_sm, ys, x1.reshape(N, D), gate, mod6, g_post_ffn.reshape(1, D))
    return out.reshape(B, S, D)


def _layer(x, c, positions, w_ada, b_ada, g_pre_mix, g_post_mix, g_pre_ffn, g_post_ffn,
           w_in, cmp_pos, w_cmp_k1, w_cmp_k2, w_cmp_v1, w_cmp_v2, g_out_nsa, g_out_dil, w_o,
           w_router, b_router, w_up, b_up, w_down, b_down):
    B, S, D = x.shape
    mod6 = _ada(c, w_ada, b_ada).reshape(B, 6, D)
    (q_raw_t, q_rot_t, kv16, ks, vs_t, kw, vw_t, gates, qbs, kbs, vbs) = _proj(x, mod6, g_pre_mix, positions, w_in)
    kcf, vcf = _cmpmlp(kv16, cmp_pos, w_cmp_k1, w_cmp_k2, w_cmp_v1, w_cmp_v2)
    o_cmp, bias_t = _cmpsel(q_raw_t, kcf, vcf)
    o_slc = _slc(q_rot_t, bias_t, ks, vs_t)
    o_win = _win(q_rot_t, kw, vw_t)
    ods, lses = [], []
    for (window, dil), qb, kb, vb in zip(DIL_CONFIGS, qbs, kbs, vbs):
        o, lse = _band(qb, kb, vb, dil=dil, max_dist=window // dil, tq=128, tk=128, nt=2)
        ods.append(o)
        lses.append(lse)
    x1 = _out(o_cmp, o_slc, o_win, gates, ods, lses, x, mod6, g_out_nsa, g_out_dil, w_o, g_post_mix)
    h2, top_idx, gate, rank, counts = _route(x1, mod6, g_pre_ffn, w_router, b_router)
    tm_moe = 512
    pos, meta, n_tiles = _moe_layout(top_idx[:, :TOP_K], rank[:, :TOP_K], counts[0, :N_EXPERTS], tm_moe)
    pos_flat = pos.reshape(B * S * TOP_K)
    xs = _dispatch(h2, pos_flat, meta["pad_start"], meta["pad_len"], meta["n_valid"], n_tiles, tm_moe)
    ys = _moe(xs, meta, w_up, b_up, w_down, b_down, tm_moe)
    return _final(ys, pos_flat, gate, x1, mod6, g_post_ffn)


def kernel(x, c, positions, w_ada, b_ada, g_pre_mix, g_post_mix, g_pre_ffn, g_post_ffn, w_in, cmp_pos,
           w_cmp_k1, w_cmp_k2, w_cmp_v1, w_cmp_v2, g_out_nsa, g_out_dil, w_o, w_router, b_router,
           w_up, b_up, w_down, b_down):
    depth = w_ada.shape[0]
    for l in range(depth):
        x = _layer(x, c, positions, w_ada[l], b_ada[l], g_pre_mix[l], g_post_mix[l], g_pre_ffn[l],
                   g_post_ffn[l], w_in[l], cmp_pos[l], w_cmp_k1[l], w_cmp_k2[l], w_cmp_v1[l], w_cmp_v2[l],
                   g_out_nsa[l], g_out_dil[l], w_o[l], w_router[l], b_router[l], w_up[l], b_up[l],
                   w_down[l], b_down[l])
    return x
```

```python
import functools

import numpy as np
import jax
import jax.numpy as jnp
from jax import lax
from jax.experimental import pallas as pl
from jax.experimental.pallas import tpu as pltpu

F32 = jnp.float32
BF16 = jnp.bfloat16
I32 = jnp.int32

D_MODEL = 1024
HEAD_DIM = 64
N_HEADS_NSA = 8
N_KV_NSA = 2
GROUP_NSA = 4
N_HEADS_DIL = 8
ROPE_THETA = 500000.0
ROPE_DIM = 16
ROPE_HALF = 8
CMP_BLOCK = 32
CMP_STRIDE = 16
CMP_HIDDEN = 256
SLC_BLOCK = 64
SLC_SHIFT = 6
SLC_TOPK = 16
WIN_NSA = 512
DIL_CONFIGS = ((128, 1), (512, 4), (2048, 16))
N_EXPERTS = 32
TOP_K = 4
D_FF = 1024
SWIGLU_LIMIT = 7.0
SWIGLU_ALPHA = 1.702
RMS_EPS = 1e-6
NEG_INF = -1e30
FORCE_SCORE = 1e9
SCALE = HEAD_DIM ** -0.5
LOG2E = 1.4426950408889634
DEN_ROWS = 16

Q_NSA = 512
KV_NSA = 128
Q_DIL = 512
LANES = 128
SUBLANES = 8
VMEM_LIMIT = 56 * 1024 * 1024
DMA_UNROLL = 32

_NT = (((1,), (1,)), ((), ()))


def _cparams(sem):
    return pltpu.CompilerParams(dimension_semantics=sem, vmem_limit_bytes=VMEM_LIMIT)


def _rms(x, g):
    return x * lax.rsqrt(jnp.mean(x * x, axis=-1, keepdims=True) + RMS_EPS) * g


def _hi_lo(a):
    hi = a.astype(BF16)
    return hi, (a - hi.astype(F32)).astype(BF16)


def _expand_dot(a, e2):
    return jnp.dot(jnp.concatenate(_hi_lo(a), axis=1), e2, preferred_element_type=F32)


def _dot3(a, b):
    ah, al = _hi_lo(a)
    bh, bl = _hi_lo(b)
    return (jnp.dot(ah, bh, preferred_element_type=F32) + jnp.dot(ah, bl, preferred_element_type=F32)
            + jnp.dot(al, bh, preferred_element_type=F32))


def _ada_kernel(c_ref, w_ref, b_ref, o_ref):
    c = c_ref[...]
    a = c * jax.nn.sigmoid(c)
    o_ref[...] = _dot3(a, w_ref[...]) + b_ref[...]


def _ada(c, w_ada, b_ada):
    B, D = c.shape
    n = w_ada.shape[1] // D
    return pl.pallas_call(
        _ada_kernel,
        grid=(n,),
        in_specs=[pl.BlockSpec((B, D), lambda j: (0, 0)),
                  pl.BlockSpec((D, D), lambda j: (0, j)),
                  pl.BlockSpec((1, D), lambda j: (0, j))],
        out_specs=pl.BlockSpec((B, D), lambda j: (0, j)),
        out_shape=jax.ShapeDtypeStruct((B, n * D), F32),
        compiler_params=_cparams(("arbitrary",)),
        name="ada",
    )(c, w_ada, b_ada.reshape(1, -1))


def _rope_tables(pos_row, inv_col, pass_row, expand):
    ang = inv_col * pos_row
    parts = []
    for v in (jnp.cos(ang), jnp.sin(ang)):
        hi = v.astype(BF16)
        parts += [hi, (v - hi.astype(F32)).astype(BF16)]
    tab_t = jnp.concatenate(parts, axis=0)
    out = lax.dot_general(tab_t, expand, (((0,), (0,)), ((), ())), preferred_element_type=F32)
    return out[:, :LANES] + pass_row, out[:, LANES:2 * LANES], out[:, 2 * LANES:]


def _rope_blk(x, cs, s1, s2):
    return x * cs + pltpu.roll(x, LANES - ROPE_HALF, 1) * s1 + pltpu.roll(x, ROPE_HALF, 1) * s2


def _proj_kernel(x_ref, mod_ref, g_ref, pos_ref, tab_ref, exp_ref, wa_ref, wg_ref, wb_ref,
                 qrawt_ref, qrott_ref, kv16_ref, ks_ref, vst_ref, kw_ref, vwt_ref, gt_ref,
                 *dil_refs_and_scratch):
    n_cfg = len(DIL_CONFIGS)
    qb_refs = dil_refs_and_scratch[0:n_cfg]
    kb_refs = dil_refs_and_scratch[n_cfg:2 * n_cfg]
    vb_refs = dil_refs_and_scratch[2 * n_cfg:3 * n_cfg]
    st_sc = dil_refs_and_scratch[3 * n_cfg]
    tm = x_ref.shape[1]

    def emit_dilated(val, blk, refs):
        st_sc[...] = val
        for (_, d), ref in zip(DIL_CONFIGS, refs):
            for r in range(d):
                piece = val if d == 1 else st_sc[pl.ds(r, tm // d, stride=d), :]
                lo = r * Q_DIL + blk * LANES
                ref[0, :, lo:lo + LANES] = piece.astype(BF16)

    x = x_ref[0]
    sh = mod_ref[0, 0:1, :]
    sc = mod_ref[0, 1:2, :]
    h = (_rms(x, g_ref[...]) * (1.0 + sc) + sh).astype(BF16)
    cs, s1, s2 = _rope_tables(pos_ref[0], tab_ref[0:ROPE_HALF, 0:1], tab_ref[ROPE_HALF:ROPE_HALF + 1, :],
                              exp_ref[...])

    pa = jnp.dot(h, wa_ref[...], preferred_element_type=F32)
    for r in range(Q_NSA // LANES):
        blk = pa[:, r * LANES:(r + 1) * LANES]
        qrawt_ref[0, r * LANES:(r + 1) * LANES, :] = (blk * (SCALE * LOG2E)).T.astype(BF16)
        rot = _rope_blk(blk, cs, s1, s2) * (SCALE * LOG2E)
        qrott_ref[0, r * LANES:(r + 1) * LANES, :] = rot.T.astype(BF16)
    o = Q_NSA
    for t in range(2):
        st_sc[...] = pa[:, o + t * LANES:o + (t + 1) * LANES]
        for j in range(CMP_STRIDE):
            kv16_ref[t, 0, :, j * LANES:(j + 1) * LANES] = st_sc[pl.ds(j, tm // CMP_STRIDE, stride=CMP_STRIDE), :]
    ks_ref[0] = _rope_blk(pa[:, o + 2 * LANES:o + 3 * LANES], cs, s1, s2).astype(BF16)
    vst_ref[0] = pa[:, o + 3 * LANES:o + 4 * LANES].T.astype(BF16)
    kw_ref[0] = _rope_blk(pa[:, o + 4 * LANES:o + 5 * LANES], cs, s1, s2).astype(BF16)
    vwt_ref[0] = pa[:, o + 5 * LANES:o + 6 * LANES].T.astype(BF16)

    gt_ref[0] = jnp.dot(h, wg_ref[...], preferred_element_type=F32)

    pb = jnp.dot(h, wb_ref[...], preferred_element_type=F32)
    for blk in range(Q_DIL // LANES):
        lanes = slice(blk * LANES, (blk + 1) * LANES)
        emit_dilated(_rope_blk(pb[:, lanes], cs, s1, s2) * SCALE, blk, qb_refs)
        emit_dilated(_rope_blk(pb[:, Q_DIL + blk * LANES:Q_DIL + (blk + 1) * LANES], cs, s1, s2), blk, kb_refs)
        emit_dilated(pb[:, 2 * Q_DIL + blk * LANES:2 * Q_DIL + (blk + 1) * LANES], blk, vb_refs)


def _nsa_perm():
    cols = []
    for r in range(GROUP_NSA):
        for hk in range(N_KV_NSA):
            hq = hk * GROUP_NSA + r
            cols.extend(range(hq * HEAD_DIM, (hq + 1) * HEAD_DIM))
    return np.asarray(cols, np.int32)


def _rope_const_tables():
    half = ROPE_HALF
    inv = ROPE_THETA ** (-jnp.arange(half, dtype=F32) / half)
    lane = np.arange(LANES) % HEAD_DIM
    passthrough = (lane >= ROPE_DIM).astype(np.float32)
    tab = jnp.zeros((2 * half, LANES), F32).at[:half, 0].set(inv).at[half].set(passthrough)
    expand = np.zeros((4 * half, 3 * LANES), np.float32)
    for l in range(LANES):
        f = lane[l] % half
        if lane[l] < ROPE_DIM:
            expand[f, l] = expand[half + f, l] = 1.0
        if lane[l] < half:
            expand[2 * half + f, LANES + l] = expand[3 * half + f, LANES + l] = -1.0
        elif lane[l] < ROPE_DIM:
            expand[2 * half + f, 2 * LANES + l] = expand[3 * half + f, 2 * LANES + l] = 1.0
    return tab, jnp.asarray(expand, BF16)


def _proj(x, mod6, g_pre, positions, w_in, tm=512):
    B, S, D = x.shape
    perm = _nsa_perm()
    gate_lo = Q_NSA + 6 * KV_NSA
    n_gate = 3 * N_HEADS_NSA
    w_a = jnp.concatenate([w_in[:, :Q_NSA][:, perm], w_in[:, Q_NSA:gate_lo]], axis=1).astype(BF16)
    w_g = jnp.pad(w_in[:, gate_lo:gate_lo + n_gate], ((0, 0), (0, LANES - n_gate))).astype(BF16)
    w_b = w_in[:, gate_lo + n_gate:].astype(BF16)
    pos = positions.astype(F32).reshape(B, 1, S)
    tab, expand = _rope_const_tables()
    wa_n, wb_n = w_a.shape[1], w_b.shape[1]

    def tok(width, dtype):
        return (pl.BlockSpec((1, tm, width), lambda b, i: (b, i, 0)),
                jax.ShapeDtypeStruct((B, S, width), dtype))

    def tok_t(width, dtype):
        return (pl.BlockSpec((1, width, tm), lambda b, i: (b, 0, i)),
                jax.ShapeDtypeStruct((B, width, S), dtype))

    def dil_view(d):
        return (pl.BlockSpec((1, tm // d, d * Q_DIL), lambda b, i: (b, i, 0)),
                jax.ShapeDtypeStruct((B, S // d, d * Q_DIL), BF16))

    kv16 = (pl.BlockSpec((2, 1, tm // CMP_STRIDE, CMP_STRIDE * KV_NSA), lambda b, i: (0, b, i, 0)),
            jax.ShapeDtypeStruct((2, B, S // CMP_STRIDE, CMP_STRIDE * KV_NSA), F32))
    outs = [tok_t(Q_NSA, BF16), tok_t(Q_NSA, BF16), kv16,
            tok(LANES, BF16), tok_t(LANES, BF16), tok(LANES, BF16), tok_t(LANES, BF16), tok(LANES, F32)]
    outs += [dil_view(d) for _ in range(3) for _, d in DIL_CONFIGS]
    res = pl.pallas_call(
        _proj_kernel,
        grid=(B, S // tm),
        in_specs=[pl.BlockSpec((1, tm, D), lambda b, i: (b, i, 0)),
                  pl.BlockSpec((1, 6, D), lambda b, i: (b, 0, 0)),
                  pl.BlockSpec((1, D), lambda b, i: (0, 0)),
                  pl.BlockSpec((1, 1, tm), lambda b, i: (b, 0, i)),
                  pl.BlockSpec(tab.shape, lambda b, i: (0, 0)),
                  pl.BlockSpec(expand.shape, lambda b, i: (0, 0)),
                  pl.BlockSpec((D, wa_n), lambda b, i: (0, 0)),
                  pl.BlockSpec((D, LANES), lambda b, i: (0, 0)),
                  pl.BlockSpec((D, wb_n), lambda b, i: (0, 0))],
        out_specs=[o[0] for o in outs],
        out_shape=[o[1] for o in outs],
        scratch_shapes=[pltpu.VMEM((tm, LANES), F32)],
        compiler_params=_cparams(("parallel", "parallel")),
        name="proj",
    )(x, mod6, g_pre.reshape(1, D), pos, tab, expand, w_a, w_g, w_b)
    n_cfg = len(DIL_CONFIGS)
    n0 = len(outs) - 3 * n_cfg
    return tuple(res[:n0]) + (res[n0:n0 + n_cfg], res[n0 + n_cfg:n0 + 2 * n_cfg], res[n0 + 2 * n_cfg:])


def _cmpmlp_kernel(a_ref, p_ref, w1_ref, w1x_ref, w2x_ref, o_ref):
    a = a_ref[0, 0].astype(BF16)
    bias = jnp.dot(p_ref[...], w1_ref[0], preferred_element_type=F32)[0:1]
    n16 = a.shape[0]
    out = jnp.zeros((n16, KV_NSA), F32)
    for h in range(N_KV_NSA):
        u = jnp.dot(a, w1x_ref[0, h, 0], preferred_element_type=F32)
        v = jnp.dot(a, w1x_ref[0, h, 1], preferred_element_type=F32)
        hid = jax.nn.gelu(u + pltpu.roll(v, n16 - 1, 0) + bias)
        out = out + jnp.dot(hid.astype(BF16), w2x_ref[0, h], preferred_element_type=F32)
    o_ref[0, 0] = out.astype(BF16)


def _cmpmlp(a, cmp_pos, w_k1, w_k2, w_v1, w_v2):
    _, B, n16, _ = a.shape
    seg = CMP_STRIDE * HEAD_DIM

    def expand1(w):
        w = w.reshape(2, CMP_STRIDE, HEAD_DIM, CMP_HIDDEN)
        z = jnp.zeros_like(w)
        per_head = [jnp.concatenate([w, z] if h == 0 else [z, w], axis=2) for h in range(N_KV_NSA)]
        return jnp.stack(per_head, axis=0).reshape(N_KV_NSA, 2, CMP_STRIDE * KV_NSA, CMP_HIDDEN)

    def expand2(w):
        z = jnp.zeros_like(w)
        return jnp.stack([jnp.concatenate([w, z] if h == 0 else [z, w], axis=1) for h in range(N_KV_NSA)], axis=0)

    w1 = jnp.stack([w_k1, w_v1], axis=0).astype(BF16)
    w1x = jnp.stack([expand1(w_k1), expand1(w_v1)], axis=0).astype(BF16)
    w2x = jnp.stack([expand2(w_k2), expand2(w_v2)], axis=0).astype(BF16)
    p8 = jnp.broadcast_to(cmp_pos.reshape(1, CMP_BLOCK * HEAD_DIM), (8, CMP_BLOCK * HEAD_DIM)).astype(BF16)
    out = pl.pallas_call(
        _cmpmlp_kernel,
        grid=(2, B),
        in_specs=[pl.BlockSpec((1, 1, n16, CMP_STRIDE * KV_NSA), lambda t, b: (t, b, 0, 0)),
                  pl.BlockSpec((8, 2 * seg), lambda t, b: (0, 0)),
                  pl.BlockSpec((1, 2 * seg, CMP_HIDDEN), lambda t, b: (t, 0, 0)),
                  pl.BlockSpec((1, N_KV_NSA, 2, CMP_STRIDE * KV_NSA, CMP_HIDDEN), lambda t, b: (t, 0, 0, 0, 0)),
                  pl.BlockSpec((1, N_KV_NSA, CMP_HIDDEN, KV_NSA), lambda t, b: (t, 0, 0, 0))],
        out_specs=pl.BlockSpec((1, 1, n16, KV_NSA), lambda t, b: (t, b, 0, 0)),
        out_shape=jax.ShapeDtypeStruct((2, B, n16, KV_NSA), BF16),
        compiler_params=_cparams(("parallel", "parallel")),
        name="cmpmlp",
    )(a, p8, w1, w1x, w2x)
    return out[0], out[1]


def _query_groups_t(qt_ref, tq):
    half0 = lax.broadcasted_iota(I32, (LANES, tq), 0) < HEAD_DIM
    zero = jnp.zeros((LANES, tq), BF16)
    groups = []
    for hk in range(N_KV_NSA):
        keep = half0 if hk == 0 else jnp.logical_not(half0)
        for r in range(GROUP_NSA):
            groups.append(jnp.where(keep, qt_ref[0, r * LANES:(r + 1) * LANES, :], zero))
    return groups


def _store_heads_t(ot, tq, o_ref):
    half0 = lax.broadcasted_iota(I32, (tq, LANES), 1) < HEAD_DIM
    for r in range(GROUP_NSA):
        o0 = ot[:, r * tq:(r + 1) * tq].T
        o1 = ot[:, (GROUP_NSA + r) * tq:(GROUP_NSA + r + 1) * tq].T
        o_ref[0, :, r * LANES:(r + 1) * LANES] = jnp.where(half0, o0, o1)


def _cmpsel_kernel(qt_ref, kc_ref, vct_ref, ovt_ref, o_ref, bias_ref, *, tq):
    i = pl.program_id(1)
    n16 = kc_ref.shape[1]
    cols = 8 * tq
    qa = jnp.concatenate(_query_groups_t(qt_ref, tq), axis=1)
    st = jnp.dot(kc_ref[0], qa, preferred_element_type=F32)
    t = i * tq + (lax.broadcasted_iota(I32, (n16, cols), 1) & (tq - 1))
    c = lax.broadcasted_iota(I32, (n16, cols), 0)
    valid = (c * CMP_STRIDE + (CMP_BLOCK - 1)) <= t
    st = jnp.where(valid, st, NEG_INF)
    m = jnp.max(st, axis=0, keepdims=True)
    e = jnp.exp2(st - m)
    l = jnp.sum(e, axis=0, keepdims=True)
    p = jnp.where(valid, e * (1.0 / l), 0.0)
    ot = jnp.dot(vct_ref[0], p.astype(BF16), preferred_element_type=F32)
    _store_heads_t(ot, tq, o_ref)

    n_slc = ovt_ref.shape[0]
    j = lax.broadcasted_iota(I32, (n_slc, tq), 0)
    cur = (i * tq + lax.broadcasted_iota(I32, (n_slc, tq), 1)) >> SLC_SHIFT
    forced = (j == 0) | (j == cur) | (j == cur - 1)
    ovt = ovt_ref[...]
    biases = []
    for hk in range(N_KV_NSA):
        ps = p[:, hk * GROUP_NSA * tq:(hk * GROUP_NSA + 1) * tq]
        for r in range(1, GROUP_NSA):
            ps = ps + p[:, (hk * GROUP_NSA + r) * tq:(hk * GROUP_NSA + r + 1) * tq]
        hi = ps.astype(BF16)
        lo = (ps - hi.astype(F32)).astype(BF16)
        pslc = (jnp.dot(ovt, hi, preferred_element_type=F32)
                + jnp.dot(ovt, lo, preferred_element_type=F32))
        score = jnp.where(forced, FORCE_SCORE, jnp.where(j <= cur, pslc, -1.0))
        rank = jnp.zeros((n_slc, tq), I32)
        for ii in range(n_slc):
            ri = score[ii:ii + 1, :]
            beats = (ri > score) | ((ri == score) & (j > ii))
            rank = rank + beats.astype(I32)
        biases.append(jnp.where(rank < SLC_TOPK, 0.0, NEG_INF))
    bias_ref[0] = jnp.concatenate(biases, axis=0).astype(BF16)


def _overlap_t(S):
    n16 = S // CMP_STRIDE
    n_slc = S // SLC_BLOCK
    cs = np.arange(n16) * CMP_STRIDE
    js = np.arange(n_slc) * SLC_BLOCK
    ov = np.clip(np.minimum(cs[:, None] + CMP_BLOCK, js[None, :] + SLC_BLOCK)
                 - np.maximum(cs[:, None], js[None, :]), 0, None).astype(np.float32) / CMP_BLOCK
    ov[n16 - 1] = 0.0
    return jnp.asarray(ov.T, BF16)


def _cmpsel(q_raw_t, kcf, vcf, tq=256):
    B, _, S = q_raw_t.shape
    n16 = S // CMP_STRIDE
    n_slc = S // SLC_BLOCK
    assert n_slc == HEAD_DIM, "selection bias is laid out as one 64-lane half per kv head"
    return pl.pallas_call(
        functools.partial(_cmpsel_kernel, tq=tq),
        grid=(B, S // tq),
        in_specs=[pl.BlockSpec((1, Q_NSA, tq), lambda b, i: (b, 0, i)),
                  pl.BlockSpec((1, n16, KV_NSA), lambda b, i: (b, 0, 0)),
                  pl.BlockSpec((1, KV_NSA, n16), lambda b, i: (b, 0, 0)),
                  pl.BlockSpec((n_slc, n16), lambda b, i: (0, 0))],
        out_specs=[pl.BlockSpec((1, tq, Q_NSA), lambda b, i: (b, i, 0)),
                   pl.BlockSpec((1, 2 * n_slc, tq), lambda b, i: (b, 0, i))],
        out_shape=[jax.ShapeDtypeStruct((B, S, Q_NSA), F32),
                   jax.ShapeDtypeStruct((B, 2 * n_slc, S), BF16)],
        compiler_params=_cparams(("parallel", "parallel")),
        name="cmpsel",
    )(q_raw_t, kcf, vcf.transpose(0, 2, 1), _overlap_t(S))


def _slc_kernel(qi_ref, kj_ref, qt_ref, bt_ref, k_ref, vt_ref, o_ref, qa_sc, m_sc, acc_sc, *, tq, tk):
    step = pl.program_id(1)
    i = qi_ref[step]
    kj = kj_ref[step]
    last = (i * tq + tq - 1) // tk
    cols = 8 * tq

    @pl.when(kj == 0)
    def _():
        half0 = lax.broadcasted_iota(I32, (LANES, tq), 0) < HEAD_DIM
        zero = jnp.zeros((LANES, tq), BF16)
        bt = bt_ref[0]
        groups = []
        for hk in range(N_KV_NSA):
            keep = half0 if hk == 0 else jnp.logical_not(half0)
            bh = jnp.where(keep, bt, zero)
            for r in range(GROUP_NSA):
                qb = qt_ref[0, r * LANES:(r + 1) * LANES, :]
                groups.append(jnp.concatenate([jnp.where(keep, qb, zero), bh], axis=0))
        qa_sc[...] = jnp.concatenate(groups, axis=1)
        m_sc[...] = jnp.full((1, cols), NEG_INF, F32)
        acc_sc[...] = jnp.zeros(acc_sc.shape, F32)

    def update(on_diagonal):
        kblk = (kj * tk + lax.broadcasted_iota(I32, (tk, LANES), 0)) >> SLC_SHIFT
        lane = lax.broadcasted_iota(I32, (tk, LANES), 1) & (HEAD_DIM - 1)
        onehot = jnp.where(kblk == lane, 1.0, 0.0).astype(BF16)
        kaug = jnp.concatenate([k_ref[0], onehot], axis=1)
        st = jnp.dot(kaug, qa_sc[...], preferred_element_type=F32)
        if on_diagonal:
            kpos = kj * tk + lax.broadcasted_iota(I32, (tk, cols), 0)
            t = i * tq + (lax.broadcasted_iota(I32, (tk, cols), 1) & (tq - 1))
            st = jnp.where(kpos <= t, st, NEG_INF)
        m_old = m_sc[...]
        m_new = jnp.maximum(m_old, jnp.max(st, axis=0, keepdims=True))
        alpha = jnp.exp2(m_old - m_new)
        p = jnp.exp2(st - m_new).astype(BF16)
        vt_ones = jnp.concatenate([vt_ref[0], jnp.ones((DEN_ROWS, tk), BF16)], axis=0)
        acc_sc[...] = alpha * acc_sc[...] + jnp.dot(vt_ones, p, preferred_element_type=F32)
        m_sc[...] = m_new

    @pl.when(kj < last)
    def _():
        update(False)

    @pl.when(kj == last)
    def _():
        update(True)
        acc = acc_sc[...]
        _store_heads_t(acc[:LANES] * (1.0 / acc[LANES:LANES + 1]), tq, o_ref)


def _slc(q_rot_t, bias_t, ks, vs_t, tq=512, tk=512):
    B, _, S = q_rot_t.shape
    assert tk % tq == 0
    qi, kj = [], []
    for i in range(S // tq):
        for j in range((i * tq + tq - 1) // tk + 1):
            qi.append(i)
            kj.append(j)
    grid_spec = pltpu.PrefetchScalarGridSpec(
        num_scalar_prefetch=2,
        grid=(B, len(qi)),
        in_specs=[pl.BlockSpec((1, Q_NSA, tq), lambda b, s, qi, kj: (b, 0, qi[s])),
                  pl.BlockSpec((1, LANES, tq), lambda b, s, qi, kj: (b, 0, qi[s])),
                  pl.BlockSpec((1, tk, KV_NSA), lambda b, s, qi, kj: (b, kj[s], 0)),
                  pl.BlockSpec((1, KV_NSA, tk), lambda b, s, qi, kj: (b, 0, kj[s]))],
        out_specs=pl.BlockSpec((1, tq, Q_NSA), lambda b, s, qi, kj: (b, qi[s], 0)),
        scratch_shapes=[pltpu.VMEM((2 * LANES, 8 * tq), BF16),
                        pltpu.VMEM((1, 8 * tq), F32),
                        pltpu.VMEM((LANES + DEN_ROWS, 8 * tq), F32)],
    )
    return pl.pallas_call(
        functools.partial(_slc_kernel, tq=tq, tk=tk),
        grid_spec=grid_spec,
        out_shape=jax.ShapeDtypeStruct((B, S, Q_NSA), F32),
        compiler_params=_cparams(("parallel", "arbitrary")),
        name="slc",
    )(jnp.asarray(qi, I32), jnp.asarray(kj, I32), q_rot_t, bias_t, ks, vs_t)


def _win_kernel(qt_ref, k0_ref, k1_ref, k2_ref, v0_ref, v1_ref, v2_ref, o_ref, *, tq, tk, max_dist):
    i = pl.program_id(1)
    a = (i * tq) // tk
    cols = 8 * tq
    qa = jnp.concatenate(_query_groups_t(qt_ref, tq), axis=1)
    t = i * tq + (lax.broadcasted_iota(I32, (tk, cols), 1) & (tq - 1))
    row = lax.broadcasted_iota(I32, (tk, cols), 0)
    sts = []
    for jt, k_ref in enumerate((k0_ref, k1_ref, k2_ref)):
        st = jnp.dot(k_ref[0], qa, preferred_element_type=F32)
        kpos = (a - 2 + jt) * tk + row
        if jt == 0:
            st = jnp.where((t - kpos <= max_dist) & (kpos >= 0), st, NEG_INF)
        elif jt == 1:
            st = jnp.where(kpos >= 0, st, NEG_INF)
        else:
            st = jnp.where(kpos <= t, st, NEG_INF)
        sts.append(st)
    m = jnp.max(sts[0], axis=0, keepdims=True)
    for st in sts[1:]:
        m = jnp.maximum(m, jnp.max(st, axis=0, keepdims=True))
    acc = jnp.zeros((LANES + DEN_ROWS, cols), F32)
    for st, v_ref in zip(sts, (v0_ref, v1_ref, v2_ref)):
        vt_ones = jnp.concatenate([v_ref[0], jnp.ones((DEN_ROWS, tk), BF16)], axis=0)
        acc = acc + jnp.dot(vt_ones, jnp.exp2(st - m).astype(BF16), preferred_element_type=F32)
    _store_heads_t(acc[:LANES] * (1.0 / acc[LANES:LANES + 1]), tq, o_ref)


def _win(q_rot_t, kw, vw_t, tq=256, tk=256):
    B, _, S = q_rot_t.shape
    max_dist = WIN_NSA - 1
    assert tk % tq == 0 and 2 * tk == max_dist + 1

    def k_map(jt):
        return lambda b, i: (b, jnp.maximum((i * tq) // tk - 2 + jt, 0), 0)

    def v_map(jt):
        return lambda b, i: (b, 0, jnp.maximum((i * tq) // tk - 2 + jt, 0))

    return pl.pallas_call(
        functools.partial(_win_kernel, tq=tq, tk=tk, max_dist=max_dist),
        grid=(B, S // tq),
        in_specs=[pl.BlockSpec((1, Q_NSA, tq), lambda b, i: (b, 0, i))]
        + [pl.BlockSpec((1, tk, KV_NSA), k_map(jt)) for jt in range(3)]
        + [pl.BlockSpec((1, KV_NSA, tk), v_map(jt)) for jt in range(3)],
        out_specs=pl.BlockSpec((1, tq, Q_NSA), lambda b, i: (b, i, 0)),
        out_shape=jax.ShapeDtypeStruct((B, S, Q_NSA), F32),
        compiler_params=_cparams(("parallel", "parallel")),
        name="win",
    )(q_rot_t, kw, kw, kw, vw_t, vw_t, vw_t)


def _band_kernel(*refs, tq, tk, nt, max_dist):
    q_ref = refs[0]
    k_refs = refs[1:1 + nt]
    v_refs = refs[1 + nt:1 + 2 * nt]
    o_ref = refs[1 + 2 * nt]
    lse_ref = refs[2 + 2 * nt]
    i = pl.program_id(2)
    a = (i * tq) // tk
    lane = lax.broadcasted_iota(I32, (tq, LANES), 1)
    half0 = lane < HEAD_DIM
    row_t = i * tq + (lax.broadcasted_iota(I32, (2 * tq, nt * tk), 0) & (tq - 1))
    kpos = (a - (nt - 1)) * tk + lax.broadcasted_iota(I32, (2 * tq, nt * tk), 1)
    dist = row_t - kpos
    ok = (dist >= 0) & (dist <= max_dist) & (kpos >= 0)
    ones = jnp.ones((nt * tk, LANES), BF16)
    lse_acc = jnp.zeros((tq, LANES), F32)
    for blk in range(Q_DIL // LANES):
        lanes = slice(blk * LANES, (blk + 1) * LANES)
        qb = q_ref[0, :, lanes]
        zero = jnp.zeros_like(qb)
        qs = jnp.concatenate([jnp.where(half0, qb, zero), jnp.where(half0, zero, qb)], axis=0)
        kcat = jnp.concatenate([k_ref[0, :, lanes] for k_ref in k_refs], axis=0)
        vcat = jnp.concatenate([v_ref[0, :, lanes] for v_ref in v_refs], axis=0)
        s = lax.dot_general(qs, kcat, _NT, preferred_element_type=F32)
        s = jnp.where(ok, s, NEG_INF)
        m = jnp.max(s, axis=-1, keepdims=True)
        p = jnp.exp(s - m).astype(BF16)
        ol = jnp.dot(p, jnp.concatenate([vcat, ones], axis=1), preferred_element_type=F32)
        den = ol[:, LANES:]
        o = ol[:, :LANES] * (1.0 / den)
        o_ref[0, :, lanes] = jnp.where(half0, o[:tq], o[tq:])
        lse = m + jnp.log(den[:, 0:1])
        lse_acc = jnp.where(lane == 2 * blk, lse[:tq], lse_acc)
        lse_acc = jnp.where(lane == 2 * blk + 1, lse[tq:], lse_acc)
    lse_ref[0] = lse_acc


def _band(q, k, v, *, dil, max_dist, tq, tk, nt):
    B, m, _ = q.shape

    def kv_map(jt):
        return lambda b, r, i: (b, jnp.maximum((i * tq) // tk - (nt - 1) + jt, 0), r)

    tok_spec = pl.BlockSpec((1, tq, Q_DIL), lambda b, r, i: (b, i, r))
    kv_specs = [pl.BlockSpec((1, tk, Q_DIL), kv_map(jt)) for jt in range(nt)]
    return pl.pallas_call(
        functools.partial(_band_kernel, tq=tq, tk=tk, nt=nt, max_dist=max_dist),
        grid=(B, dil, m // tq),
        in_specs=[tok_spec] + kv_specs + kv_specs,
        out_specs=[tok_spec, pl.BlockSpec((1, tq, LANES), lambda b, r, i: (b, i, r))],
        out_shape=[jax.ShapeDtypeStruct((B, m, dil * Q_DIL), F32),
                   jax.ShapeDtypeStruct((B, m, dil * LANES), F32)],
        compiler_params=_cparams(("parallel", "parallel", "parallel")),
        name=f"band_d{dil}_w{max_dist}",
    )(q, *([k] * nt), *([v] * nt))


def _out_kernel(ocmp_ref, oslc_ref, owin_ref, gt_ref, od1_ref, od4_ref, od16_ref, l1_ref, l4_ref, l16_ref,
                x_ref, mod_ref, eg_ref, ed_ref, gnsa_ref, gdil_ref, wo_ref, gpost_ref, o_ref, st_sc):
    tm = x_ref.shape[1]

    def token_order(ref, d, n_blk):
        if d == 1:
            return ref[0]
        blocks = []
        for blk in range(n_blk):
            for r in range(d):
                lo = (r * n_blk + blk) * LANES
                st_sc[pl.ds(r, tm // d, stride=d), :] = ref[0, :, lo:lo + LANES]
            blocks.append(st_sc[...])
        return blocks[0] if n_blk == 1 else jnp.concatenate(blocks, axis=1)

    sg = jax.nn.sigmoid(gt_ref[0])
    g3 = _expand_dot(sg, eg_ref[...])
    oa = (g3[:, :Q_NSA] * ocmp_ref[0] + g3[:, Q_NSA:2 * Q_NSA] * oslc_ref[0] + g3[:, 2 * Q_NSA:] * owin_ref[0])
    ya = _rms(oa, gnsa_ref[...])

    dils = [d for _, d in DIL_CONFIGS]
    n_blk = Q_DIL // LANES
    l1, l4, l16 = [token_order(ref, d, 1) for ref, d in zip((l1_ref, l4_ref, l16_ref), dils)]
    mx = jnp.maximum(jnp.maximum(l1, l4), l16)
    e1, e4, e16 = jnp.exp(l1 - mx), jnp.exp(l4 - mx), jnp.exp(l16 - mx)
    den = e1 + e4 + e16
    ed = ed_ref[...]
    od1, od4, od16 = [token_order(ref, d, n_blk) for ref, d in zip((od1_ref, od4_ref, od16_ref), dils)]
    inv = 1.0 / den
    ob = (_expand_dot(e1 * inv, ed) * od1 + _expand_dot(e4 * inv, ed) * od4 + _expand_dot(e16 * inv, ed) * od16)
    yb = _rms(ob, gdil_ref[...])

    y = jnp.concatenate([ya, yb], axis=1).astype(BF16)
    z = jnp.dot(y, wo_ref[...], preferred_element_type=F32)
    gt_m = mod_ref[0, 2:3, :]
    o_ref[0] = x_ref[0] + gt_m * _rms(z, gpost_ref[...])


def _gate_expanders():
    perm = _nsa_perm()
    eg = np.zeros((LANES, 3 * Q_NSA), np.float32)
    for lane_out, col in enumerate(perm):
        hq = col // HEAD_DIM
        for c in range(3):
            eg[hq * 3 + c, c * Q_NSA + lane_out] = 1.0
    ed = np.zeros((LANES, Q_DIL), np.float32)
    for h in range(N_HEADS_DIL):
        ed[h, h * HEAD_DIM:(h + 1) * HEAD_DIM] = 1.0
    return jnp.asarray(np.concatenate([eg, eg]), BF16), jnp.asarray(np.concatenate([ed, ed]), BF16)


def _out(o_cmp, o_slc, o_win, gates, ods, lses, x, mod6, g_out_nsa, g_out_dil, w_o, g_post, tm=1024):
    B, S, D = x.shape
    perm = _nsa_perm()
    eg, ed = _gate_expanders()
    w_o_p = jnp.concatenate([w_o[:Q_NSA][perm], w_o[Q_NSA:]], axis=0).astype(BF16)
    g_nsa_p = g_out_nsa[perm].reshape(1, Q_NSA)

    def tok(width):
        return pl.BlockSpec((1, tm, width), lambda b, i: (b, i, 0))

    def const(shape):
        return pl.BlockSpec(shape, lambda b, i: (0,) * len(shape))

    def view(d, width):
        return pl.BlockSpec((1, tm // d, d * width), lambda b, i: (b, i, 0))

    dils = [d for _, d in DIL_CONFIGS]
    return pl.pallas_call(
        _out_kernel,
        grid=(B, S // tm),
        in_specs=[tok(Q_NSA), tok(Q_NSA), tok(Q_NSA), tok(LANES)]
        + [view(d, Q_DIL) for d in dils] + [view(d, LANES) for d in dils]
        + [tok(D), pl.BlockSpec((1, 6, D), lambda b, i: (b, 0, 0)),
           const(eg.shape), const(ed.shape), const((1, Q_NSA)), const((1, Q_DIL)),
           const((D, D)), const((1, D))],
        out_specs=tok(D),
        out_shape=jax.ShapeDtypeStruct((B, S, D), F32),
        scratch_shapes=[pltpu.VMEM((tm, LANES), F32)],
        compiler_params=_cparams(("parallel", "parallel")),
        name="out",
    )(o_cmp, o_slc, o_win, gates, *ods, *lses, x, mod6, eg, ed, g_nsa_p, g_out_dil.reshape(1, Q_DIL),
      w_o_p, g_post.reshape(1, D))


def _route_kernel(x_ref, mod_ref, g_ref, wr_ref, br_ref, h_ref, idx_ref, gate_ref, rank_ref, cnt_ref, run_sc):
    i = pl.program_id(0)

    @pl.when(i == 0)
    def _():
        run_sc[...] = jnp.zeros_like(run_sc)

    sh = mod_ref[0, 3:4, :]
    sc = mod_ref[0, 4:5, :]
    h = _rms(x_ref[...], g_ref[...]) * (1.0 + sc) + sh
    h_ref[...] = h
    logits = _dot3(h, wr_ref[...]) + br_ref[...]
    tm = logits.shape[0]
    lane = lax.broadcasted_iota(I32, (tm, LANES), 1)
    work = jnp.where(lane < N_EXPERTS, logits, -jnp.inf)
    idx_out = jnp.zeros((tm, LANES), I32)
    val_out = jnp.full((tm, LANES), -jnp.inf, F32)
    rank_out = jnp.zeros((tm, LANES), F32)
    tri = jnp.where(lax.broadcasted_iota(I32, (tm, tm), 1) < lax.broadcasted_iota(I32, (tm, tm), 0),
                    1.0, 0.0).astype(BF16)
    onehots = []
    for k in range(TOP_K):
        mx = jnp.max(work, axis=-1, keepdims=True)
        ix = jnp.min(jnp.where(work == mx, lane, LANES), axis=-1, keepdims=True)
        hit = lane == ix
        idx_out = jnp.where(lane == k, ix, idx_out)
        val_out = jnp.where(lane == k, mx, val_out)
        work = jnp.where(hit, -jnp.inf, work)
        onehots.append(jnp.where(hit, 1.0, 0.0))
    earlier = jnp.dot(tri, jnp.concatenate(onehots, axis=1).astype(BF16), preferred_element_type=F32)
    base = run_sc[0:1, :]
    for k, onehot in enumerate(onehots):
        before = earlier[:, k * LANES:(k + 1) * LANES] + base
        rank_k = jnp.sum(onehot * before, axis=-1, keepdims=True)
        rank_out = jnp.where(lane == k, rank_k, rank_out)
        base = base + jnp.sum(onehot, axis=0, keepdims=True)
    run_sc[...] = jnp.broadcast_to(base, run_sc.shape)
    e = jnp.exp(val_out - val_out[:, 0:1])
    idx_ref[...] = idx_out
    gate_ref[...] = e / jnp.sum(e, axis=-1, keepdims=True)
    rank_ref[...] = rank_out.astype(I32)
    cnt_ref[...] = run_sc[...].astype(I32)


def _route(x1, mod6, g_pre_ffn, w_router, b_router, tm=512):
    B, S, D = x1.shape
    N = B * S
    spb = S // tm
    wr = jnp.pad(w_router, ((0, 0), (0, LANES - N_EXPERTS)))
    br = jnp.pad(b_router, (0, LANES - N_EXPERTS)).reshape(1, LANES)

    def tok(width):
        return pl.BlockSpec((tm, width), lambda i: (i, 0))

    return pl.pallas_call(
        _route_kernel,
        grid=(N // tm,),
        in_specs=[tok(D), pl.BlockSpec((1, 6, D), lambda i: (i // spb, 0, 0)),
                  pl.BlockSpec((1, D), lambda i: (0, 0)),
                  pl.BlockSpec((D, LANES), lambda i: (0, 0)),
                  pl.BlockSpec((1, LANES), lambda i: (0, 0))],
        out_specs=[tok(D), tok(LANES), tok(LANES), tok(LANES), pl.BlockSpec((8, LANES), lambda i: (0, 0))],
        out_shape=[jax.ShapeDtypeStruct((N, D), F32),
                   jax.ShapeDtypeStruct((N, LANES), I32),
                   jax.ShapeDtypeStruct((N, LANES), F32),
                   jax.ShapeDtypeStruct((N, LANES), I32),
                   jax.ShapeDtypeStruct((8, LANES), I32)],
        scratch_shapes=[pltpu.VMEM((8, LANES), F32)],
        compiler_params=_cparams(("arbitrary",)),
        name="route",
    )(x1.reshape(N, D), mod6, g_pre_ffn.reshape(1, D), wr, br)


def _dispatch_kernel(pos_ref, pad0_ref, padn_ref, nt_ref, h_ref, xs_out, zbuf, sem, zsem, *, tm, tmx, n_tiles):
    i = pl.program_id(0)
    bits = tmx.bit_length() - 1

    def pad_copies(e, fn):
        p0 = pad0_ref[e]
        head = (-p0) & (SUBLANES - 1)
        head = jnp.minimum(head, padn_ref[e])
        for r in range(SUBLANES - 1):
            @pl.when(r < head)
            def _():
                fn(pltpu.make_async_copy(zbuf.at[pl.ds(0, 1), :], xs_out.at[pl.ds(p0 + r, 1), :], zsem))
        a = p0 + head
        n = padn_ref[e] - head
        for b in range(3, bits):
            size = 1 << b
            off = (n >> (b + 1)) << (b + 1)

            @pl.when((n & size) != 0)
            def _():
                fn(pltpu.make_async_copy(zbuf.at[pl.ds(0, size), :],
                                         xs_out.at[pl.ds(pl.multiple_of(a + off, SUBLANES), size), :], zsem))

    def tail_copy(t, fn):
        fn(pltpu.make_async_copy(zbuf, xs_out.at[pl.ds(pl.multiple_of(t * tmx, tmx), tmx), :], zsem))

    def for_all_fill(fn):
        def per_expert(e, carry):
            pad_copies(e, fn)
            return carry
        lax.fori_loop(0, N_EXPERTS, per_expert, 0)

        def per_tile(t, carry):
            tail_copy(t, fn)
            return carry
        lax.fori_loop(nt_ref[0], n_tiles, per_tile, 0)

    @pl.when(i == 0)
    def _():
        zbuf[...] = jnp.zeros_like(zbuf)
        for_all_fill(lambda cp: cp.start())
        for_all_fill(lambda cp: cp.wait())

    def body(r, carry):
        for k in range(TOP_K):
            dst_row = pos_ref[(i * tm + r) * TOP_K + k]
            pltpu.make_async_copy(h_ref.at[pl.ds(r, 1), :], xs_out.at[pl.ds(dst_row, 1), :], sem).start()
        return carry

    lax.fori_loop(0, tm, body, 0, unroll=DMA_UNROLL // TOP_K)
    for k in range(TOP_K):
        pltpu.make_async_copy(h_ref, xs_out.at[pl.ds(0, tm), :], sem).wait()


def _dispatch(h2, pos_flat, pad_start, pad_len, n_valid, n_tiles, tmx, tm=1024):
    N, D = h2.shape
    assert tmx & (tmx - 1) == 0
    grid_spec = pltpu.PrefetchScalarGridSpec(
        num_scalar_prefetch=4,
        grid=(N // tm,),
        in_specs=[pl.BlockSpec((tm, D), lambda i, *_: (i, 0))],
        out_specs=pl.BlockSpec(memory_space=pl.ANY),
        scratch_shapes=[pltpu.VMEM((tmx, D), F32), pltpu.SemaphoreType.DMA(()), pltpu.SemaphoreType.DMA(())],
    )
    return pl.pallas_call(
        functools.partial(_dispatch_kernel, tm=tm, tmx=tmx, n_tiles=n_tiles),
        grid_spec=grid_spec,
        out_shape=jax.ShapeDtypeStruct((n_tiles * tmx, D), F32),
        compiler_params=_cparams(("arbitrary",)),
        name="dispatch",
    )(pos_flat, pad_start, pad_len, n_valid, h2)


def _moe_kernel(te_ref, nt_ref, nx_ref, sl_ref, rows_ref, xs_ref, wup_hbm, bup_ref, wdn_hbm, bdn_ref, pm_ref, y_ref,
                wup_in, wdn_in, wup_sc, wdn_sc, sems):
    i = pl.program_id(0)
    n_valid = nt_ref[0]
    e = te_ref[i]
    new_expert = (i == 0) | (e != te_ref[jnp.maximum(i - 1, 0)])
    slot = sl_ref[i]
    n_blk = wup_sc.shape[1] // (2 * LANES)

    def weight_copies(ex, s):
        return (pltpu.make_async_copy(wup_hbm.at[ex], wup_in.at[s], sems.at[0, s]),
                pltpu.make_async_copy(wdn_hbm.at[ex], wdn_in.at[s], sems.at[1, s]))

    @pl.when(i == 0)
    def _():
        for cp in weight_copies(e, slot):
            cp.start()

    @pl.when(new_expert & (i < n_valid))
    def _():
        for cp in weight_copies(e, slot):
            cp.wait()
        nx = nx_ref[i]

        @pl.when(nx >= 0)
        def _():
            for cp in weight_copies(nx, 1 - slot):
                cp.start()

        pm = pm_ref[...]
        for blk in range(n_blk):
            cols = slice(blk * 2 * LANES, (blk + 1) * 2 * LANES)
            w = wup_in[slot, :, cols].astype(BF16)
            wup_sc[:, cols] = jnp.dot(w, pm, preferred_element_type=F32).astype(BF16)
        wdn_sc[...] = wdn_in[slot].astype(BF16)

    def expert_mlp(n_rows):
        xs = xs_ref[0:n_rows, :].astype(BF16)
        u = jnp.dot(xs, wup_sc[...], preferred_element_type=F32) + bup_ref[0]
        acts = []
        for blk in range(n_blk):
            ug = jnp.minimum(u[:, blk * 2 * LANES:blk * 2 * LANES + LANES], SWIGLU_LIMIT)
            ul = jnp.clip(u[:, blk * 2 * LANES + LANES:(blk + 1) * 2 * LANES], -SWIGLU_LIMIT, SWIGLU_LIMIT)
            acts.append((ug * jax.nn.sigmoid(SWIGLU_ALPHA * ug) * (ul + 1.0)).astype(BF16))
        act = jnp.concatenate(acts, axis=1)
        y_ref[0:n_rows, :] = jnp.dot(act, wdn_sc[...], preferred_element_type=F32) + bdn_ref[0]

    tm = y_ref.shape[0]
    half = tm // 2
    rows = rows_ref[i]

    @pl.when((i < n_valid) & (rows > half))
    def _():
        expert_mlp(tm)

    @pl.when((i < n_valid) & (rows <= half))
    def _():
        expert_mlp(half)
        y_ref[half:, :] = jnp.zeros((tm - half, y_ref.shape[1]), F32)

    @pl.when(i >= n_valid)
    def _():
        y_ref[...] = jnp.zeros_like(y_ref)


def _moe_layout(top_idx, rank, counts, tm):
    N = top_idx.shape[0]
    n_tiles = N * TOP_K // tm + N_EXPERTS
    e_ids = jnp.arange(N_EXPERTS, dtype=I32)
    tiles_e = (counts + tm - 1) // tm
    tile_end = jnp.sum(jnp.where(e_ids[None, :] <= e_ids[:, None], tiles_e[None, :], 0), axis=1)
    start = (tile_end - tiles_e) * tm
    pos = rank + jnp.sum(jnp.where(top_idx[:, :, None] == e_ids[None, None, :], start[None, None, :], 0), axis=-1)
    n_valid = tile_end[N_EXPERTS - 1]
    tile_ids = jnp.arange(n_tiles, dtype=I32)
    tile_e = jnp.sum((jnp.minimum(tile_ids, n_valid - 1)[:, None] >= tile_end[None, :]).astype(I32), axis=1)
    prev_e = jnp.concatenate([jnp.full((1,), -1, I32), tile_e[:-1]])
    new = (tile_e != prev_e) & (tile_ids < n_valid)
    ordinal = jnp.sum(jnp.where(tile_ids[None, :] <= tile_ids[:, None], new[None, :].astype(I32), 0), axis=1) - 1
    next_first = jnp.sum(jnp.where(tile_e[:, None] == e_ids[None, :], tile_end[None, :], 0), axis=1)
    next_e = jnp.sum(jnp.where(next_first[:, None] == tile_ids[None, :], tile_e[None, :], 0), axis=1)
    next_e = jnp.where(next_first < n_valid, next_e, -1)
    is_e = tile_e[:, None] == e_ids[None, :]
    rows_end = jnp.sum(jnp.where(is_e, (start + counts)[None, :], 0), axis=1)
    tile_rows = jnp.clip(rows_end - tile_ids * tm, 0, tm)
    meta = dict(tile_e=tile_e.astype(I32), n_valid=n_valid.astype(I32).reshape(1), next_e=next_e.astype(I32),
                slot=(ordinal & 1).astype(I32), tile_rows=tile_rows.astype(I32),
                pad_start=(start + counts).astype(I32), pad_len=(tiles_e * tm - counts).astype(I32))
    return pos.astype(I32), meta, n_tiles


def _glu_perm():
    pm = np.zeros((2 * LANES, 2 * LANES), np.float32)
    for j in range(LANES):
        pm[2 * j, j] = 1.0
        pm[2 * j + 1, LANES + j] = 1.0
    return jnp.asarray(pm, BF16)


def _moe(xs, meta, w_up, b_up, w_down, b_down, tm):
    P, D = xs.shape
    n_blk = D_FF // LANES
    b_up_p = b_up.reshape(N_EXPERTS, n_blk, LANES, 2).transpose(0, 1, 3, 2).reshape(N_EXPERTS, 1, 2 * D_FF)
    b_dn = b_down.reshape(N_EXPERTS, 1, D)

    def bmap(i, te, *_):
        return (te[i], 0, 0)

    grid_spec = pltpu.PrefetchScalarGridSpec(
        num_scalar_prefetch=5,
        grid=(P // tm,),
        in_specs=[pl.BlockSpec((tm, D), lambda i, *_: (i, 0)),
                  pl.BlockSpec(memory_space=pl.ANY), pl.BlockSpec((1, 1, 2 * D_FF), bmap),
                  pl.BlockSpec(memory_space=pl.ANY), pl.BlockSpec((1, 1, D), bmap),
                  pl.BlockSpec((2 * LANES, 2 * LANES), lambda i, *_: (0, 0))],
        out_specs=pl.BlockSpec((tm, D), lambda i, *_: (i, 0)),
        scratch_shapes=[pltpu.VMEM((2, D, 2 * D_FF), F32), pltpu.VMEM((2, D_FF, D), F32),
                        pltpu.VMEM((D, 2 * D_FF), BF16), pltpu.VMEM((D_FF, D), BF16),
                        pltpu.SemaphoreType.DMA((2, 2))],
    )
    return pl.pallas_call(
        _moe_kernel,
        grid_spec=grid_spec,
        out_shape=jax.ShapeDtypeStruct((P, D), F32),
        compiler_params=_cparams(("arbitrary",)),
        name="moe",
    )(meta["tile_e"], meta["n_valid"], meta["next_e"], meta["slot"], meta["tile_rows"], xs, w_up, b_up_p, w_down,
      b_dn, _glu_perm())


def _final_kernel(pos_ref, ys_hbm, x_ref, gate_ref, gtf_ref, g_ref, o_ref, buf, sems, *, tm, n_steps):
    i = pl.program_id(0)
    slot = i & 1

    n = TOP_K * tm

    def gather(step, dst, sem):
        def body(j, carry):
            pltpu.make_async_copy(ys_hbm.at[pl.ds(pos_ref[step * n + j], 1), :], dst.at[pl.ds(j, 1), :],
                                  sem).start()
            return carry
        lax.fori_loop(0, n, body, 0, unroll=DMA_UNROLL)

    @pl.when(i == 0)
    def _():
        gather(0, buf.at[0], sems.at[0])

    @pl.when(i + 1 < n_steps)
    def _():
        gather(i + 1, buf.at[1 - slot], sems.at[1 - slot])

    pltpu.make_async_copy(ys_hbm.at[pl.ds(0, n), :], buf.at[slot], sems.at[slot]).wait()
    gate = gate_ref[...]
    y = gate[:, 0:1] * buf[slot, 0:tm, :]
    for k in range(1, TOP_K):
        y = y + gate[:, k:k + 1] * buf[slot, k * tm:(k + 1) * tm, :]
    o_ref[...] = x_ref[...] + gtf_ref[0, 5:6, :] * _rms(y, g_ref[...])


def _final(ys, pos_flat, gate, x1, mod6, g_post_ffn, tm=512):
    B, S, D = x1.shape
    N = B * S
    n_steps = N // tm
    steps_per_b = S // tm
    pos_sm = pos_flat.reshape(n_steps, tm, TOP_K).transpose(0, 2, 1).reshape(N * TOP_K)
    grid_spec = pltpu.PrefetchScalarGridSpec(
        num_scalar_prefetch=1,
        grid=(n_steps,),
        in_specs=[pl.BlockSpec(memory_space=pl.ANY),
                  pl.BlockSpec((tm, D), lambda i, p: (i, 0)),
                  pl.BlockSpec((tm, LANES), lambda i, p: (i, 0)),
                  pl.BlockSpec((1, 6, D), lambda i, p: (i // steps_per_b, 0, 0)),
                  pl.BlockSpec((1, D), lambda i, p: (0, 0))],
        out_specs=pl.BlockSpec((tm, D), lambda i, p: (i, 0)),
        scratch_shapes=[pltpu.VMEM((2, TOP_K * tm, D), F32), pltpu.SemaphoreType.DMA((2,))],
    )
    out = pl.pallas_call(
        functools.partial(_final_kernel, tm=tm, n_steps=n_steps),
        grid_spec=grid_spec,
        out_shape=jax.ShapeDtypeStruct((N, D), F32),
        compiler_params=_cparams(("arbitrary",)),
        name="final",
    )(pos_sm, ys, x1.reshape(N, D), gate, mod6, g_post_ffn.reshape(1, D))
    return out.reshape(B, S, D)


def _layer(x, c, positions, w_ada, b_ada, g_pre_mix, g_post_mix, g_pre_ffn, g_post_ffn,
           w_in, cmp_pos, w_cmp_k1, w_cmp_k2, w_cmp_v1, w_cmp_v2, g_out_nsa, g_out_dil, w_o,
           w_router, b_router, w_up, b_up, w_down, b_down):
    B, S, D = x.shape
    mod6 = _ada(c, w_ada, b_ada).reshape(B, 6, D)
    (q_raw_t, q_rot_t, kv16, ks, vs_t, kw, vw_t, gates, qbs, kbs, vbs) = _proj(x, mod6, g_pre_mix, positions, w_in)
    kcf, vcf = _cmpmlp(kv16, cmp_pos, w_cmp_k1, w_cmp_k2, w_cmp_v1, w_cmp_v2)
    o_cmp, bias_t = _cmpsel(q_raw_t, kcf, vcf)
    o_slc = _slc(q_rot_t, bias_t, ks, vs_t)
    o_win = _win(q_rot_t, kw, vw_t)
    ods, lses = [], []
    for (window, dil), qb, kb, vb in zip(DIL_CONFIGS, qbs, kbs, vbs):
        o, lse = _band(qb, kb, vb, dil=dil, max_dist=window // dil, tq=128, tk=128, nt=2)
        ods.append(o)
        lses.append(lse)
    x1 = _out(o_cmp, o_slc, o_win, gates, ods, lses, x, mod6, g_out_nsa, g_out_dil, w_o, g_post_mix)
    h2, top_idx, gate, rank, counts = _route(x1, mod6, g_pre_ffn, w_router, b_router)
    tm_moe = 512
    pos, meta, n_tiles = _moe_layout(top_idx[:, :TOP_K], rank[:, :TOP_K], counts[0, :N_EXPERTS], tm_moe)
    pos_flat = pos.reshape(B * S * TOP_K)
    xs = _dispatch(h2, pos_flat, meta["pad_start"], meta["pad_len"], meta["n_valid"], n_tiles, tm_moe)
    ys = _moe(xs, meta, w_up, b_up, w_down, b_down, tm_moe)
    return _final(ys, pos_flat, gate, x1, mod6, g_post_ffn)


def kernel(x, c, positions, w_ada, b_ada, g_pre_mix, g_post_mix, g_pre_ffn, g_post_ffn, w_in, cmp_pos,
           w_cmp_k1, w_cmp_k2, w_cmp_v1, w_cmp_v2, g_out_nsa, g_out_dil, w_o, w_router, b_router,
           w_up, b_up, w_down, b_down):
    depth = w_ada.shape[0]
    for l in range(depth):
        x = _layer(x, c, positions, w_ada[l], b_ada[l], g_pre_mix[l], g_post_mix[l], g_pre_ffn[l],
                   g_post_ffn[l], w_in[l], cmp_pos[l], w_cmp_k1[l], w_cmp_k2[l], w_cmp_v1[l], w_cmp_v2[l],
                   g_out_nsa[l], g_out_dil[l], w_o[l], w_router[l], b_router[l], w_up[l], b_up[l],
                   w_down[l], b_down[l])
    return x
```

```python
import functools

import numpy as np
import jax
import jax.numpy as jnp
from jax import lax
from jax.experimental import pallas as pl
from jax.experimental.pallas import tpu as pltpu

F32 = jnp.float32
BF16 = jnp.bfloat16
I32 = jnp.int32

D_MODEL = 1024
HEAD_DIM = 64
N_HEADS_NSA = 8
N_KV_NSA = 2
GROUP_NSA = 4
N_HEADS_DIL = 8
ROPE_THETA = 500000.0
ROPE_DIM = 16
ROPE_HALF = 8
CMP_BLOCK = 32
CMP_STRIDE = 16
CMP_HIDDEN = 256
SLC_BLOCK = 64
SLC_SHIFT = 6
SLC_TOPK = 16
WIN_NSA = 512
DIL_CONFIGS = ((128, 1), (512, 4), (2048, 16))
N_EXPERTS = 32
TOP_K = 4
D_FF = 1024
SWIGLU_LIMIT = 7.0
SWIGLU_ALPHA = 1.702
RMS_EPS = 1e-6
NEG_INF = -1e30
FORCE_SCORE = 1e9
SCALE = HEAD_DIM ** -0.5
LOG2E = 1.4426950408889634
DEN_ROWS = 16

Q_NSA = 512
KV_NSA = 128
Q_DIL = 512
LANES = 128
SUBLANES = 8
VMEM_LIMIT = 56 * 1024 * 1024
DMA_UNROLL = 32
MOE_PARTS = 4

_NT = (((1,), (1,)), ((), ()))


def _cparams(sem):
    return pltpu.CompilerParams(dimension_semantics=sem, vmem_limit_bytes=VMEM_LIMIT)


def _rms(x, g):
    return x * lax.rsqrt(jnp.mean(x * x, axis=-1, keepdims=True) + RMS_EPS) * g


def _hi_lo(a):
    hi = a.astype(BF16)
    return hi, (a - hi.astype(F32)).astype(BF16)


def _expand_dot(a, e2):
    return jnp.dot(jnp.concatenate(_hi_lo(a), axis=1), e2, preferred_element_type=F32)


def _dot3(a, b):
    ah, al = _hi_lo(a)
    bh, bl = _hi_lo(b)
    return (jnp.dot(ah, bh, preferred_element_type=F32) + jnp.dot(ah, bl, preferred_element_type=F32)
            + jnp.dot(al, bh, preferred_element_type=F32))


def _ada_kernel(c_ref, w_ref, b_ref, o_ref):
    c = c_ref[...]
    a = c * jax.nn.sigmoid(c)
    o_ref[...] = _dot3(a, w_ref[...]) + b_ref[...]


def _ada(c, w_ada, b_ada):
    B, D = c.shape
    n = w_ada.shape[1] // D
    return pl.pallas_call(
        _ada_kernel,
        grid=(n,),
        in_specs=[pl.BlockSpec((B, D), lambda j: (0, 0)),
                  pl.BlockSpec((D, D), lambda j: (0, j)),
                  pl.BlockSpec((1, D), lambda j: (0, j))],
        out_specs=pl.BlockSpec((B, D), lambda j: (0, j)),
        out_shape=jax.ShapeDtypeStruct((B, n * D), F32),
        compiler_params=_cparams(("arbitrary",)),
        name="ada",
    )(c, w_ada, b_ada.reshape(1, -1))


def _rope_tables(pos_row, inv_col, pass_row, expand):
    ang = inv_col * pos_row
    parts = []
    for v in (jnp.cos(ang), jnp.sin(ang)):
        hi = v.astype(BF16)
        parts += [hi, (v - hi.astype(F32)).astype(BF16)]
    tab_t = jnp.concatenate(parts, axis=0)
    out = lax.dot_general(tab_t, expand, (((0,), (0,)), ((), ())), preferred_element_type=F32)
    return out[:, :LANES] + pass_row, out[:, LANES:2 * LANES], out[:, 2 * LANES:]


def _rope_blk(x, cs, s1, s2):
    return x * cs + pltpu.roll(x, LANES - ROPE_HALF, 1) * s1 + pltpu.roll(x, ROPE_HALF, 1) * s2


def _proj_kernel(x_ref, mod_ref, g_ref, pos_ref, tab_ref, exp_ref, wa_ref, wg_ref, wb_ref,
                 qrawt_ref, qrott_ref, kv16_ref, ks_ref, vst_ref, kw_ref, vwt_ref, gt_ref,
                 *dil_refs_and_scratch):
    n_cfg = len(DIL_CONFIGS)
    qb_refs = dil_refs_and_scratch[0:n_cfg]
    kb_refs = dil_refs_and_scratch[n_cfg:2 * n_cfg]
    vb_refs = dil_refs_and_scratch[2 * n_cfg:3 * n_cfg]
    st_sc = dil_refs_and_scratch[3 * n_cfg]
    tm = x_ref.shape[1]

    def emit_dilated(val, blk, refs):
        st_sc[...] = val
        for (_, d), ref in zip(DIL_CONFIGS, refs):
            for r in range(d):
                piece = val if d == 1 else st_sc[pl.ds(r, tm // d, stride=d), :]
                lo = r * Q_DIL + blk * LANES
                ref[0, :, lo:lo + LANES] = piece.astype(BF16)

    x = x_ref[0]
    sh = mod_ref[0, 0:1, :]
    sc = mod_ref[0, 1:2, :]
    h = (_rms(x, g_ref[...]) * (1.0 + sc) + sh).astype(BF16)
    cs, s1, s2 = _rope_tables(pos_ref[0], tab_ref[0:ROPE_HALF, 0:1], tab_ref[ROPE_HALF:ROPE_HALF + 1, :],
                              exp_ref[...])

    pa = jnp.dot(h, wa_ref[...], preferred_element_type=F32)
    for r in range(Q_NSA // LANES):
        blk = pa[:, r * LANES:(r + 1) * LANES]
        qrawt_ref[0, r * LANES:(r + 1) * LANES, :] = (blk * (SCALE * LOG2E)).T.astype(BF16)
        rot = _rope_blk(blk, cs, s1, s2) * (SCALE * LOG2E)
        qrott_ref[0, r * LANES:(r + 1) * LANES, :] = rot.T.astype(BF16)
    o = Q_NSA
    for t in range(2):
        st_sc[...] = pa[:, o + t * LANES:o + (t + 1) * LANES]
        for j in range(CMP_STRIDE):
            kv16_ref[t, 0, :, j * LANES:(j + 1) * LANES] = st_sc[pl.ds(j, tm // CMP_STRIDE, stride=CMP_STRIDE), :]
    ks_ref[0] = _rope_blk(pa[:, o + 2 * LANES:o + 3 * LANES], cs, s1, s2).astype(BF16)
    vst_ref[0] = pa[:, o + 3 * LANES:o + 4 * LANES].T.astype(BF16)
    kw_ref[0] = _rope_blk(pa[:, o + 4 * LANES:o + 5 * LANES], cs, s1, s2).astype(BF16)
    vwt_ref[0] = pa[:, o + 5 * LANES:o + 6 * LANES].T.astype(BF16)

    gt_ref[0] = jnp.dot(h, wg_ref[...], preferred_element_type=F32)

    pb = jnp.dot(h, wb_ref[...], preferred_element_type=F32)
    for blk in range(Q_DIL // LANES):
        lanes = slice(blk * LANES, (blk + 1) * LANES)
        emit_dilated(_rope_blk(pb[:, lanes], cs, s1, s2) * SCALE, blk, qb_refs)
        emit_dilated(_rope_blk(pb[:, Q_DIL + blk * LANES:Q_DIL + (blk + 1) * LANES], cs, s1, s2), blk, kb_refs)
        emit_dilated(pb[:, 2 * Q_DIL + blk * LANES:2 * Q_DIL + (blk + 1) * LANES], blk, vb_refs)


def _nsa_perm():
    cols = []
    for r in range(GROUP_NSA):
        for hk in range(N_KV_NSA):
            hq = hk * GROUP_NSA + r
            cols.extend(range(hq * HEAD_DIM, (hq + 1) * HEAD_DIM))
    return np.asarray(cols, np.int32)


def _rope_const_tables():
    half = ROPE_HALF
    inv = ROPE_THETA ** (-jnp.arange(half, dtype=F32) / half)
    lane = np.arange(LANES) % HEAD_DIM
    passthrough = (lane >= ROPE_DIM).astype(np.float32)
    tab = jnp.zeros((2 * half, LANES), F32).at[:half, 0].set(inv).at[half].set(passthrough)
    expand = np.zeros((4 * half, 3 * LANES), np.float32)
    for l in range(LANES):
        f = lane[l] % half
        if lane[l] < ROPE_DIM:
            expand[f, l] = expand[half + f, l] = 1.0
        if lane[l] < half:
            expand[2 * half + f, LANES + l] = expand[3 * half + f, LANES + l] = -1.0
        elif lane[l] < ROPE_DIM:
            expand[2 * half + f, 2 * LANES + l] = expand[3 * half + f, 2 * LANES + l] = 1.0
    return tab, jnp.asarray(expand, BF16)


def _proj(x, mod6, g_pre, positions, w_in, tm=512):
    B, S, D = x.shape
    perm = _nsa_perm()
    gate_lo = Q_NSA + 6 * KV_NSA
    n_gate = 3 * N_HEADS_NSA
    w_a = jnp.concatenate([w_in[:, :Q_NSA][:, perm], w_in[:, Q_NSA:gate_lo]], axis=1).astype(BF16)
    w_g = jnp.pad(w_in[:, gate_lo:gate_lo + n_gate], ((0, 0), (0, LANES - n_gate))).astype(BF16)
    w_b = w_in[:, gate_lo + n_gate:].astype(BF16)
    pos = positions.astype(F32).reshape(B, 1, S)
    tab, expand = _rope_const_tables()
    wa_n, wb_n = w_a.shape[1], w_b.shape[1]

    def tok(width, dtype):
        return (pl.BlockSpec((1, tm, width), lambda b, i: (b, i, 0)),
                jax.ShapeDtypeStruct((B, S, width), dtype))

    def tok_t(width, dtype):
        return (pl.BlockSpec((1, width, tm), lambda b, i: (b, 0, i)),
                jax.ShapeDtypeStruct((B, width, S), dtype))

    def dil_view(d):
        return (pl.BlockSpec((1, tm // d, d * Q_DIL), lambda b, i: (b, i, 0)),
                jax.ShapeDtypeStruct((B, S // d, d * Q_DIL), BF16))

    kv16 = (pl.BlockSpec((2, 1, tm // CMP_STRIDE, CMP_STRIDE * KV_NSA), lambda b, i: (0, b, i, 0)),
            jax.ShapeDtypeStruct((2, B, S // CMP_STRIDE, CMP_STRIDE * KV_NSA), F32))
    outs = [tok_t(Q_NSA, BF16), tok_t(Q_NSA, BF16), kv16,
            tok(LANES, BF16), tok_t(LANES, BF16), tok(LANES, BF16), tok_t(LANES, BF16), tok(LANES, F32)]
    outs += [dil_view(d) for _ in range(3) for _, d in DIL_CONFIGS]
    res = pl.pallas_call(
        _proj_kernel,
        grid=(B, S // tm),
        in_specs=[pl.BlockSpec((1, tm, D), lambda b, i: (b, i, 0)),
                  pl.BlockSpec((1, 6, D), lambda b, i: (b, 0, 0)),
                  pl.BlockSpec((1, D), lambda b, i: (0, 0)),
                  pl.BlockSpec((1, 1, tm), lambda b, i: (b, 0, i)),
                  pl.BlockSpec(tab.shape, lambda b, i: (0, 0)),
                  pl.BlockSpec(expand.shape, lambda b, i: (0, 0)),
                  pl.BlockSpec((D, wa_n), lambda b, i: (0, 0)),
                  pl.BlockSpec((D, LANES), lambda b, i: (0, 0)),
                  pl.BlockSpec((D, wb_n), lambda b, i: (0, 0))],
        out_specs=[o[0] for o in outs],
        out_shape=[o[1] for o in outs],
        scratch_shapes=[pltpu.VMEM((tm, LANES), F32)],
        compiler_params=_cparams(("parallel", "parallel")),
        name="proj",
    )(x, mod6, g_pre.reshape(1, D), pos, tab, expand, w_a, w_g, w_b)
    n_cfg = len(DIL_CONFIGS)
    n0 = len(outs) - 3 * n_cfg
    return tuple(res[:n0]) + (res[n0:n0 + n_cfg], res[n0 + n_cfg:n0 + 2 * n_cfg], res[n0 + 2 * n_cfg:])


def _cmpmlp_kernel(a_ref, p_ref, w1_ref, w1x_ref, w2x_ref, o_ref):
    a = a_ref[0, 0].astype(BF16)
    bias = jnp.dot(p_ref[...], w1_ref[0], preferred_element_type=F32)[0:1]
    n16 = a.shape[0]
    out = jnp.zeros((n16, KV_NSA), F32)
    for h in range(N_KV_NSA):
        u = jnp.dot(a, w1x_ref[0, h, 0], preferred_element_type=F32)
        v = jnp.dot(a, w1x_ref[0, h, 1], preferred_element_type=F32)
        hid = jax.nn.gelu(u + pltpu.roll(v, n16 - 1, 0) + bias)
        out = out + jnp.dot(hid.astype(BF16), w2x_ref[0, h], preferred_element_type=F32)
    o_ref[0, 0] = out.astype(BF16)


def _cmpmlp(a, cmp_pos, w_k1, w_k2, w_v1, w_v2):
    _, B, n16, _ = a.shape
    seg = CMP_STRIDE * HEAD_DIM

    def expand1(w):
        w = w.reshape(2, CMP_STRIDE, HEAD_DIM, CMP_HIDDEN)
        z = jnp.zeros_like(w)
        per_head = [jnp.concatenate([w, z] if h == 0 else [z, w], axis=2) for h in range(N_KV_NSA)]
        return jnp.stack(per_head, axis=0).reshape(N_KV_NSA, 2, CMP_STRIDE * KV_NSA, CMP_HIDDEN)

    def expand2(w):
        z = jnp.zeros_like(w)
        return jnp.stack([jnp.concatenate([w, z] if h == 0 else [z, w], axis=1) for h in range(N_KV_NSA)], axis=0)

    w1 = jnp.stack([w_k1, w_v1], axis=0).astype(BF16)
    w1x = jnp.stack([expand1(w_k1), expand1(w_v1)], axis=0).astype(BF16)
    w2x = jnp.stack([expand2(w_k2), expand2(w_v2)], axis=0).astype(BF16)
    p8 = jnp.broadcast_to(cmp_pos.reshape(1, CMP_BLOCK * HEAD_DIM), (8, CMP_BLOCK * HEAD_DIM)).astype(BF16)
    out = pl.pallas_call(
        _cmpmlp_kernel,
        grid=(2, B),
        in_specs=[pl.BlockSpec((1, 1, n16, CMP_STRIDE * KV_NSA), lambda t, b: (t, b, 0, 0)),
                  pl.BlockSpec((8, 2 * seg), lambda t, b: (0, 0)),
                  pl.BlockSpec((1, 2 * seg, CMP_HIDDEN), lambda t, b: (t, 0, 0)),
                  pl.BlockSpec((1, N_KV_NSA, 2, CMP_STRIDE * KV_NSA, CMP_HIDDEN), lambda t, b: (t, 0, 0, 0, 0)),
                  pl.BlockSpec((1, N_KV_NSA, CMP_HIDDEN, KV_NSA), lambda t, b: (t, 0, 0, 0))],
        out_specs=pl.BlockSpec((1, 1, n16, KV_NSA), lambda t, b: (t, b, 0, 0)),
        out_shape=jax.ShapeDtypeStruct((2, B, n16, KV_NSA), BF16),
        compiler_params=_cparams(("parallel", "parallel")),
        name="cmpmlp",
    )(a, p8, w1, w1x, w2x)
    return out[0], out[1]


def _query_groups_t(qt_ref, tq):
    half0 = lax.broadcasted_iota(I32, (LANES, tq), 0) < HEAD_DIM
    zero = jnp.zeros((LANES, tq), BF16)
    groups = []
    for hk in range(N_KV_NSA):
        keep = half0 if hk == 0 else jnp.logical_not(half0)
        for r in range(GROUP_NSA):
            groups.append(jnp.where(keep, qt_ref[0, r * LANES:(r + 1) * LANES, :], zero))
    return groups


def _store_heads_t(ot, tq, o_ref):
    half0 = lax.broadcasted_iota(I32, (tq, LANES), 1) < HEAD_DIM
    for r in range(GROUP_NSA):
        o0 = ot[:, r * tq:(r + 1) * tq].T
        o1 = ot[:, (GROUP_NSA + r) * tq:(GROUP_NSA + r + 1) * tq].T
        o_ref[0, :, r * LANES:(r + 1) * LANES] = jnp.where(half0, o0, o1)


def _cmpsel_kernel(qt_ref, kc_ref, vct_ref, ovt_ref, o_ref, bias_ref, *, tq):
    i = pl.program_id(1)
    n16 = kc_ref.shape[1]
    cols = 8 * tq
    qa = jnp.concatenate(_query_groups_t(qt_ref, tq), axis=1)
    st = jnp.dot(kc_ref[0], qa, preferred_element_type=F32)
    t = i * tq + (lax.broadcasted_iota(I32, (n16, cols), 1) & (tq - 1))
    c = lax.broadcasted_iota(I32, (n16, cols), 0)
    valid = (c * CMP_STRIDE + (CMP_BLOCK - 1)) <= t
    st = jnp.where(valid, st, NEG_INF)
    m = jnp.max(st, axis=0, keepdims=True)
    e = jnp.exp2(st - m)
    l = jnp.sum(e, axis=0, keepdims=True)
    p = jnp.where(valid, e * (1.0 / l), 0.0)
    ot = jnp.dot(vct_ref[0], p.astype(BF16), preferred_element_type=F32)
    _store_heads_t(ot, tq, o_ref)

    n_slc = ovt_ref.shape[0]
    j = lax.broadcasted_iota(I32, (n_slc, tq), 0)
    cur = (i * tq + lax.broadcasted_iota(I32, (n_slc, tq), 1)) >> SLC_SHIFT
    forced = (j == 0) | (j == cur) | (j == cur - 1)
    ovt = ovt_ref[...]
    biases = []
    for hk in range(N_KV_NSA):
        ps = p[:, hk * GROUP_NSA * tq:(hk * GROUP_NSA + 1) * tq]
        for r in range(1, GROUP_NSA):
            ps = ps + p[:, (hk * GROUP_NSA + r) * tq:(hk * GROUP_NSA + r + 1) * tq]
        hi = ps.astype(BF16)
        lo = (ps - hi.astype(F32)).astype(BF16)
        pslc = (jnp.dot(ovt, hi, preferred_element_type=F32)
                + jnp.dot(ovt, lo, preferred_element_type=F32))
        score = jnp.where(forced, FORCE_SCORE, jnp.where(j <= cur, pslc, -1.0))
        rank = jnp.zeros((n_slc, tq), I32)
        for ii in range(n_slc):
            ri = score[ii:ii + 1, :]
            beats = (ri > score) | ((ri == score) & (j > ii))
            rank = rank + beats.astype(I32)
        biases.append(jnp.where(rank < SLC_TOPK, 0.0, NEG_INF))
    bias_ref[0] = jnp.concatenate(biases, axis=0).astype(BF16)


def _overlap_t(S):
    n16 = S // CMP_STRIDE
    n_slc = S // SLC_BLOCK
    cs = np.arange(n16) * CMP_STRIDE
    js = np.arange(n_slc) * SLC_BLOCK
    ov = np.clip(np.minimum(cs[:, None] + CMP_BLOCK, js[None, :] + SLC_BLOCK)
                 - np.maximum(cs[:, None], js[None, :]), 0, None).astype(np.float32) / CMP_BLOCK
    ov[n16 - 1] = 0.0
    return jnp.asarray(ov.T, BF16)


def _cmpsel(q_raw_t, kcf, vcf, tq=256):
    B, _, S = q_raw_t.shape
    n16 = S // CMP_STRIDE
    n_slc = S // SLC_BLOCK
    assert n_slc == HEAD_DIM, "selection bias is laid out as one 64-lane half per kv head"
    return pl.pallas_call(
        functools.partial(_cmpsel_kernel, tq=tq),
        grid=(B, S // tq),
        in_specs=[pl.BlockSpec((1, Q_NSA, tq), lambda b, i: (b, 0, i)),
                  pl.BlockSpec((1, n16, KV_NSA), lambda b, i: (b, 0, 0)),
                  pl.BlockSpec((1, KV_NSA, n16), lambda b, i: (b, 0, 0)),
                  pl.BlockSpec((n_slc, n16), lambda b, i: (0, 0))],
        out_specs=[pl.BlockSpec((1, tq, Q_NSA), lambda b, i: (b, i, 0)),
                   pl.BlockSpec((1, 2 * n_slc, tq), lambda b, i: (b, 0, i))],
        out_shape=[jax.ShapeDtypeStruct((B, S, Q_NSA), F32),
                   jax.ShapeDtypeStruct((B, 2 * n_slc, S), BF16)],
        compiler_params=_cparams(("parallel", "parallel")),
        name="cmpsel",
    )(q_raw_t, kcf, vcf.transpose(0, 2, 1), _overlap_t(S))


def _slc_kernel(qi_ref, kj_ref, qt_ref, bt_ref, k_ref, vt_ref, o_ref, qa_sc, m_sc, acc_sc, *, tq, tk):
    step = pl.program_id(1)
    i = qi_ref[step]
    kj = kj_ref[step]
    last = (i * tq + tq - 1) // tk
    cols = 8 * tq

    @pl.when(kj == 0)
    def _():
        half0 = lax.broadcasted_iota(I32, (LANES, tq), 0) < HEAD_DIM
        zero = jnp.zeros((LANES, tq), BF16)
        bt = bt_ref[0]
        groups = []
        for hk in range(N_KV_NSA):
            keep = half0 if hk == 0 else jnp.logical_not(half0)
            bh = jnp.where(keep, bt, zero)
            for r in range(GROUP_NSA):
                qb = qt_ref[0, r * LANES:(r + 1) * LANES, :]
                groups.append(jnp.concatenate([jnp.where(keep, qb, zero), bh], axis=0))
        qa_sc[...] = jnp.concatenate(groups, axis=1)
        m_sc[...] = jnp.full((1, cols), NEG_INF, F32)
        acc_sc[...] = jnp.zeros(acc_sc.shape, F32)

    def update(on_diagonal):
        kblk = (kj * tk + lax.broadcasted_iota(I32, (tk, LANES), 0)) >> SLC_SHIFT
        lane = lax.broadcasted_iota(I32, (tk, LANES), 1) & (HEAD_DIM - 1)
        onehot = jnp.where(kblk == lane, 1.0, 0.0).astype(BF16)
        kaug = jnp.concatenate([k_ref[0], onehot], axis=1)
        st = jnp.dot(kaug, qa_sc[...], preferred_element_type=F32)
        if on_diagonal:
            kpos = kj * tk + lax.broadcasted_iota(I32, (tk, cols), 0)
            t = i * tq + (lax.broadcasted_iota(I32, (tk, cols), 1) & (tq - 1))
            st = jnp.where(kpos <= t, st, NEG_INF)
        m_old = m_sc[...]
        m_new = jnp.maximum(m_old, jnp.max(st, axis=0, keepdims=True))
        alpha = jnp.exp2(m_old - m_new)
        p = jnp.exp2(st - m_new).astype(BF16)
        vt_ones = jnp.concatenate([vt_ref[0], jnp.ones((DEN_ROWS, tk), BF16)], axis=0)
        acc_sc[...] = alpha * acc_sc[...] + jnp.dot(vt_ones, p, preferred_element_type=F32)
        m_sc[...] = m_new

    @pl.when(kj < last)
    def _():
        update(False)

    @pl.when(kj == last)
    def _():
        update(True)
        acc = acc_sc[...]
        _store_heads_t(acc[:LANES] * (1.0 / acc[LANES:LANES + 1]), tq, o_ref)


def _slc(q_rot_t, bias_t, ks, vs_t, tq=512, tk=512):
    B, _, S = q_rot_t.shape
    assert tk % tq == 0
    qi, kj = [], []
    for i in range(S // tq):
        for j in range((i * tq + tq - 1) // tk + 1):
            qi.append(i)
            kj.append(j)
    grid_spec = pltpu.PrefetchScalarGridSpec(
        num_scalar_prefetch=2,
        grid=(B, len(qi)),
        in_specs=[pl.BlockSpec((1, Q_NSA, tq), lambda b, s, qi, kj: (b, 0, qi[s])),
                  pl.BlockSpec((1, LANES, tq), lambda b, s, qi, kj: (b, 0, qi[s])),
                  pl.BlockSpec((1, tk, KV_NSA), lambda b, s, qi, kj: (b, kj[s], 0)),
                  pl.BlockSpec((1, KV_NSA, tk), lambda b, s, qi, kj: (b, 0, kj[s]))],
        out_specs=pl.BlockSpec((1, tq, Q_NSA), lambda b, s, qi, kj: (b, qi[s], 0)),
        scratch_shapes=[pltpu.VMEM((2 * LANES, 8 * tq), BF16),
                        pltpu.VMEM((1, 8 * tq), F32),
                        pltpu.VMEM((LANES + DEN_ROWS, 8 * tq), F32)],
    )
    return pl.pallas_call(
        functools.partial(_slc_kernel, tq=tq, tk=tk),
        grid_spec=grid_spec,
        out_shape=jax.ShapeDtypeStruct((B, S, Q_NSA), F32),
        compiler_params=_cparams(("parallel", "arbitrary")),
        name="slc",
    )(jnp.asarray(qi, I32), jnp.asarray(kj, I32), q_rot_t, bias_t, ks, vs_t)


def _win_kernel(qt_ref, k0_ref, k1_ref, k2_ref, v0_ref, v1_ref, v2_ref, o_ref, *, tq, tk, max_dist):
    i = pl.program_id(1)
    a = (i * tq) // tk
    cols = 8 * tq
    qa = jnp.concatenate(_query_groups_t(qt_ref, tq), axis=1)
    t = i * tq + (lax.broadcasted_iota(I32, (tk, cols), 1) & (tq - 1))
    row = lax.broadcasted_iota(I32, (tk, cols), 0)
    sts = []
    for jt, k_ref in enumerate((k0_ref, k1_ref, k2_ref)):
        st = jnp.dot(k_ref[0], qa, preferred_element_type=F32)
        kpos = (a - 2 + jt) * tk + row
        if jt == 0:
            st = jnp.where((t - kpos <= max_dist) & (kpos >= 0), st, NEG_INF)
        elif jt == 1:
            st = jnp.where(kpos >= 0, st, NEG_INF)
        else:
            st = jnp.where(kpos <= t, st, NEG_INF)
        sts.append(st)
    m = jnp.max(sts[0], axis=0, keepdims=True)
    for st in sts[1:]:
        m = jnp.maximum(m, jnp.max(st, axis=0, keepdims=True))
    acc = jnp.zeros((LANES + DEN_ROWS, cols), F32)
    for st, v_ref in zip(sts, (v0_ref, v1_ref, v2_ref)):
        vt_ones = jnp.concatenate([v_ref[0], jnp.ones((DEN_ROWS, tk), BF16)], axis=0)
        acc = acc + jnp.dot(vt_ones, jnp.exp2(st - m).astype(BF16), preferred_element_type=F32)
    _store_heads_t(acc[:LANES] * (1.0 / acc[LANES:LANES + 1]), tq, o_ref)


def _win(q_rot_t, kw, vw_t, tq=256, tk=256):
    B, _, S = q_rot_t.shape
    max_dist = WIN_NSA - 1
    assert tk % tq == 0 and 2 * tk == max_dist + 1

    def k_map(jt):
        return lambda b, i: (b, jnp.maximum((i * tq) // tk - 2 + jt, 0), 0)

    def v_map(jt):
        return lambda b, i: (b, 0, jnp.maximum((i * tq) // tk - 2 + jt, 0))

    return pl.pallas_call(
        functools.partial(_win_kernel, tq=tq, tk=tk, max_dist=max_dist),
        grid=(B, S // tq),
        in_specs=[pl.BlockSpec((1, Q_NSA, tq), lambda b, i: (b, 0, i))]
        + [pl.BlockSpec((1, tk, KV_NSA), k_map(jt)) for jt in range(3)]
        + [pl.BlockSpec((1, KV_NSA, tk), v_map(jt)) for jt in range(3)],
        out_specs=pl.BlockSpec((1, tq, Q_NSA), lambda b, i: (b, i, 0)),
        out_shape=jax.ShapeDtypeStruct((B, S, Q_NSA), F32),
        compiler_params=_cparams(("parallel", "parallel")),
        name="win",
    )(q_rot_t, kw, kw, kw, vw_t, vw_t, vw_t)


def _band_kernel(*refs, tq, tk, nt, max_dist):
    q_ref = refs[0]
    k_refs = refs[1:1 + nt]
    v_refs = refs[1 + nt:1 + 2 * nt]
    o_ref = refs[1 + 2 * nt]
    lse_ref = refs[2 + 2 * nt]
    i = pl.program_id(2)
    a = (i * tq) // tk
    lane = lax.broadcasted_iota(I32, (tq, LANES), 1)
    half0 = lane < HEAD_DIM
    row_t = i * tq + (lax.broadcasted_iota(I32, (2 * tq, nt * tk), 0) & (tq - 1))
    kpos = (a - (nt - 1)) * tk + lax.broadcasted_iota(I32, (2 * tq, nt * tk), 1)
    dist = row_t - kpos
    ok = (dist >= 0) & (dist <= max_dist) & (kpos >= 0)
    ones = jnp.ones((nt * tk, LANES), BF16)
    lse_acc = jnp.zeros((tq, LANES), F32)
    for blk in range(Q_DIL // LANES):
        lanes = slice(blk * LANES, (blk + 1) * LANES)
        qb = q_ref[0, :, lanes]
        zero = jnp.zeros_like(qb)
        qs = jnp.concatenate([jnp.where(half0, qb, zero), jnp.where(half0, zero, qb)], axis=0)
        kcat = jnp.concatenate([k_ref[0, :, lanes] for k_ref in k_refs], axis=0)
        vcat = jnp.concatenate([v_ref[0, :, lanes] for v_ref in v_refs], axis=0)
        s = lax.dot_general(qs, kcat, _NT, preferred_element_type=F32)
        s = jnp.where(ok, s, NEG_INF)
        m = jnp.max(s, axis=-1, keepdims=True)
        p = jnp.exp(s - m).astype(BF16)
        ol = jnp.dot(p, jnp.concatenate([vcat, ones], axis=1), preferred_element_type=F32)
        den = ol[:, LANES:]
        o = ol[:, :LANES] * (1.0 / den)
        o_ref[0, :, lanes] = jnp.where(half0, o[:tq], o[tq:])
        lse = m + jnp.log(den[:, 0:1])
        lse_acc = jnp.where(lane == 2 * blk, lse[:tq], lse_acc)
        lse_acc = jnp.where(lane == 2 * blk + 1, lse[tq:], lse_acc)
    lse_ref[0] = lse_acc


def _band(q, k, v, *, dil, max_dist, tq, tk, nt):
    B, m, _ = q.shape

    def kv_map(jt):
        return lambda b, r, i: (b, jnp.maximum((i * tq) // tk - (nt - 1) + jt, 0), r)

    tok_spec = pl.BlockSpec((1, tq, Q_DIL), lambda b, r, i: (b, i, r))
    kv_specs = [pl.BlockSpec((1, tk, Q_DIL), kv_map(jt)) for jt in range(nt)]
    return pl.pallas_call(
        functools.partial(_band_kernel, tq=tq, tk=tk, nt=nt, max_dist=max_dist),
        grid=(B, dil, m // tq),
        in_specs=[tok_spec] + kv_specs + kv_specs,
        out_specs=[tok_spec, pl.BlockSpec((1, tq, LANES), lambda b, r, i: (b, i, r))],
        out_shape=[jax.ShapeDtypeStruct((B, m, dil * Q_DIL), F32),
                   jax.ShapeDtypeStruct((B, m, dil * LANES), F32)],
        compiler_params=_cparams(("parallel", "parallel", "parallel")),
        name=f"band_d{dil}_w{max_dist}",
    )(q, *([k] * nt), *([v] * nt))


def _out_kernel(ocmp_ref, oslc_ref, owin_ref, gt_ref, od1_ref, od4_ref, od16_ref, l1_ref, l4_ref, l16_ref,
                x_ref, mod_ref, eg_ref, ed_ref, gnsa_ref, gdil_ref, wo_ref, gpost_ref, o_ref, st_sc):
    tm = x_ref.shape[1]

    def token_order(ref, d, n_blk):
        if d == 1:
            return ref[0]
        blocks = []
        for blk in range(n_blk):
            for r in range(d):
                lo = (r * n_blk + blk) * LANES
                st_sc[pl.ds(r, tm // d, stride=d), :] = ref[0, :, lo:lo + LANES]
            blocks.append(st_sc[...])
        return blocks[0] if n_blk == 1 else jnp.concatenate(blocks, axis=1)

    sg = jax.nn.sigmoid(gt_ref[0])
    g3 = _expand_dot(sg, eg_ref[...])
    oa = (g3[:, :Q_NSA] * ocmp_ref[0] + g3[:, Q_NSA:2 * Q_NSA] * oslc_ref[0] + g3[:, 2 * Q_NSA:] * owin_ref[0])
    ya = _rms(oa, gnsa_ref[...])

    dils = [d for _, d in DIL_CONFIGS]
    n_blk = Q_DIL // LANES
    l1, l4, l16 = [token_order(ref, d, 1) for ref, d in zip((l1_ref, l4_ref, l16_ref), dils)]
    mx = jnp.maximum(jnp.maximum(l1, l4), l16)
    e1, e4, e16 = jnp.exp(l1 - mx), jnp.exp(l4 - mx), jnp.exp(l16 - mx)
    den = e1 + e4 + e16
    ed = ed_ref[...]
    od1, od4, od16 = [token_order(ref, d, n_blk) for ref, d in zip((od1_ref, od4_ref, od16_ref), dils)]
    inv = 1.0 / den
    ob = (_expand_dot(e1 * inv, ed) * od1 + _expand_dot(e4 * inv, ed) * od4 + _expand_dot(e16 * inv, ed) * od16)
    yb = _rms(ob, gdil_ref[...])

    y = jnp.concatenate([ya, yb], axis=1).astype(BF16)
    z = jnp.dot(y, wo_ref[...], preferred_element_type=F32)
    gt_m = mod_ref[0, 2:3, :]
    o_ref[0] = x_ref[0] + gt_m * _rms(z, gpost_ref[...])


def _gate_expanders():
    perm = _nsa_perm()
    eg = np.zeros((LANES, 3 * Q_NSA), np.float32)
    for lane_out, col in enumerate(perm):
        hq = col // HEAD_DIM
        for c in range(3):
            eg[hq * 3 + c, c * Q_NSA + lane_out] = 1.0
    ed = np.zeros((LANES, Q_DIL), np.float32)
    for h in range(N_HEADS_DIL):
        ed[h, h * HEAD_DIM:(h + 1) * HEAD_DIM] = 1.0
    return jnp.asarray(np.concatenate([eg, eg]), BF16), jnp.asarray(np.concatenate([ed, ed]), BF16)


def _out(o_cmp, o_slc, o_win, gates, ods, lses, x, mod6, g_out_nsa, g_out_dil, w_o, g_post, tm=1024):
    B, S, D = x.shape
    perm = _nsa_perm()
    eg, ed = _gate_expanders()
    w_o_p = jnp.concatenate([w_o[:Q_NSA][perm], w_o[Q_NSA:]], axis=0).astype(BF16)
    g_nsa_p = g_out_nsa[perm].reshape(1, Q_NSA)

    def tok(width):
        return pl.BlockSpec((1, tm, width), lambda b, i: (b, i, 0))

    def const(shape):
        return pl.BlockSpec(shape, lambda b, i: (0,) * len(shape))

    def view(d, width):
        return pl.BlockSpec((1, tm // d, d * width), lambda b, i: (b, i, 0))

    dils = [d for _, d in DIL_CONFIGS]
    return pl.pallas_call(
        _out_kernel,
        grid=(B, S // tm),
        in_specs=[tok(Q_NSA), tok(Q_NSA), tok(Q_NSA), tok(LANES)]
        + [view(d, Q_DIL) for d in dils] + [view(d, LANES) for d in dils]
        + [tok(D), pl.BlockSpec((1, 6, D), lambda b, i: (b, 0, 0)),
           const(eg.shape), const(ed.shape), const((1, Q_NSA)), const((1, Q_DIL)),
           const((D, D)), const((1, D))],
        out_specs=tok(D),
        out_shape=jax.ShapeDtypeStruct((B, S, D), F32),
        scratch_shapes=[pltpu.VMEM((tm, LANES), F32)],
        compiler_params=_cparams(("parallel", "parallel")),
        name="out",
    )(o_cmp, o_slc, o_win, gates, *ods, *lses, x, mod6, eg, ed, g_nsa_p, g_out_dil.reshape(1, Q_DIL),
      w_o_p, g_post.reshape(1, D))


def _route_kernel(x_ref, mod_ref, g_ref, wr_ref, br_ref, h_ref, idx_ref, gate_ref, rank_ref, cnt_ref, run_sc):
    i = pl.program_id(0)

    @pl.when(i == 0)
    def _():
        run_sc[...] = jnp.zeros_like(run_sc)

    sh = mod_ref[0, 3:4, :]
    sc = mod_ref[0, 4:5, :]
    h = _rms(x_ref[...], g_ref[...]) * (1.0 + sc) + sh
    h_ref[...] = h
    logits = _dot3(h, wr_ref[...]) + br_ref[...]
    tm = logits.shape[0]
    lane = lax.broadcasted_iota(I32, (tm, LANES), 1)
    work = jnp.where(lane < N_EXPERTS, logits, -jnp.inf)
    idx_out = jnp.zeros((tm, LANES), I32)
    val_out = jnp.full((tm, LANES), -jnp.inf, F32)
    rank_out = jnp.zeros((tm, LANES), F32)
    tri = jnp.where(lax.broadcasted_iota(I32, (tm, tm), 1) < lax.broadcasted_iota(I32, (tm, tm), 0),
                    1.0, 0.0).astype(BF16)
    onehots = []
    for k in range(TOP_K):
        mx = jnp.max(work, axis=-1, keepdims=True)
        ix = jnp.min(jnp.where(work == mx, lane, LANES), axis=-1, keepdims=True)
        hit = lane == ix
        idx_out = jnp.where(lane == k, ix, idx_out)
        val_out = jnp.where(lane == k, mx, val_out)
        work = jnp.where(hit, -jnp.inf, work)
        onehots.append(jnp.where(hit, 1.0, 0.0))
    earlier = jnp.dot(tri, jnp.concatenate(onehots, axis=1).astype(BF16), preferred_element_type=F32)
    base = run_sc[0:1, :]
    for k, onehot in enumerate(onehots):
        before = earlier[:, k * LANES:(k + 1) * LANES] + base
        rank_k = jnp.sum(onehot * before, axis=-1, keepdims=True)
        rank_out = jnp.where(lane == k, rank_k, rank_out)
        base = base + jnp.sum(onehot, axis=0, keepdims=True)
    run_sc[...] = jnp.broadcast_to(base, run_sc.shape)
    e = jnp.exp(val_out - val_out[:, 0:1])
    idx_ref[...] = idx_out
    gate_ref[...] = e / jnp.sum(e, axis=-1, keepdims=True)
    rank_ref[...] = rank_out.astype(I32)
    cnt_ref[...] = run_sc[...].astype(I32)


def _route(x1, mod6, g_pre_ffn, w_router, b_router, tm=512):
    B, S, D = x1.shape
    N = B * S
    spb = S // tm
    wr = jnp.pad(w_router, ((0, 0), (0, LANES - N_EXPERTS)))
    br = jnp.pad(b_router, (0, LANES - N_EXPERTS)).reshape(1, LANES)

    def tok(width):
        return pl.BlockSpec((tm, width), lambda i: (i, 0))

    return pl.pallas_call(
        _route_kernel,
        grid=(N // tm,),
        in_specs=[tok(D), pl.BlockSpec((1, 6, D), lambda i: (i // spb, 0, 0)),
                  pl.BlockSpec((1, D), lambda i: (0, 0)),
                  pl.BlockSpec((D, LANES), lambda i: (0, 0)),
                  pl.BlockSpec((1, LANES), lambda i: (0, 0))],
        out_specs=[tok(D), tok(LANES), tok(LANES), tok(LANES), pl.BlockSpec((8, LANES), lambda i: (0, 0))],
        out_shape=[jax.ShapeDtypeStruct((N, D), F32),
                   jax.ShapeDtypeStruct((N, LANES), I32),
                   jax.ShapeDtypeStruct((N, LANES), F32),
                   jax.ShapeDtypeStruct((N, LANES), I32),
                   jax.ShapeDtypeStruct((8, LANES), I32)],
        scratch_shapes=[pltpu.VMEM((8, LANES), F32)],
        compiler_params=_cparams(("arbitrary",)),
        name="route",
    )(x1.reshape(N, D), mod6, g_pre_ffn.reshape(1, D), wr, br)


def _dispatch_kernel(pos_ref, pad0_ref, padn_ref, nt_ref, h_ref, xs_out, zbuf, sem, zsem, *, tm, tmx, n_tiles):
    i = pl.program_id(0)
    bits = tmx.bit_length() - 1

    def pad_copies(e, fn):
        p0 = pad0_ref[e]
        head = (-p0) & (SUBLANES - 1)
        head = jnp.minimum(head, padn_ref[e])
        for r in range(SUBLANES - 1):
            @pl.when(r < head)
            def _():
                fn(pltpu.make_async_copy(zbuf.at[pl.ds(0, 1), :], xs_out.at[pl.ds(p0 + r, 1), :], zsem))
        a = p0 + head
        n = padn_ref[e] - head
        for b in range(3, bits):
            size = 1 << b
            off = (n >> (b + 1)) << (b + 1)

            @pl.when((n & size) != 0)
            def _():
                fn(pltpu.make_async_copy(zbuf.at[pl.ds(0, size), :],
                                         xs_out.at[pl.ds(pl.multiple_of(a + off, SUBLANES), size), :], zsem))

    def tail_copy(t, fn):
        fn(pltpu.make_async_copy(zbuf, xs_out.at[pl.ds(pl.multiple_of(t * tmx, tmx), tmx), :], zsem))

    def for_all_fill(fn):
        def per_expert(e, carry):
            pad_copies(e, fn)
            return carry
        lax.fori_loop(0, N_EXPERTS, per_expert, 0)

        def per_tile(t, carry):
            tail_copy(t, fn)
            return carry
        lax.fori_loop(nt_ref[0], n_tiles, per_tile, 0)

    @pl.when(i == 0)
    def _():
        zbuf[...] = jnp.zeros_like(zbuf)
        for_all_fill(lambda cp: cp.start())
        for_all_fill(lambda cp: cp.wait())

    def body(r, carry):
        for k in range(TOP_K):
            dst_row = pos_ref[(i * tm + r) * TOP_K + k]
            pltpu.make_async_copy(h_ref.at[pl.ds(r, 1), :], xs_out.at[pl.ds(dst_row, 1), :], sem).start()
        return carry

    lax.fori_loop(0, tm, body, 0, unroll=DMA_UNROLL // TOP_K)
    for k in range(TOP_K):
        pltpu.make_async_copy(h_ref, xs_out.at[pl.ds(0, tm), :], sem).wait()


def _dispatch(h2, pos_flat, pad_start, pad_len, n_valid, n_tiles, tmx, tm=1024):
    N, D = h2.shape
    assert tmx & (tmx - 1) == 0
    grid_spec = pltpu.PrefetchScalarGridSpec(
        num_scalar_prefetch=4,
        grid=(N // tm,),
        in_specs=[pl.BlockSpec((tm, D), lambda i, *_: (i, 0))],
        out_specs=pl.BlockSpec(memory_space=pl.ANY),
        scratch_shapes=[pltpu.VMEM((tmx, D), F32), pltpu.SemaphoreType.DMA(()), pltpu.SemaphoreType.DMA(())],
    )
    return pl.pallas_call(
        functools.partial(_dispatch_kernel, tm=tm, tmx=tmx, n_tiles=n_tiles),
        grid_spec=grid_spec,
        out_shape=jax.ShapeDtypeStruct((n_tiles * tmx, D), F32),
        compiler_params=_cparams(("arbitrary",)),
        name="dispatch",
    )(pos_flat, pad_start, pad_len, n_valid, h2)


def _moe_kernel(te_ref, nt_ref, nx_ref, sl_ref, rows_ref, xs_ref, wup_hbm, bup_ref, wdn_hbm, bdn_ref, pm_ref, y_ref,
                wup_in, wdn_in, wup_sc, wdn_sc, sems):
    i = pl.program_id(0)
    n_valid = nt_ref[0]
    e = te_ref[i]
    new_expert = (i == 0) | (e != te_ref[jnp.maximum(i - 1, 0)])
    slot = sl_ref[i]
    n_blk = wup_sc.shape[1] // (2 * LANES)

    def weight_copies(ex, s):
        return (pltpu.make_async_copy(wup_hbm.at[ex], wup_in.at[s], sems.at[0, s]),
                pltpu.make_async_copy(wdn_hbm.at[ex], wdn_in.at[s], sems.at[1, s]))

    @pl.when(i == 0)
    def _():
        for cp in weight_copies(e, slot):
            cp.start()

    @pl.when(new_expert & (i < n_valid))
    def _():
        for cp in weight_copies(e, slot):
            cp.wait()
        nx = nx_ref[i]

        @pl.when(nx >= 0)
        def _():
            for cp in weight_copies(nx, 1 - slot):
                cp.start()

        pm = pm_ref[...]
        for blk in range(n_blk):
            cols = slice(blk * 2 * LANES, (blk + 1) * 2 * LANES)
            w = wup_in[slot, :, cols].astype(BF16)
            wup_sc[:, cols] = jnp.dot(w, pm, preferred_element_type=F32).astype(BF16)
        wdn_sc[...] = wdn_in[slot].astype(BF16)

    def expert_mlp(n_rows):
        xs = xs_ref[0:n_rows, :].astype(BF16)
        u = jnp.dot(xs, wup_sc[...], preferred_element_type=F32) + bup_ref[0]
        acts = []
        for blk in range(n_blk):
            ug = jnp.minimum(u[:, blk * 2 * LANES:blk * 2 * LANES + LANES], SWIGLU_LIMIT)
            ul = jnp.clip(u[:, blk * 2 * LANES + LANES:(blk + 1) * 2 * LANES], -SWIGLU_LIMIT, SWIGLU_LIMIT)
            acts.append((ug * jax.nn.sigmoid(SWIGLU_ALPHA * ug) * (ul + 1.0)).astype(BF16))
        act = jnp.concatenate(acts, axis=1)
        y_ref[0:n_rows, :] = jnp.dot(act, wdn_sc[...], preferred_element_type=F32) + bdn_ref[0]

    tm = y_ref.shape[0]
    rows = rows_ref[i]
    for part in range(1, MOE_PARTS + 1):
        n_rows = part * tm // MOE_PARTS

        @pl.when((i < n_valid) & (rows > n_rows - tm // MOE_PARTS) & (rows <= n_rows))
        def _():
            expert_mlp(n_rows)
            if n_rows < tm:
                y_ref[n_rows:, :] = jnp.zeros((tm - n_rows, y_ref.shape[1]), F32)

    @pl.when(i >= n_valid)
    def _():
        y_ref[...] = jnp.zeros_like(y_ref)


def _moe_layout(top_idx, rank, counts, tm):
    N = top_idx.shape[0]
    n_tiles = N * TOP_K // tm + N_EXPERTS
    e_ids = jnp.arange(N_EXPERTS, dtype=I32)
    tiles_e = (counts + tm - 1) // tm
    tile_end = jnp.sum(jnp.where(e_ids[None, :] <= e_ids[:, None], tiles_e[None, :], 0), axis=1)
    start = (tile_end - tiles_e) * tm
    pos = rank + jnp.sum(jnp.where(top_idx[:, :, None] == e_ids[None, None, :], start[None, None, :], 0), axis=-1)
    n_valid = tile_end[N_EXPERTS - 1]
    tile_ids = jnp.arange(n_tiles, dtype=I32)
    tile_e = jnp.sum((jnp.minimum(tile_ids, n_valid - 1)[:, None] >= tile_end[None, :]).astype(I32), axis=1)
    prev_e = jnp.concatenate([jnp.full((1,), -1, I32), tile_e[:-1]])
    new = (tile_e != prev_e) & (tile_ids < n_valid)
    ordinal = jnp.sum(jnp.where(tile_ids[None, :] <= tile_ids[:, None], new[None, :].astype(I32), 0), axis=1) - 1
    next_first = jnp.sum(jnp.where(tile_e[:, None] == e_ids[None, :], tile_end[None, :], 0), axis=1)
    next_e = jnp.sum(jnp.where(next_first[:, None] == tile_ids[None, :], tile_e[None, :], 0), axis=1)
    next_e = jnp.where(next_first < n_valid, next_e, -1)
    is_e = tile_e[:, None] == e_ids[None, :]
    rows_end = jnp.sum(jnp.where(is_e, (start + counts)[None, :], 0), axis=1)
    tile_rows = jnp.clip(rows_end - tile_ids * tm, 0, tm)
    meta = dict(tile_e=tile_e.astype(I32), n_valid=n_valid.astype(I32).reshape(1), next_e=next_e.astype(I32),
                slot=(ordinal & 1).astype(I32), tile_rows=tile_rows.astype(I32),
                pad_start=(start + counts).astype(I32), pad_len=(tiles_e * tm - counts).astype(I32))
    return pos.astype(I32), meta, n_tiles


def _glu_perm():
    pm = np.zeros((2 * LANES, 2 * LANES), np.float32)
    for j in range(LANES):
        pm[2 * j, j] = 1.0
        pm[2 * j + 1, LANES + j] = 1.0
    return jnp.asarray(pm, BF16)


def _moe(xs, meta, w_up, b_up, w_down, b_down, tm):
    P, D = xs.shape
    n_blk = D_FF // LANES
    b_up_p = b_up.reshape(N_EXPERTS, n_blk, LANES, 2).transpose(0, 1, 3, 2).reshape(N_EXPERTS, 1, 2 * D_FF)
    b_dn = b_down.reshape(N_EXPERTS, 1, D)

    def bmap(i, te, *_):
        return (te[i], 0, 0)

    grid_spec = pltpu.PrefetchScalarGridSpec(
        num_scalar_prefetch=5,
        grid=(P // tm,),
        in_specs=[pl.BlockSpec((tm, D), lambda i, *_: (i, 0)),
                  pl.BlockSpec(memory_space=pl.ANY), pl.BlockSpec((1, 1, 2 * D_FF), bmap),
                  pl.BlockSpec(memory_space=pl.ANY), pl.BlockSpec((1, 1, D), bmap),
                  pl.BlockSpec((2 * LANES, 2 * LANES), lambda i, *_: (0, 0))],
        out_specs=pl.BlockSpec((tm, D), lambda i, *_: (i, 0)),
        scratch_shapes=[pltpu.VMEM((2, D, 2 * D_FF), F32), pltpu.VMEM((2, D_FF, D), F32),
                        pltpu.VMEM((D, 2 * D_FF), BF16), pltpu.VMEM((D_FF, D), BF16),
                        pltpu.SemaphoreType.DMA((2, 2))],
    )
    return pl.pallas_call(
        _moe_kernel,
        grid_spec=grid_spec,
        out_shape=jax.ShapeDtypeStruct((P, D), F32),
        compiler_params=_cparams(("arbitrary",)),
        name="moe",
    )(meta["tile_e"], meta["n_valid"], meta["next_e"], meta["slot"], meta["tile_rows"], xs, w_up, b_up_p, w_down,
      b_dn, _glu_perm())


def _final_kernel(pos_ref, ys_hbm, x_ref, gate_ref, gtf_ref, g_ref, o_ref, buf, sems, *, tm, n_steps):
    i = pl.program_id(0)
    slot = i & 1

    n = TOP_K * tm

    def gather(step, dst, sem):
        def body(j, carry):
            pltpu.make_async_copy(ys_hbm.at[pl.ds(pos_ref[step * n + j], 1), :], dst.at[pl.ds(j, 1), :],
                                  sem).start()
            return carry
        lax.fori_loop(0, n, body, 0, unroll=DMA_UNROLL)

    @pl.when(i == 0)
    def _():
        gather(0, buf.at[0], sems.at[0])

    @pl.when(i + 1 < n_steps)
    def _():
        gather(i + 1, buf.at[1 - slot], sems.at[1 - slot])

    pltpu.make_async_copy(ys_hbm.at[pl.ds(0, n), :], buf.at[slot], sems.at[slot]).wait()
    gate = gate_ref[...]
    y = gate[:, 0:1] * buf[slot, 0:tm, :]
    for k in range(1, TOP_K):
        y = y + gate[:, k:k + 1] * buf[slot, k * tm:(k + 1) * tm, :]
    o_ref[...] = x_ref[...] + gtf_ref[0, 5:6, :] * _rms(y, g_ref[...])


def _final(ys, pos_flat, gate, x1, mod6, g_post_ffn, tm=512):
    B, S, D = x1.shape
    N = B * S
    n_steps = N // tm
    steps_per_b = S // tm
    pos_sm = pos_flat.reshape(n_steps, tm, TOP_K).transpose(0, 2, 1).reshape(N * TOP_K)
    grid_spec = pltpu.PrefetchScalarGridSpec(
        num_scalar_prefetch=1,
        grid=(n_steps,),
        in_specs=[pl.BlockSpec(memory_space=pl.ANY),
                  pl.BlockSpec((tm, D), lambda i, p: (i, 0)),
                  pl.BlockSpec((tm, LANES), lambda i, p: (i, 0)),
                  pl.BlockSpec((1, 6, D), lambda i, p: (i // steps_per_b, 0, 0)),
                  pl.BlockSpec((1, D), lambda i, p: (0, 0))],
        out_specs=pl.BlockSpec((tm, D), lambda i, p: (i, 0)),
        scratch_shapes=[pltpu.VMEM((2, TOP_K * tm, D), F32), pltpu.SemaphoreType.DMA((2,))],
    )
    out = pl.pallas_call(
        functools.partial(_final_kernel, tm=tm, n_steps=n_steps),
        grid_spec=grid_spec,
        out_shape=jax.ShapeDtypeStruct((N, D), F32),
        compiler_params=_cparams(("arbitrary",)),
        name="final",
    )(pos_sm, ys, x1.reshape(N, D), gate, mod6, g_post_ffn.reshape(1, D))
    return out.reshape(B, S, D)


def _layer(x, c, positions, w_ada, b_ada, g_pre_mix, g_post_mix, g_pre_ffn, g_post_ffn,
           w_in, cmp_pos, w_cmp_k1, w_cmp_k2, w_cmp_v1, w_cmp_v2, g_out_nsa, g_out_dil, w_o,
           w_router, b_router, w_up, b_up, w_down, b_down):
    B, S, D = x.shape
    mod6 = _ada(c, w_ada, b_ada).reshape(B, 6, D)
    (q_raw_t, q_rot_t, kv16, ks, vs_t, kw, vw_t, gates, qbs, kbs, vbs) = _proj(x, mod6, g_pre_mix, positions, w_in)
    kcf, vcf = _cmpmlp(kv16, cmp_pos, w_cmp_k1, w_cmp_k2, w_cmp_v1, w_cmp_v2)
    o_cmp, bias_t = _cmpsel(q_raw_t, kcf, vcf)
    o_slc = _slc(q_rot_t, bias_t, ks, vs_t)
    o_win = _win(q_rot_t, kw, vw_t)
    ods, lses = [], []
    for (window, dil), qb, kb, vb in zip(DIL_CONFIGS, qbs, kbs, vbs):
        o, lse = _band(qb, kb, vb, dil=dil, max_dist=window // dil, tq=128, tk=128, nt=2)
        ods.append(o)
        lses.append(lse)
    x1 = _out(o_cmp, o_slc, o_win, gates, ods, lses, x, mod6, g_out_nsa, g_out_dil, w_o, g_post_mix)
    h2, top_idx, gate, rank, counts = _route(x1, mod6, g_pre_ffn, w_router, b_router)
    tm_moe = 512
    pos, meta, n_tiles = _moe_layout(top_idx[:, :TOP_K], rank[:, :TOP_K], counts[0, :N_EXPERTS], tm_moe)
    pos_flat = pos.reshape(B * S * TOP_K)
    xs = _dispatch(h2, pos_flat, meta["pad_start"], meta["pad_len"], meta["n_valid"], n_tiles, tm_moe)
    ys = _moe(xs, meta, w_up, b_up, w_down, b_down, tm_moe)
    return _final(ys, pos_flat, gate, x1, mod6, g_post_ffn)


def kernel(x, c, positions, w_ada, b_ada, g_pre_mix, g_post_mix, g_pre_ffn, g_post_ffn, w_in, cmp_pos,
           w_cmp_k1, w_cmp_k2, w_cmp_v1, w_cmp_v2, g_out_nsa, g_out_dil, w_o, w_router, b_router,
           w_up, b_up, w_down, b_down):
    depth = w_ada.shape[0]
    for l in range(depth):
        x = _layer(x, c, positions, w_ada[l], b_ada[l], g_pre_mix[l], g_post_mix[l], g_pre_ffn[l],
                   g_post_ffn[l], w_in[l], cmp_pos[l], w_cmp_k1[l], w_cmp_k2[l], w_cmp_v1[l], w_cmp_v2[l],
                   g_out_nsa[l], g_out_dil[l], w_o[l], w_router[l], b_router[l], w_up[l], b_up[l],
                   w_down[l], b_down[l])
    return x
```

```python
import functools

import numpy as np
import jax
import jax.numpy as jnp
from jax import lax
from jax.experimental import pallas as pl
from jax.experimental.pallas import tpu as pltpu

F32 = jnp.float32
BF16 = jnp.bfloat16
I32 = jnp.int32

D_MODEL = 1024
HEAD_DIM = 64
N_HEADS_NSA = 8
N_KV_NSA = 2
GROUP_NSA = 4
N_HEADS_DIL = 8
ROPE_THETA = 500000.0
ROPE_DIM = 16
ROPE_HALF = 8
CMP_BLOCK = 32
CMP_STRIDE = 16
CMP_HIDDEN = 256
SLC_BLOCK = 64
SLC_SHIFT = 6
SLC_TOPK = 16
WIN_NSA = 512
DIL_CONFIGS = ((128, 1), (512, 4), (2048, 16))
N_EXPERTS = 32
TOP_K = 4
D_FF = 1024
SWIGLU_LIMIT = 7.0
SWIGLU_ALPHA = 1.702
RMS_EPS = 1e-6
NEG_INF = -1e30
FORCE_SCORE = 1e9
SCALE = HEAD_DIM ** -0.5
LOG2E = 1.4426950408889634
DEN_ROWS = 16

Q_NSA = 512
KV_NSA = 128
Q_DIL = 512
LANES = 128
SUBLANES = 8
VMEM_LIMIT = 56 * 1024 * 1024
DMA_UNROLL = 32
MOE_PARTS = 4

_NT = (((1,), (1,)), ((), ()))


def _cparams(sem):
    return pltpu.CompilerParams(dimension_semantics=sem, vmem_limit_bytes=VMEM_LIMIT)


def _rms(x, g):
    return x * lax.rsqrt(jnp.mean(x * x, axis=-1, keepdims=True) + RMS_EPS) * g


def _hi_lo(a):
    hi = a.astype(BF16)
    return hi, (a - hi.astype(F32)).astype(BF16)


def _expand_dot(a, e2):
    return jnp.dot(jnp.concatenate(_hi_lo(a), axis=1), e2, preferred_element_type=F32)


def _dot3(a, b):
    ah, al = _hi_lo(a)
    bh, bl = _hi_lo(b)
    return (jnp.dot(ah, bh, preferred_element_type=F32) + jnp.dot(ah, bl, preferred_element_type=F32)
            + jnp.dot(al, bh, preferred_element_type=F32))


def _ada_kernel(c_ref, w_ref, b_ref, o_ref):
    c = c_ref[...]
    a = c * jax.nn.sigmoid(c)
    o_ref[...] = _dot3(a, w_ref[...]) + b_ref[...]


def _ada(c, w_ada, b_ada):
    B, D = c.shape
    n = w_ada.shape[1] // D
    return pl.pallas_call(
        _ada_kernel,
        grid=(n,),
        in_specs=[pl.BlockSpec((B, D), lambda j: (0, 0)),
                  pl.BlockSpec((D, D), lambda j: (0, j)),
                  pl.BlockSpec((1, D), lambda j: (0, j))],
        out_specs=pl.BlockSpec((B, D), lambda j: (0, j)),
        out_shape=jax.ShapeDtypeStruct((B, n * D), F32),
        compiler_params=_cparams(("arbitrary",)),
        name="ada",
    )(c, w_ada, b_ada.reshape(1, -1))


def _rope_tables(pos_row, inv_col, pass_row, expand):
    ang = inv_col * pos_row
    parts = []
    for v in (jnp.cos(ang), jnp.sin(ang)):
        hi = v.astype(BF16)
        parts += [hi, (v - hi.astype(F32)).astype(BF16)]
    tab_t = jnp.concatenate(parts, axis=0)
    out = lax.dot_general(tab_t, expand, (((0,), (0,)), ((), ())), preferred_element_type=F32)
    return out[:, :LANES] + pass_row, out[:, LANES:2 * LANES], out[:, 2 * LANES:]


def _rope_blk(x, cs, s1, s2):
    return x * cs + pltpu.roll(x, LANES - ROPE_HALF, 1) * s1 + pltpu.roll(x, ROPE_HALF, 1) * s2


def _proj_kernel(x_ref, mod_ref, g_ref, pos_ref, tab_ref, exp_ref, wa_ref, wg_ref, wb_ref,
                 qrawt_ref, qrott_ref, kv16_ref, ks_ref, vst_ref, kw_ref, vwt_ref, gt_ref,
                 *dil_refs_and_scratch):
    n_cfg = len(DIL_CONFIGS)
    qb_refs = dil_refs_and_scratch[0:n_cfg]
    kb_refs = dil_refs_and_scratch[n_cfg:2 * n_cfg]
    vb_refs = dil_refs_and_scratch[2 * n_cfg:3 * n_cfg]
    st_sc = dil_refs_and_scratch[3 * n_cfg]
    tm = x_ref.shape[1]

    def emit_dilated(val, blk, refs):
        st_sc[...] = val
        for (_, d), ref in zip(DIL_CONFIGS, refs):
            for r in range(d):
                piece = val if d == 1 else st_sc[pl.ds(r, tm // d, stride=d), :]
                lo = r * Q_DIL + blk * LANES
                ref[0, :, lo:lo + LANES] = piece.astype(BF16)

    x = x_ref[0]
    sh = mod_ref[0, 0:1, :]
    sc = mod_ref[0, 1:2, :]
    h = (_rms(x, g_ref[...]) * (1.0 + sc) + sh).astype(BF16)
    cs, s1, s2 = _rope_tables(pos_ref[0], tab_ref[0:ROPE_HALF, 0:1], tab_ref[ROPE_HALF:ROPE_HALF + 1, :],
                              exp_ref[...])

    pa = jnp.dot(h, wa_ref[...], preferred_element_type=F32)
    for r in range(Q_NSA // LANES):
        blk = pa[:, r * LANES:(r + 1) * LANES]
        qrawt_ref[0, r * LANES:(r + 1) * LANES, :] = (blk * (SCALE * LOG2E)).T.astype(BF16)
        rot = _rope_blk(blk, cs, s1, s2) * (SCALE * LOG2E)
        qrott_ref[0, r * LANES:(r + 1) * LANES, :] = rot.T.astype(BF16)
    o = Q_NSA
    for t in range(2):
        st_sc[...] = pa[:, o + t * LANES:o + (t + 1) * LANES]
        for j in range(CMP_STRIDE):
            kv16_ref[t, 0, :, j * LANES:(j + 1) * LANES] = st_sc[pl.ds(j, tm // CMP_STRIDE, stride=CMP_STRIDE), :]
    ks_ref[0] = _rope_blk(pa[:, o + 2 * LANES:o + 3 * LANES], cs, s1, s2).astype(BF16)
    vst_ref[0] = pa[:, o + 3 * LANES:o + 4 * LANES].T.astype(BF16)
    kw_ref[0] = _rope_blk(pa[:, o + 4 * LANES:o + 5 * LANES], cs, s1, s2).astype(BF16)
    vwt_ref[0] = pa[:, o + 5 * LANES:o + 6 * LANES].T.astype(BF16)

    gt_ref[0] = jnp.dot(h, wg_ref[...], preferred_element_type=F32)

    pb = jnp.dot(h, wb_ref[...], preferred_element_type=F32)
    for blk in range(Q_DIL // LANES):
        lanes = slice(blk * LANES, (blk + 1) * LANES)
        emit_dilated(_rope_blk(pb[:, lanes], cs, s1, s2) * SCALE, blk, qb_refs)
        emit_dilated(_rope_blk(pb[:, Q_DIL + blk * LANES:Q_DIL + (blk + 1) * LANES], cs, s1, s2), blk, kb_refs)
        emit_dilated(pb[:, 2 * Q_DIL + blk * LANES:2 * Q_DIL + (blk + 1) * LANES], blk, vb_refs)


def _nsa_perm():
    cols = []
    for r in range(GROUP_NSA):
        for hk in range(N_KV_NSA):
            hq = hk * GROUP_NSA + r
            cols.extend(range(hq * HEAD_DIM, (hq + 1) * HEAD_DIM))
    return np.asarray(cols, np.int32)


def _rope_const_tables():
    half = ROPE_HALF
    inv = ROPE_THETA ** (-jnp.arange(half, dtype=F32) / half)
    lane = np.arange(LANES) % HEAD_DIM
    passthrough = (lane >= ROPE_DIM).astype(np.float32)
    tab = jnp.zeros((2 * half, LANES), F32).at[:half, 0].set(inv).at[half].set(passthrough)
    expand = np.zeros((4 * half, 3 * LANES), np.float32)
    for l in range(LANES):
        f = lane[l] % half
        if lane[l] < ROPE_DIM:
            expand[f, l] = expand[half + f, l] = 1.0
        if lane[l] < half:
            expand[2 * half + f, LANES + l] = expand[3 * half + f, LANES + l] = -1.0
        elif lane[l] < ROPE_DIM:
            expand[2 * half + f, 2 * LANES + l] = expand[3 * half + f, 2 * LANES + l] = 1.0
    return tab, jnp.asarray(expand, BF16)


def _proj(x, mod6, g_pre, positions, w_in, tm=512):
    B, S, D = x.shape
    perm = _nsa_perm()
    gate_lo = Q_NSA + 6 * KV_NSA
    n_gate = 3 * N_HEADS_NSA
    w_a = jnp.concatenate([w_in[:, :Q_NSA][:, perm], w_in[:, Q_NSA:gate_lo]], axis=1).astype(BF16)
    w_g = jnp.pad(w_in[:, gate_lo:gate_lo + n_gate], ((0, 0), (0, LANES - n_gate))).astype(BF16)
    w_b = w_in[:, gate_lo + n_gate:].astype(BF16)
    pos = positions.astype(F32).reshape(B, 1, S)
    tab, expand = _rope_const_tables()
    wa_n, wb_n = w_a.shape[1], w_b.shape[1]

    def tok(width, dtype):
        return (pl.BlockSpec((1, tm, width), lambda b, i: (b, i, 0)),
                jax.ShapeDtypeStruct((B, S, width), dtype))

    def tok_t(width, dtype):
        return (pl.BlockSpec((1, width, tm), lambda b, i: (b, 0, i)),
                jax.ShapeDtypeStruct((B, width, S), dtype))

    def dil_view(d):
        return (pl.BlockSpec((1, tm // d, d * Q_DIL), lambda b, i: (b, i, 0)),
                jax.ShapeDtypeStruct((B, S // d, d * Q_DIL), BF16))

    kv16 = (pl.BlockSpec((2, 1, tm // CMP_STRIDE, CMP_STRIDE * KV_NSA), lambda b, i: (0, b, i, 0)),
            jax.ShapeDtypeStruct((2, B, S // CMP_STRIDE, CMP_STRIDE * KV_NSA), F32))
    outs = [tok_t(Q_NSA, BF16), tok_t(Q_NSA, BF16), kv16,
            tok(LANES, BF16), tok_t(LANES, BF16), tok(LANES, BF16), tok_t(LANES, BF16), tok(LANES, F32)]
    outs += [dil_view(d) for _ in range(3) for _, d in DIL_CONFIGS]
    res = pl.pallas_call(
        _proj_kernel,
        grid=(B, S // tm),
        in_specs=[pl.BlockSpec((1, tm, D), lambda b, i: (b, i, 0)),
                  pl.BlockSpec((1, 6, D), lambda b, i: (b, 0, 0)),
                  pl.BlockSpec((1, D), lambda b, i: (0, 0)),
                  pl.BlockSpec((1, 1, tm), lambda b, i: (b, 0, i)),
                  pl.BlockSpec(tab.shape, lambda b, i: (0, 0)),
                  pl.BlockSpec(expand.shape, lambda b, i: (0, 0)),
                  pl.BlockSpec((D, wa_n), lambda b, i: (0, 0)),
                  pl.BlockSpec((D, LANES), lambda b, i: (0, 0)),
                  pl.BlockSpec((D, wb_n), lambda b, i: (0, 0))],
        out_specs=[o[0] for o in outs],
        out_shape=[o[1] for o in outs],
        scratch_shapes=[pltpu.VMEM((tm, LANES), F32)],
        compiler_params=_cparams(("parallel", "parallel")),
        name="proj",
    )(x, mod6, g_pre.reshape(1, D), pos, tab, expand, w_a, w_g, w_b)
    n_cfg = len(DIL_CONFIGS)
    n0 = len(outs) - 3 * n_cfg
    return tuple(res[:n0]) + (res[n0:n0 + n_cfg], res[n0 + n_cfg:n0 + 2 * n_cfg], res[n0 + 2 * n_cfg:])


def _cmpmlp_kernel(a_ref, p_ref, w1_ref, w1x_ref, w2x_ref, o_ref):
    a = a_ref[0, 0].astype(BF16)
    bias = jnp.dot(p_ref[...], w1_ref[0], preferred_element_type=F32)[0:1]
    n16 = a.shape[0]
    out = jnp.zeros((n16, KV_NSA), F32)
    for h in range(N_KV_NSA):
        u = jnp.dot(a, w1x_ref[0, h, 0], preferred_element_type=F32)
        v = jnp.dot(a, w1x_ref[0, h, 1], preferred_element_type=F32)
        hid = jax.nn.gelu(u + pltpu.roll(v, n16 - 1, 0) + bias)
        out = out + jnp.dot(hid.astype(BF16), w2x_ref[0, h], preferred_element_type=F32)
    o_ref[0, 0] = out.astype(BF16)


def _cmpmlp(a, cmp_pos, w_k1, w_k2, w_v1, w_v2):
    _, B, n16, _ = a.shape
    seg = CMP_STRIDE * HEAD_DIM

    def expand1(w):
        w = w.reshape(2, CMP_STRIDE, HEAD_DIM, CMP_HIDDEN)
        z = jnp.zeros_like(w)
        per_head = [jnp.concatenate([w, z] if h == 0 else [z, w], axis=2) for h in range(N_KV_NSA)]
        return jnp.stack(per_head, axis=0).reshape(N_KV_NSA, 2, CMP_STRIDE * KV_NSA, CMP_HIDDEN)

    def expand2(w):
        z = jnp.zeros_like(w)
        return jnp.stack([jnp.concatenate([w, z] if h == 0 else [z, w], axis=1) for h in range(N_KV_NSA)], axis=0)

    w1 = jnp.stack([w_k1, w_v1], axis=0).astype(BF16)
    w1x = jnp.stack([expand1(w_k1), expand1(w_v1)], axis=0).astype(BF16)
    w2x = jnp.stack([expand2(w_k2), expand2(w_v2)], axis=0).astype(BF16)
    p8 = jnp.broadcast_to(cmp_pos.reshape(1, CMP_BLOCK * HEAD_DIM), (8, CMP_BLOCK * HEAD_DIM)).astype(BF16)
    out = pl.pallas_call(
        _cmpmlp_kernel,
        grid=(2, B),
        in_specs=[pl.BlockSpec((1, 1, n16, CMP_STRIDE * KV_NSA), lambda t, b: (t, b, 0, 0)),
                  pl.BlockSpec((8, 2 * seg), lambda t, b: (0, 0)),
                  pl.BlockSpec((1, 2 * seg, CMP_HIDDEN), lambda t, b: (t, 0, 0)),
                  pl.BlockSpec((1, N_KV_NSA, 2, CMP_STRIDE * KV_NSA, CMP_HIDDEN), lambda t, b: (t, 0, 0, 0, 0)),
                  pl.BlockSpec((1, N_KV_NSA, CMP_HIDDEN, KV_NSA), lambda t, b: (t, 0, 0, 0))],
        out_specs=pl.BlockSpec((1, 1, n16, KV_NSA), lambda t, b: (t, b, 0, 0)),
        out_shape=jax.ShapeDtypeStruct((2, B, n16, KV_NSA), BF16),
        compiler_params=_cparams(("parallel", "parallel")),
        name="cmpmlp",
    )(a, p8, w1, w1x, w2x)
    return out[0], out[1]


def _query_groups_t(qt_ref, tq):
    half0 = lax.broadcasted_iota(I32, (LANES, tq), 0) < HEAD_DIM
    zero = jnp.zeros((LANES, tq), BF16)
    groups = []
    for hk in range(N_KV_NSA):
        keep = half0 if hk == 0 else jnp.logical_not(half0)
        for r in range(GROUP_NSA):
            groups.append(jnp.where(keep, qt_ref[0, r * LANES:(r + 1) * LANES, :], zero))
    return groups


def _store_heads_t(ot, tq, o_ref):
    half0 = lax.broadcasted_iota(I32, (tq, LANES), 1) < HEAD_DIM
    for r in range(GROUP_NSA):
        o0 = ot[:, r * tq:(r + 1) * tq].T
        o1 = ot[:, (GROUP_NSA + r) * tq:(GROUP_NSA + r + 1) * tq].T
        o_ref[0, :, r * LANES:(r + 1) * LANES] = jnp.where(half0, o0, o1)


def _cmpsel_kernel(qt_ref, kc_ref, vct_ref, ovt_ref, o_ref, bias_ref, *, tq):
    i = pl.program_id(1)
    n16 = kc_ref.shape[1]
    cols = 8 * tq
    qa = jnp.concatenate(_query_groups_t(qt_ref, tq), axis=1)
    st = jnp.dot(kc_ref[0], qa, preferred_element_type=F32)
    t = i * tq + (lax.broadcasted_iota(I32, (n16, cols), 1) & (tq - 1))
    c = lax.broadcasted_iota(I32, (n16, cols), 0)
    valid = (c * CMP_STRIDE + (CMP_BLOCK - 1)) <= t
    st = jnp.where(valid, st, NEG_INF)
    m = jnp.max(st, axis=0, keepdims=True)
    e = jnp.exp2(st - m)
    l = jnp.sum(e, axis=0, keepdims=True)
    p = jnp.where(valid, e * (1.0 / l), 0.0)
    ot = jnp.dot(vct_ref[0], p.astype(BF16), preferred_element_type=F32)
    _store_heads_t(ot, tq, o_ref)

    n_slc = ovt_ref.shape[0]
    j = lax.broadcasted_iota(I32, (n_slc, tq), 0)
    cur = (i * tq + lax.broadcasted_iota(I32, (n_slc, tq), 1)) >> SLC_SHIFT
    forced = (j == 0) | (j == cur) | (j == cur - 1)
    ovt = ovt_ref[...]
    biases = []
    for hk in range(N_KV_NSA):
        ps = p[:, hk * GROUP_NSA * tq:(hk * GROUP_NSA + 1) * tq]
        for r in range(1, GROUP_NSA):
            ps = ps + p[:, (hk * GROUP_NSA + r) * tq:(hk * GROUP_NSA + r + 1) * tq]
        hi = ps.astype(BF16)
        lo = (ps - hi.astype(F32)).astype(BF16)
        pslc = (jnp.dot(ovt, hi, preferred_element_type=F32)
                + jnp.dot(ovt, lo, preferred_element_type=F32))
        score = jnp.where(forced, FORCE_SCORE, jnp.where(j <= cur, pslc, -1.0))
        rank = jnp.zeros((n_slc, tq), I32)
        for ii in range(n_slc):
            ri = score[ii:ii + 1, :]
            beats = (ri > score) | ((ri == score) & (j > ii))
            rank = rank + beats.astype(I32)
        biases.append(jnp.where(rank < SLC_TOPK, 0.0, NEG_INF))
    bias_ref[0] = jnp.concatenate(biases, axis=0).astype(BF16)


def _overlap_t(S):
    n16 = S // CMP_STRIDE
    n_slc = S // SLC_BLOCK
    cs = np.arange(n16) * CMP_STRIDE
    js = np.arange(n_slc) * SLC_BLOCK
    ov = np.clip(np.minimum(cs[:, None] + CMP_BLOCK, js[None, :] + SLC_BLOCK)
                 - np.maximum(cs[:, None], js[None, :]), 0, None).astype(np.float32) / CMP_BLOCK
    ov[n16 - 1] = 0.0
    return jnp.asarray(ov.T, BF16)


def _cmpsel(q_raw_t, kcf, vcf, tq=256):
    B, _, S = q_raw_t.shape
    n16 = S // CMP_STRIDE
    n_slc = S // SLC_BLOCK
    assert n_slc == HEAD_DIM, "selection bias is laid out as one 64-lane half per kv head"
    return pl.pallas_call(
        functools.partial(_cmpsel_kernel, tq=tq),
        grid=(B, S // tq),
        in_specs=[pl.BlockSpec((1, Q_NSA, tq), lambda b, i: (b, 0, i)),
                  pl.BlockSpec((1, n16, KV_NSA), lambda b, i: (b, 0, 0)),
                  pl.BlockSpec((1, KV_NSA, n16), lambda b, i: (b, 0, 0)),
                  pl.BlockSpec((n_slc, n16), lambda b, i: (0, 0))],
        out_specs=[pl.BlockSpec((1, tq, Q_NSA), lambda b, i: (b, i, 0)),
                   pl.BlockSpec((1, 2 * n_slc, tq), lambda b, i: (b, 0, i))],
        out_shape=[jax.ShapeDtypeStruct((B, S, Q_NSA), F32),
                   jax.ShapeDtypeStruct((B, 2 * n_slc, S), BF16)],
        compiler_params=_cparams(("parallel", "parallel")),
        name="cmpsel",
    )(q_raw_t, kcf, vcf.transpose(0, 2, 1), _overlap_t(S))


def _slc_kernel(qi_ref, kj_ref, qt_ref, bt_ref, k_ref, vt_ref, o_ref, qa_sc, m_sc, acc_sc, *, tq, tk):
    step = pl.program_id(1)
    i = qi_ref[step]
    kj = kj_ref[step]
    last = (i * tq + tq - 1) // tk
    cols = 8 * tq

    @pl.when(kj == 0)
    def _():
        half0 = lax.broadcasted_iota(I32, (LANES, tq), 0) < HEAD_DIM
        zero = jnp.zeros((LANES, tq), BF16)
        bt = bt_ref[0]
        groups = []
        for hk in range(N_KV_NSA):
            keep = half0 if hk == 0 else jnp.logical_not(half0)
            bh = jnp.where(keep, bt, zero)
            for r in range(GROUP_NSA):
                qb = qt_ref[0, r * LANES:(r + 1) * LANES, :]
                groups.append(jnp.concatenate([jnp.where(keep, qb, zero), bh], axis=0))
        qa_sc[...] = jnp.concatenate(groups, axis=1)
        m_sc[...] = jnp.full((1, cols), NEG_INF, F32)
        acc_sc[...] = jnp.zeros(acc_sc.shape, F32)

    def update(on_diagonal):
        kblk = (kj * tk + lax.broadcasted_iota(I32, (tk, LANES), 0)) >> SLC_SHIFT
        lane = lax.broadcasted_iota(I32, (tk, LANES), 1) & (HEAD_DIM - 1)
        onehot = jnp.where(kblk == lane, 1.0, 0.0).astype(BF16)
        kaug = jnp.concatenate([k_ref[0], onehot], axis=1)
        st = jnp.dot(kaug, qa_sc[...], preferred_element_type=F32)
        if on_diagonal:
            kpos = kj * tk + lax.broadcasted_iota(I32, (tk, cols), 0)
            t = i * tq + (lax.broadcasted_iota(I32, (tk, cols), 1) & (tq - 1))
            st = jnp.where(kpos <= t, st, NEG_INF)
        m_old = m_sc[...]
        m_new = jnp.maximum(m_old, jnp.max(st, axis=0, keepdims=True))
        alpha = jnp.exp2(m_old - m_new)
        p = jnp.exp2(st - m_new).astype(BF16)
        vt_ones = jnp.concatenate([vt_ref[0], jnp.ones((DEN_ROWS, tk), BF16)], axis=0)
        acc_sc[...] = alpha * acc_sc[...] + jnp.dot(vt_ones, p, preferred_element_type=F32)
        m_sc[...] = m_new

    @pl.when(kj < last)
    def _():
        update(False)

    @pl.when(kj == last)
    def _():
        update(True)
        acc = acc_sc[...]
        _store_heads_t(acc[:LANES] * (1.0 / acc[LANES:LANES + 1]), tq, o_ref)


def _slc(q_rot_t, bias_t, ks, vs_t, tq=512, tk=512):
    B, _, S = q_rot_t.shape
    assert tk % tq == 0
    qi, kj = [], []
    for i in range(S // tq):
        for j in range((i * tq + tq - 1) // tk + 1):
            qi.append(i)
            kj.append(j)
    grid_spec = pltpu.PrefetchScalarGridSpec(
        num_scalar_prefetch=2,
        grid=(B, len(qi)),
        in_specs=[pl.BlockSpec((1, Q_NSA, tq), lambda b, s, qi, kj: (b, 0, qi[s])),
                  pl.BlockSpec((1, LANES, tq), lambda b, s, qi, kj: (b, 0, qi[s])),
                  pl.BlockSpec((1, tk, KV_NSA), lambda b, s, qi, kj: (b, kj[s], 0)),
                  pl.BlockSpec((1, KV_NSA, tk), lambda b, s, qi, kj: (b, 0, kj[s]))],
        out_specs=pl.BlockSpec((1, tq, Q_NSA), lambda b, s, qi, kj: (b, qi[s], 0)),
        scratch_shapes=[pltpu.VMEM((2 * LANES, 8 * tq), BF16),
                        pltpu.VMEM((1, 8 * tq), F32),
                        pltpu.VMEM((LANES + DEN_ROWS, 8 * tq), F32)],
    )
    return pl.pallas_call(
        functools.partial(_slc_kernel, tq=tq, tk=tk),
        grid_spec=grid_spec,
        out_shape=jax.ShapeDtypeStruct((B, S, Q_NSA), F32),
        compiler_params=_cparams(("parallel", "arbitrary")),
        name="slc",
    )(jnp.asarray(qi, I32), jnp.asarray(kj, I32), q_rot_t, bias_t, ks, vs_t)


def _win_kernel(qt_ref, k0_ref, k1_ref, k2_ref, v0_ref, v1_ref, v2_ref, o_ref, *, tq, tk, max_dist):
    i = pl.program_id(1)
    a = (i * tq) // tk
    cols = 8 * tq
    qa = jnp.concatenate(_query_groups_t(qt_ref, tq), axis=1)
    t = i * tq + (lax.broadcasted_iota(I32, (tk, cols), 1) & (tq - 1))
    row = lax.broadcasted_iota(I32, (tk, cols), 0)
    sts = []
    for jt, k_ref in enumerate((k0_ref, k1_ref, k2_ref)):
        st = jnp.dot(k_ref[0], qa, preferred_element_type=F32)
        kpos = (a - 2 + jt) * tk + row
        if jt == 0:
            st = jnp.where((t - kpos <= max_dist) & (kpos >= 0), st, NEG_INF)
        elif jt == 1:
            st = jnp.where(kpos >= 0, st, NEG_INF)
        else:
            st = jnp.where(kpos <= t, st, NEG_INF)
        sts.append(st)
    m = jnp.max(sts[0], axis=0, keepdims=True)
    for st in sts[1:]:
        m = jnp.maximum(m, jnp.max(st, axis=0, keepdims=True))
    acc = jnp.zeros((LANES + DEN_ROWS, cols), F32)
    for st, v_ref in zip(sts, (v0_ref, v1_ref, v2_ref)):
        vt_ones = jnp.concatenate([v_ref[0], jnp.ones((DEN_ROWS, tk), BF16)], axis=0)
        acc = acc + jnp.dot(vt_ones, jnp.exp2(st - m).astype(BF16), preferred_element_type=F32)
    _store_heads_t(acc[:LANES] * (1.0 / acc[LANES:LANES + 1]), tq, o_ref)


def _win(q_rot_t, kw, vw_t, tq=256, tk=256):
    B, _, S = q_rot_t.shape
    max_dist = WIN_NSA - 1
    assert tk % tq == 0 and 2 * tk == max_dist + 1

    def k_map(jt):
        return lambda b, i: (b, jnp.maximum((i * tq) // tk - 2 + jt, 0), 0)

    def v_map(jt):
        return lambda b, i: (b, 0, jnp.maximum((i * tq) // tk - 2 + jt, 0))

    return pl.pallas_call(
        functools.partial(_win_kernel, tq=tq, tk=tk, max_dist=max_dist),
        grid=(B, S // tq),
        in_specs=[pl.BlockSpec((1, Q_NSA, tq), lambda b, i: (b, 0, i))]
        + [pl.BlockSpec((1, tk, KV_NSA), k_map(jt)) for jt in range(3)]
        + [pl.BlockSpec((1, KV_NSA, tk), v_map(jt)) for jt in range(3)],
        out_specs=pl.BlockSpec((1, tq, Q_NSA), lambda b, i: (b, i, 0)),
        out_shape=jax.ShapeDtypeStruct((B, S, Q_NSA), F32),
        compiler_params=_cparams(("parallel", "parallel")),
        name="win",
    )(q_rot_t, kw, kw, kw, vw_t, vw_t, vw_t)


def _band_kernel(*refs, tq, tk, nt, max_dist):
    q_ref = refs[0]
    k_refs = refs[1:1 + nt]
    v_refs = refs[1 + nt:1 + 2 * nt]
    o_ref = refs[1 + 2 * nt]
    lse_ref = refs[2 + 2 * nt]
    i = pl.program_id(2)
    a = (i * tq) // tk
    lane = lax.broadcasted_iota(I32, (tq, LANES), 1)
    half0 = lane < HEAD_DIM
    row_t = i * tq + (lax.broadcasted_iota(I32, (2 * tq, nt * tk), 0) & (tq - 1))
    kpos = (a - (nt - 1)) * tk + lax.broadcasted_iota(I32, (2 * tq, nt * tk), 1)
    dist = row_t - kpos
    ok = (dist >= 0) & (dist <= max_dist) & (kpos >= 0)
    ones = jnp.ones((nt * tk, LANES), BF16)
    lse_acc = jnp.zeros((tq, LANES), F32)
    for blk in range(Q_DIL // LANES):
        lanes = slice(blk * LANES, (blk + 1) * LANES)
        qb = q_ref[0, :, lanes]
        zero = jnp.zeros_like(qb)
        qs = jnp.concatenate([jnp.where(half0, qb, zero), jnp.where(half0, zero, qb)], axis=0)
        kcat = jnp.concatenate([k_ref[0, :, lanes] for k_ref in k_refs], axis=0)
        vcat = jnp.concatenate([v_ref[0, :, lanes] for v_ref in v_refs], axis=0)
        s = lax.dot_general(qs, kcat, _NT, preferred_element_type=F32)
        s = jnp.where(ok, s, NEG_INF)
        m = jnp.max(s, axis=-1, keepdims=True)
        p = jnp.exp(s - m).astype(BF16)
        ol = jnp.dot(p, jnp.concatenate([vcat, ones], axis=1), preferred_element_type=F32)
        den = ol[:, LANES:]
        o = ol[:, :LANES] * (1.0 / den)
        o_ref[0, :, lanes] = jnp.where(half0, o[:tq], o[tq:])
        lse = m + jnp.log(den[:, 0:1])
        lse_acc = jnp.where(lane == 2 * blk, lse[:tq], lse_acc)
        lse_acc = jnp.where(lane == 2 * blk + 1, lse[tq:], lse_acc)
    lse_ref[0] = lse_acc


def _band(q, k, v, *, dil, max_dist, tq, tk, nt):
    B, m, _ = q.shape

    def kv_map(jt):
        return lambda b, r, i: (b, jnp.maximum((i * tq) // tk - (nt - 1) + jt, 0), r)

    tok_spec = pl.BlockSpec((1, tq, Q_DIL), lambda b, r, i: (b, i, r))
    kv_specs = [pl.BlockSpec((1, tk, Q_DIL), kv_map(jt)) for jt in range(nt)]
    return pl.pallas_call(
        functools.partial(_band_kernel, tq=tq, tk=tk, nt=nt, max_dist=max_dist),
        grid=(B, dil, m // tq),
        in_specs=[tok_spec] + kv_specs + kv_specs,
        out_specs=[tok_spec, pl.BlockSpec((1, tq, LANES), lambda b, r, i: (b, i, r))],
        out_shape=[jax.ShapeDtypeStruct((B, m, dil * Q_DIL), F32),
                   jax.ShapeDtypeStruct((B, m, dil * LANES), F32)],
        compiler_params=_cparams(("parallel", "parallel", "parallel")),
        name=f"band_d{dil}_w{max_dist}",
    )(q, *([k] * nt), *([v] * nt))


def _out_kernel(ocmp_ref, oslc_ref, owin_ref, gt_ref, od1_ref, od4_ref, od16_ref, l1_ref, l4_ref, l16_ref,
                x_ref, mod_ref, eg_ref, ed_ref, gnsa_ref, gdil_ref, wo_ref, gpost_ref, o_ref, st_sc):
    tm = x_ref.shape[1]

    def token_order(ref, d, n_blk):
        if d == 1:
            return ref[0]
        blocks = []
        for blk in range(n_blk):
            for r in range(d):
                lo = (r * n_blk + blk) * LANES
                st_sc[pl.ds(r, tm // d, stride=d), :] = ref[0, :, lo:lo + LANES]
            blocks.append(st_sc[...])
        return blocks[0] if n_blk == 1 else jnp.concatenate(blocks, axis=1)

    sg = jax.nn.sigmoid(gt_ref[0])
    g3 = _expand_dot(sg, eg_ref[...])
    oa = (g3[:, :Q_NSA] * ocmp_ref[0] + g3[:, Q_NSA:2 * Q_NSA] * oslc_ref[0] + g3[:, 2 * Q_NSA:] * owin_ref[0])
    ya = _rms(oa, gnsa_ref[...])

    dils = [d for _, d in DIL_CONFIGS]
    n_blk = Q_DIL // LANES
    l1, l4, l16 = [token_order(ref, d, 1) for ref, d in zip((l1_ref, l4_ref, l16_ref), dils)]
    mx = jnp.maximum(jnp.maximum(l1, l4), l16)
    e1, e4, e16 = jnp.exp(l1 - mx), jnp.exp(l4 - mx), jnp.exp(l16 - mx)
    den = e1 + e4 + e16
    ed = ed_ref[...]
    od1, od4, od16 = [token_order(ref, d, n_blk) for ref, d in zip((od1_ref, od4_ref, od16_ref), dils)]
    inv = 1.0 / den
    ob = (_expand_dot(e1 * inv, ed) * od1 + _expand_dot(e4 * inv, ed) * od4 + _expand_dot(e16 * inv, ed) * od16)
    yb = _rms(ob, gdil_ref[...])

    y = jnp.concatenate([ya, yb], axis=1).astype(BF16)
    z = jnp.dot(y, wo_ref[...], preferred_element_type=F32)
    gt_m = mod_ref[0, 2:3, :]
    o_ref[0] = x_ref[0] + gt_m * _rms(z, gpost_ref[...])


def _gate_expanders():
    perm = _nsa_perm()
    eg = np.zeros((LANES, 3 * Q_NSA), np.float32)
    for lane_out, col in enumerate(perm):
        hq = col // HEAD_DIM
        for c in range(3):
            eg[hq * 3 + c, c * Q_NSA + lane_out] = 1.0
    ed = np.zeros((LANES, Q_DIL), np.float32)
    for h in range(N_HEADS_DIL):
        ed[h, h * HEAD_DIM:(h + 1) * HEAD_DIM] = 1.0
    return jnp.asarray(np.concatenate([eg, eg]), BF16), jnp.asarray(np.concatenate([ed, ed]), BF16)


def _out(o_cmp, o_slc, o_win, gates, ods, lses, x, mod6, g_out_nsa, g_out_dil, w_o, g_post, tm=1024):
    B, S, D = x.shape
    perm = _nsa_perm()
    eg, ed = _gate_expanders()
    w_o_p = jnp.concatenate([w_o[:Q_NSA][perm], w_o[Q_NSA:]], axis=0).astype(BF16)
    g_nsa_p = g_out_nsa[perm].reshape(1, Q_NSA)

    def tok(width):
        return pl.BlockSpec((1, tm, width), lambda b, i: (b, i, 0))

    def const(shape):
        return pl.BlockSpec(shape, lambda b, i: (0,) * len(shape))

    def view(d, width):
        return pl.BlockSpec((1, tm // d, d * width), lambda b, i: (b, i, 0))

    dils = [d for _, d in DIL_CONFIGS]
    return pl.pallas_call(
        _out_kernel,
        grid=(B, S // tm),
        in_specs=[tok(Q_NSA), tok(Q_NSA), tok(Q_NSA), tok(LANES)]
        + [view(d, Q_DIL) for d in dils] + [view(d, LANES) for d in dils]
        + [tok(D), pl.BlockSpec((1, 6, D), lambda b, i: (b, 0, 0)),
           const(eg.shape), const(ed.shape), const((1, Q_NSA)), const((1, Q_DIL)),
           const((D, D)), const((1, D))],
        out_specs=tok(D),
        out_shape=jax.ShapeDtypeStruct((B, S, D), F32),
        scratch_shapes=[pltpu.VMEM((tm, LANES), F32)],
        compiler_params=_cparams(("parallel", "parallel")),
        name="out",
    )(o_cmp, o_slc, o_win, gates, *ods, *lses, x, mod6, eg, ed, g_nsa_p, g_out_dil.reshape(1, Q_DIL),
      w_o_p, g_post.reshape(1, D))


def _route_kernel(x_ref, mod_ref, g_ref, wr_ref, br_ref, h_ref, idx_ref, gate_ref, rank_ref, cnt_ref, run_sc):
    i = pl.program_id(0)

    @pl.when(i == 0)
    def _():
        run_sc[...] = jnp.zeros_like(run_sc)

    sh = mod_ref[0, 3:4, :]
    sc = mod_ref[0, 4:5, :]
    h = _rms(x_ref[...], g_ref[...]) * (1.0 + sc) + sh
    h_ref[...] = h
    logits = _dot3(h, wr_ref[...]) + br_ref[...]
    tm = logits.shape[0]
    lane = lax.broadcasted_iota(I32, (tm, LANES), 1)
    work = jnp.where(lane < N_EXPERTS, logits, -jnp.inf)
    idx_out = jnp.zeros((tm, LANES), I32)
    val_out = jnp.full((tm, LANES), -jnp.inf, F32)
    rank_out = jnp.zeros((tm, LANES), F32)
    tri = jnp.where(lax.broadcasted_iota(I32, (tm, tm), 1) < lax.broadcasted_iota(I32, (tm, tm), 0),
                    1.0, 0.0).astype(BF16)
    onehots = []
    for k in range(TOP_K):
        mx = jnp.max(work, axis=-1, keepdims=True)
        ix = jnp.min(jnp.where(work == mx, lane, LANES), axis=-1, keepdims=True)
        hit = lane == ix
        idx_out = jnp.where(lane == k, ix, idx_out)
        val_out = jnp.where(lane == k, mx, val_out)
        work = jnp.where(hit, -jnp.inf, work)
        onehots.append(jnp.where(hit, 1.0, 0.0))
    earlier = jnp.dot(tri, jnp.concatenate(onehots, axis=1).astype(BF16), preferred_element_type=F32)
    base = run_sc[0:1, :]
    for k, onehot in enumerate(onehots):
        before = earlier[:, k * LANES:(k + 1) * LANES] + base
        rank_k = jnp.sum(onehot * before, axis=-1, keepdims=True)
        rank_out = jnp.where(lane == k, rank_k, rank_out)
        base = base + jnp.sum(onehot, axis=0, keepdims=True)
    run_sc[...] = jnp.broadcast_to(base, run_sc.shape)
    e = jnp.exp(val_out - val_out[:, 0:1])
    idx_ref[...] = idx_out
    gate_ref[...] = e / jnp.sum(e, axis=-1, keepdims=True)
    rank_ref[...] = rank_out.astype(I32)
    cnt_ref[...] = run_sc[...].astype(I32)


def _route(x1, mod6, g_pre_ffn, w_router, b_router, tm=512):
    B, S, D = x1.shape
    N = B * S
    spb = S // tm
    wr = jnp.pad(w_router, ((0, 0), (0, LANES - N_EXPERTS)))
    br = jnp.pad(b_router, (0, LANES - N_EXPERTS)).reshape(1, LANES)

    def tok(width):
        return pl.BlockSpec((tm, width), lambda i: (i, 0))

    return pl.pallas_call(
        _route_kernel,
        grid=(N // tm,),
        in_specs=[tok(D), pl.BlockSpec((1, 6, D), lambda i: (i // spb, 0, 0)),
                  pl.BlockSpec((1, D), lambda i: (0, 0)),
                  pl.BlockSpec((D, LANES), lambda i: (0, 0)),
                  pl.BlockSpec((1, LANES), lambda i: (0, 0))],
        out_specs=[tok(D), tok(LANES), tok(LANES), tok(LANES), pl.BlockSpec((8, LANES), lambda i: (0, 0))],
        out_shape=[jax.ShapeDtypeStruct((N, D), F32),
                   jax.ShapeDtypeStruct((N, LANES), I32),
                   jax.ShapeDtypeStruct((N, LANES), F32),
                   jax.ShapeDtypeStruct((N, LANES), I32),
                   jax.ShapeDtypeStruct((8, LANES), I32)],
        scratch_shapes=[pltpu.VMEM((8, LANES), F32)],
        compiler_params=_cparams(("arbitrary",)),
        name="route",
    )(x1.reshape(N, D), mod6, g_pre_ffn.reshape(1, D), wr, br)


def _dispatch_kernel(pos_ref, pad0_ref, padn_ref, nt_ref, h_ref, xs_out, zbuf, sem, zsem, *, tm, tmx, n_tiles):
    i = pl.program_id(0)
    bits = tmx.bit_length() - 1

    def pad_copies(e, fn):
        p0 = pad0_ref[e]
        head = (-p0) & (SUBLANES - 1)
        head = jnp.minimum(head, padn_ref[e])
        for r in range(SUBLANES - 1):
            @pl.when(r < head)
            def _():
                fn(pltpu.make_async_copy(zbuf.at[pl.ds(0, 1), :], xs_out.at[pl.ds(p0 + r, 1), :], zsem))
        a = p0 + head
        n = padn_ref[e] - head
        for b in range(3, bits):
            size = 1 << b
            off = (n >> (b + 1)) << (b + 1)

            @pl.when((n & size) != 0)
            def _():
                fn(pltpu.make_async_copy(zbuf.at[pl.ds(0, size), :],
                                         xs_out.at[pl.ds(pl.multiple_of(a + off, SUBLANES), size), :], zsem))

    def tail_copy(t, fn):
        fn(pltpu.make_async_copy(zbuf, xs_out.at[pl.ds(pl.multiple_of(t * tmx, tmx), tmx), :], zsem))

    def for_all_fill(fn):
        def per_expert(e, carry):
            pad_copies(e, fn)
            return carry
        lax.fori_loop(0, N_EXPERTS, per_expert, 0)

        def per_tile(t, carry):
            tail_copy(t, fn)
            return carry
        lax.fori_loop(nt_ref[0], n_tiles, per_tile, 0)

    @pl.when(i == 0)
    def _():
        zbuf[...] = jnp.zeros_like(zbuf)
        for_all_fill(lambda cp: cp.start())
        for_all_fill(lambda cp: cp.wait())

    def body(r, carry):
        for k in range(TOP_K):
            dst_row = pos_ref[(i * tm + r) * TOP_K + k]
            pltpu.make_async_copy(h_ref.at[pl.ds(r, 1), :], xs_out.at[pl.ds(dst_row, 1), :], sem).start(
                priority=k % 2)
        return carry

    lax.fori_loop(0, tm, body, 0, unroll=DMA_UNROLL // TOP_K)
    for k in range(TOP_K):
        pltpu.make_async_copy(h_ref, xs_out.at[pl.ds(0, tm), :], sem).wait()


def _dispatch(h2, pos_flat, pad_start, pad_len, n_valid, n_tiles, tmx, tm=1024):
    N, D = h2.shape
    assert tmx & (tmx - 1) == 0
    grid_spec = pltpu.PrefetchScalarGridSpec(
        num_scalar_prefetch=4,
        grid=(N // tm,),
        in_specs=[pl.BlockSpec((tm, D), lambda i, *_: (i, 0))],
        out_specs=pl.BlockSpec(memory_space=pl.ANY),
        scratch_shapes=[pltpu.VMEM((tmx, D), F32), pltpu.SemaphoreType.DMA(()), pltpu.SemaphoreType.DMA(())],
    )
    return pl.pallas_call(
        functools.partial(_dispatch_kernel, tm=tm, tmx=tmx, n_tiles=n_tiles),
        grid_spec=grid_spec,
        out_shape=jax.ShapeDtypeStruct((n_tiles * tmx, D), F32),
        compiler_params=_cparams(("arbitrary",)),
        name="dispatch",
    )(pos_flat, pad_start, pad_len, n_valid, h2)


def _moe_kernel(te_ref, nt_ref, nx_ref, sl_ref, rows_ref, xs_ref, wup_hbm, bup_ref, wdn_hbm, bdn_ref, pm_ref, y_ref,
                wup_in, wdn_in, wup_sc, wdn_sc, sems):
    i = pl.program_id(0)
    n_valid = nt_ref[0]
    e = te_ref[i]
    new_expert = (i == 0) | (e != te_ref[jnp.maximum(i - 1, 0)])
    slot = sl_ref[i]
    n_blk = wup_sc.shape[1] // (2 * LANES)

    def weight_copies(ex, s):
        return (pltpu.make_async_copy(wup_hbm.at[ex], wup_in.at[s], sems.at[0, s]),
                pltpu.make_async_copy(wdn_hbm.at[ex], wdn_in.at[s], sems.at[1, s]))

    @pl.when(i == 0)
    def _():
        for cp in weight_copies(e, slot):
            cp.start()

    @pl.when(new_expert & (i < n_valid))
    def _():
        for cp in weight_copies(e, slot):
            cp.wait()
        nx = nx_ref[i]

        @pl.when(nx >= 0)
        def _():
            for cp in weight_copies(nx, 1 - slot):
                cp.start()

        pm = pm_ref[...]
        for blk in range(n_blk):
            cols = slice(blk * 2 * LANES, (blk + 1) * 2 * LANES)
            w = wup_in[slot, :, cols].astype(BF16)
            wup_sc[:, cols] = jnp.dot(w, pm, preferred_element_type=F32).astype(BF16)
        wdn_sc[...] = wdn_in[slot].astype(BF16)

    def expert_mlp(n_rows):
        xs = xs_ref[0:n_rows, :].astype(BF16)
        u = jnp.dot(xs, wup_sc[...], preferred_element_type=F32) + bup_ref[0]
        acts = []
        for blk in range(n_blk):
            ug = jnp.minimum(u[:, blk * 2 * LANES:blk * 2 * LANES + LANES], SWIGLU_LIMIT)
            ul = jnp.clip(u[:, blk * 2 * LANES + LANES:(blk + 1) * 2 * LANES], -SWIGLU_LIMIT, SWIGLU_LIMIT)
            acts.append((ug * jax.nn.sigmoid(SWIGLU_ALPHA * ug) * (ul + 1.0)).astype(BF16))
        act = jnp.concatenate(acts, axis=1)
        y_ref[0:n_rows, :] = jnp.dot(act, wdn_sc[...], preferred_element_type=F32) + bdn_ref[0]

    tm = y_ref.shape[0]
    rows = rows_ref[i]
    for part in range(1, MOE_PARTS + 1):
        n_rows = part * tm // MOE_PARTS

        @pl.when((i < n_valid) & (rows > n_rows - tm // MOE_PARTS) & (rows <= n_rows))
        def _():
            expert_mlp(n_rows)
            if n_rows < tm:
                y_ref[n_rows:, :] = jnp.zeros((tm - n_rows, y_ref.shape[1]), F32)

    @pl.when(i >= n_valid)
    def _():
        y_ref[...] = jnp.zeros_like(y_ref)


def _moe_layout(top_idx, rank, counts, tm):
    N = top_idx.shape[0]
    n_tiles = N * TOP_K // tm + N_EXPERTS
    e_ids = jnp.arange(N_EXPERTS, dtype=I32)
    tiles_e = (counts + tm - 1) // tm
    tile_end = jnp.sum(jnp.where(e_ids[None, :] <= e_ids[:, None], tiles_e[None, :], 0), axis=1)
    start = (tile_end - tiles_e) * tm
    pos = rank + jnp.sum(jnp.where(top_idx[:, :, None] == e_ids[None, None, :], start[None, None, :], 0), axis=-1)
    n_valid = tile_end[N_EXPERTS - 1]
    tile_ids = jnp.arange(n_tiles, dtype=I32)
    tile_e = jnp.sum((jnp.minimum(tile_ids, n_valid - 1)[:, None] >= tile_end[None, :]).astype(I32), axis=1)
    prev_e = jnp.concatenate([jnp.full((1,), -1, I32), tile_e[:-1]])
    new = (tile_e != prev_e) & (tile_ids < n_valid)
    ordinal = jnp.sum(jnp.where(tile_ids[None, :] <= tile_ids[:, None], new[None, :].astype(I32), 0), axis=1) - 1
    next_first = jnp.sum(jnp.where(tile_e[:, None] == e_ids[None, :], tile_end[None, :], 0), axis=1)
    next_e = jnp.sum(jnp.where(next_first[:, None] == tile_ids[None, :], tile_e[None, :], 0), axis=1)
    next_e = jnp.where(next_first < n_valid, next_e, -1)
    is_e = tile_e[:, None] == e_ids[None, :]
    rows_end = jnp.sum(jnp.where(is_e, (start + counts)[None, :], 0), axis=1)
    tile_rows = jnp.clip(rows_end - tile_ids * tm, 0, tm)
    meta = dict(tile_e=tile_e.astype(I32), n_valid=n_valid.astype(I32).reshape(1), next_e=next_e.astype(I32),
                slot=(ordinal & 1).astype(I32), tile_rows=tile_rows.astype(I32),
                pad_start=(start + counts).astype(I32), pad_len=(tiles_e * tm - counts).astype(I32))
    return pos.astype(I32), meta, n_tiles


def _glu_perm():
    pm = np.zeros((2 * LANES, 2 * LANES), np.float32)
    for j in range(LANES):
        pm[2 * j, j] = 1.0
        pm[2 * j + 1, LANES + j] = 1.0
    return jnp.asarray(pm, BF16)


def _moe(xs, meta, w_up, b_up, w_down, b_down, tm):
    P, D = xs.shape
    n_blk = D_FF // LANES
    b_up_p = b_up.reshape(N_EXPERTS, n_blk, LANES, 2).transpose(0, 1, 3, 2).reshape(N_EXPERTS, 1, 2 * D_FF)
    b_dn = b_down.reshape(N_EXPERTS, 1, D)

    def bmap(i, te, *_):
        return (te[i], 0, 0)

    grid_spec = pltpu.PrefetchScalarGridSpec(
        num_scalar_prefetch=5,
        grid=(P // tm,),
        in_specs=[pl.BlockSpec((tm, D), lambda i, *_: (i, 0)),
                  pl.BlockSpec(memory_space=pl.ANY), pl.BlockSpec((1, 1, 2 * D_FF), bmap),
                  pl.BlockSpec(memory_space=pl.ANY), pl.BlockSpec((1, 1, D), bmap),
                  pl.BlockSpec((2 * LANES, 2 * LANES), lambda i, *_: (0, 0))],
        out_specs=pl.BlockSpec((tm, D), lambda i, *_: (i, 0)),
        scratch_shapes=[pltpu.VMEM((2, D, 2 * D_FF), F32), pltpu.VMEM((2, D_FF, D), F32),
                        pltpu.VMEM((D, 2 * D_FF), BF16), pltpu.VMEM((D_FF, D), BF16),
                        pltpu.SemaphoreType.DMA((2, 2))],
    )
    return pl.pallas_call(
        _moe_kernel,
        grid_spec=grid_spec,
        out_shape=jax.ShapeDtypeStruct((P, D), F32),
        compiler_params=_cparams(("arbitrary",)),
        name="moe",
    )(meta["tile_e"], meta["n_valid"], meta["next_e"], meta["slot"], meta["tile_rows"], xs, w_up, b_up_p, w_down,
      b_dn, _glu_perm())


def _final_kernel(pos_ref, ys_hbm, x_ref, gate_ref, gtf_ref, g_ref, o_ref, buf, sems, *, tm, n_steps):
    i = pl.program_id(0)
    slot = i & 1

    n = TOP_K * tm

    def gather(step, dst, sem):
        def body(g, carry):
            for u in range(DMA_UNROLL):
                j = g * DMA_UNROLL + u
                pltpu.make_async_copy(ys_hbm.at[pl.ds(pos_ref[step * n + j], 1), :], dst.at[pl.ds(j, 1), :],
                                      sem).start(priority=u % 2)
            return carry
        lax.fori_loop(0, n // DMA_UNROLL, body, 0)

    @pl.when(i == 0)
    def _():
        gather(0, buf.at[0], sems.at[0])

    @pl.when(i + 1 < n_steps)
    def _():
        gather(i + 1, buf.at[1 - slot], sems.at[1 - slot])

    pltpu.make_async_copy(ys_hbm.at[pl.ds(0, n), :], buf.at[slot], sems.at[slot]).wait()
    gate = gate_ref[...]
    y = gate[:, 0:1] * buf[slot, 0:tm, :]
    for k in range(1, TOP_K):
        y = y + gate[:, k:k + 1] * buf[slot, k * tm:(k + 1) * tm, :]
    o_ref[...] = x_ref[...] + gtf_ref[0, 5:6, :] * _rms(y, g_ref[...])


def _final(ys, pos_flat, gate, x1, mod6, g_post_ffn, tm=512):
    B, S, D = x1.shape
    N = B * S
    n_steps = N // tm
    steps_per_b = S // tm
    pos_sm = pos_flat.reshape(n_steps, tm, TOP_K).transpose(0, 2, 1).reshape(N * TOP_K)
    grid_spec = pltpu.PrefetchScalarGridSpec(
        num_scalar_prefetch=1,
        grid=(n_steps,),
        in_specs=[pl.BlockSpec(memory_space=pl.ANY),
                  pl.BlockSpec((tm, D), lambda i, p: (i, 0)),
                  pl.BlockSpec((tm, LANES), lambda i, p: (i, 0)),
                  pl.BlockSpec((1, 6, D), lambda i, p: (i // steps_per_b, 0, 0)),
                  pl.BlockSpec((1, D), lambda i, p: (0, 0))],
        out_specs=pl.BlockSpec((tm, D), lambda i, p: (i, 0)),
        scratch_shapes=[pltpu.VMEM((2, TOP_K * tm, D), F32), pltpu.SemaphoreType.DMA((2,))],
    )
    out = pl.pallas_call(
        functools.partial(_final_kernel, tm=tm, n_steps=n_steps),
        grid_spec=grid_spec,
        out_shape=jax.ShapeDtypeStruct((N, D), F32),
        compiler_params=_cparams(("arbitrary",)),
        name="final",
    )(pos_sm, ys, x1.reshape(N, D), gate, mod6, g_post_ffn.reshape(1, D))
    return out.reshape(B, S, D)


def _layer(x, c, positions, w_ada, b_ada, g_pre_mix, g_post_mix, g_pre_ffn, g_post_ffn,
           w_in, cmp_pos, w_cmp_k1, w_cmp_k2, w_cmp_v1, w_cmp_v2, g_out_nsa, g_out_dil, w_o,
           w_router, b_router, w_up, b_up, w_down, b_down):
    B, S, D = x.shape
    mod6 = _ada(c, w_ada, b_ada).reshape(B, 6, D)
    (q_raw_t, q_rot_t, kv16, ks, vs_t, kw, vw_t, gates, qbs, kbs, vbs) = _proj(x, mod6, g_pre_mix, positions, w_in)
    kcf, vcf = _cmpmlp(kv16, cmp_pos, w_cmp_k1, w_cmp_k2, w_cmp_v1, w_cmp_v2)
    o_cmp, bias_t = _cmpsel(q_raw_t, kcf, vcf)
    o_slc = _slc(q_rot_t, bias_t, ks, vs_t)
    o_win = _win(q_rot_t, kw, vw_t)
    ods, lses = [], []
    for (window, dil), qb, kb, vb in zip(DIL_CONFIGS, qbs, kbs, vbs):
        o, lse = _band(qb, kb, vb, dil=dil, max_dist=window // dil, tq=128, tk=128, nt=2)
        ods.append(o)
        lses.append(lse)
    x1 = _out(o_cmp, o_slc, o_win, gates, ods, lses, x, mod6, g_out_nsa, g_out_dil, w_o, g_post_mix)
    h2, top_idx, gate, rank, counts = _route(x1, mod6, g_pre_ffn, w_router, b_router)
    tm_moe = 512
    pos, meta, n_tiles = _moe_layout(top_idx[:, :TOP_K], rank[:, :TOP_K], counts[0, :N_EXPERTS], tm_moe)
    pos_flat = pos.reshape(B * S * TOP_K)
    xs = _dispatch(h2, pos_flat, meta["pad_start"], meta["pad_len"], meta["n_valid"], n_tiles, tm_moe)
    ys = _moe(xs, meta, w_up, b_up, w_down, b_down, tm_moe)
    return _final(ys, pos_flat, gate, x1, mod6, g_post_ffn)


def kernel(x, c, positions, w_ada, b_ada, g_pre_mix, g_post_mix, g_pre_ffn, g_post_ffn, w_in, cmp_pos,
           w_cmp_k1, w_cmp_k2, w_cmp_v1, w_cmp_v2, g_out_nsa, g_out_dil, w_o, w_router, b_router,
           w_up, b_up, w_down, b_down):
    depth = w_ada.shape[0]
    for l in range(depth):
        x = _layer(x, c, positions, w_ada[l], b_ada[l], g_pre_mix[l], g_post_mix[l], g_pre_ffn[l],
                   g_post_ffn[l], w_in[l], cmp_pos[l], w_cmp_k1[l], w_cmp_k2[l], w_cmp_v1[l], w_cmp_v2[l],
                   g_out_nsa[l], g_out_dil[l], w_o[l], w_router[l], b_router[l], w_up[l], b_up[l],
                   w_down[l], b_down[l])
    return x
```
